```python
import math
import jax
import jax.numpy as jnp
from jax import lax
import numpy as np

D_MODEL = 2048
BATCH = 4
SEQ = 4096
DEPTH = 2

GRID_W = 64
CTX_LEN = 256
N_EVEN = (DEPTH + 1) // 2
N_ODD = DEPTH // 2
N_MOD = 6
EPS = 1e-6
NEG_INF = -1e30

A_WIDTH = D_MODEL // 2
S5_GROUP = 16
S5_GROUPS = A_WIDTH // S5_GROUP
S5_STATE = 64
B_WIDTH = D_MODEL // 2
GLA_HEADS = 4
GLA_DV = B_WIDTH // GLA_HEADS
GLA_DK = GLA_DV // 2
GLA_QK = GLA_HEADS * GLA_DK
GLA_RANK = 16
GLA_TAU = 16.0
GLA_CHUNK = 64
AB_IN = A_WIDTH + 2 * GLA_QK + 2 * B_WIDTH + 2 * GLA_RANK
AB_OUT = A_WIDTH + B_WIDTH
NA_HEAD_DIM = 128
NA_HEADS = D_MODEL // NA_HEAD_DIM
NA_KH_MAX = 8
NA_KW = 16
ROPE_THETA = 10000.0
MOE_GROUPS = 4
MOE_PER_GROUP = 8
MOE_EXPERTS = MOE_GROUPS * MOE_PER_GROUP
MOE_HIDDEN = D_MODEL // 2
MOE_TOP_K = 2
MOE_BLOCK = 128

F32 = jnp.float32

kernel_name = 'hybrid_s5_gla_natten_hmoe_block'


def rmsnorm(x, g):
    x32 = x.astype(F32)
    y = x32 * lax.rsqrt(jnp.mean(x32 * x32, axis=-1, keepdims=True) + EPS)
    return (y * g.astype(F32)).astype(x.dtype)


def modulate(x, g, shift, scale):
    return rmsnorm(x, g) * (1 + scale) + shift


def axial_rope(t):
    L, d = t.shape[1], t.shape[-1]
    half = d // 2
    nf = half // 2
    inv = ROPE_THETA ** (-jnp.arange(nf, dtype=F32) / nf)
    pos = jnp.arange(L)

    def rot(z, p):
        ang = p.astype(F32)[:, None] * inv[None, :]
        cos = jnp.cos(ang)[None, :, None, :]
        sin = jnp.sin(ang)[None, :, None, :]
        z1, z2 = z[..., :nf], z[..., nf:]
        return jnp.concatenate([z1 * cos - z2 * sin, z1 * sin + z2 * cos], axis=-1)

    t32 = t.astype(F32)
    out = jnp.concatenate([rot(t32[..., :half], pos // GRID_W), rot(t32[..., half:], pos % GRID_W)], axis=-1)
    return out.astype(t.dtype)


def s5_scan(u, h0_re, h0_im, lam_re, lam_im, log_dt, b_re, b_im, c_re, c_im):
    lr = jnp.minimum(lam_re.astype(F32), -1e-4)
    li = lam_im.astype(F32)
    dt = jnp.exp(log_dt.astype(F32))[:, None]
    mag = jnp.exp(lr * dt)
    abar_re = mag * jnp.cos(li * dt)
    abar_im = mag * jnp.sin(li * dt)
    num_re = abar_re - 1.0
    den = lr * lr + li * li
    f_re = (num_re * lr + abar_im * li) / den
    f_im = (abar_im * lr - num_re * li) / den
    bu_re = jnp.einsum('blgi,gpi->blgp', u, b_re.astype(F32))
    bu_im = jnp.einsum('blgi,gpi->blgp', u, b_im.astype(F32))
    x_re = f_re * bu_re - f_im * bu_im
    x_im = f_re * bu_im + f_im * bu_re
    L = u.shape[1]
    a_re = jnp.broadcast_to(abar_re, (1, L) + abar_re.shape)
    a_im = jnp.broadcast_to(abar_im, (1, L) + abar_im.shape)

    def combine(e1, e2):
        a1r, a1i, b1r, b1i = e1
        a2r, a2i, b2r, b2i = e2
        return (a2r * a1r - a2i * a1i, a2r * a1i + a2i * a1r,
                a2r * b1r - a2i * b1i + b2r, a2r * b1i + a2i * b1r + b2i)

    pr, pim, hr, hi = lax.associative_scan(combine, (a_re, a_im, x_re, x_im), axis=1)
    h0r, h0i = h0_re[:, None], h0_im[:, None]
    hr = hr + pr * h0r - pim * h0i
    hi = hi + pr * h0i + pim * h0r
    y = jnp.einsum('blgp,gip->blgi', hr, c_re.astype(F32)) - jnp.einsum('blgp,gip->blgi', hi, c_im.astype(F32))
    return y, hr[:, -1], hi[:, -1]


def s5_mixer(u_x, u_c, lam_re, lam_im, log_dt, b_re, b_im, c_re, c_im, d_skip, w_glu):
    def groups(u):
        return u.astype(F32).reshape(u.shape[0], u.shape[1], S5_GROUPS, S5_GROUP)

    ux, uc = groups(u_x), groups(u_c)
    zero = jnp.zeros((u_x.shape[0], S5_GROUPS, S5_STATE), F32)

    def prm(d):
        return (lam_re[d], lam_im[d], log_dt[d], b_re[d], b_im[d], c_re[d], c_im[d])

    def flip(t):
        return jnp.flip(t, axis=1)

    yc_f, hf_re, hf_im = s5_scan(uc, zero, zero, *prm(0))
    yx_f, _, _ = s5_scan(ux, hf_re, hf_im, *prm(0))
    yc_b, hb_re, hb_im = s5_scan(flip(uc), zero, zero, *prm(1))
    yx_b, _, _ = s5_scan(flip(ux), hb_re, hb_im, *prm(1))
    dsk = d_skip.astype(F32).reshape(S5_GROUPS, S5_GROUP)
    wg = w_glu.astype(F32)

    def out(u, yf, yb, like):
        Bn, L = u.shape[:2]
        y = (yf + flip(yb) + dsk * u).reshape(Bn, L, A_WIDTH)
        g = jax.nn.gelu(y)
        return (g * jax.nn.sigmoid(g @ wg)).astype(like.dtype)

    return out(ux, yx_f, yx_b, u_x), out(uc, yc_f, yc_b, u_c)


def gla_chunked(q, k, v, log_a, s0):
    Bn, H, L, dk = q.shape
    dv = v.shape[-1]
    n, C = L // GLA_CHUNK, GLA_CHUNK
    q = q.reshape(Bn, H, n, C, dk)
    k = k.reshape(Bn, H, n, C, dk)
    v = v.reshape(Bn, H, n, C, dv)
    b = jnp.cumsum(log_a.reshape(Bn, H, n, C, dk), axis=3)
    b_last = b[:, :, :, -1:, :]
    q_in = q * jnp.exp(b)
    k_in = k * jnp.exp(-b)
    k_out = k * jnp.exp(b_last - b)
    mask = jnp.tril(jnp.ones((C, C), dtype=bool))
    att = jnp.where(mask, jnp.einsum('bhncd,bhnsd->bhncs', q_in, k_in), 0.0)
    o = jnp.einsum('bhncs,bhnse->bhnce', att, v)
    u_chunk = jnp.einsum('bhncd,bhnce->bhnde', k_out, v)
    decay = jnp.exp(b_last[:, :, :, 0, :])

    def step(s, inp):
        dec, uc = inp
        return dec[..., None] * s + uc, s

    s_fin, s_before = lax.scan(step, s0, (jnp.moveaxis(decay, 2, 0), jnp.moveaxis(u_chunk, 2, 0)))
    s_before = jnp.moveaxis(s_before, 0, 2)
    o = o + jnp.einsum('bhncd,bhnde->bhnce', q_in, s_before)
    return o.reshape(Bn, H, L, dv), s_fin


def gla_mixer(p_x, p_c, w_gate2, b_gate, norm_g):
    o1 = GLA_QK
    o2 = 2 * GLA_QK
    o3 = o2 + B_WIDTH
    o4 = o3 + B_WIDTH

    def prep(p, rope):
        Bn, L, _ = p.shape
        q = p[..., :o1].reshape(Bn, L, GLA_HEADS, GLA_DK)
        k = p[..., o1:o2].reshape(Bn, L, GLA_HEADS, GLA_DK)
        v = p[..., o2:o3].reshape(Bn, L, GLA_HEADS, GLA_DV)
        r = p[..., o3:o4]
        lowrank = p[..., o4:]
        if rope:
            q, k = axial_rope(q), axial_rope(k)
        q = q * GLA_DK ** -0.5

        def gate(d):
            z = lowrank[..., d * GLA_RANK:(d + 1) * GLA_RANK] @ w_gate2[d] + b_gate[d]
            return (jax.nn.log_sigmoid(z.astype(F32)) / GLA_TAU).reshape(Bn, L, GLA_HEADS, GLA_DK)

        def bhld(t):
            return jnp.swapaxes(t, 1, 2).astype(F32)

        return bhld(q), bhld(k), bhld(v), r, bhld(gate(0)), bhld(gate(1))

    qx, kx, vx, rx, ax_f, ax_b = prep(p_x, True)
    qc, kc, vc, rc, ac_f, ac_b = prep(p_c, False)
    s0 = jnp.zeros((qx.shape[0], GLA_HEADS, GLA_DK, GLA_DV), F32)

    def fl(t):
        return jnp.flip(t, axis=2)

    oc_f, sc_f = gla_chunked(qc, kc, vc, ac_f, s0)
    ox_f, _ = gla_chunked(qx, kx, vx, ax_f, sc_f)
    oc_b, sc_b = gla_chunked(fl(qc), fl(kc), fl(vc), fl(ac_b), s0)
    ox_b, _ = gla_chunked(fl(qx), fl(kx), fl(vx), fl(ax_b), sc_b)
    g = norm_g.astype(F32).reshape(GLA_HEADS, GLA_DV)

    def finish(o, r):
        o = jnp.swapaxes(o, 1, 2)
        o = o * lax.rsqrt(jnp.mean(o * o, axis=-1, keepdims=True) + EPS) * g
        Bn, L = o.shape[:2]
        return (o.reshape(Bn, L, B_WIDTH) * jax.nn.silu(r.astype(F32))).astype(r.dtype)

    return finish(ox_f + fl(ox_b), rx), finish(oc_f + fl(oc_b), rc)


def ab_mixer(hx, hc, w_in, w_out, s5_lam_re, s5_lam_im, s5_log_dt, s5_b_re, s5_b_im, s5_c_re, s5_c_im,
             s5_d, s5_w_glu, gla_w_gate2, gla_b_gate, gla_norm_g):
    px, pc = hx @ w_in, hc @ w_in
    ax, ac = s5_mixer(px[..., :A_WIDTH], pc[..., :A_WIDTH], s5_lam_re, s5_lam_im, s5_log_dt,
                      s5_b_re, s5_b_im, s5_c_re, s5_c_im, s5_d, s5_w_glu)
    bx, bc = gla_mixer(px[..., A_WIDTH:], pc[..., A_WIDTH:], gla_w_gate2, gla_b_gate, gla_norm_g)
    return (jnp.concatenate([ax, bx], axis=-1) @ w_out, jnp.concatenate([ac, bc], axis=-1) @ w_out)


def na_mixer(hx, hc, w_qkv, w_out, q_norm, k_norm, rpb, ctx_out):
    Bn, L, D = hx.shape
    rows = L // GRID_W
    kh = min(NA_KH_MAX, rows)
    scale = NA_HEAD_DIM ** -0.5

    def heads(h):
        qkv = (h @ w_qkv).reshape(h.shape[0], h.shape[1], 3, NA_HEADS, NA_HEAD_DIM)
        return rmsnorm(qkv[:, :, 0], q_norm), rmsnorm(qkv[:, :, 1], k_norm), qkv[:, :, 2]

    qx, kx, vx = heads(hx)
    qc, kc, vc = heads(hc)
    qg = qx.reshape(Bn, rows, GRID_W, NA_HEADS, NA_HEAD_DIM)
    kg = kx.reshape(Bn, rows, GRID_W, NA_HEADS, NA_HEAD_DIM)
    vg = vx.reshape(Bn, rows, GRID_W, NA_HEADS, NA_HEAD_DIM)
    col = jnp.arange(GRID_W)
    cs = jnp.clip(col - NA_KW // 2, 0, GRID_W - NA_KW)
    col_ok = (col[None, :] >= cs[:, None]) & (col[None, :] < cs[:, None] + NA_KW)
    dc_idx = jnp.clip(col[None, :] - col[:, None] + NA_KW - 1, 0, 2 * NA_KW - 2)
    rpb_c = rpb.astype(F32)[:, :, dc_idx]

    def row_block(r):
        start = jnp.clip(r - kh // 2, 0, rows - kh)
        q_r = lax.dynamic_index_in_dim(qg, r, axis=1, keepdims=False)
        k_b = lax.dynamic_slice_in_dim(kg, start, kh, axis=1).reshape(Bn, kh * GRID_W, NA_HEADS, NA_HEAD_DIM)
        v_b = lax.dynamic_slice_in_dim(vg, start, kh, axis=1).reshape(Bn, kh * GRID_W, NA_HEADS, NA_HEAD_DIM)
        dr_idx = start + jnp.arange(kh) - r + NA_KH_MAX - 1
        bias = jnp.take(rpb_c, dr_idx, axis=1).transpose(0, 2, 1, 3)
        bias = jnp.where(col_ok[None, :, None, :], bias, NEG_INF).reshape(NA_HEADS, GRID_W, kh * GRID_W)
        s_lat = jnp.einsum('bqhd,bkhd->bhqk', q_r, k_b).astype(F32) * scale + bias
        s_ctx = jnp.einsum('bqhd,bkhd->bhqk', q_r, kc).astype(F32) * scale
        p = jax.nn.softmax(jnp.concatenate([s_lat, s_ctx], axis=-1), axis=-1).astype(vx.dtype)
        nl = kh * GRID_W
        return (jnp.einsum('bhqk,bkhd->bqhd', p[..., :nl], v_b)
                + jnp.einsum('bhqk,bkhd->bqhd', p[..., nl:], vc))

    o = lax.map(row_block, jnp.arange(rows))
    o_x = jnp.moveaxis(o, 0, 1).reshape(Bn, L, D) @ w_out
    if not ctx_out:
        return o_x, None
    s = jnp.einsum('bqhd,bkhd->bhqk', qc, kc).astype(F32) * scale
    p = jax.nn.softmax(s, axis=-1).astype(vc.dtype)
    o_c = jnp.einsum('bhqk,bkhd->bqhd', p, vc).reshape(hc.shape) @ w_out
    return o_x, o_c


def hier_moe(h, w_route_group, w_route_expert, w_gate, w_up, w_down):
    N, D = h.shape
    g_prob = jax.nn.softmax((h @ w_route_group).astype(F32), axis=-1)
    p_top, g_idx = lax.top_k(g_prob, 1)
    e_logits = (h @ w_route_expert).astype(F32).reshape(N, MOE_GROUPS, MOE_PER_GROUP)
    e_sel = jnp.take_along_axis(e_logits, g_idx[:, :, None], axis=1)[:, 0]
    e_val, e_idx = lax.top_k(e_sel, MOE_TOP_K)
    w = jax.nn.softmax(e_val, axis=-1) * p_top
    eid = (g_idx * MOE_PER_GROUP + e_idx).reshape(-1)
    tok = jnp.repeat(jnp.arange(N, dtype=jnp.int32), MOE_TOP_K)
    wt = w.reshape(-1)
    M = N * MOE_TOP_K
    order = jnp.argsort(eid)
    s_eid, s_tok, s_w = eid[order], tok[order], wt[order]
    counts = jnp.bincount(eid, length=MOE_EXPERTS)
    padded = (counts + MOE_BLOCK - 1) // MOE_BLOCK * MOE_BLOCK
    start = jnp.cumsum(counts) - counts
    pend = jnp.cumsum(padded)
    pstart = pend - padded
    dest = pstart[s_eid] + jnp.arange(M) - start[s_eid]
    P = (M + MOE_EXPERTS * (MOE_BLOCK - 1) + MOE_BLOCK - 1) // MOE_BLOCK * MOE_BLOCK
    nb = P // MOE_BLOCK
    buf_tok = jnp.full((P,), N, jnp.int32).at[dest].set(s_tok)
    buf_w = jnp.zeros((P,), F32).at[dest].set(s_w)
    blk_e = jnp.minimum(jnp.searchsorted(pend, jnp.arange(nb) * MOE_BLOCK, side='right'), MOE_EXPERTS - 1)
    h_pad = jnp.concatenate([h, jnp.zeros((1, D), h.dtype)], axis=0)
    xb = h_pad[buf_tok].reshape(nb, MOE_BLOCK, D)

    def expert_block(args):
        xblk, e = args
        return (jax.nn.silu(xblk @ w_gate[e]) * (xblk @ w_up[e])) @ w_down[e]

    yb = lax.map(expert_block, (xb, blk_e)).reshape(P, D)
    y = jax.ops.segment_sum(yb * buf_w.astype(yb.dtype)[:, None], buf_tok, num_segments=N + 1)
    return y[:N]


def setup_inputs(seed: int = 0) -> dict:
    key = jax.random.key(seed)
    ks = jax.random.split(key, 40)

    def nrm(i, shape, s):
        return jax.random.normal(ks[i], shape, F32) * s

    D = D_MODEL
    G, P, I = S5_GROUPS, S5_STATE, S5_GROUP
    E, F = MOE_EXPERTS, MOE_HIDDEN
    lam_im = jnp.broadcast_to(math.pi * jnp.arange(P, dtype=F32), (N_EVEN, 2, G, P))
    return {
        'x': nrm(0, (BATCH, SEQ, D), 1.0),
        'c': nrm(1, (BATCH, D), 1.0),
        'ctx': nrm(2, (BATCH, CTX_LEN, D), 1.0),
        'c_ctx': nrm(3, (D,), 1.0),
        'ada_w': nrm(4, (DEPTH, D, N_MOD * D), 0.5 * D ** -0.5),
        'ada_b': nrm(5, (DEPTH, N_MOD * D), 0.02),
        'norm1_g': 1.0 + nrm(6, (DEPTH, D), 0.02),
        'norm2_g': 1.0 + nrm(7, (DEPTH, D), 0.02),
        'ab_w_in': nrm(8, (N_EVEN, D, AB_IN), D ** -0.5),
        'ab_w_out': nrm(9, (N_EVEN, AB_OUT, D), AB_OUT ** -0.5),
        's5_lam_re': -0.5 + nrm(10, (N_EVEN, 2, G, P), 0.01),
        's5_lam_im': lam_im + nrm(11, (N_EVEN, 2, G, P), 0.01),
        's5_log_dt': jax.random.uniform(ks[12], (N_EVEN, 2, G), F32, minval=math.log(1e-3), maxval=math.log(1e-1)),
        's5_b_re': nrm(13, (N_EVEN, 2, G, P, I), (2 * I) ** -0.5),
        's5_b_im': nrm(14, (N_EVEN, 2, G, P, I), (2 * I) ** -0.5),
        's5_c_re': nrm(15, (N_EVEN, 2, G, I, P), P ** -0.5),
        's5_c_im': nrm(16, (N_EVEN, 2, G, I, P), P ** -0.5),
        's5_d': nrm(17, (N_EVEN, A_WIDTH), 1.0),
        's5_w_glu': nrm(18, (N_EVEN, A_WIDTH, A_WIDTH), A_WIDTH ** -0.5),
        'gla_w_gate2': nrm(19, (N_EVEN, 2, GLA_RANK, GLA_QK), GLA_RANK ** -0.5),
        'gla_b_gate': nrm(20, (N_EVEN, 2, GLA_QK), 0.1),
        'gla_norm_g': 1.0 + nrm(21, (N_EVEN, B_WIDTH), 0.02),
        'na_w_qkv': nrm(22, (N_ODD, D, 3 * D), D ** -0.5),
        'na_w_out': nrm(23, (N_ODD, D, D), D ** -0.5),
        'na_q_norm': 1.0 + nrm(24, (N_ODD, NA_HEAD_DIM), 0.02),
        'na_k_norm': 1.0 + nrm(25, (N_ODD, NA_HEAD_DIM), 0.02),
        'na_rpb': nrm(26, (N_ODD, NA_HEADS, 2 * NA_KH_MAX - 1, 2 * NA_KW - 1), 0.05),
        'moe_w_route_group': nrm(27, (DEPTH, D, MOE_GROUPS), D ** -0.5),
        'moe_w_route_expert': nrm(28, (DEPTH, D, E), D ** -0.5),
        'moe_w_gate': nrm(29, (DEPTH, E, D, F), D ** -0.5),
        'moe_w_up': nrm(30, (DEPTH, E, D, F), D ** -0.5),
        'moe_w_down': nrm(31, (DEPTH, E, F, D), F ** -0.5),
    }


def reference(x, c, ctx, c_ctx, ada_w, ada_b, norm1_g, norm2_g, ab_w_in, ab_w_out,
              s5_lam_re, s5_lam_im, s5_log_dt, s5_b_re, s5_b_im, s5_c_re, s5_c_im, s5_d, s5_w_glu,
              gla_w_gate2, gla_b_gate, gla_norm_g, na_w_qkv, na_w_out, na_q_norm, na_k_norm, na_rpb,
              moe_w_route_group, moe_w_route_expert, moe_w_gate, moe_w_up, moe_w_down):
    D = x.shape[-1]
    sc = jax.nn.silu(c)
    scc = jax.nn.silu(c_ctx)
    for i in range(DEPTH):
        j = i // 2
        last = i == DEPTH - 1
        mod_x = (sc @ ada_w[i] + ada_b[i])[:, None, :]
        mod_c = (scc @ ada_w[i] + ada_b[i])[None, None, :]
        sh1, sc1, g1, sh2, sc2, g2 = jnp.split(mod_x, N_MOD, axis=-1)
        csh1, csc1, cg1, csh2, csc2, cg2 = jnp.split(mod_c, N_MOD, axis=-1)
        hx = modulate(x, norm1_g[i], sh1, sc1)
        hc = modulate(ctx, norm1_g[i], csh1, csc1)
        if i % 2 == 0:
            ox, oc = ab_mixer(hx, hc, ab_w_in[j], ab_w_out[j], s5_lam_re[j], s5_lam_im[j], s5_log_dt[j],
                              s5_b_re[j], s5_b_im[j], s5_c_re[j], s5_c_im[j], s5_d[j], s5_w_glu[j],
                              gla_w_gate2[j], gla_b_gate[j], gla_norm_g[j])
        else:
            ox, oc = na_mixer(hx, hc, na_w_qkv[j], na_w_out[j], na_q_norm[j], na_k_norm[j], na_rpb[j],
                              not last)
        x = x + g1 * ox
        hx = modulate(x, norm2_g[i], sh2, sc2)
        moe_w = (moe_w_route_group[i], moe_w_route_expert[i], moe_w_gate[i], moe_w_up[i], moe_w_down[i])
        if last:
            x = x + g2 * hier_moe(hx.reshape(-1, D), *moe_w).reshape(x.shape)
        else:
            ctx = ctx + cg1 * oc
            hc = modulate(ctx, norm2_g[i], csh2, csc2)
            nx = hx.shape[0] * hx.shape[1]
            y = hier_moe(jnp.concatenate([hx.reshape(-1, D), hc.reshape(-1, D)], axis=0), *moe_w)
            x = x + g2 * y[:nx].reshape(x.shape)
            ctx = ctx + cg2 * y[nx:].reshape(ctx.shape)
    return x
```

```python
import functools
import math

import jax
import jax.numpy as jnp
from jax import lax
from jax.experimental import pallas as pl
from jax.experimental.pallas import tpu as pltpu

F32 = jnp.float32
BF16 = jnp.bfloat16
HIGHEST = lax.Precision.HIGHEST

V7X_VMEM_BYTES = 64 * 1024 * 1024
VMEM_LIMIT = V7X_VMEM_BYTES - 12 * 1024 * 1024
LANES = 128

EPS = 1e-6
NEG_INF = -1e30
GRID_W = 64
N_MOD = 6
S5_GROUP = 16
S5_STATE = 64
S5_CHUNK = 16
S5_SUPER = 16
GLA_HEADS = 4
GLA_RANK = 16
GLA_TAU = 16.0
GLA_CHUNK = 64
GLA_BLOCK = 256
NA_HEAD_DIM = 128
NA_KH = 8
NA_KW = 16
ROPE_THETA = 10000.0
MOE_GROUPS = 4
MOE_PER_GROUP = 8
MOE_EXPERTS = MOE_GROUPS * MOE_PER_GROUP
MOE_ROWS = 256


def _cparams(sem):
    return pltpu.CompilerParams(dimension_semantics=sem, vmem_limit_bytes=VMEM_LIMIT)


def _nt(a, b):
    return lax.dot_general(a, b, (((1,), (1,)), ((), ())), preferred_element_type=F32)


def _ada_kernel(c_ref, w_ref, b_ref, o_ref):
    c = c_ref[...]
    s = c * jax.nn.sigmoid(c)
    o_ref[...] = jnp.dot(s.astype(BF16), w_ref[...].astype(BF16), preferred_element_type=F32) + b_ref[...]


def _ada_mod(cvec, w, b):
    R, D = cvec.shape
    N = w.shape[1]
    tn = 1024
    out = pl.pallas_call(
        _ada_kernel,
        grid=(N // tn,),
        in_specs=[pl.BlockSpec((R, D), lambda j: (0, 0)),
                  pl.BlockSpec((D, tn), lambda j: (0, j)),
                  pl.BlockSpec((1, tn), lambda j: (0, j))],
        out_specs=pl.BlockSpec((R, tn), lambda j: (0, j)),
        out_shape=jax.ShapeDtypeStruct((R, N), F32),
        compiler_params=_cparams(("arbitrary",)),
        name="ada_mod",
    )(cvec, w, b.reshape(1, N))
    return out.reshape(R, N_MOD, 1, D)


def _mod_sel(nct, nb):
    if nct == 0:
        return lambda b, i: b
    return lambda b, i: jnp.where(i < nct, nb, b)


def _normmod(x, g, sh, sc):
    y = x * lax.rsqrt(jnp.mean(x * x, axis=-1, keepdims=True) + EPS) * g
    return y * (1.0 + sc) + sh


def _normmod_mm_kernel(u_ref, g_ref, sh_ref, sc_ref, w_ref, o_ref, xn_ref):
    @pl.when(pl.program_id(2) == 0)
    def _():
        xn_ref[...] = _normmod(u_ref[...], g_ref[...], sh_ref[...], sc_ref[...]).astype(BF16)

    o_ref[...] = jnp.dot(xn_ref[...], w_ref[...], preferred_element_type=F32).astype(o_ref.dtype)


def _normmod_matmul(u, gamma, mods, k_shift, k_scale, w, *, n_ctx, tm, tn, out_dtype=F32):
    B, T, D = u.shape
    N = w.shape[1]
    sel = _mod_sel(n_ctx // tm, B)
    return pl.pallas_call(
        _normmod_mm_kernel,
        grid=(B, T // tm, N // tn),
        in_specs=[pl.BlockSpec((None, tm, D), lambda b, i, j: (b, i, 0)),
                  pl.BlockSpec((1, D), lambda b, i, j: (0, 0)),
                  pl.BlockSpec((None, None, 1, D), lambda b, i, j: (sel(b, i), k_shift, 0, 0)),
                  pl.BlockSpec((None, None, 1, D), lambda b, i, j: (sel(b, i), k_scale, 0, 0)),
                  pl.BlockSpec((D, tn), lambda b, i, j: (0, j))],
        out_specs=pl.BlockSpec((None, tm, tn), lambda b, i, j: (b, i, j)),
        out_shape=jax.ShapeDtypeStruct((B, T, N), out_dtype),
        scratch_shapes=[pltpu.VMEM((tm, D), BF16)],
        compiler_params=_cparams(("arbitrary", "arbitrary", "arbitrary")),
        name="normmod_matmul",
    )(u, gamma.reshape(1, D), mods, mods, w)


def _mm_res_kernel(a_ref, w_ref, r_ref, g_ref, o_ref):
    acc = jnp.dot(a_ref[...], w_ref[...], preferred_element_type=F32)
    o_ref[...] = r_ref[...] + g_ref[...] * acc


def _matmul_residual(a, w, res, mods, k_gate, *, n_ctx, row0, tm, tn):
    B, Ta, K = a.shape
    N = w.shape[1]
    sel = _mod_sel(n_ctx // tm, B)
    r0 = row0 // tm
    return pl.pallas_call(
        _mm_res_kernel,
        grid=(B, Ta // tm, N // tn),
        in_specs=[pl.BlockSpec((None, tm, K), lambda b, i, j: (b, i, 0)),
                  pl.BlockSpec((K, tn), lambda b, i, j: (0, j)),
                  pl.BlockSpec((None, tm, tn), lambda b, i, j: (b, i + r0, j)),
                  pl.BlockSpec((None, None, 1, tn), lambda b, i, j: (sel(b, i), k_gate, 0, j))],
        out_specs=pl.BlockSpec((None, tm, tn), lambda b, i, j: (b, i, j)),
        out_shape=jax.ShapeDtypeStruct((B, Ta, N), F32),
        compiler_params=_cparams(("arbitrary", "arbitrary", "arbitrary")),
        name="matmul_residual",
    )(a, w, res, mods)


def _s5_operators(lam_re, lam_im, log_dt, b_re, b_im, c_re, c_im):
    T = S5_CHUNK
    lr = jnp.minimum(lam_re.astype(F32), -1e-4)
    li = lam_im.astype(F32)
    dt = jnp.exp(log_dt.astype(F32))[..., None]
    mag = jnp.exp(lr * dt)
    a_re = mag * jnp.cos(li * dt)
    a_im = mag * jnp.sin(li * dt)
    num_re = a_re - 1.0
    den = lr * lr + li * li
    f = lax.complex((num_re * lr + a_im * li) / den, (a_im * lr - num_re * li) / den)
    a = lax.complex(a_re, a_im)
    bbar = f[..., None] * lax.complex(b_re.astype(F32), b_im.astype(F32))
    cc = lax.complex(c_re.astype(F32), c_im.astype(F32))

    def powers(z, n):
        return jnp.concatenate([jnp.ones_like(z)[None], jnp.cumprod(jnp.broadcast_to(z, (n,) + z.shape), axis=0)])

    pw = powers(a, T)
    a16 = pw[T]
    pw16 = powers(a16, S5_SUPER)
    a256 = pw16[S5_SUPER]
    kk = jnp.real(jnp.einsum('dgip,kdgp,dgpj->dkgij', cc, pw[:T], bbar))
    G = a.shape[1]
    I = S5_GROUP
    s_idx = jnp.arange(T)[:, None]
    t_idx = jnp.arange(T)[None, :]

    def toeplitz(kd, lag):
        m = jnp.where((lag >= 0)[:, :, None, None, None], kd[jnp.clip(lag, 0, T - 1)], 0.0)
        return jnp.transpose(m, (2, 0, 4, 1, 3)).reshape(G, T * I, T * I)

    mt = toeplitz(kk[0], t_idx - s_idx) + toeplitz(kk[1], s_idx - t_idx)

    def v_op(d, pw_s):
        vc = jnp.transpose(pw_s[:, :, :, None] * bbar[d][None], (1, 0, 3, 2)).reshape(G, T * I, S5_STATE)
        vr, vi = jnp.real(vc), jnp.imag(vc)
        return jnp.concatenate([vr, vi, -vi, vr], axis=-1)

    vt = jnp.concatenate([v_op(0, pw[:T, 0][::-1]), v_op(1, pw[:T, 1])], axis=-1)

    def w_op(d, pw_t):
        ca = jnp.transpose(cc[d][None] * pw_t[:, :, None, :], (1, 3, 0, 2)).reshape(G, S5_STATE, T * I)
        return jnp.concatenate([jnp.real(ca), -jnp.imag(ca)], axis=1)

    w = jnp.concatenate([w_op(0, pw[1:T + 1, 0]), w_op(1, pw[1:T + 1, 1][::-1])], axis=1)

    def dup(z):
        return (jnp.concatenate([jnp.real(z), jnp.real(z)], -1), jnp.concatenate([jnp.imag(z), jnp.imag(z)], -1))

    rows = []
    for d in range(2):
        rows += [*dup(a16[d]), *dup(a256[d])]
    dec = jnp.stack(rows, axis=1)
    prs = []
    for d in range(2):
        pr, pi = dup(jnp.transpose(pw16[:S5_SUPER, d], (1, 0, 2)))
        prs += [pr, pi]
    pwt = jnp.concatenate(prs, axis=1)
    return mt.astype(BF16), vt.astype(BF16), w.astype(BF16), dec, pwt


def _s5_kernel(x_ref, mt_ref, vt_ref, w_ref, dec_ref, pw_ref, dsk_ref, y_ref, hin_ref, hs_ref, *, kx, kp):
    nj = S5_SUPER
    x = x_ref[...]
    xb = x.astype(BF16)
    y = jnp.dot(xb, mt_ref[...], preferred_element_type=F32) + x * dsk_ref[...]
    vv = jnp.dot(xb, vt_ref[...], preferred_element_type=F32)
    dec = dec_ref[...]
    pw = pw_ref[...]
    hs_ref[...] = jnp.zeros_like(hs_ref)
    for d in range(2):
        ar16, ai16, ar256, ai256 = (dec[4 * d + n:4 * d + n + 1] for n in range(4))

        def v(j, d=d):
            return vv[j * kp:(j + 1) * kp, 256 * d:256 * d + LANES]

        def jv(j, d=d):
            return vv[j * kp:(j + 1) * kp, 256 * d + LANES:256 * (d + 1)]

        order = list(range(nj)) if d == 0 else list(range(nj - 1, -1, -1))
        loc = [None] * nj
        l = jnp.zeros((kp, LANES), F32)
        g = jnp.zeros((kp, LANES), F32)
        for n, j in enumerate(order):
            if n > 0:
                p = order[n - 1]
                l, g = ar16 * l + ai16 * g + v(p), ar16 * g - ai16 * l + jv(p)
            loc[j] = l
        p = order[-1]
        e, je = ar16 * l + ai16 * g + v(p), ar16 * g - ai16 * l + jv(p)
        seq = [kx] + (list(range(kx)) if d == 0 else list(range(kx - 1, -1, -1)))
        cur = jnp.zeros((1, LANES), F32)
        jcur = jnp.zeros((1, LANES), F32)
        for k in seq:
            hs_ref[2 * d, k:k + 1, :] = cur
            hs_ref[2 * d + 1, k:k + 1, :] = jcur
            cur, jcur = (ar256 * cur + ai256 * jcur + e[k:k + 1], ar256 * jcur - ai256 * cur + je[k:k + 1])
        hs = hs_ref[2 * d]
        jhs = hs_ref[2 * d + 1]
        for n, j in enumerate(order):
            pr = pw[32 * d + n:32 * d + n + 1]
            pi = pw[32 * d + 16 + n:32 * d + 16 + n + 1]
            hin_ref[j * kp:(j + 1) * kp, LANES * d:LANES * (d + 1)] = loc[j] + pr * hs + pi * jhs
    y_ref[...] = y + jnp.dot(hin_ref[...].astype(BF16), w_ref[...], preferred_element_type=F32)


def _s5_scan(xg, ops, dsk, *, kx, kp):
    B, G, R, C = xg.shape
    mt, vt, w, dec, pw = ops
    return pl.pallas_call(
        functools.partial(_s5_kernel, kx=kx, kp=kp),
        grid=(G, B),
        in_specs=[pl.BlockSpec((None, None, R, C), lambda g, b: (b, g, 0, 0)),
                  pl.BlockSpec((None, C, C), lambda g, b: (g, 0, 0)),
                  pl.BlockSpec((None, C, 2 * C), lambda g, b: (g, 0, 0)),
                  pl.BlockSpec((None, C, C), lambda g, b: (g, 0, 0)),
                  pl.BlockSpec((None, 8, LANES), lambda g, b: (g, 0, 0)),
                  pl.BlockSpec((None, 64, LANES), lambda g, b: (g, 0, 0)),
                  pl.BlockSpec((None, 1, C), lambda g, b: (g, 0, 0))],
        out_specs=pl.BlockSpec((None, None, R, C), lambda g, b: (b, g, 0, 0)),
        out_shape=jax.ShapeDtypeStruct((B, G, R, C), F32),
        scratch_shapes=[pltpu.VMEM((R, C), F32), pltpu.VMEM((4, kp, LANES), F32)],
        compiler_params=_cparams(("arbitrary", "arbitrary")),
        name="s5_scan",
    )(xg, mt, vt, w, dec, pw, dsk)


def _s5_to_groups(u, n_ctx, kp):
    B, T, A = u.shape
    G = A // S5_GROUP
    kx = (T - n_ctx) // (S5_CHUNK * S5_SUPER)
    ux = u[:, n_ctx:].reshape(B, kx, S5_SUPER, S5_CHUNK, G, S5_GROUP)
    uc = u[:, :n_ctx].reshape(B, 1, S5_SUPER, S5_CHUNK, G, S5_GROUP)
    pad = jnp.zeros((B, kp - kx - 1, S5_SUPER, S5_CHUNK, G, S5_GROUP), u.dtype)
    z = jnp.concatenate([ux, uc, pad], axis=1)
    z = jnp.transpose(z, (0, 4, 2, 1, 3, 5))
    return z.reshape(B, G, S5_SUPER * kp, S5_CHUNK * S5_GROUP)


def _s5_from_groups(y, n_ctx, kx, kp):
    B, G = y.shape[:2]
    z = y.reshape(B, G, S5_SUPER, kp, S5_CHUNK, S5_GROUP)
    z = jnp.transpose(z, (0, 3, 2, 4, 1, 5))
    yx = z[:, :kx].reshape(B, kx * S5_SUPER * S5_CHUNK, G * S5_GROUP)
    yc = z[:, kx].reshape(B, n_ctx, G * S5_GROUP)
    return jnp.concatenate([yc, yx], axis=1)


def _s5_glu_kernel(y_ref, w_ref, o_ref):
    g = jax.nn.gelu(y_ref[...])
    z = jnp.dot(g.astype(BF16), w_ref[...], preferred_element_type=F32)
    o_ref[...] = (g * jax.nn.sigmoid(z)).astype(o_ref.dtype)


def _s5_glu(y, w, *, tm):
    B, T, A = y.shape
    return pl.pallas_call(
        _s5_glu_kernel,
        grid=(B, T // tm),
        in_specs=[pl.BlockSpec((None, tm, A), lambda b, i: (b, i, 0)),
                  pl.BlockSpec((A, A), lambda b, i: (0, 0))],
        out_specs=pl.BlockSpec((None, tm, A), lambda b, i: (b, i, 0)),
        out_shape=jax.ShapeDtypeStruct((B, T, A), BF16),
        compiler_params=_cparams(("arbitrary", "arbitrary")),
        name="s5_glu",
    )(y, w)


def _log_sigmoid(z):
    return jnp.minimum(z, 0.0) - jnp.log(1.0 + jnp.exp(-jnp.abs(z)))


def _gla_kernel(q_ref, k_ref, v_ref, r_ref, lr_ref, cos_ref, sin_ref, wg_ref, wgt_ref, bg_ref, bgt_ref, ng_ref,
                o_ref, s_ref, of_ref, *, nblk, dk):
    p = pl.program_id(2)
    s = pl.program_id(3)
    C = GLA_CHUNK
    nch = GLA_BLOCK // C
    blk = jnp.where(p == 0, s, jnp.where(s == 0, 0, nblk - s))

    @pl.when(s == 0)
    def _():
        s_ref[...] = jnp.zeros_like(s_ref)

    lane = lax.broadcasted_iota(jnp.int32, (C, dk), 1)
    lower = (lane & 63) < 32
    row = lax.broadcasted_iota(jnp.int32, (C, C), 0)
    col = lax.broadcasted_iota(jnp.int32, (C, C), 1)
    eye = (lax.broadcasted_iota(jnp.int32, (dk, dk), 0) == lax.broadcasted_iota(jnp.int32, (dk, dk), 1)).astype(BF16)
    wg = wg_ref[...]
    wgt = wgt_ref[...]
    bg = bg_ref[...]
    bgt = bgt_ref[...]

    def rope(z, cos, sin):
        return z * cos + jnp.where(lower, pltpu.roll(z, dk - 32, 1), pltpu.roll(z, 32, 1)) * sin

    def run(reverse):
        mask = (row <= col) if reverse else (row >= col)
        tri = mask.astype(F32)
        for n in range(nch):
            c0 = (nch - 1 - n if reverse else n) * C
            rows = pl.ds(c0, C)
            cos = cos_ref[rows, :]
            sin = sin_ref[rows, :]
            q = rope(q_ref[rows, :], cos, sin) * (dk ** -0.5)
            k = rope(k_ref[rows, :], cos, sin)
            vb = v_ref[rows, :].astype(BF16)
            lrb = lr_ref[rows, :].astype(BF16)
            la = _log_sigmoid(jnp.dot(lrb, wg, preferred_element_type=F32) + bg) * (1.0 / GLA_TAU)
            lat = _log_sigmoid(_nt(wgt, lrb) + bgt) * (1.0 / GLA_TAU)
            b = jnp.dot(tri, la, precision=HIGHEST, preferred_element_type=F32)
            btot = jnp.sum(la, axis=0, keepdims=True)
            dcol = jnp.exp(jnp.sum(lat, axis=1, keepdims=True))
            qi = (q * jnp.exp(b)).astype(BF16)
            ki = (k * jnp.exp(-b)).astype(BF16)
            ko = (k * jnp.exp(btot - b)).astype(BF16)
            att = jnp.where(mask, _nt(qi, ki), 0.0).astype(BF16)
            st = s_ref[...]
            o = jnp.dot(att, vb, preferred_element_type=F32) + jnp.dot(qi, st.astype(BF16), preferred_element_type=F32)
            kot = _nt(eye, ko).astype(BF16)
            s_ref[...] = st * dcol + jnp.dot(kot, vb, preferred_element_type=F32)
            tok = pl.ds(pl.multiple_of(blk * GLA_BLOCK, GLA_BLOCK) + c0, C)
            if not reverse:
                of_ref[tok, :] = o
            else:
                o = o + of_ref[tok, :]
                o = o * lax.rsqrt(jnp.mean(o * o, axis=-1, keepdims=True) + EPS) * ng_ref[...]
                r = r_ref[rows, :]
                o_ref[rows, :] = (o * (r * jax.nn.sigmoid(r))).astype(o_ref.dtype)

    @pl.when(p == 0)
    def _():
        run(False)

    @pl.when(p == 1)
    def _():
        run(True)


def _rope_tables(n_ctx, seq, dk):
    nf = dk // 4
    inv = ROPE_THETA ** (-jnp.arange(nf, dtype=F32) / nf)
    pos = jnp.arange(seq)
    ang_r = (pos // GRID_W).astype(F32)[:, None] * inv[None, :]
    ang_c = (pos % GRID_W).astype(F32)[:, None] * inv[None, :]
    cos = jnp.concatenate([jnp.cos(ang_r)] * 2 + [jnp.cos(ang_c)] * 2, axis=-1)
    sin = jnp.concatenate([-jnp.sin(ang_r), jnp.sin(ang_r), -jnp.sin(ang_c), jnp.sin(ang_c)], axis=-1)
    cos = jnp.concatenate([jnp.ones((n_ctx, dk), F32), cos], axis=0)
    sin = jnp.concatenate([jnp.zeros((n_ctx, dk), F32), sin], axis=0)
    return cos, sin


def _gla(px, w_gate2, b_gate, norm_g, *, n_ctx, a_width):
    B, T, _ = px.shape
    H = GLA_HEADS
    qk = w_gate2.shape[-1]
    dk = qk // H
    bw = norm_g.shape[-1]
    dv = bw // H
    nblk = T // GLA_BLOCK
    cos, sin = _rope_tables(n_ctx, T - n_ctx, dk)
    lr_col0 = a_width + 2 * qk + 2 * bw
    wg = jnp.zeros((2, H, LANES, dk), F32)
    for d in range(2):
        wg = wg.at[d, :, d * GLA_RANK:(d + 1) * GLA_RANK, :].set(
            jnp.transpose(w_gate2[d].reshape(GLA_RANK, H, dk), (1, 0, 2)))
    wgt = jnp.swapaxes(wg, 2, 3).astype(BF16)
    wg = wg.astype(BF16)
    bg = b_gate.astype(F32).reshape(2, H, 1, dk)
    bgt = jnp.broadcast_to(b_gate.astype(F32).reshape(2, H, dk, 1), (2, H, dk, GLA_CHUNK))
    ng = norm_g.astype(F32).reshape(H, 1, dv)

    def blk(p, s):
        return jnp.where(p == 0, s, jnp.where(s == 0, 0, nblk - s))

    qb, kb = a_width // dk, (a_width + qk) // dk
    vbk, rbk = (a_width + 2 * qk) // dv, (a_width + 2 * qk + bw) // dv
    lb = lr_col0 // LANES
    return pl.pallas_call(
        functools.partial(_gla_kernel, nblk=nblk, dk=dk),
        grid=(B, H, 2, nblk),
        in_specs=[pl.BlockSpec((None, GLA_BLOCK, dk), lambda b, h, p, s: (b, blk(p, s), qb + h)),
                  pl.BlockSpec((None, GLA_BLOCK, dk), lambda b, h, p, s: (b, blk(p, s), kb + h)),
                  pl.BlockSpec((None, GLA_BLOCK, dv), lambda b, h, p, s: (b, blk(p, s), vbk + h)),
                  pl.BlockSpec((None, GLA_BLOCK, dv), lambda b, h, p, s: (b, blk(p, s), rbk + h)),
                  pl.BlockSpec((None, GLA_BLOCK, LANES), lambda b, h, p, s: (b, blk(p, s), lb)),
                  pl.BlockSpec((GLA_BLOCK, dk), lambda b, h, p, s: (blk(p, s), 0)),
                  pl.BlockSpec((GLA_BLOCK, dk), lambda b, h, p, s: (blk(p, s), 0)),
                  pl.BlockSpec((None, None, LANES, dk), lambda b, h, p, s: (p, h, 0, 0)),
                  pl.BlockSpec((None, None, dk, LANES), lambda b, h, p, s: (p, h, 0, 0)),
                  pl.BlockSpec((None, None, 1, dk), lambda b, h, p, s: (p, h, 0, 0)),
                  pl.BlockSpec((None, None, dk, GLA_CHUNK), lambda b, h, p, s: (p, h, 0, 0)),
                  pl.BlockSpec((None, 1, dv), lambda b, h, p, s: (h, 0, 0))],
        out_specs=pl.BlockSpec((None, GLA_BLOCK, dv), lambda b, h, p, s: (b, jnp.where(p == 0, 0, blk(p, s)), h)),
        out_shape=jax.ShapeDtypeStruct((B, T, bw), BF16),
        scratch_shapes=[pltpu.VMEM((dk, dv), F32), pltpu.VMEM((T, dv), F32)],
        compiler_params=_cparams(("arbitrary", "arbitrary", "arbitrary", "arbitrary")),
        name="gla",
    )(px, px, px, px, px, cos, sin, wg, wgt, bg, bgt, ng)


def _na_kernel(q_ref, k_ref, v_ref, bias_ref, qn_ref, kn_ref, o_ref, kb_ref, vb_ref, *, n_ctx, rows):
    hd = NA_HEAD_DIM
    W = GRID_W
    T = k_ref.shape[0]
    step = n_ctx

    def prep(i, c):
        sl = pl.ds(pl.multiple_of(i * step, step), step)
        kk = k_ref[sl, :]
        kb_ref[sl, :] = (kk * lax.rsqrt(jnp.mean(kk * kk, axis=-1, keepdims=True) + EPS) * kn_ref[...]).astype(BF16)
        vb_ref[sl, :] = v_ref[sl, :].astype(BF16)
        return c

    lax.fori_loop(0, T // step, prep, 0)
    scale = hd ** -0.5

    def row(r, c):
        start = jnp.clip(r - NA_KH // 2, 0, rows - NA_KH)
        qs = pl.ds(pl.multiple_of(n_ctx + r * W, W), W)
        q = q_ref[qs, :]
        q = (q * lax.rsqrt(jnp.mean(q * q, axis=-1, keepdims=True) + EPS) * qn_ref[...] * scale).astype(BF16)
        ws = pl.ds(pl.multiple_of(n_ctx + start * W, W), NA_KH * W)
        s_lat = _nt(q, kb_ref[ws, :]) + bias_ref[start - r + NA_KH - 1]
        s_ctx = _nt(q, kb_ref[0:n_ctx, :])
        m = jnp.maximum(jnp.max(s_lat, axis=-1, keepdims=True), jnp.max(s_ctx, axis=-1, keepdims=True))
        p_lat = jnp.exp(s_lat - m)
        p_ctx = jnp.exp(s_ctx - m)
        den = jnp.sum(p_lat, axis=-1, keepdims=True) + jnp.sum(p_ctx, axis=-1, keepdims=True)
        o = (jnp.dot(p_lat.astype(BF16), vb_ref[ws, :], preferred_element_type=F32)
             + jnp.dot(p_ctx.astype(BF16), vb_ref[0:n_ctx, :], preferred_element_type=F32))
        o_ref[pl.ds(pl.multiple_of(r * W, W), W), :] = (o / den).astype(o_ref.dtype)
        return c

    lax.fori_loop(0, rows, row, 0)


def _na_bias(rpb):
    H = rpb.shape[0]
    col = jnp.arange(GRID_W)
    cs = jnp.clip(col - NA_KW // 2, 0, GRID_W - NA_KW)
    col_ok = (col[None, :] >= cs[:, None]) & (col[None, :] < cs[:, None] + NA_KW)
    dc_idx = jnp.clip(col[None, :] - col[:, None] + NA_KW - 1, 0, 2 * NA_KW - 2)
    rpb_c = jnp.where(col_ok[None, None], rpb.astype(F32)[:, :, dc_idx], NEG_INF)
    idx = jnp.arange(NA_KH)[:, None] + jnp.arange(NA_KH)[None, :]
    b = rpb_c[:, idx]
    return jnp.transpose(b, (0, 1, 3, 2, 4)).reshape(H, NA_KH, GRID_W, NA_KH * GRID_W)


def _na_attention(qkv, q_norm, k_norm, rpb, *, n_ctx):
    B, T, D3 = qkv.shape
    D = D3 // 3
    H = D // NA_HEAD_DIM
    seq = T - n_ctx
    rows = seq // GRID_W
    bias = _na_bias(rpb)
    hd = NA_HEAD_DIM
    return pl.pallas_call(
        functools.partial(_na_kernel, n_ctx=n_ctx, rows=rows),
        grid=(B, H),
        in_specs=[pl.BlockSpec((None, T, hd), lambda b, h: (b, 0, h)),
                  pl.BlockSpec((None, T, hd), lambda b, h: (b, 0, H + h)),
                  pl.BlockSpec((None, T, hd), lambda b, h: (b, 0, 2 * H + h)),
                  pl.BlockSpec((None, NA_KH, GRID_W, NA_KH * GRID_W), lambda b, h: (h, 0, 0, 0)),
                  pl.BlockSpec((1, hd), lambda b, h: (0, 0)),
                  pl.BlockSpec((1, hd), lambda b, h: (0, 0))],
        out_specs=pl.BlockSpec((None, seq, hd), lambda b, h: (b, 0, h)),
        out_shape=jax.ShapeDtypeStruct((B, seq, D), BF16),
        scratch_shapes=[pltpu.VMEM((T, hd), BF16), pltpu.VMEM((T, hd), BF16)],
        compiler_params=_cparams(("arbitrary", "arbitrary")),
        name="na_attention",
    )(qkv, qkv, qkv, bias, q_norm.astype(F32).reshape(1, hd), k_norm.astype(F32).reshape(1, hd))


def _router_kernel(u_ref, g_ref, sh_ref, sc_ref, wr_ref, h_ref, mf_ref, mi_ref):
    h = _normmod(u_ref[...], g_ref[...], sh_ref[...], sc_ref[...])
    h_ref[...] = h
    logits = jnp.dot(h, wr_ref[...], precision=HIGHEST, preferred_element_type=F32)
    lane = lax.broadcasted_iota(jnp.int32, logits.shape, 1).astype(F32)
    big = 1e9

    def first_max(vals):
        m = jnp.max(vals, axis=-1, keepdims=True)
        return m, jnp.min(jnp.where(vals == m, lane, big), axis=-1, keepdims=True)

    gl = jnp.where(lane < MOE_GROUPS, logits, -jnp.inf)
    gmax, gidx = first_max(gl)
    p_top = 1.0 / jnp.sum(jnp.exp(gl - gmax), axis=-1, keepdims=True)
    lo = MOE_GROUPS + MOE_PER_GROUP * gidx
    el = jnp.where((lane >= lo) & (lane < lo + MOE_PER_GROUP), logits, -jnp.inf)
    v1, i1 = first_max(el)
    v2, i2 = first_max(jnp.where(lane == i1, -jnp.inf, el))
    e2 = jnp.exp(v2 - v1)
    w1 = p_top / (1.0 + e2)
    w2 = p_top * e2 / (1.0 + e2)
    mf_ref[...] = jnp.where(lane == 0, w1, jnp.where(lane == 1, w2, 0.0))
    mi_ref[...] = jnp.where(lane == 0, i1 - MOE_GROUPS, jnp.where(lane == 1, i2 - MOE_GROUPS, 0.0)).astype(jnp.int32)


def _router(u, gamma, mods, k_shift, k_scale, wr, *, n_ctx, tm):
    B, T, D = u.shape
    sel = _mod_sel(n_ctx // tm, B)
    return pl.pallas_call(
        _router_kernel,
        grid=(B, T // tm),
        in_specs=[pl.BlockSpec((None, tm, D), lambda b, i: (b, i, 0)),
                  pl.BlockSpec((1, D), lambda b, i: (0, 0)),
                  pl.BlockSpec((None, None, 1, D), lambda b, i: (sel(b, i), k_shift, 0, 0)),
                  pl.BlockSpec((None, None, 1, D), lambda b, i: (sel(b, i), k_scale, 0, 0)),
                  pl.BlockSpec((D, LANES), lambda b, i: (0, 0))],
        out_specs=[pl.BlockSpec((None, tm, D), lambda b, i: (b, i, 0)),
                   pl.BlockSpec((None, tm, LANES), lambda b, i: (b, i, 0)),
                   pl.BlockSpec((None, tm, LANES), lambda b, i: (b, i, 0))],
        out_shape=[jax.ShapeDtypeStruct((B, T, D), F32),
                   jax.ShapeDtypeStruct((B, T, LANES), F32),
                   jax.ShapeDtypeStruct((B, T, LANES), jnp.int32)],
        compiler_params=_cparams(("arbitrary", "arbitrary")),
        name="moe_router",
    )(u, gamma.reshape(1, D), mods, mods, wr)


def _moe_plan(eid2):
    N = eid2.shape[0]
    M = 2 * N
    E = MOE_EXPERTS
    eid = eid2.reshape(-1)
    order = jnp.argsort(eid, stable=True).astype(jnp.int32)
    s_eid = eid[order]
    counts = jnp.zeros((E,), jnp.int32).at[eid].add(1)
    padded = (counts + MOE_ROWS - 1) // MOE_ROWS * MOE_ROWS
    start = jnp.cumsum(counts) - counts
    pend = jnp.cumsum(padded)
    pstart = pend - padded
    dest = (pstart[s_eid] + jnp.arange(M, dtype=jnp.int32) - start[s_eid]).astype(jnp.int32)
    P = (M + E * (MOE_ROWS - 1) + MOE_ROWS - 1) // MOE_ROWS * MOE_ROWS
    nb = P // MOE_ROWS
    slot_tok = jnp.zeros((P,), jnp.int32).at[dest].set(order // 2)
    pos = jnp.zeros((M,), jnp.int32).at[order].set(dest).reshape(N, 2)
    blk_e = jnp.minimum(jnp.searchsorted(pend, jnp.arange(nb, dtype=jnp.int32) * MOE_ROWS, side='right'),
                        E - 1).astype(jnp.int32)
    n_active = (pend[-1] // MOE_ROWS).astype(jnp.int32).reshape(1)
    return slot_tok.reshape(nb, MOE_ROWS), pos, blk_e, n_active


def _row_copy(src_hbm, idx_smem, dst_ref, sem, r):
    return pltpu.make_async_copy(src_hbm.at[pl.ds(idx_smem[r], 1), :], dst_ref.at[pl.ds(r, 1), :], sem)


def _gather_kernel(idx_hbm, src_hbm, o_ref, idx_smem, sem_idx, sem_rows):
    i = pl.program_id(0)
    R = o_ref.shape[0]
    cp = pltpu.make_async_copy(idx_hbm.at[i], idx_smem, sem_idx)
    cp.start()
    cp.wait()

    def issue(r, c):
        _row_copy(src_hbm, idx_smem, o_ref, sem_rows, r).start()
        return c

    lax.fori_loop(0, R, issue, 0)

    def drain(r, c):
        _row_copy(src_hbm, idx_smem, o_ref, sem_rows, r).wait()
        return c

    lax.fori_loop(0, R, drain, 0)


def _gather_rows(src, idx):
    nb, R = idx.shape
    D = src.shape[1]
    return pl.pallas_call(
        _gather_kernel,
        grid=(nb,),
        in_specs=[pl.BlockSpec(memory_space=pl.ANY), pl.BlockSpec(memory_space=pl.ANY)],
        out_specs=pl.BlockSpec((R, D), lambda i: (i, 0)),
        out_shape=jax.ShapeDtypeStruct((nb * R, D), src.dtype),
        scratch_shapes=[pltpu.SMEM((R,), jnp.int32), pltpu.SemaphoreType.DMA, pltpu.SemaphoreType.DMA],
        compiler_params=_cparams(("arbitrary",)),
        name="moe_gather",
    )(idx, src)


def _expert_kernel(blk_e_ref, nact_ref, x_ref, wg_ref, wu_ref, wd_ref, o_ref):
    i = pl.program_id(0)

    @pl.when(i < nact_ref[0])
    def _():
        xb = x_ref[...].astype(BF16)
        g = jnp.dot(xb, wg_ref[...], preferred_element_type=F32)
        u = jnp.dot(xb, wu_ref[...], preferred_element_type=F32)
        hmid = (g * jax.nn.sigmoid(g) * u).astype(BF16)
        o_ref[...] = jnp.dot(hmid, wd_ref[...], preferred_element_type=F32)

    @pl.when(i >= nact_ref[0])
    def _():
        o_ref[...] = jnp.zeros_like(o_ref)


def _expert_mlp(xs, blk_e, n_active, wg, wu, wd):
    P, D = xs.shape
    F = wg.shape[2]
    nb = P // MOE_ROWS
    grid_spec = pltpu.PrefetchScalarGridSpec(
        num_scalar_prefetch=2,
        grid=(nb,),
        in_specs=[pl.BlockSpec((MOE_ROWS, D), lambda i, be, na: (i, 0)),
                  pl.BlockSpec((None, D, F), lambda i, be, na: (be[i], 0, 0)),
                  pl.BlockSpec((None, D, F), lambda i, be, na: (be[i], 0, 0)),
                  pl.BlockSpec((None, F, D), lambda i, be, na: (be[i], 0, 0))],
        out_specs=pl.BlockSpec((MOE_ROWS, D), lambda i, be, na: (i, 0)),
    )
    return pl.pallas_call(
        _expert_kernel,
        grid_spec=grid_spec,
        out_shape=jax.ShapeDtypeStruct((P, D), F32),
        compiler_params=_cparams(("arbitrary",)),
        name="moe_experts",
    )(blk_e, n_active, xs, wg, wu, wd)


def _combine_kernel(p1_hbm, p2_hbm, yb_hbm, mf_ref, r_ref, g_ref, o_ref, i1_smem, i2_smem, g1_ref, g2_ref,
                    sem_idx, sem_rows, *, tiles_per_batch):
    b = pl.program_id(0)
    i = pl.program_id(1)
    t = b * tiles_per_batch + i
    R = o_ref.shape[0]
    c1 = pltpu.make_async_copy(p1_hbm.at[t], i1_smem, sem_idx)
    c2 = pltpu.make_async_copy(p2_hbm.at[t], i2_smem, sem_idx)
    c1.start()
    c2.start()
    c1.wait()
    c2.wait()

    def issue(r, c):
        _row_copy(yb_hbm, i1_smem, g1_ref, sem_rows, r).start()
        _row_copy(yb_hbm, i2_smem, g2_ref, sem_rows, r).start()
        return c

    lax.fori_loop(0, R, issue, 0)

    def drain(r, c):
        _row_copy(yb_hbm, i1_smem, g1_ref, sem_rows, r).wait()
        _row_copy(yb_hbm, i2_smem, g2_ref, sem_rows, r).wait()
        return c

    lax.fori_loop(0, R, drain, 0)
    mf = mf_ref[...]
    y = mf[:, 0:1] * g1_ref[...] + mf[:, 1:2] * g2_ref[...]
    o_ref[...] = r_ref[...] + g_ref[...] * y


def _moe_combine(yb, pos, mf, res, mods, k_gate, *, n_ctx, row0, n_rows):
    B, T, _ = mf.shape
    D = yb.shape[1]
    R = MOE_ROWS
    tpb = T // R
    r0 = row0 // R
    sel = _mod_sel((n_ctx - row0) // R if n_ctx > row0 else 0, B)
    p1 = pos[:, 0].reshape(B * tpb, R)
    p2 = pos[:, 1].reshape(B * tpb, R)
    return pl.pallas_call(
        functools.partial(_combine_kernel, tiles_per_batch=tpb),
        grid=(B, n_rows // R),
        in_specs=[pl.BlockSpec(memory_space=pl.ANY), pl.BlockSpec(memory_space=pl.ANY),
                  pl.BlockSpec(memory_space=pl.ANY),
                  pl.BlockSpec((None, R, LANES), lambda b, i: (b, i + r0, 0)),
                  pl.BlockSpec((None, R, D), lambda b, i: (b, i + r0, 0)),
                  pl.BlockSpec((None, None, 1, D), lambda b, i: (sel(b, i), k_gate, 0, 0))],
        out_specs=pl.BlockSpec((None, R, D), lambda b, i: (b, i, 0)),
        out_shape=jax.ShapeDtypeStruct((B, n_rows, D), F32),
        scratch_shapes=[pltpu.SMEM((R,), jnp.int32), pltpu.SMEM((R,), jnp.int32),
                        pltpu.VMEM((R, D), F32), pltpu.VMEM((R, D), F32),
                        pltpu.SemaphoreType.DMA, pltpu.SemaphoreType.DMA],
        compiler_params=_cparams(("arbitrary", "arbitrary")),
        name="moe_combine",
    )(p1, p2, yb, mf, res, mods)


def _hier_moe(u, gamma, mods, w_route_group, w_route_expert, w_gate, w_up, w_down, *, n_ctx, row0, n_rows):
    B, T, D = u.shape
    wr = jnp.zeros((D, LANES), F32)
    wr = wr.at[:, :MOE_GROUPS].set(w_route_group.astype(F32))
    wr = wr.at[:, MOE_GROUPS:MOE_GROUPS + MOE_EXPERTS].set(w_route_expert.astype(F32))
    h, mf, mi = _router(u, gamma, mods, 3, 4, wr, n_ctx=n_ctx, tm=MOE_ROWS)
    slot_tok, pos, blk_e, n_active = _moe_plan(mi[:, :, :2].reshape(B * T, 2))
    xs = _gather_rows(h.reshape(B * T, D), slot_tok)
    yb = _expert_mlp(xs, blk_e, n_active, w_gate.astype(BF16), w_up.astype(BF16), w_down.astype(BF16))
    return _moe_combine(yb, pos, mf, u, mods, 5, n_ctx=n_ctx, row0=row0, n_rows=n_rows)


def _ab_layer(u, mods, gamma, w_in, w_out, s5p, s5_d, s5_w_glu, gla_w_gate2, gla_b_gate, gla_norm_g, *, n_ctx):
    B, T, D = u.shape
    a_width = s5_w_glu.shape[0]
    n_in = w_in.shape[1]
    n_pad = (n_in + LANES - 1) // LANES * LANES
    w_in_b = jnp.pad(w_in, ((0, 0), (0, n_pad - n_in))).astype(BF16)
    px = _normmod_matmul(u, gamma, mods, 0, 1, w_in_b, n_ctx=n_ctx, tm=256, tn=n_pad // 3)
    kx = (T - n_ctx) // (S5_CHUNK * S5_SUPER)
    kp = (kx + 1 + 7) // 8 * 8
    G = a_width // S5_GROUP
    ops = _s5_operators(*s5p)
    dsk = jnp.tile(s5_d.astype(F32).reshape(G, 1, S5_GROUP), (1, S5_CHUNK, 1)).reshape(G, 1, S5_CHUNK * S5_GROUP)
    xg = _s5_to_groups(px[:, :, :a_width], n_ctx, kp)
    yg = _s5_scan(xg, ops, dsk, kx=kx, kp=kp)
    ya = _s5_glu(_s5_from_groups(yg, n_ctx, kx, kp), s5_w_glu.astype(BF16), tm=256)
    yb = _gla(px, gla_w_gate2, gla_b_gate, gla_norm_g, n_ctx=n_ctx, a_width=a_width)
    ab = jnp.concatenate([ya, yb], axis=-1)
    return _matmul_residual(ab, w_out.astype(BF16), u, mods, 2, n_ctx=n_ctx, row0=0, tm=256, tn=1024)


def _na_layer(u, mods, gamma, w_qkv, w_out, q_norm, k_norm, rpb, *, n_ctx):
    qkv = _normmod_matmul(u, gamma, mods, 0, 1, w_qkv.astype(BF16), n_ctx=n_ctx, tm=256, tn=2048)
    o = _na_attention(qkv, q_norm, k_norm, rpb, n_ctx=n_ctx)
    return _matmul_residual(o, w_out.astype(BF16), u, mods, 2, n_ctx=0, row0=n_ctx, tm=256, tn=1024)


def kernel(x, c, ctx, c_ctx, ada_w, ada_b, norm1_g, norm2_g, ab_w_in, ab_w_out, s5_lam_re, s5_lam_im, s5_log_dt,
           s5_b_re, s5_b_im, s5_c_re, s5_c_im, s5_d, s5_w_glu, gla_w_gate2, gla_b_gate, gla_norm_g, na_w_qkv,
           na_w_out, na_q_norm, na_k_norm, na_rpb, moe_w_route_group, moe_w_route_expert, moe_w_gate, moe_w_up,
           moe_w_down):
    B, seq, D = x.shape
    n_ctx = ctx.shape[1]
    depth = ada_w.shape[0]
    assert depth == 2 and n_ctx == S5_CHUNK * S5_SUPER and seq % (GRID_W * 4) == 0
    u = jnp.concatenate([ctx, x], axis=1)
    cvec = jnp.zeros((8, D), F32).at[:B].set(c).at[B].set(c_ctx)
    mods = _ada_mod(cvec, ada_w[0], ada_b[0])[:B + 1]
    s5p = (s5_lam_re[0], s5_lam_im[0], s5_log_dt[0], s5_b_re[0], s5_b_im[0], s5_c_re[0], s5_c_im[0])
    u = _ab_layer(u, mods, norm1_g[0], ab_w_in[0], ab_w_out[0], s5p, s5_d[0], s5_w_glu[0], gla_w_gate2[0],
                  gla_b_gate[0], gla_norm_g[0], n_ctx=n_ctx)
    u = _hier_moe(u, norm2_g[0], mods, moe_w_route_group[0], moe_w_route_expert[0], moe_w_gate[0], moe_w_up[0],
                  moe_w_down[0], n_ctx=n_ctx, row0=0, n_rows=n_ctx + seq)
    mods = _ada_mod(cvec, ada_w[1], ada_b[1])[:B + 1]
    xl = _na_layer(u, mods, norm1_g[1], na_w_qkv[0], na_w_out[0], na_q_norm[0], na_k_norm[0], na_rpb[0], n_ctx=n_ctx)
    return _hier_moe(xl, norm2_g[1], mods, moe_w_route_group[1], moe_w_route_expert[1], moe_w_gate[1], moe_w_up[1],
                     moe_w_down[1], n_ctx=0, row0=0, n_rows=seq)
```

```python
import functools
import math

import jax
import jax.numpy as jnp
from jax import lax
from jax.experimental import pallas as pl
from jax.experimental.pallas import tpu as pltpu

F32 = jnp.float32
BF16 = jnp.bfloat16
HIGHEST = lax.Precision.HIGHEST

V7X_VMEM_BYTES = 64 * 1024 * 1024
VMEM_LIMIT = V7X_VMEM_BYTES - 12 * 1024 * 1024
LANES = 128

EPS = 1e-6
NEG_INF = -1e30
GRID_W = 64
N_MOD = 6
S5_GROUP = 16
S5_STATE = 64
S5_CHUNK = 16
S5_SUPER = 16
GLA_HEADS = 4
GLA_RANK = 16
GLA_TAU = 16.0
GLA_CHUNK = 64
GLA_BLOCK = 256
NA_HEAD_DIM = 128
NA_KH = 8
NA_KW = 16
NA_QROWS = 4
NA_UROWS = NA_KH + NA_QROWS - 1
ROPE_THETA = 10000.0
MOE_GROUPS = 4
MOE_PER_GROUP = 8
MOE_EXPERTS = MOE_GROUPS * MOE_PER_GROUP
MOE_ROWS = 256


def _cparams(sem):
    return pltpu.CompilerParams(dimension_semantics=sem, vmem_limit_bytes=VMEM_LIMIT)


def _nt(a, b):
    return lax.dot_general(a, b, (((1,), (1,)), ((), ())), preferred_element_type=F32)


def _ada_kernel(c_ref, w_ref, b_ref, o_ref):
    c = c_ref[...]
    s = c * jax.nn.sigmoid(c)
    o_ref[...] = jnp.dot(s.astype(BF16), w_ref[...].astype(BF16), preferred_element_type=F32) + b_ref[...]


def _ada_mod(cvec, w, b):
    R, D = cvec.shape
    N = w.shape[1]
    tn = 1024
    out = pl.pallas_call(
        _ada_kernel,
        grid=(N // tn,),
        in_specs=[pl.BlockSpec((R, D), lambda j: (0, 0)),
                  pl.BlockSpec((D, tn), lambda j: (0, j)),
                  pl.BlockSpec((1, tn), lambda j: (0, j))],
        out_specs=pl.BlockSpec((R, tn), lambda j: (0, j)),
        out_shape=jax.ShapeDtypeStruct((R, N), F32),
        compiler_params=_cparams(("arbitrary",)),
        name="ada_mod",
    )(cvec, w, b.reshape(1, N))
    return out.reshape(R, N_MOD, 1, D)


def _mod_sel(nct, nb):
    if nct == 0:
        return lambda b, i: b
    return lambda b, i: jnp.where(i < nct, nb, b)


def _normmod(x, g, sh, sc):
    y = x * lax.rsqrt(jnp.mean(x * x, axis=-1, keepdims=True) + EPS) * g
    return y * (1.0 + sc) + sh


def _normmod_mm_kernel(u_ref, g_ref, sh_ref, sc_ref, w_ref, o_ref, *, tn):
    xn = _normmod(u_ref[...], g_ref[...], sh_ref[...], sc_ref[...]).astype(BF16)
    for n0 in range(0, o_ref.shape[-1], tn):
        o_ref[:, n0:n0 + tn] = jnp.dot(xn, w_ref[:, n0:n0 + tn], preferred_element_type=F32).astype(o_ref.dtype)


def _normmod_matmul(u, gamma, mods, k_shift, k_scale, w, *, n_ctx, tm, tn, out_dtype=F32):
    B, T, D = u.shape
    N = w.shape[1]
    sel = _mod_sel(n_ctx // tm, B)
    return pl.pallas_call(
        functools.partial(_normmod_mm_kernel, tn=tn),
        grid=(B, T // tm),
        in_specs=[pl.BlockSpec((None, tm, D), lambda b, i: (b, i, 0)),
                  pl.BlockSpec((1, D), lambda b, i: (0, 0)),
                  pl.BlockSpec((None, None, 1, D), lambda b, i: (sel(b, i), k_shift, 0, 0)),
                  pl.BlockSpec((None, None, 1, D), lambda b, i: (sel(b, i), k_scale, 0, 0)),
                  pl.BlockSpec((D, N), lambda b, i: (0, 0), pipeline_mode=pl.Buffered(1))],
        out_specs=pl.BlockSpec((None, tm, N), lambda b, i: (b, i, 0)),
        out_shape=jax.ShapeDtypeStruct((B, T, N), out_dtype),
        compiler_params=_cparams(("arbitrary", "arbitrary")),
        name="normmod_matmul",
    )(u, gamma.reshape(1, D), mods, mods, w)


def _mm_res_kernel(a_ref, w_ref, r_ref, g_ref, o_ref, *, tn):
    a = a_ref[...]
    for n0 in range(0, o_ref.shape[-1], tn):
        acc = jnp.dot(a, w_ref[:, n0:n0 + tn], preferred_element_type=F32)
        o_ref[:, n0:n0 + tn] = r_ref[:, n0:n0 + tn] + g_ref[:, n0:n0 + tn] * acc


def _matmul_residual(a, w, res, mods, k_gate, *, n_ctx, row0, tm, tn):
    B, Ta, K = a.shape
    N = w.shape[1]
    sel = _mod_sel(n_ctx // tm, B)
    r0 = row0 // tm
    return pl.pallas_call(
        functools.partial(_mm_res_kernel, tn=tn),
        grid=(B, Ta // tm),
        in_specs=[pl.BlockSpec((None, tm, K), lambda b, i: (b, i, 0)),
                  pl.BlockSpec((K, N), lambda b, i: (0, 0), pipeline_mode=pl.Buffered(1)),
                  pl.BlockSpec((None, tm, N), lambda b, i: (b, i + r0, 0)),
                  pl.BlockSpec((None, None, 1, N), lambda b, i: (sel(b, i), k_gate, 0, 0))],
        out_specs=pl.BlockSpec((None, tm, N), lambda b, i: (b, i, 0)),
        out_shape=jax.ShapeDtypeStruct((B, Ta, N), F32),
        compiler_params=_cparams(("arbitrary", "arbitrary")),
        name="matmul_residual",
    )(a, w, res, mods)


def _s5_operators(lam_re, lam_im, log_dt, b_re, b_im, c_re, c_im):
    T = S5_CHUNK
    lr = jnp.minimum(lam_re.astype(F32), -1e-4)
    li = lam_im.astype(F32)
    dt = jnp.exp(log_dt.astype(F32))[..., None]
    mag = jnp.exp(lr * dt)
    a_re = mag * jnp.cos(li * dt)
    a_im = mag * jnp.sin(li * dt)
    num_re = a_re - 1.0
    den = lr * lr + li * li
    f_re = (num_re * lr + a_im * li) / den
    f_im = (a_im * lr - num_re * li) / den
    bb_re = f_re[..., None] * b_re.astype(F32) - f_im[..., None] * b_im.astype(F32)
    bb_im = f_re[..., None] * b_im.astype(F32) + f_im[..., None] * b_re.astype(F32)
    cr, ci = c_re.astype(F32), c_im.astype(F32)

    def powers(step, n):
        m = (step * jnp.arange(n + 1, dtype=F32))[:, None, None, None]
        mg = jnp.exp(m * (lr * dt))
        return mg * jnp.cos(m * (li * dt)), mg * jnp.sin(m * (li * dt))

    pr, pi = powers(1, T)
    pr16, pi16 = powers(T, S5_SUPER)
    ca_re = cr[None] * pr[:T, :, :, None, :] - ci[None] * pi[:T, :, :, None, :]
    ca_im = cr[None] * pi[:T, :, :, None, :] + ci[None] * pr[:T, :, :, None, :]
    kk = (jnp.einsum('kdgip,dgpj->dkgij', ca_re, bb_re) - jnp.einsum('kdgip,dgpj->dkgij', ca_im, bb_im))
    G = lr.shape[1]
    I = S5_GROUP
    s_idx = jnp.arange(T)[:, None]
    t_idx = jnp.arange(T)[None, :]

    def toeplitz(kd, lag):
        m = jnp.where((lag >= 0)[:, :, None, None, None], kd[jnp.clip(lag, 0, T - 1)], 0.0)
        return jnp.transpose(m, (2, 0, 4, 1, 3)).reshape(G, T * I, T * I)

    mt = toeplitz(kk[0], t_idx - s_idx) + toeplitz(kk[1], s_idx - t_idx)

    def v_op(d, qr, qi):
        vr = qr[:, :, :, None] * bb_re[d][None] - qi[:, :, :, None] * bb_im[d][None]
        vi = qr[:, :, :, None] * bb_im[d][None] + qi[:, :, :, None] * bb_re[d][None]
        vr = jnp.transpose(vr, (1, 0, 3, 2)).reshape(G, T * I, S5_STATE)
        vi = jnp.transpose(vi, (1, 0, 3, 2)).reshape(G, T * I, S5_STATE)
        return jnp.concatenate([vr, vi, -vi, vr], axis=-1)

    vt = jnp.concatenate([v_op(0, pr[:T, 0][::-1], pi[:T, 0][::-1]), v_op(1, pr[:T, 1], pi[:T, 1])], axis=-1)

    def w_op(d, qr, qi):
        wr = cr[d][None] * qr[:, :, None, :] - ci[d][None] * qi[:, :, None, :]
        wi = cr[d][None] * qi[:, :, None, :] + ci[d][None] * qr[:, :, None, :]
        wr = jnp.transpose(wr, (1, 3, 0, 2)).reshape(G, S5_STATE, T * I)
        wi = jnp.transpose(wi, (1, 3, 0, 2)).reshape(G, S5_STATE, T * I)
        return jnp.concatenate([wr, -wi], axis=1)

    w = jnp.concatenate([w_op(0, pr[1:T + 1, 0], pi[1:T + 1, 0]),
                         w_op(1, pr[1:T + 1, 1][::-1], pi[1:T + 1, 1][::-1])], axis=1)

    def dup(z):
        return jnp.concatenate([z, z], -1)

    rows = []
    for d in range(2):
        rows += [dup(pr16[1, d]), dup(pi16[1, d]), dup(pr16[S5_SUPER, d]), dup(pi16[S5_SUPER, d])]
    dec = jnp.stack(rows, axis=1)
    prs = []
    for d in range(2):
        prs += [dup(jnp.transpose(pr16[:S5_SUPER, d], (1, 0, 2))), dup(jnp.transpose(pi16[:S5_SUPER, d], (1, 0, 2)))]
    pwt = jnp.concatenate(prs, axis=1)
    return mt.astype(BF16), vt.astype(BF16), w.astype(BF16), dec, pwt


def _s5_kernel(x_ref, mt_ref, vt_ref, w_ref, dec_ref, pw_ref, dsk_ref, y_ref, hin_ref, hs_ref, *, kx, kp):
    nj = S5_SUPER
    x = x_ref[...]
    xb = x.astype(BF16)
    y = jnp.dot(xb, mt_ref[...], preferred_element_type=F32) + x * dsk_ref[...]
    vv = jnp.dot(xb, vt_ref[...], preferred_element_type=F32)
    dec = dec_ref[...]
    pw = pw_ref[...]
    hs_ref[...] = jnp.zeros_like(hs_ref)
    for d in range(2):
        ar16, ai16, ar256, ai256 = (dec[4 * d + n:4 * d + n + 1] for n in range(4))

        def v(j, d=d):
            return vv[j * kp:(j + 1) * kp, 256 * d:256 * d + LANES]

        def jv(j, d=d):
            return vv[j * kp:(j + 1) * kp, 256 * d + LANES:256 * (d + 1)]

        order = list(range(nj)) if d == 0 else list(range(nj - 1, -1, -1))
        loc = [None] * nj
        l = jnp.zeros((kp, LANES), F32)
        g = jnp.zeros((kp, LANES), F32)
        for n, j in enumerate(order):
            if n > 0:
                p = order[n - 1]
                l, g = ar16 * l + ai16 * g + v(p), ar16 * g - ai16 * l + jv(p)
            loc[j] = l
        p = order[-1]
        e, je = ar16 * l + ai16 * g + v(p), ar16 * g - ai16 * l + jv(p)
        seq = [kx] + (list(range(kx)) if d == 0 else list(range(kx - 1, -1, -1)))
        cur = jnp.zeros((1, LANES), F32)
        jcur = jnp.zeros((1, LANES), F32)
        for k in seq:
            hs_ref[2 * d, k:k + 1, :] = cur
            hs_ref[2 * d + 1, k:k + 1, :] = jcur
            cur, jcur = (ar256 * cur + ai256 * jcur + e[k:k + 1], ar256 * jcur - ai256 * cur + je[k:k + 1])
        hs = hs_ref[2 * d]
        jhs = hs_ref[2 * d + 1]
        for n, j in enumerate(order):
            pr = pw[32 * d + n:32 * d + n + 1]
            pi = pw[32 * d + 16 + n:32 * d + 16 + n + 1]
            hin_ref[j * kp:(j + 1) * kp, LANES * d:LANES * (d + 1)] = loc[j] + pr * hs + pi * jhs
    y_ref[...] = y + jnp.dot(hin_ref[...].astype(BF16), w_ref[...], preferred_element_type=F32)


def _s5_scan(xg, ops, dsk, *, kx, kp):
    B, G, R, C = xg.shape
    mt, vt, w, dec, pw = ops
    return pl.pallas_call(
        functools.partial(_s5_kernel, kx=kx, kp=kp),
        grid=(G, B),
        in_specs=[pl.BlockSpec((None, None, R, C), lambda g, b: (b, g, 0, 0)),
                  pl.BlockSpec((None, C, C), lambda g, b: (g, 0, 0)),
                  pl.BlockSpec((None, C, 2 * C), lambda g, b: (g, 0, 0)),
                  pl.BlockSpec((None, C, C), lambda g, b: (g, 0, 0)),
                  pl.BlockSpec((None, 8, LANES), lambda g, b: (g, 0, 0)),
                  pl.BlockSpec((None, 64, LANES), lambda g, b: (g, 0, 0)),
                  pl.BlockSpec((None, 1, C), lambda g, b: (g, 0, 0))],
        out_specs=pl.BlockSpec((None, None, R, C), lambda g, b: (b, g, 0, 0)),
        out_shape=jax.ShapeDtypeStruct((B, G, R, C), F32),
        scratch_shapes=[pltpu.VMEM((R, C), F32), pltpu.VMEM((4, kp, LANES), F32)],
        compiler_params=_cparams(("arbitrary", "arbitrary")),
        name="s5_scan",
    )(xg, mt, vt, w, dec, pw, dsk)


def _s5_to_groups(u, n_ctx, kp):
    B, T, A = u.shape
    G = A // S5_GROUP
    kx = (T - n_ctx) // (S5_CHUNK * S5_SUPER)
    ux = u[:, n_ctx:].reshape(B, kx, S5_SUPER, S5_CHUNK, G, S5_GROUP)
    uc = u[:, :n_ctx].reshape(B, 1, S5_SUPER, S5_CHUNK, G, S5_GROUP)
    pad = jnp.zeros((B, kp - kx - 1, S5_SUPER, S5_CHUNK, G, S5_GROUP), u.dtype)
    z = jnp.concatenate([ux, uc, pad], axis=1)
    z = jnp.transpose(z, (0, 4, 2, 1, 3, 5))
    return z.reshape(B, G, S5_SUPER * kp, S5_CHUNK * S5_GROUP)


def _s5_from_groups(y, n_ctx, kx, kp):
    B, G = y.shape[:2]
    z = y.reshape(B, G, S5_SUPER, kp, S5_CHUNK, S5_GROUP)
    z = jnp.transpose(z, (0, 3, 2, 4, 1, 5))
    yx = z[:, :kx].reshape(B, kx * S5_SUPER * S5_CHUNK, G * S5_GROUP)
    yc = z[:, kx].reshape(B, n_ctx, G * S5_GROUP)
    return jnp.concatenate([yc, yx], axis=1)


def _s5_glu_kernel(y_ref, w_ref, o_ref):
    g = jax.nn.gelu(y_ref[...])
    z = jnp.dot(g.astype(BF16), w_ref[...], preferred_element_type=F32)
    o_ref[...] = (g * jax.nn.sigmoid(z)).astype(o_ref.dtype)


def _s5_glu(y, w, *, tm):
    B, T, A = y.shape
    return pl.pallas_call(
        _s5_glu_kernel,
        grid=(B, T // tm),
        in_specs=[pl.BlockSpec((None, tm, A), lambda b, i: (b, i, 0)),
                  pl.BlockSpec((A, A), lambda b, i: (0, 0))],
        out_specs=pl.BlockSpec((None, tm, A), lambda b, i: (b, i, 0)),
        out_shape=jax.ShapeDtypeStruct((B, T, A), BF16),
        compiler_params=_cparams(("arbitrary", "arbitrary")),
        name="s5_glu",
    )(y, w)


def _log_sigmoid(z):
    return jnp.minimum(z, 0.0) - jnp.log(1.0 + jnp.exp(-jnp.abs(z)))


def _gla_kernel(q_ref, k_ref, v_ref, r_ref, lr_ref, cos_ref, sin_ref, wg_ref, wgt_ref, bg_ref, bgt_ref, ng_ref,
                o_ref, s_ref, of_ref, *, nblk, dk):
    p = pl.program_id(2)
    s = pl.program_id(3)
    C = GLA_CHUNK
    nch = GLA_BLOCK // C
    blk = jnp.where(p == 0, s, jnp.where(s == 0, 0, nblk - s))

    @pl.when(s == 0)
    def _():
        s_ref[...] = jnp.zeros_like(s_ref)

    lane = lax.broadcasted_iota(jnp.int32, (C, dk), 1)
    lower = (lane & 63) < 32
    row = lax.broadcasted_iota(jnp.int32, (C, C), 0)
    col = lax.broadcasted_iota(jnp.int32, (C, C), 1)
    eye = (lax.broadcasted_iota(jnp.int32, (dk, dk), 0) == lax.broadcasted_iota(jnp.int32, (dk, dk), 1)).astype(BF16)
    wg = wg_ref[...]
    wgt = wgt_ref[...]
    bg = bg_ref[...]
    bgt = bgt_ref[...]

    def rope(z, cos, sin):
        return z * cos + jnp.where(lower, pltpu.roll(z, dk - 32, 1), pltpu.roll(z, 32, 1)) * sin

    def run(reverse):
        mask = (row <= col) if reverse else (row >= col)
        tri = mask.astype(F32)
        for n in range(nch):
            c0 = (nch - 1 - n if reverse else n) * C
            rows = pl.ds(c0, C)
            cos = cos_ref[rows, :]
            sin = sin_ref[rows, :]
            q = rope(q_ref[rows, :], cos, sin) * (dk ** -0.5)
            k = rope(k_ref[rows, :], cos, sin)
            vb = v_ref[rows, :].astype(BF16)
            lrb = lr_ref[rows, :].astype(BF16)
            la = _log_sigmoid(jnp.dot(lrb, wg, preferred_element_type=F32) + bg) * (1.0 / GLA_TAU)
            lat = _log_sigmoid(_nt(wgt, lrb) + bgt) * (1.0 / GLA_TAU)
            b = jnp.dot(tri, la, precision=HIGHEST, preferred_element_type=F32)
            btot = jnp.sum(la, axis=0, keepdims=True)
            dcol = jnp.exp(jnp.sum(lat, axis=1, keepdims=True))
            qi = (q * jnp.exp(b)).astype(BF16)
            ki = (k * jnp.exp(-b)).astype(BF16)
            ko = (k * jnp.exp(btot - b)).astype(BF16)
            att = jnp.where(mask, _nt(qi, ki), 0.0).astype(BF16)
            st = s_ref[...]
            o = jnp.dot(att, vb, preferred_element_type=F32) + jnp.dot(qi, st.astype(BF16), preferred_element_type=F32)
            kot = _nt(eye, ko).astype(BF16)
            s_ref[...] = st * dcol + jnp.dot(kot, vb, preferred_element_type=F32)
            tok = pl.ds(pl.multiple_of(blk * GLA_BLOCK, GLA_BLOCK) + c0, C)
            if not reverse:
                of_ref[tok, :] = o
            else:
                o = o + of_ref[tok, :]
                o = o * lax.rsqrt(jnp.mean(o * o, axis=-1, keepdims=True) + EPS) * ng_ref[...]
                r = r_ref[rows, :]
                o_ref[rows, :] = (o * (r * jax.nn.sigmoid(r))).astype(o_ref.dtype)

    @pl.when(p == 0)
    def _():
        run(False)

    @pl.when(p == 1)
    def _():
        run(True)


def _rope_tables(n_ctx, seq, dk):
    nf = dk // 4
    inv = ROPE_THETA ** (-jnp.arange(nf, dtype=F32) / nf)
    pos = jnp.arange(seq)
    ang_r = (pos // GRID_W).astype(F32)[:, None] * inv[None, :]
    ang_c = (pos % GRID_W).astype(F32)[:, None] * inv[None, :]
    cos = jnp.concatenate([jnp.cos(ang_r)] * 2 + [jnp.cos(ang_c)] * 2, axis=-1)
    sin = jnp.concatenate([-jnp.sin(ang_r), jnp.sin(ang_r), -jnp.sin(ang_c), jnp.sin(ang_c)], axis=-1)
    cos = jnp.concatenate([jnp.ones((n_ctx, dk), F32), cos], axis=0)
    sin = jnp.concatenate([jnp.zeros((n_ctx, dk), F32), sin], axis=0)
    return cos, sin


def _gla(px, w_gate2, b_gate, norm_g, *, n_ctx, a_width):
    B, T, _ = px.shape
    H = GLA_HEADS
    qk = w_gate2.shape[-1]
    dk = qk // H
    bw = norm_g.shape[-1]
    dv = bw // H
    nblk = T // GLA_BLOCK
    cos, sin = _rope_tables(n_ctx, T - n_ctx, dk)
    lr_col0 = a_width + 2 * qk + 2 * bw
    wg = jnp.zeros((2, H, LANES, dk), F32)
    for d in range(2):
        wg = wg.at[d, :, d * GLA_RANK:(d + 1) * GLA_RANK, :].set(
            jnp.transpose(w_gate2[d].reshape(GLA_RANK, H, dk), (1, 0, 2)))
    wgt = jnp.swapaxes(wg, 2, 3).astype(BF16)
    wg = wg.astype(BF16)
    bg = b_gate.astype(F32).reshape(2, H, 1, dk)
    bgt = jnp.broadcast_to(b_gate.astype(F32).reshape(2, H, dk, 1), (2, H, dk, GLA_CHUNK))
    ng = norm_g.astype(F32).reshape(H, 1, dv)

    def blk(p, s):
        return jnp.where(p == 0, s, jnp.where(s == 0, 0, nblk - s))

    qb, kb = a_width // dk, (a_width + qk) // dk
    vbk, rbk = (a_width + 2 * qk) // dv, (a_width + 2 * qk + bw) // dv
    lb = lr_col0 // LANES
    return pl.pallas_call(
        functools.partial(_gla_kernel, nblk=nblk, dk=dk),
        grid=(B, H, 2, nblk),
        in_specs=[pl.BlockSpec((None, GLA_BLOCK, dk), lambda b, h, p, s: (b, blk(p, s), qb + h)),
                  pl.BlockSpec((None, GLA_BLOCK, dk), lambda b, h, p, s: (b, blk(p, s), kb + h)),
                  pl.BlockSpec((None, GLA_BLOCK, dv), lambda b, h, p, s: (b, blk(p, s), vbk + h)),
                  pl.BlockSpec((None, GLA_BLOCK, dv), lambda b, h, p, s: (b, blk(p, s), rbk + h)),
                  pl.BlockSpec((None, GLA_BLOCK, LANES), lambda b, h, p, s: (b, blk(p, s), lb)),
                  pl.BlockSpec((GLA_BLOCK, dk), lambda b, h, p, s: (blk(p, s), 0)),
                  pl.BlockSpec((GLA_BLOCK, dk), lambda b, h, p, s: (blk(p, s), 0)),
                  pl.BlockSpec((None, None, LANES, dk), lambda b, h, p, s: (p, h, 0, 0)),
                  pl.BlockSpec((None, None, dk, LANES), lambda b, h, p, s: (p, h, 0, 0)),
                  pl.BlockSpec((None, None, 1, dk), lambda b, h, p, s: (p, h, 0, 0)),
                  pl.BlockSpec((None, None, dk, GLA_CHUNK), lambda b, h, p, s: (p, h, 0, 0)),
                  pl.BlockSpec((None, 1, dv), lambda b, h, p, s: (h, 0, 0))],
        out_specs=pl.BlockSpec((None, GLA_BLOCK, dv), lambda b, h, p, s: (b, jnp.where(p == 0, 0, blk(p, s)), h)),
        out_shape=jax.ShapeDtypeStruct((B, T, bw), BF16),
        scratch_shapes=[pltpu.VMEM((dk, dv), F32), pltpu.VMEM((T, dv), F32)],
        compiler_params=_cparams(("arbitrary", "arbitrary", "arbitrary", "arbitrary")),
        name="gla",
    )(px, px, px, px, px, cos, sin, wg, wgt, bg, bgt, ng)


def _na_kernel(q_ref, k_ref, v_ref, bias_ref, qn_ref, kn_ref, o_ref, kb_ref, vb_ref, *, n_ctx, rows):
    hd = NA_HEAD_DIM
    W = GRID_W
    T = k_ref.shape[0]
    step = n_ctx

    def prep(i, c):
        sl = pl.ds(pl.multiple_of(i * step, step), step)
        kk = k_ref[sl, :]
        kb_ref[sl, :] = (kk * lax.rsqrt(jnp.mean(kk * kk, axis=-1, keepdims=True) + EPS) * kn_ref[...]).astype(BF16)
        vb_ref[sl, :] = v_ref[sl, :].astype(BF16)
        return c

    lax.fori_loop(0, T // step, prep, 0)
    scale = hd ** -0.5
    ngrp = rows // NA_QROWS
    nq = NA_QROWS * W
    nk = NA_UROWS * W

    def group(g, c):
        r0 = g * NA_QROWS
        ustart = jnp.clip(r0 - NA_KH // 2, 0, rows - NA_UROWS)
        case = jnp.where(g == 0, 0, jnp.where(g == ngrp - 1, 2, 1))
        q = q_ref[pl.ds(pl.multiple_of(n_ctx + r0 * W, W), nq), :]
        q = (q * lax.rsqrt(jnp.mean(q * q, axis=-1, keepdims=True) + EPS) * qn_ref[...] * scale).astype(BF16)
        ws = pl.ds(pl.multiple_of(n_ctx + ustart * W, W), nk)
        s_lat = _nt(q, kb_ref[ws, :]) + bias_ref[case]
        s_ctx = _nt(q, kb_ref[0:n_ctx, :])
        m = jnp.maximum(jnp.max(s_lat, axis=-1, keepdims=True), jnp.max(s_ctx, axis=-1, keepdims=True))
        p_lat = jnp.exp(s_lat - m)
        p_ctx = jnp.exp(s_ctx - m)
        den = jnp.sum(p_lat, axis=-1, keepdims=True) + jnp.sum(p_ctx, axis=-1, keepdims=True)
        o = (jnp.dot(p_lat.astype(BF16), vb_ref[ws, :], preferred_element_type=F32)
             + jnp.dot(p_ctx.astype(BF16), vb_ref[0:n_ctx, :], preferred_element_type=F32))
        o_ref[pl.ds(pl.multiple_of(r0 * W, W), nq), :] = (o / den).astype(o_ref.dtype)
        return c

    lax.fori_loop(0, ngrp, group, 0)


def _na_bias(rpb, rows):
    H = rpb.shape[0]
    col = jnp.arange(GRID_W)
    cs = jnp.clip(col - NA_KW // 2, 0, GRID_W - NA_KW)
    col_ok = (col[None, :] >= cs[:, None]) & (col[None, :] < cs[:, None] + NA_KW)
    dc_idx = jnp.clip(col[None, :] - col[:, None] + NA_KW - 1, 0, 2 * NA_KW - 2)
    rpb_c = jnp.where(col_ok[None, None], rpb.astype(F32)[:, :, dc_idx], NEG_INF)
    ngrp = rows // NA_QROWS
    tables = []
    for g in (0, 1, ngrp - 1):
        r0 = g * NA_QROWS
        ustart = min(max(r0 - NA_KH // 2, 0), rows - NA_UROWS)
        qr = r0 + jnp.arange(NA_QROWS)
        start = jnp.clip(qr - NA_KH // 2, 0, rows - NA_KH)
        kr = ustart + jnp.arange(NA_UROWS)
        ok = (kr[None, :] >= start[:, None]) & (kr[None, :] < start[:, None] + NA_KH)
        idx = jnp.clip(kr[None, :] - qr[:, None] + NA_KH - 1, 0, 2 * NA_KH - 2)
        t = jnp.where(ok[None, :, :, None, None], rpb_c[:, idx], NEG_INF)
        tables.append(jnp.transpose(t, (0, 1, 3, 2, 4)).reshape(H, NA_QROWS * GRID_W, NA_UROWS * GRID_W))
    return jnp.stack(tables, axis=1)


def _na_attention(qkv, q_norm, k_norm, rpb, *, n_ctx):
    B, T, D3 = qkv.shape
    D = D3 // 3
    H = D // NA_HEAD_DIM
    seq = T - n_ctx
    rows = seq // GRID_W
    assert rows % NA_QROWS == 0 and rows >= NA_UROWS + NA_QROWS
    bias = _na_bias(rpb, rows)
    hd = NA_HEAD_DIM
    return pl.pallas_call(
        functools.partial(_na_kernel, n_ctx=n_ctx, rows=rows),
        grid=(B, H),
        in_specs=[pl.BlockSpec((None, T, hd), lambda b, h: (b, 0, h)),
                  pl.BlockSpec((None, T, hd), lambda b, h: (b, 0, H + h)),
                  pl.BlockSpec((None, T, hd), lambda b, h: (b, 0, 2 * H + h)),
                  pl.BlockSpec((None, 3, NA_QROWS * GRID_W, NA_UROWS * GRID_W), lambda b, h: (h, 0, 0, 0)),
                  pl.BlockSpec((1, hd), lambda b, h: (0, 0)),
                  pl.BlockSpec((1, hd), lambda b, h: (0, 0))],
        out_specs=pl.BlockSpec((None, seq, hd), lambda b, h: (b, 0, h)),
        out_shape=jax.ShapeDtypeStruct((B, seq, D), BF16),
        scratch_shapes=[pltpu.VMEM((T, hd), BF16), pltpu.VMEM((T, hd), BF16)],
        compiler_params=_cparams(("arbitrary", "arbitrary")),
        name="na_attention",
    )(qkv, qkv, qkv, bias, q_norm.astype(F32).reshape(1, hd), k_norm.astype(F32).reshape(1, hd))


def _router_kernel(u_ref, g_ref, sh_ref, sc_ref, wr_ref, h_ref, mf_ref, mi_ref):
    h = _normmod(u_ref[...], g_ref[...], sh_ref[...], sc_ref[...])
    h_ref[...] = h
    logits = jnp.dot(h, wr_ref[...], precision=HIGHEST, preferred_element_type=F32)
    lane = lax.broadcasted_iota(jnp.int32, logits.shape, 1).astype(F32)
    big = 1e9

    def first_max(vals):
        m = jnp.max(vals, axis=-1, keepdims=True)
        return m, jnp.min(jnp.where(vals == m, lane, big), axis=-1, keepdims=True)

    gl = jnp.where(lane < MOE_GROUPS, logits, -jnp.inf)
    gmax, gidx = first_max(gl)
    p_top = 1.0 / jnp.sum(jnp.exp(gl - gmax), axis=-1, keepdims=True)
    lo = MOE_GROUPS + MOE_PER_GROUP * gidx
    el = jnp.where((lane >= lo) & (lane < lo + MOE_PER_GROUP), logits, -jnp.inf)
    v1, i1 = first_max(el)
    v2, i2 = first_max(jnp.where(lane == i1, -jnp.inf, el))
    e2 = jnp.exp(v2 - v1)
    w1 = p_top / (1.0 + e2)
    w2 = p_top * e2 / (1.0 + e2)
    mf_ref[...] = jnp.where(lane == 0, w1, jnp.where(lane == 1, w2, 0.0))
    mi_ref[...] = jnp.where(lane == 0, i1 - MOE_GROUPS, jnp.where(lane == 1, i2 - MOE_GROUPS, 0.0)).astype(jnp.int32)


def _router(u, gamma, mods, k_shift, k_scale, wr, *, n_ctx, tm):
    B, T, D = u.shape
    sel = _mod_sel(n_ctx // tm, B)
    return pl.pallas_call(
        _router_kernel,
        grid=(B, T // tm),
        in_specs=[pl.BlockSpec((None, tm, D), lambda b, i: (b, i, 0)),
                  pl.BlockSpec((1, D), lambda b, i: (0, 0)),
                  pl.BlockSpec((None, None, 1, D), lambda b, i: (sel(b, i), k_shift, 0, 0)),
                  pl.BlockSpec((None, None, 1, D), lambda b, i: (sel(b, i), k_scale, 0, 0)),
                  pl.BlockSpec((D, LANES), lambda b, i: (0, 0))],
        out_specs=[pl.BlockSpec((None, tm, D), lambda b, i: (b, i, 0)),
                   pl.BlockSpec((None, tm, LANES), lambda b, i: (b, i, 0)),
                   pl.BlockSpec((None, tm, LANES), lambda b, i: (b, i, 0))],
        out_shape=[jax.ShapeDtypeStruct((B, T, D), F32),
                   jax.ShapeDtypeStruct((B, T, LANES), F32),
                   jax.ShapeDtypeStruct((B, T, LANES), jnp.int32)],
        compiler_params=_cparams(("arbitrary", "arbitrary")),
        name="moe_router",
    )(u, gamma.reshape(1, D), mods, mods, wr)


def _moe_plan(eid2):
    N = eid2.shape[0]
    M = 2 * N
    E = MOE_EXPERTS
    eid = eid2.reshape(-1)
    order = jnp.argsort(eid, stable=True).astype(jnp.int32)
    s_eid = eid[order]
    counts = jnp.zeros((E,), jnp.int32).at[eid].add(1)
    padded = (counts + MOE_ROWS - 1) // MOE_ROWS * MOE_ROWS
    start = jnp.cumsum(counts) - counts
    pend = jnp.cumsum(padded)
    pstart = pend - padded
    dest = (pstart[s_eid] + jnp.arange(M, dtype=jnp.int32) - start[s_eid]).astype(jnp.int32)
    P = (M + E * (MOE_ROWS - 1) + MOE_ROWS - 1) // MOE_ROWS * MOE_ROWS
    nb = P // MOE_ROWS
    slot_tok = jnp.zeros((P,), jnp.int32).at[dest].set(order // 2)
    pos = jnp.zeros((M,), jnp.int32).at[order].set(dest).reshape(N, 2)
    blk_e = jnp.minimum(jnp.searchsorted(pend, jnp.arange(nb, dtype=jnp.int32) * MOE_ROWS, side='right'),
                        E - 1).astype(jnp.int32)
    n_active = (pend[-1] // MOE_ROWS).astype(jnp.int32).reshape(1)
    return slot_tok.reshape(nb, MOE_ROWS), pos, blk_e, n_active


def _row_copy(src_hbm, idx_smem, dst_ref, sem, r):
    return pltpu.make_async_copy(src_hbm.at[pl.ds(idx_smem[r], 1), :], dst_ref.at[pl.ds(r, 1), :], sem)


def _expert_kernel(blk_e_ref, nact_ref, idx_hbm, h_hbm, wg_ref, wu_ref, wd_ref, o_ref, idx0, idx1, xbuf, sem_idx,
                   sem_rows):
    i = pl.program_id(0)
    nact = nact_ref[0]
    R = MOE_ROWS
    slot = lax.rem(i, 2)
    idx_bufs = (idx0, idx1)

    def idx_copy(blk, s):
        return pltpu.make_async_copy(idx_hbm.at[blk], idx_bufs[s], sem_idx.at[s])

    def issue_rows(s):
        def body(r, c):
            _row_copy(h_hbm, idx_bufs[s], xbuf.at[s], sem_rows.at[s], r).start()
            return c

        lax.fori_loop(0, R, body, 0, unroll=8)

    @pl.when(i == 0)
    def _():
        first = idx_copy(0, 0)
        first.start()
        first.wait()
        issue_rows(0)

        @pl.when(nact > 1)
        def _():
            idx_copy(1, 1).start()

    def prefetch(s):
        @pl.when(i + 1 < nact)
        def _():
            idx_copy(i + 1, 1 - s).wait()
            issue_rows(1 - s)

        @pl.when(i + 2 < nact)
        def _():
            idx_copy(i + 2, s).start()

    @pl.when(slot == 0)
    def _():
        prefetch(0)

    @pl.when(slot == 1)
    def _():
        prefetch(1)

    @pl.when(i < nact)
    def _():
        pltpu.make_async_copy(h_hbm.at[pl.ds(0, R), :], xbuf.at[slot], sem_rows.at[slot]).wait()
        xb = xbuf[slot].astype(BF16)
        g = jnp.dot(xb, wg_ref[...], preferred_element_type=F32)
        u = jnp.dot(xb, wu_ref[...], preferred_element_type=F32)
        hmid = (g * jax.nn.sigmoid(g) * u).astype(BF16)
        o_ref[...] = jnp.dot(hmid, wd_ref[...], preferred_element_type=F32)

    @pl.when(i >= nact)
    def _():
        o_ref[...] = jnp.zeros_like(o_ref)


def _expert_mlp(h, slot_tok, blk_e, n_active, wg, wu, wd):
    nb, R = slot_tok.shape
    D = h.shape[1]
    F = wg.shape[2]
    grid_spec = pltpu.PrefetchScalarGridSpec(
        num_scalar_prefetch=2,
        grid=(nb,),
        in_specs=[pl.BlockSpec(memory_space=pl.ANY),
                  pl.BlockSpec(memory_space=pl.ANY),
                  pl.BlockSpec((None, D, F), lambda i, be, na: (be[i], 0, 0)),
                  pl.BlockSpec((None, D, F), lambda i, be, na: (be[i], 0, 0)),
                  pl.BlockSpec((None, F, D), lambda i, be, na: (be[i], 0, 0))],
        out_specs=pl.BlockSpec((R, D), lambda i, be, na: (i, 0)),
        scratch_shapes=[pltpu.SMEM((R,), jnp.int32), pltpu.SMEM((R,), jnp.int32), pltpu.VMEM((2, R, D), F32),
                        pltpu.SemaphoreType.DMA((2,)), pltpu.SemaphoreType.DMA((2,))],
    )
    return pl.pallas_call(
        _expert_kernel,
        grid_spec=grid_spec,
        out_shape=jax.ShapeDtypeStruct((nb * R, D), F32),
        compiler_params=_cparams(("arbitrary",)),
        name="moe_experts",
    )(blk_e, n_active, slot_tok, h, wg, wu, wd)


def _combine_kernel(p1_hbm, p2_hbm, yb_hbm, mf_ref, r_ref, g_ref, o_ref, i1_smem, i2_smem, g1_ref, g2_ref,
                    sem_idx, sem_rows, *, tiles_per_batch):
    b = pl.program_id(0)
    i = pl.program_id(1)
    t = b * tiles_per_batch + i
    R = o_ref.shape[0]
    c1 = pltpu.make_async_copy(p1_hbm.at[t], i1_smem, sem_idx)
    c2 = pltpu.make_async_copy(p2_hbm.at[t], i2_smem, sem_idx)
    c1.start()
    c2.start()
    c1.wait()
    c2.wait()

    def issue(r, c):
        _row_copy(yb_hbm, i1_smem, g1_ref, sem_rows, r).start()
        _row_copy(yb_hbm, i2_smem, g2_ref, sem_rows, r).start()
        return c

    lax.fori_loop(0, R, issue, 0, unroll=8)
    pltpu.make_async_copy(yb_hbm.at[pl.ds(0, R), :], g1_ref, sem_rows).wait()
    pltpu.make_async_copy(yb_hbm.at[pl.ds(0, R), :], g2_ref, sem_rows).wait()
    mf = mf_ref[...]
    y = mf[:, 0:1] * g1_ref[...] + mf[:, 1:2] * g2_ref[...]
    o_ref[...] = r_ref[...] + g_ref[...] * y


def _moe_combine(yb, pos, mf, res, mods, k_gate, *, n_ctx, row0, n_rows):
    B, T, _ = mf.shape
    D = yb.shape[1]
    R = MOE_ROWS
    tpb = T // R
    r0 = row0 // R
    sel = _mod_sel((n_ctx - row0) // R if n_ctx > row0 else 0, B)
    p1 = pos[:, 0].reshape(B * tpb, R)
    p2 = pos[:, 1].reshape(B * tpb, R)
    return pl.pallas_call(
        functools.partial(_combine_kernel, tiles_per_batch=tpb),
        grid=(B, n_rows // R),
        in_specs=[pl.BlockSpec(memory_space=pl.ANY), pl.BlockSpec(memory_space=pl.ANY),
                  pl.BlockSpec(memory_space=pl.ANY),
                  pl.BlockSpec((None, R, LANES), lambda b, i: (b, i + r0, 0)),
                  pl.BlockSpec((None, R, D), lambda b, i: (b, i + r0, 0)),
                  pl.BlockSpec((None, None, 1, D), lambda b, i: (sel(b, i), k_gate, 0, 0))],
        out_specs=pl.BlockSpec((None, R, D), lambda b, i: (b, i, 0)),
        out_shape=jax.ShapeDtypeStruct((B, n_rows, D), F32),
        scratch_shapes=[pltpu.SMEM((R,), jnp.int32), pltpu.SMEM((R,), jnp.int32),
                        pltpu.VMEM((R, D), F32), pltpu.VMEM((R, D), F32),
                        pltpu.SemaphoreType.DMA, pltpu.SemaphoreType.DMA],
        compiler_params=_cparams(("arbitrary", "arbitrary")),
        name="moe_combine",
    )(p1, p2, yb, mf, res, mods)


def _hier_moe(u, gamma, mods, w_route_group, w_route_expert, w_gate, w_up, w_down, *, n_ctx, row0, n_rows):
    B, T, D = u.shape
    wr = jnp.zeros((D, LANES), F32)
    wr = wr.at[:, :MOE_GROUPS].set(w_route_group.astype(F32))
    wr = wr.at[:, MOE_GROUPS:MOE_GROUPS + MOE_EXPERTS].set(w_route_expert.astype(F32))
    h, mf, mi = _router(u, gamma, mods, 3, 4, wr, n_ctx=n_ctx, tm=MOE_ROWS)
    slot_tok, pos, blk_e, n_active = _moe_plan(mi[:, :, :2].reshape(B * T, 2))
    yb = _expert_mlp(h.reshape(B * T, D), slot_tok, blk_e, n_active, w_gate.astype(BF16), w_up.astype(BF16),
                     w_down.astype(BF16))
    return _moe_combine(yb, pos, mf, u, mods, 5, n_ctx=n_ctx, row0=row0, n_rows=n_rows)


def _ab_layer(u, mods, gamma, w_in, w_out, s5p, s5_d, s5_w_glu, gla_w_gate2, gla_b_gate, gla_norm_g, *, n_ctx):
    B, T, D = u.shape
    a_width = s5_w_glu.shape[0]
    n_in = w_in.shape[1]
    n_pad = (n_in + LANES - 1) // LANES * LANES
    w_in_b = jnp.pad(w_in, ((0, 0), (0, n_pad - n_in))).astype(BF16)
    px = _normmod_matmul(u, gamma, mods, 0, 1, w_in_b, n_ctx=n_ctx, tm=256, tn=n_pad // 3)
    kx = (T - n_ctx) // (S5_CHUNK * S5_SUPER)
    kp = (kx + 1 + 7) // 8 * 8
    G = a_width // S5_GROUP
    ops = _s5_operators(*s5p)
    dsk = jnp.tile(s5_d.astype(F32).reshape(G, 1, S5_GROUP), (1, S5_CHUNK, 1)).reshape(G, 1, S5_CHUNK * S5_GROUP)
    xg = _s5_to_groups(px[:, :, :a_width], n_ctx, kp)
    yg = _s5_scan(xg, ops, dsk, kx=kx, kp=kp)
    ya = _s5_glu(_s5_from_groups(yg, n_ctx, kx, kp), s5_w_glu.astype(BF16), tm=256)
    yb = _gla(px, gla_w_gate2, gla_b_gate, gla_norm_g, n_ctx=n_ctx, a_width=a_width)
    ab = jnp.concatenate([ya, yb], axis=-1)
    return _matmul_residual(ab, w_out.astype(BF16), u, mods, 2, n_ctx=n_ctx, row0=0, tm=256, tn=1024)


def _na_layer(u, mods, gamma, w_qkv, w_out, q_norm, k_norm, rpb, *, n_ctx):
    qkv = _normmod_matmul(u, gamma, mods, 0, 1, w_qkv.astype(BF16), n_ctx=n_ctx, tm=256, tn=2048)
    o = _na_attention(qkv, q_norm, k_norm, rpb, n_ctx=n_ctx)
    return _matmul_residual(o, w_out.astype(BF16), u, mods, 2, n_ctx=0, row0=n_ctx, tm=256, tn=1024)


def kernel(x, c, ctx, c_ctx, ada_w, ada_b, norm1_g, norm2_g, ab_w_in, ab_w_out, s5_lam_re, s5_lam_im, s5_log_dt,
           s5_b_re, s5_b_im, s5_c_re, s5_c_im, s5_d, s5_w_glu, gla_w_gate2, gla_b_gate, gla_norm_g, na_w_qkv,
           na_w_out, na_q_norm, na_k_norm, na_rpb, moe_w_route_group, moe_w_route_expert, moe_w_gate, moe_w_up,
           moe_w_down):
    B, seq, D = x.shape
    n_ctx = ctx.shape[1]
    depth = ada_w.shape[0]
    assert depth == 2 and n_ctx == S5_CHUNK * S5_SUPER and seq % (GRID_W * 4) == 0
    u = jnp.concatenate([ctx, x], axis=1)
    cvec = jnp.zeros((8, D), F32).at[:B].set(c).at[B].set(c_ctx)
    mods = _ada_mod(cvec, ada_w[0], ada_b[0])[:B + 1]
    s5p = (s5_lam_re[0], s5_lam_im[0], s5_log_dt[0], s5_b_re[0], s5_b_im[0], s5_c_re[0], s5_c_im[0])
    u = _ab_layer(u, mods, norm1_g[0], ab_w_in[0], ab_w_out[0], s5p, s5_d[0], s5_w_glu[0], gla_w_gate2[0],
                  gla_b_gate[0], gla_norm_g[0], n_ctx=n_ctx)
    u = _hier_moe(u, norm2_g[0], mods, moe_w_route_group[0], moe_w_route_expert[0], moe_w_gate[0], moe_w_up[0],
                  moe_w_down[0], n_ctx=n_ctx, row0=0, n_rows=n_ctx + seq)
    mods = _ada_mod(cvec, ada_w[1], ada_b[1])[:B + 1]
    xl = _na_layer(u, mods, norm1_g[1], na_w_qkv[0], na_w_out[0], na_q_norm[0], na_k_norm[0], na_rpb[0], n_ctx=n_ctx)
    return _hier_moe(xl, norm2_g[1], mods, moe_w_route_group[1], moe_w_route_expert[1], moe_w_gate[1], moe_w_up[1],
                     moe_w_down[1], n_ctx=0, row0=0, n_rows=seq)
```

```python
import functools
import math

import jax
import jax.numpy as jnp
from jax import lax
from jax.experimental import pallas as pl
from jax.experimental.pallas import tpu as pltpu

F32 = jnp.float32
BF16 = jnp.bfloat16
HIGHEST = lax.Precision.HIGHEST

V7X_VMEM_BYTES = 64 * 1024 * 1024
VMEM_LIMIT = V7X_VMEM_BYTES - 12 * 1024 * 1024
LANES = 128

EPS = 1e-6
NEG_INF = -1e30
GRID_W = 64
N_MOD = 6
S5_GROUP = 16
S5_STATE = 64
S5_CHUNK = 16
S5_SUPER = 16
GLA_HEADS = 4
GLA_RANK = 16
GLA_TAU = 16.0
GLA_CHUNK = 64
GLA_BLOCK = 256
NA_HEAD_DIM = 128
NA_KH = 8
NA_KW = 16
NA_QROWS = 4
NA_UROWS = NA_KH + NA_QROWS - 1
ROPE_THETA = 10000.0
MOE_GROUPS = 4
MOE_PER_GROUP = 8
MOE_EXPERTS = MOE_GROUPS * MOE_PER_GROUP
MOE_ROWS = 256


def _cparams(sem):
    return pltpu.CompilerParams(dimension_semantics=sem, vmem_limit_bytes=VMEM_LIMIT)


def _nt(a, b):
    return lax.dot_general(a, b, (((1,), (1,)), ((), ())), preferred_element_type=F32)


def _ada_kernel(c_ref, w_ref, b_ref, o_ref):
    c = c_ref[...]
    s = c * jax.nn.sigmoid(c)
    o_ref[...] = jnp.dot(s.astype(BF16), w_ref[...].astype(BF16), preferred_element_type=F32) + b_ref[...]


def _ada_mod(cvec, w, b):
    R, D = cvec.shape
    N = w.shape[1]
    tn = 1024
    out = pl.pallas_call(
        _ada_kernel,
        grid=(N // tn,),
        in_specs=[pl.BlockSpec((R, D), lambda j: (0, 0)),
                  pl.BlockSpec((D, tn), lambda j: (0, j)),
                  pl.BlockSpec((1, tn), lambda j: (0, j))],
        out_specs=pl.BlockSpec((R, tn), lambda j: (0, j)),
        out_shape=jax.ShapeDtypeStruct((R, N), F32),
        compiler_params=_cparams(("arbitrary",)),
        name="ada_mod",
    )(cvec, w, b.reshape(1, N))
    return out.reshape(R, N_MOD, 1, D)


def _mod_sel(nct, nb):
    if nct == 0:
        return lambda b, i: b
    return lambda b, i: jnp.where(i < nct, nb, b)


def _normmod(x, g, sh, sc):
    y = x * lax.rsqrt(jnp.mean(x * x, axis=-1, keepdims=True) + EPS) * g
    return y * (1.0 + sc) + sh


def _normmod_mm_kernel(u_ref, g_ref, sh_ref, sc_ref, w_ref, o_ref, *, tn):
    xn = _normmod(u_ref[...], g_ref[...], sh_ref[...], sc_ref[...]).astype(BF16)
    for n0 in range(0, o_ref.shape[-1], tn):
        o_ref[:, n0:n0 + tn] = jnp.dot(xn, w_ref[:, n0:n0 + tn], preferred_element_type=F32).astype(o_ref.dtype)


def _normmod_matmul(u, gamma, mods, k_shift, k_scale, w, *, n_ctx, tm, tn, out_dtype=F32):
    B, T, D = u.shape
    N = w.shape[1]
    sel = _mod_sel(n_ctx // tm, B)
    return pl.pallas_call(
        functools.partial(_normmod_mm_kernel, tn=tn),
        grid=(B, T // tm),
        in_specs=[pl.BlockSpec((None, tm, D), lambda b, i: (b, i, 0)),
                  pl.BlockSpec((1, D), lambda b, i: (0, 0)),
                  pl.BlockSpec((None, None, 1, D), lambda b, i: (sel(b, i), k_shift, 0, 0)),
                  pl.BlockSpec((None, None, 1, D), lambda b, i: (sel(b, i), k_scale, 0, 0)),
                  pl.BlockSpec((D, N), lambda b, i: (0, 0), pipeline_mode=pl.Buffered(1))],
        out_specs=pl.BlockSpec((None, tm, N), lambda b, i: (b, i, 0)),
        out_shape=jax.ShapeDtypeStruct((B, T, N), out_dtype),
        compiler_params=_cparams(("arbitrary", "arbitrary")),
        name="normmod_matmul",
    )(u, gamma.reshape(1, D), mods, mods, w)


def _mm_res_kernel(*refs, n_lhs, tn):
    a_refs, w_ref, r_ref, g_ref, o_ref = refs[:n_lhs], refs[n_lhs], refs[n_lhs + 1], refs[n_lhs + 2], refs[n_lhs + 3]
    lhs = [a[...] for a in a_refs]
    for n0 in range(0, o_ref.shape[-1], tn):
        acc, k0 = None, 0
        for a in lhs:
            part = jnp.dot(a, w_ref[k0:k0 + a.shape[1], n0:n0 + tn], preferred_element_type=F32)
            acc = part if acc is None else acc + part
            k0 += a.shape[1]
        o_ref[:, n0:n0 + tn] = r_ref[:, n0:n0 + tn] + g_ref[:, n0:n0 + tn] * acc


def _matmul_residual(lhs, w, res, mods, k_gate, *, n_ctx, row0, tm, tn):
    B, Ta, _ = lhs[0].shape
    K, N = w.shape
    sel = _mod_sel(n_ctx // tm, B)
    r0 = row0 // tm
    return pl.pallas_call(
        functools.partial(_mm_res_kernel, n_lhs=len(lhs), tn=tn),
        grid=(B, Ta // tm),
        in_specs=[pl.BlockSpec((None, tm, a.shape[2]), lambda b, i: (b, i, 0)) for a in lhs]
        + [pl.BlockSpec((K, N), lambda b, i: (0, 0), pipeline_mode=pl.Buffered(1)),
           pl.BlockSpec((None, tm, N), lambda b, i: (b, i + r0, 0)),
           pl.BlockSpec((None, None, 1, N), lambda b, i: (sel(b, i), k_gate, 0, 0))],
        out_specs=pl.BlockSpec((None, tm, N), lambda b, i: (b, i, 0)),
        out_shape=jax.ShapeDtypeStruct((B, Ta, N), F32),
        compiler_params=_cparams(("arbitrary", "arbitrary")),
        name="matmul_residual",
    )(*lhs, w, res, mods)


def _s5_operators(lam_re, lam_im, log_dt, b_re, b_im, c_re, c_im):
    T = S5_CHUNK
    lr = jnp.minimum(lam_re.astype(F32), -1e-4)
    li = lam_im.astype(F32)
    dt = jnp.exp(log_dt.astype(F32))[..., None]
    mag = jnp.exp(lr * dt)
    a_re = mag * jnp.cos(li * dt)
    a_im = mag * jnp.sin(li * dt)
    num_re = a_re - 1.0
    den = lr * lr + li * li
    f_re = (num_re * lr + a_im * li) / den
    f_im = (a_im * lr - num_re * li) / den
    bb_re = f_re[..., None] * b_re.astype(F32) - f_im[..., None] * b_im.astype(F32)
    bb_im = f_re[..., None] * b_im.astype(F32) + f_im[..., None] * b_re.astype(F32)
    cr, ci = c_re.astype(F32), c_im.astype(F32)

    def powers(step, n):
        m = (step * jnp.arange(n + 1, dtype=F32))[:, None, None, None]
        mg = jnp.exp(m * (lr * dt))
        return mg * jnp.cos(m * (li * dt)), mg * jnp.sin(m * (li * dt))

    pr, pi = powers(1, T)
    pr16, pi16 = powers(T, S5_SUPER)
    ca_re = cr[None] * pr[:T, :, :, None, :] - ci[None] * pi[:T, :, :, None, :]
    ca_im = cr[None] * pi[:T, :, :, None, :] + ci[None] * pr[:T, :, :, None, :]
    kk = (jnp.einsum('kdgip,dgpj->dkgij', ca_re, bb_re) - jnp.einsum('kdgip,dgpj->dkgij', ca_im, bb_im))
    G = lr.shape[1]
    I = S5_GROUP
    s_idx = jnp.arange(T)[:, None]
    t_idx = jnp.arange(T)[None, :]

    def toeplitz(kd, lag):
        m = jnp.where((lag >= 0)[:, :, None, None, None], kd[jnp.clip(lag, 0, T - 1)], 0.0)
        return jnp.transpose(m, (2, 0, 4, 1, 3)).reshape(G, T * I, T * I)

    mt = toeplitz(kk[0], t_idx - s_idx) + toeplitz(kk[1], s_idx - t_idx)

    def v_op(d, qr, qi):
        vr = qr[:, :, :, None] * bb_re[d][None] - qi[:, :, :, None] * bb_im[d][None]
        vi = qr[:, :, :, None] * bb_im[d][None] + qi[:, :, :, None] * bb_re[d][None]
        vr = jnp.transpose(vr, (1, 0, 3, 2)).reshape(G, T * I, S5_STATE)
        vi = jnp.transpose(vi, (1, 0, 3, 2)).reshape(G, T * I, S5_STATE)
        return jnp.concatenate([vr, vi, -vi, vr], axis=-1)

    vt = jnp.concatenate([v_op(0, pr[:T, 0][::-1], pi[:T, 0][::-1]), v_op(1, pr[:T, 1], pi[:T, 1])], axis=-1)

    def w_op(d, qr, qi):
        wr = cr[d][None] * qr[:, :, None, :] - ci[d][None] * qi[:, :, None, :]
        wi = cr[d][None] * qi[:, :, None, :] + ci[d][None] * qr[:, :, None, :]
        wr = jnp.transpose(wr, (1, 3, 0, 2)).reshape(G, S5_STATE, T * I)
        wi = jnp.transpose(wi, (1, 3, 0, 2)).reshape(G, S5_STATE, T * I)
        return jnp.concatenate([wr, -wi], axis=1)

    w = jnp.concatenate([w_op(0, pr[1:T + 1, 0], pi[1:T + 1, 0]),
                         w_op(1, pr[1:T + 1, 1][::-1], pi[1:T + 1, 1][::-1])], axis=1)

    def dup(z):
        return jnp.concatenate([z, z], -1)

    rows = []
    for d in range(2):
        rows += [dup(pr16[1, d]), dup(pi16[1, d]), dup(pr16[S5_SUPER, d]), dup(pi16[S5_SUPER, d])]
    dec = jnp.stack(rows, axis=1)
    prs = []
    for d in range(2):
        prs += [dup(jnp.transpose(pr16[:S5_SUPER, d], (1, 0, 2))), dup(jnp.transpose(pi16[:S5_SUPER, d], (1, 0, 2)))]
    pwt = jnp.concatenate(prs, axis=1)
    return mt.astype(BF16), vt.astype(BF16), w.astype(BF16), dec, pwt


S5_LANE_GROUPS = LANES // S5_GROUP


def _piece_transpose(arrs, lane):
    for d in (4, 2, 1):
        bit = ((lane // S5_GROUP) & d) != 0
        new = list(arrs)
        for r in range(S5_LANE_GROUPS):
            if r & d == 0:
                a, b = arrs[r], arrs[r + d]
                new[r] = jnp.where(bit, pltpu.roll(b, S5_GROUP * d, 1), a)
                new[r + d] = jnp.where(bit, b, pltpu.roll(a, LANES - S5_GROUP * d, 1))
        arrs = new
    return arrs


def _s5_kernel(u_ref, mt_ref, vt_ref, w_ref, dec_ref, pw_ref, dsk_ref, y_ref, xg_ref, yg_ref, vv_ref, hin_ref,
               hs_ref, *, nsc):
    nj = S5_SUPER
    nch = u_ref.shape[0] // S5_CHUNK
    ng = S5_LANE_GROUPS
    lane = lax.broadcasted_iota(jnp.int32, (nch, LANES), 1)
    for h in range(2):
        zs = [u_ref[pl.ds(ng * h + t, nch, stride=S5_CHUNK), :] for t in range(ng)]
        for g, a in enumerate(_piece_transpose(zs, lane)):
            xg_ref[g, :, LANES * h:LANES * (h + 1)] = a
    hs_ref[...] = jnp.zeros_like(hs_ref)

    def group(g, carry):
        x = xg_ref[g]
        xb = x.astype(BF16)
        yg_ref[g] = jnp.dot(xb, mt_ref[g], preferred_element_type=F32) + x * dsk_ref[g]
        vv = jnp.dot(xb, vt_ref[g], preferred_element_type=F32)
        for n in range(4):
            vv_ref[n] = vv[:, LANES * n:LANES * (n + 1)]
        dec = dec_ref[g]
        pw = pw_ref[g]
        for d in range(2):
            ar16, ai16, ar256, ai256 = (dec[4 * d + n:4 * d + n + 1] for n in range(4))

            def v(j, d=d):
                return vv_ref[2 * d, pl.ds(j, nsc, stride=nj), :]

            def jv(j, d=d):
                return vv_ref[2 * d + 1, pl.ds(j, nsc, stride=nj), :]

            order = list(range(nj)) if d == 0 else list(range(nj - 1, -1, -1))
            loc = [None] * nj
            l = jnp.zeros((nsc, LANES), F32)
            gg = jnp.zeros((nsc, LANES), F32)
            for n, j in enumerate(order):
                if n > 0:
                    p = order[n - 1]
                    l, gg = ar16 * l + ai16 * gg + v(p), ar16 * gg - ai16 * l + jv(p)
                loc[j] = l
            p = order[-1]
            e, je = ar16 * l + ai16 * gg + v(p), ar16 * gg - ai16 * l + jv(p)
            seq = list(range(nsc)) if d == 0 else [0] + list(range(nsc - 1, 0, -1))
            cur = jnp.zeros((1, LANES), F32)
            jcur = jnp.zeros((1, LANES), F32)
            for m in seq:
                hs_ref[2 * d, m:m + 1, :] = cur
                hs_ref[2 * d + 1, m:m + 1, :] = jcur
                cur, jcur = (ar256 * cur + ai256 * jcur + e[m:m + 1], ar256 * jcur - ai256 * cur + je[m:m + 1])
            hs = hs_ref[2 * d, 0:nsc, :]
            jhs = hs_ref[2 * d + 1, 0:nsc, :]
            for n, j in enumerate(order):
                pr = pw[32 * d + n:32 * d + n + 1]
                pi = pw[32 * d + 16 + n:32 * d + 16 + n + 1]
                hin_ref[d, pl.ds(j, nsc, stride=nj), :] = loc[j] + pr * hs + pi * jhs
        hin = jnp.concatenate([hin_ref[0], hin_ref[1]], axis=1).astype(BF16)
        yg_ref[g] += jnp.dot(hin, w_ref[g], preferred_element_type=F32)
        return carry

    lax.fori_loop(0, ng, group, 0)
    for h in range(2):
        ys = [yg_ref[g, :, LANES * h:LANES * (h + 1)] for g in range(ng)]
        for t, a in enumerate(_piece_transpose(ys, lane)):
            y_ref[pl.ds(ng * h + t, nch, stride=S5_CHUNK), :] = a


def _s5_scan(px, ops, dsk, *, a_width):
    B, T, _ = px.shape
    mt, vt, w, dec, pw = ops
    ng = S5_LANE_GROUPS
    C = S5_CHUNK * S5_GROUP
    nch = T // S5_CHUNK
    nsc = nch // S5_SUPER
    nsp = (nsc + 7) // 8 * 8
    return pl.pallas_call(
        functools.partial(_s5_kernel, nsc=nsc),
        grid=(a_width // LANES, B),
        in_specs=[pl.BlockSpec((None, T, LANES), lambda gb, b: (b, 0, gb)),
                  pl.BlockSpec((ng, C, C), lambda gb, b: (gb, 0, 0)),
                  pl.BlockSpec((ng, C, 2 * C), lambda gb, b: (gb, 0, 0)),
                  pl.BlockSpec((ng, C, C), lambda gb, b: (gb, 0, 0)),
                  pl.BlockSpec((ng, 8, LANES), lambda gb, b: (gb, 0, 0)),
                  pl.BlockSpec((ng, 64, LANES), lambda gb, b: (gb, 0, 0)),
                  pl.BlockSpec((ng, 1, C), lambda gb, b: (gb, 0, 0))],
        out_specs=pl.BlockSpec((None, T, LANES), lambda gb, b: (b, 0, gb)),
        out_shape=jax.ShapeDtypeStruct((B, T, a_width), F32),
        scratch_shapes=[pltpu.VMEM((ng, nch, C), F32), pltpu.VMEM((ng, nch, C), F32),
                        pltpu.VMEM((4, nch, LANES), F32), pltpu.VMEM((2, nch, LANES), F32),
                        pltpu.VMEM((4, nsp, LANES), F32)],
        compiler_params=_cparams(("arbitrary", "arbitrary")),
        name="s5_scan",
    )(px, mt, vt, w, dec, pw, dsk)


def _s5_glu_kernel(y_ref, w_ref, o_ref):
    g = jax.nn.gelu(y_ref[...])
    z = jnp.dot(g.astype(BF16), w_ref[...], preferred_element_type=F32)
    o_ref[...] = (g * jax.nn.sigmoid(z)).astype(o_ref.dtype)


def _s5_glu(y, w, *, tm):
    B, T, A = y.shape
    return pl.pallas_call(
        _s5_glu_kernel,
        grid=(B, T // tm),
        in_specs=[pl.BlockSpec((None, tm, A), lambda b, i: (b, i, 0)),
                  pl.BlockSpec((A, A), lambda b, i: (0, 0))],
        out_specs=pl.BlockSpec((None, tm, A), lambda b, i: (b, i, 0)),
        out_shape=jax.ShapeDtypeStruct((B, T, A), BF16),
        compiler_params=_cparams(("arbitrary", "arbitrary")),
        name="s5_glu",
    )(y, w)


def _log_sigmoid(z):
    return jnp.minimum(z, 0.0) - jnp.log(1.0 + jnp.exp(-jnp.abs(z)))


def _gla_kernel(q_ref, k_ref, v_ref, r_ref, lr_ref, cos_ref, sin_ref, wg_ref, wgt_ref, bg_ref, bgt_ref, ng_ref,
                o_ref, s_ref, of_ref, *, nblk, dk):
    p = pl.program_id(2)
    s = pl.program_id(3)
    C = GLA_CHUNK
    nch = GLA_BLOCK // C
    blk = jnp.where(p == 0, s, jnp.where(s == 0, 0, nblk - s))

    @pl.when(s == 0)
    def _():
        s_ref[...] = jnp.zeros_like(s_ref)

    NB = GLA_BLOCK
    lane = lax.broadcasted_iota(jnp.int32, (NB, dk), 1)
    lower = (lane & 63) < 32
    row = lax.broadcasted_iota(jnp.int32, (NB, NB), 0)
    col = lax.broadcasted_iota(jnp.int32, (NB, NB), 1)
    same_chunk = (row // C) == (col // C)
    tcol = lax.broadcasted_iota(jnp.int32, (dk, NB), 1) // C
    eye = (lax.broadcasted_iota(jnp.int32, (dk, dk), 0) == lax.broadcasted_iota(jnp.int32, (dk, dk), 1)).astype(BF16)

    def rope(z, cos, sin):
        return z * cos + jnp.where(lower, pltpu.roll(z, dk - 32, 1), pltpu.roll(z, 32, 1)) * sin

    def run(reverse):
        mask = same_chunk & ((row <= col) if reverse else (row >= col))
        tri = mask.astype(F32)
        cos = cos_ref[...]
        sin = sin_ref[...]
        q = rope(q_ref[...], cos, sin) * (dk ** -0.5)
        k = rope(k_ref[...], cos, sin)
        vb = v_ref[...].astype(BF16)
        lrb = lr_ref[...].astype(BF16)
        la = _log_sigmoid(jnp.dot(lrb, wg_ref[...], preferred_element_type=F32) + bg_ref[...]) * (1.0 / GLA_TAU)
        lat = _log_sigmoid(_nt(wgt_ref[...], lrb) + bgt_ref[...]) * (1.0 / GLA_TAU)
        b = jnp.dot(tri, la, precision=HIGHEST, preferred_element_type=F32)
        last = C - 1 if not reverse else 0
        btot = jnp.concatenate([jnp.broadcast_to(b[c * C + last:c * C + last + 1], (C, dk)) for c in range(nch)], 0)
        qi = (q * jnp.exp(b)).astype(BF16)
        ki = (k * jnp.exp(-b)).astype(BF16)
        ko = (k * jnp.exp(btot - b)).astype(BF16)
        att = jnp.where(mask, _nt(qi, ki), 0.0).astype(BF16)
        o = jnp.dot(att, vb, preferred_element_type=F32)
        upd, dcol = [], []
        for c in range(nch):
            kot = _nt(eye, ko[c * C:(c + 1) * C]).astype(BF16)
            upd.append(jnp.dot(kot, vb[c * C:(c + 1) * C], preferred_element_type=F32))
            dcol.append(jnp.exp(jnp.sum(jnp.where(tcol == c, lat, 0.0), axis=1, keepdims=True)))
        st = s_ref[...]
        o_state = [None] * nch
        for c in (range(nch - 1, -1, -1) if reverse else range(nch)):
            o_state[c] = jnp.dot(qi[c * C:(c + 1) * C], st.astype(BF16), preferred_element_type=F32)
            st = st * dcol[c] + upd[c]
        s_ref[...] = st
        o = o + jnp.concatenate(o_state, axis=0)
        tok = pl.ds(pl.multiple_of(blk * NB, NB), NB)
        if not reverse:
            of_ref[tok, :] = o
        else:
            o = o + of_ref[tok, :]
            o = o * lax.rsqrt(jnp.mean(o * o, axis=-1, keepdims=True) + EPS) * ng_ref[...]
            r = r_ref[...]
            o_ref[...] = (o * (r * jax.nn.sigmoid(r))).astype(o_ref.dtype)

    @pl.when(p == 0)
    def _():
        run(False)

    @pl.when(p == 1)
    def _():
        run(True)


def _rope_tables(n_ctx, seq, dk):
    nf = dk // 4
    inv = ROPE_THETA ** (-jnp.arange(nf, dtype=F32) / nf)
    pos = jnp.arange(seq)
    ang_r = (pos // GRID_W).astype(F32)[:, None] * inv[None, :]
    ang_c = (pos % GRID_W).astype(F32)[:, None] * inv[None, :]
    cos = jnp.concatenate([jnp.cos(ang_r)] * 2 + [jnp.cos(ang_c)] * 2, axis=-1)
    sin = jnp.concatenate([-jnp.sin(ang_r), jnp.sin(ang_r), -jnp.sin(ang_c), jnp.sin(ang_c)], axis=-1)
    cos = jnp.concatenate([jnp.ones((n_ctx, dk), F32), cos], axis=0)
    sin = jnp.concatenate([jnp.zeros((n_ctx, dk), F32), sin], axis=0)
    return cos, sin


def _gla(px, w_gate2, b_gate, norm_g, *, n_ctx, a_width):
    B, T, _ = px.shape
    H = GLA_HEADS
    qk = w_gate2.shape[-1]
    dk = qk // H
    bw = norm_g.shape[-1]
    dv = bw // H
    nblk = T // GLA_BLOCK
    cos, sin = _rope_tables(n_ctx, T - n_ctx, dk)
    lr_col0 = a_width + 2 * qk + 2 * bw
    wg = jnp.zeros((2, H, LANES, dk), F32)
    for d in range(2):
        wg = wg.at[d, :, d * GLA_RANK:(d + 1) * GLA_RANK, :].set(
            jnp.transpose(w_gate2[d].reshape(GLA_RANK, H, dk), (1, 0, 2)))
    wgt = jnp.swapaxes(wg, 2, 3).astype(BF16)
    wg = wg.astype(BF16)
    bg = b_gate.astype(F32).reshape(2, H, 1, dk)
    bgt = jnp.broadcast_to(b_gate.astype(F32).reshape(2, H, dk, 1), (2, H, dk, GLA_BLOCK))
    ng = norm_g.astype(F32).reshape(H, 1, dv)

    def blk(p, s):
        return jnp.where(p == 0, s, jnp.where(s == 0, 0, nblk - s))

    qb, kb = a_width // dk, (a_width + qk) // dk
    vbk, rbk = (a_width + 2 * qk) // dv, (a_width + 2 * qk + bw) // dv
    lb = lr_col0 // LANES
    return pl.pallas_call(
        functools.partial(_gla_kernel, nblk=nblk, dk=dk),
        grid=(B, H, 2, nblk),
        in_specs=[pl.BlockSpec((None, GLA_BLOCK, dk), lambda b, h, p, s: (b, blk(p, s), qb + h)),
                  pl.BlockSpec((None, GLA_BLOCK, dk), lambda b, h, p, s: (b, blk(p, s), kb + h)),
                  pl.BlockSpec((None, GLA_BLOCK, dv), lambda b, h, p, s: (b, blk(p, s), vbk + h)),
                  pl.BlockSpec((None, GLA_BLOCK, dv), lambda b, h, p, s: (b, blk(p, s), rbk + h)),
                  pl.BlockSpec((None, GLA_BLOCK, LANES), lambda b, h, p, s: (b, blk(p, s), lb)),
                  pl.BlockSpec((GLA_BLOCK, dk), lambda b, h, p, s: (blk(p, s), 0)),
                  pl.BlockSpec((GLA_BLOCK, dk), lambda b, h, p, s: (blk(p, s), 0)),
                  pl.BlockSpec((None, None, LANES, dk), lambda b, h, p, s: (p, h, 0, 0)),
                  pl.BlockSpec((None, None, dk, LANES), lambda b, h, p, s: (p, h, 0, 0)),
                  pl.BlockSpec((None, None, 1, dk), lambda b, h, p, s: (p, h, 0, 0)),
                  pl.BlockSpec((None, None, dk, GLA_BLOCK), lambda b, h, p, s: (p, h, 0, 0)),
                  pl.BlockSpec((None, 1, dv), lambda b, h, p, s: (h, 0, 0))],
        out_specs=pl.BlockSpec((None, GLA_BLOCK, dv), lambda b, h, p, s: (b, jnp.where(p == 0, 0, blk(p, s)), h)),
        out_shape=jax.ShapeDtypeStruct((B, T, bw), BF16),
        scratch_shapes=[pltpu.VMEM((dk, dv), F32), pltpu.VMEM((T, dv), F32)],
        compiler_params=_cparams(("arbitrary", "arbitrary", "arbitrary", "arbitrary")),
        name="gla",
    )(px, px, px, px, px, cos, sin, wg, wgt, bg, bgt, ng)


def _na_kernel(q_ref, k_ref, v_ref, bias_ref, qn_ref, kn_ref, o_ref, kb_ref, vb_ref, *, n_ctx, rows):
    hd = NA_HEAD_DIM
    W = GRID_W
    T = k_ref.shape[0]
    step = n_ctx

    def prep(i, c):
        sl = pl.ds(pl.multiple_of(i * step, step), step)
        kk = k_ref[sl, :]
        kb_ref[sl, :] = (kk * lax.rsqrt(jnp.mean(kk * kk, axis=-1, keepdims=True) + EPS) * kn_ref[...]).astype(BF16)
        vb_ref[sl, :] = v_ref[sl, :].astype(BF16)
        return c

    lax.fori_loop(0, T // step, prep, 0)
    scale = hd ** -0.5
    ngrp = rows // NA_QROWS
    nq = NA_QROWS * W
    nk = NA_UROWS * W

    def group(g, c):
        r0 = g * NA_QROWS
        ustart = jnp.clip(r0 - NA_KH // 2, 0, rows - NA_UROWS)
        case = jnp.where(g == 0, 0, jnp.where(g == ngrp - 1, 2, 1))
        q = q_ref[pl.ds(pl.multiple_of(n_ctx + r0 * W, W), nq), :]
        q = (q * lax.rsqrt(jnp.mean(q * q, axis=-1, keepdims=True) + EPS) * qn_ref[...] * scale).astype(BF16)
        ws = pl.ds(pl.multiple_of(n_ctx + ustart * W, W), nk)
        s_lat = _nt(q, kb_ref[ws, :]) + bias_ref[case]
        s_ctx = _nt(q, kb_ref[0:n_ctx, :])
        m = jnp.maximum(jnp.max(s_lat, axis=-1, keepdims=True), jnp.max(s_ctx, axis=-1, keepdims=True))
        p_lat = jnp.exp(s_lat - m)
        p_ctx = jnp.exp(s_ctx - m)
        den = jnp.sum(p_lat, axis=-1, keepdims=True) + jnp.sum(p_ctx, axis=-1, keepdims=True)
        o = (jnp.dot(p_lat.astype(BF16), vb_ref[ws, :], preferred_element_type=F32)
             + jnp.dot(p_ctx.astype(BF16), vb_ref[0:n_ctx, :], preferred_element_type=F32))
        o_ref[pl.ds(pl.multiple_of(r0 * W, W), nq), :] = (o / den).astype(o_ref.dtype)
        return c

    lax.fori_loop(0, ngrp, group, 0)


def _na_bias(rpb, rows):
    H = rpb.shape[0]
    col = jnp.arange(GRID_W)
    cs = jnp.clip(col - NA_KW // 2, 0, GRID_W - NA_KW)
    col_ok = (col[None, :] >= cs[:, None]) & (col[None, :] < cs[:, None] + NA_KW)
    dc_idx = jnp.clip(col[None, :] - col[:, None] + NA_KW - 1, 0, 2 * NA_KW - 2)
    rpb_c = jnp.where(col_ok[None, None], rpb.astype(F32)[:, :, dc_idx], NEG_INF)
    ngrp = rows // NA_QROWS
    tables = []
    for g in (0, 1, ngrp - 1):
        r0 = g * NA_QROWS
        ustart = min(max(r0 - NA_KH // 2, 0), rows - NA_UROWS)
        qr = r0 + jnp.arange(NA_QROWS)
        start = jnp.clip(qr - NA_KH // 2, 0, rows - NA_KH)
        kr = ustart + jnp.arange(NA_UROWS)
        ok = (kr[None, :] >= start[:, None]) & (kr[None, :] < start[:, None] + NA_KH)
        idx = jnp.clip(kr[None, :] - qr[:, None] + NA_KH - 1, 0, 2 * NA_KH - 2)
        t = jnp.where(ok[None, :, :, None, None], rpb_c[:, idx], NEG_INF)
        tables.append(jnp.transpose(t, (0, 1, 3, 2, 4)).reshape(H, NA_QROWS * GRID_W, NA_UROWS * GRID_W))
    return jnp.stack(tables, axis=1)


def _na_attention(qkv, q_norm, k_norm, rpb, *, n_ctx):
    B, T, D3 = qkv.shape
    D = D3 // 3
    H = D // NA_HEAD_DIM
    seq = T - n_ctx
    rows = seq // GRID_W
    assert rows % NA_QROWS == 0 and rows >= NA_UROWS + NA_QROWS
    bias = _na_bias(rpb, rows)
    hd = NA_HEAD_DIM
    return pl.pallas_call(
        functools.partial(_na_kernel, n_ctx=n_ctx, rows=rows),
        grid=(B, H),
        in_specs=[pl.BlockSpec((None, T, hd), lambda b, h: (b, 0, h)),
                  pl.BlockSpec((None, T, hd), lambda b, h: (b, 0, H + h)),
                  pl.BlockSpec((None, T, hd), lambda b, h: (b, 0, 2 * H + h)),
                  pl.BlockSpec((None, 3, NA_QROWS * GRID_W, NA_UROWS * GRID_W), lambda b, h: (h, 0, 0, 0)),
                  pl.BlockSpec((1, hd), lambda b, h: (0, 0)),
                  pl.BlockSpec((1, hd), lambda b, h: (0, 0))],
        out_specs=pl.BlockSpec((None, seq, hd), lambda b, h: (b, 0, h)),
        out_shape=jax.ShapeDtypeStruct((B, seq, D), BF16),
        scratch_shapes=[pltpu.VMEM((T, hd), BF16), pltpu.VMEM((T, hd), BF16)],
        compiler_params=_cparams(("arbitrary", "arbitrary")),
        name="na_attention",
    )(qkv, qkv, qkv, bias, q_norm.astype(F32).reshape(1, hd), k_norm.astype(F32).reshape(1, hd))


def _router_kernel(u_ref, g_ref, sh_ref, sc_ref, wr_ref, h_ref, mf_ref, mi_ref):
    h = _normmod(u_ref[...], g_ref[...], sh_ref[...], sc_ref[...])
    h_ref[...] = h
    logits = jnp.dot(h, wr_ref[...], precision=HIGHEST, preferred_element_type=F32)
    lane = lax.broadcasted_iota(jnp.int32, logits.shape, 1).astype(F32)
    big = 1e9

    def first_max(vals):
        m = jnp.max(vals, axis=-1, keepdims=True)
        return m, jnp.min(jnp.where(vals == m, lane, big), axis=-1, keepdims=True)

    gl = jnp.where(lane < MOE_GROUPS, logits, -jnp.inf)
    gmax, gidx = first_max(gl)
    p_top = 1.0 / jnp.sum(jnp.exp(gl - gmax), axis=-1, keepdims=True)
    lo = MOE_GROUPS + MOE_PER_GROUP * gidx
    el = jnp.where((lane >= lo) & (lane < lo + MOE_PER_GROUP), logits, -jnp.inf)
    v1, i1 = first_max(el)
    v2, i2 = first_max(jnp.where(lane == i1, -jnp.inf, el))
    e2 = jnp.exp(v2 - v1)
    w1 = p_top / (1.0 + e2)
    w2 = p_top * e2 / (1.0 + e2)
    mf_ref[...] = jnp.where(lane == 0, w1, jnp.where(lane == 1, w2, 0.0))
    mi_ref[...] = jnp.where(lane == 0, i1 - MOE_GROUPS, jnp.where(lane == 1, i2 - MOE_GROUPS, 0.0)).astype(jnp.int32)


def _router(u, gamma, mods, k_shift, k_scale, wr, *, n_ctx, tm):
    B, T, D = u.shape
    sel = _mod_sel(n_ctx // tm, B)
    return pl.pallas_call(
        _router_kernel,
        grid=(B, T // tm),
        in_specs=[pl.BlockSpec((None, tm, D), lambda b, i: (b, i, 0)),
                  pl.BlockSpec((1, D), lambda b, i: (0, 0)),
                  pl.BlockSpec((None, None, 1, D), lambda b, i: (sel(b, i), k_shift, 0, 0)),
                  pl.BlockSpec((None, None, 1, D), lambda b, i: (sel(b, i), k_scale, 0, 0)),
                  pl.BlockSpec((D, LANES), lambda b, i: (0, 0))],
        out_specs=[pl.BlockSpec((None, tm, D), lambda b, i: (b, i, 0)),
                   pl.BlockSpec((None, tm, LANES), lambda b, i: (b, i, 0)),
                   pl.BlockSpec((None, tm, LANES), lambda b, i: (b, i, 0))],
        out_shape=[jax.ShapeDtypeStruct((B, T, D), F32),
                   jax.ShapeDtypeStruct((B, T, LANES), F32),
                   jax.ShapeDtypeStruct((B, T, LANES), jnp.int32)],
        compiler_params=_cparams(("arbitrary", "arbitrary")),
        name="moe_router",
    )(u, gamma.reshape(1, D), mods, mods, wr)


def _moe_plan(eid2):
    N = eid2.shape[0]
    M = 2 * N
    E = MOE_EXPERTS
    eid = eid2.reshape(-1)
    onehot = (eid[:, None] == jnp.arange(E, dtype=jnp.int32)[None, :]).astype(jnp.int32)
    csum = jnp.cumsum(onehot, axis=0)
    rank = jnp.sum(onehot * csum, axis=1) - 1
    counts = csum[-1]
    padded = (counts + MOE_ROWS - 1) // MOE_ROWS * MOE_ROWS
    pend = jnp.cumsum(padded)
    pstart = pend - padded
    dest = (jnp.sum(onehot * pstart[None, :], axis=1) + rank).astype(jnp.int32)
    P = (M + E * (MOE_ROWS - 1) + MOE_ROWS - 1) // MOE_ROWS * MOE_ROWS
    nb = P // MOE_ROWS
    slot_tok = jnp.zeros((P,), jnp.int32).at[dest].set(jnp.arange(M, dtype=jnp.int32) // 2)
    pos = dest.reshape(N, 2)
    blk_start = jnp.arange(nb, dtype=jnp.int32) * MOE_ROWS
    blk_e = jnp.minimum(jnp.sum((pend[None, :] <= blk_start[:, None]).astype(jnp.int32), axis=1), E - 1)
    n_active = (pend[-1] // MOE_ROWS).astype(jnp.int32).reshape(1)
    return slot_tok.reshape(nb, MOE_ROWS), pos, blk_e.astype(jnp.int32), n_active


def _row_copy(src_hbm, idx_smem, dst_ref, sem, r):
    return pltpu.make_async_copy(src_hbm.at[pl.ds(idx_smem[r], 1), :], dst_ref.at[pl.ds(r, 1), :], sem)


def _expert_kernel(blk_e_ref, nact_ref, idx_hbm, h_hbm, wg_ref, wu_ref, wd_ref, o_ref, idx0, idx1, xbuf, sem_idx,
                   sem_rows):
    i = pl.program_id(0)
    nact = nact_ref[0]
    R = MOE_ROWS
    slot = lax.rem(i, 2)
    idx_bufs = (idx0, idx1)

    def idx_copy(blk, s):
        return pltpu.make_async_copy(idx_hbm.at[blk], idx_bufs[s], sem_idx.at[s])

    def issue_rows(s):
        def body(r, c):
            _row_copy(h_hbm, idx_bufs[s], xbuf.at[s], sem_rows.at[s], r).start()
            return c

        lax.fori_loop(0, R, body, 0, unroll=8)

    @pl.when(i == 0)
    def _():
        first = idx_copy(0, 0)
        first.start()
        first.wait()
        issue_rows(0)

        @pl.when(nact > 1)
        def _():
            idx_copy(1, 1).start()

    def prefetch(s):
        @pl.when(i + 1 < nact)
        def _():
            idx_copy(i + 1, 1 - s).wait()
            issue_rows(1 - s)

        @pl.when(i + 2 < nact)
        def _():
            idx_copy(i + 2, s).start()

    @pl.when(slot == 0)
    def _():
        prefetch(0)

    @pl.when(slot == 1)
    def _():
        prefetch(1)

    @pl.when(i < nact)
    def _():
        pltpu.make_async_copy(h_hbm.at[pl.ds(0, R), :], xbuf.at[slot], sem_rows.at[slot]).wait()
        xb = xbuf[slot].astype(BF16)
        g = jnp.dot(xb, wg_ref[...], preferred_element_type=F32)
        u = jnp.dot(xb, wu_ref[...], preferred_element_type=F32)
        hmid = (g * jax.nn.sigmoid(g) * u).astype(BF16)
        o_ref[...] = jnp.dot(hmid, wd_ref[...], preferred_element_type=F32)

    @pl.when(i >= nact)
    def _():
        o_ref[...] = jnp.zeros_like(o_ref)


def _expert_mlp(h, slot_tok, blk_e, n_active, wg, wu, wd):
    nb, R = slot_tok.shape
    D = h.shape[1]
    F = wg.shape[2]
    grid_spec = pltpu.PrefetchScalarGridSpec(
        num_scalar_prefetch=2,
        grid=(nb,),
        in_specs=[pl.BlockSpec(memory_space=pl.ANY),
                  pl.BlockSpec(memory_space=pl.ANY),
                  pl.BlockSpec((None, D, F), lambda i, be, na: (be[i], 0, 0)),
                  pl.BlockSpec((None, D, F), lambda i, be, na: (be[i], 0, 0)),
                  pl.BlockSpec((None, F, D), lambda i, be, na: (be[i], 0, 0))],
        out_specs=pl.BlockSpec((R, D), lambda i, be, na: (i, 0)),
        scratch_shapes=[pltpu.SMEM((R,), jnp.int32), pltpu.SMEM((R,), jnp.int32), pltpu.VMEM((2, R, D), F32),
                        pltpu.SemaphoreType.DMA((2,)), pltpu.SemaphoreType.DMA((2,))],
    )
    return pl.pallas_call(
        _expert_kernel,
        grid_spec=grid_spec,
        out_shape=jax.ShapeDtypeStruct((nb * R, D), F32),
        compiler_params=_cparams(("arbitrary",)),
        name="moe_experts",
    )(blk_e, n_active, slot_tok, h, wg, wu, wd)


def _combine_kernel(p1_hbm, p2_hbm, yb_hbm, mf_ref, r_ref, g_ref, o_ref, i1_smem, i2_smem, g1_ref, g2_ref,
                    sem_idx, sem_rows, *, tiles_per_batch):
    b = pl.program_id(0)
    i = pl.program_id(1)
    t = b * tiles_per_batch + i
    R = o_ref.shape[0]
    c1 = pltpu.make_async_copy(p1_hbm.at[t], i1_smem, sem_idx)
    c2 = pltpu.make_async_copy(p2_hbm.at[t], i2_smem, sem_idx)
    c1.start()
    c2.start()
    c1.wait()
    c2.wait()

    def issue(r, c):
        _row_copy(yb_hbm, i1_smem, g1_ref, sem_rows, r).start()
        _row_copy(yb_hbm, i2_smem, g2_ref, sem_rows, r).start()
        return c

    lax.fori_loop(0, R, issue, 0, unroll=8)
    pltpu.make_async_copy(yb_hbm.at[pl.ds(0, R), :], g1_ref, sem_rows).wait()
    pltpu.make_async_copy(yb_hbm.at[pl.ds(0, R), :], g2_ref, sem_rows).wait()
    mf = mf_ref[...]
    y = mf[:, 0:1] * g1_ref[...] + mf[:, 1:2] * g2_ref[...]
    o_ref[...] = r_ref[...] + g_ref[...] * y


def _moe_combine(yb, pos, mf, res, mods, k_gate, *, n_ctx, row0, n_rows):
    B, T, _ = mf.shape
    D = yb.shape[1]
    R = MOE_ROWS
    tpb = T // R
    r0 = row0 // R
    sel = _mod_sel((n_ctx - row0) // R if n_ctx > row0 else 0, B)
    p1 = pos[:, 0].reshape(B * tpb, R)
    p2 = pos[:, 1].reshape(B * tpb, R)
    return pl.pallas_call(
        functools.partial(_combine_kernel, tiles_per_batch=tpb),
        grid=(B, n_rows // R),
        in_specs=[pl.BlockSpec(memory_space=pl.ANY), pl.BlockSpec(memory_space=pl.ANY),
                  pl.BlockSpec(memory_space=pl.ANY),
                  pl.BlockSpec((None, R, LANES), lambda b, i: (b, i + r0, 0)),
                  pl.BlockSpec((None, R, D), lambda b, i: (b, i + r0, 0)),
                  pl.BlockSpec((None, None, 1, D), lambda b, i: (sel(b, i), k_gate, 0, 0))],
        out_specs=pl.BlockSpec((None, R, D), lambda b, i: (b, i, 0)),
        out_shape=jax.ShapeDtypeStruct((B, n_rows, D), F32),
        scratch_shapes=[pltpu.SMEM((R,), jnp.int32), pltpu.SMEM((R,), jnp.int32),
                        pltpu.VMEM((R, D), F32), pltpu.VMEM((R, D), F32),
                        pltpu.SemaphoreType.DMA, pltpu.SemaphoreType.DMA],
        compiler_params=_cparams(("arbitrary", "arbitrary")),
        name="moe_combine",
    )(p1, p2, yb, mf, res, mods)


def _hier_moe(u, gamma, mods, w_route_group, w_route_expert, w_gate, w_up, w_down, *, n_ctx, row0, n_rows):
    B, T, D = u.shape
    wr = jnp.zeros((D, LANES), F32)
    wr = wr.at[:, :MOE_GROUPS].set(w_route_group.astype(F32))
    wr = wr.at[:, MOE_GROUPS:MOE_GROUPS + MOE_EXPERTS].set(w_route_expert.astype(F32))
    h, mf, mi = _router(u, gamma, mods, 3, 4, wr, n_ctx=n_ctx, tm=MOE_ROWS)
    slot_tok, pos, blk_e, n_active = _moe_plan(mi[:, :, :2].reshape(B * T, 2))
    yb = _expert_mlp(h.reshape(B * T, D), slot_tok, blk_e, n_active, w_gate.astype(BF16), w_up.astype(BF16),
                     w_down.astype(BF16))
    return _moe_combine(yb, pos, mf, u, mods, 5, n_ctx=n_ctx, row0=row0, n_rows=n_rows)


def _ab_layer(u, mods, gamma, w_in, w_out, s5p, s5_d, s5_w_glu, gla_w_gate2, gla_b_gate, gla_norm_g, *, n_ctx):
    B, T, D = u.shape
    a_width = s5_w_glu.shape[0]
    n_in = w_in.shape[1]
    n_pad = (n_in + LANES - 1) // LANES * LANES
    w_in_b = jnp.pad(w_in, ((0, 0), (0, n_pad - n_in))).astype(BF16)
    px = _normmod_matmul(u, gamma, mods, 0, 1, w_in_b, n_ctx=n_ctx, tm=256, tn=n_pad // 3)
    G = a_width // S5_GROUP
    ops = _s5_operators(*s5p)
    dsk = jnp.tile(s5_d.astype(F32).reshape(G, 1, S5_GROUP), (1, S5_CHUNK, 1)).reshape(G, 1, S5_CHUNK * S5_GROUP)
    ya = _s5_glu(_s5_scan(px, ops, dsk, a_width=a_width), s5_w_glu.astype(BF16), tm=256)
    yb = _gla(px, gla_w_gate2, gla_b_gate, gla_norm_g, n_ctx=n_ctx, a_width=a_width)
    return _matmul_residual([ya, yb], w_out.astype(BF16), u, mods, 2, n_ctx=n_ctx, row0=0, tm=256, tn=1024)


def _na_layer(u, mods, gamma, w_qkv, w_out, q_norm, k_norm, rpb, *, n_ctx):
    qkv = _normmod_matmul(u, gamma, mods, 0, 1, w_qkv.astype(BF16), n_ctx=n_ctx, tm=256, tn=2048)
    o = _na_attention(qkv, q_norm, k_norm, rpb, n_ctx=n_ctx)
    return _matmul_residual([o], w_out.astype(BF16), u, mods, 2, n_ctx=0, row0=n_ctx, tm=256, tn=1024)


def kernel(x, c, ctx, c_ctx, ada_w, ada_b, norm1_g, norm2_g, ab_w_in, ab_w_out, s5_lam_re, s5_lam_im, s5_log_dt,
           s5_b_re, s5_b_im, s5_c_re, s5_c_im, s5_d, s5_w_glu, gla_w_gate2, gla_b_gate, gla_norm_g, na_w_qkv,
           na_w_out, na_q_norm, na_k_norm, na_rpb, moe_w_route_group, moe_w_route_expert, moe_w_gate, moe_w_up,
           moe_w_down):
    B, seq, D = x.shape
    n_ctx = ctx.shape[1]
    depth = ada_w.shape[0]
    assert depth == 2 and n_ctx == S5_CHUNK * S5_SUPER and seq % (GRID_W * 4) == 0
    u = jnp.concatenate([ctx, x], axis=1)
    cvec = jnp.zeros((8, D), F32).at[:B].set(c).at[B].set(c_ctx)
    mods = _ada_mod(cvec, ada_w[0], ada_b[0])[:B + 1]
    s5p = (s5_lam_re[0], s5_lam_im[0], s5_log_dt[0], s5_b_re[0], s5_b_im[0], s5_c_re[0], s5_c_im[0])
    u = _ab_layer(u, mods, norm1_g[0], ab_w_in[0], ab_w_out[0], s5p, s5_d[0], s5_w_glu[0], gla_w_gate2[0],
                  gla_b_gate[0], gla_norm_g[0], n_ctx=n_ctx)
    u = _hier_moe(u, norm2_g[0], mods, moe_w_route_group[0], moe_w_route_expert[0], moe_w_gate[0], moe_w_up[0],
                  moe_w_down[0], n_ctx=n_ctx, row0=0, n_rows=n_ctx + seq)
    mods = _ada_mod(cvec, ada_w[1], ada_b[1])[:B + 1]
    xl = _na_layer(u, mods, norm1_g[1], na_w_qkv[0], na_w_out[0], na_q_norm[0], na_k_norm[0], na_rpb[0], n_ctx=n_ctx)
    return _hier_moe(xl, norm2_g[1], mods, moe_w_route_group[1], moe_w_route_expert[1], moe_w_gate[1], moe_w_up[1],
                     moe_w_down[1], n_ctx=0, row0=0, n_rows=seq)
```

```python
import functools
import math

import jax
import jax.numpy as jnp
from jax import lax
from jax.experimental import pallas as pl
from jax.experimental.pallas import tpu as pltpu

F32 = jnp.float32
BF16 = jnp.bfloat16
HIGHEST = lax.Precision.HIGHEST

V7X_VMEM_BYTES = 64 * 1024 * 1024
VMEM_LIMIT = V7X_VMEM_BYTES - 12 * 1024 * 1024
LANES = 128

EPS = 1e-6
NEG_INF = -1e30
GRID_W = 64
N_MOD = 6
S5_GROUP = 16
S5_STATE = 64
S5_CHUNK = 16
S5_SUPER = 16
GLA_HEADS = 4
GLA_RANK = 16
GLA_TAU = 16.0
GLA_CHUNK = 64
GLA_BLOCK = 256
GLA_HEADS_PER_STEP = 4
NA_HEAD_DIM = 128
NA_KH = 8
NA_KW = 16
NA_QROWS = 4
NA_UROWS = NA_KH + NA_QROWS - 1
ROPE_THETA = 10000.0
MOE_GROUPS = 4
MOE_PER_GROUP = 8
MOE_EXPERTS = MOE_GROUPS * MOE_PER_GROUP
MOE_ROWS = 256


def _cparams(sem):
    return pltpu.CompilerParams(dimension_semantics=sem, vmem_limit_bytes=VMEM_LIMIT)


def _nt(a, b):
    return lax.dot_general(a, b, (((1,), (1,)), ((), ())), preferred_element_type=F32)


def _ada_kernel(c_ref, w_ref, b_ref, o_ref):
    c = c_ref[...]
    s = c * jax.nn.sigmoid(c)
    o_ref[...] = jnp.dot(s.astype(BF16), w_ref[...].astype(BF16), preferred_element_type=F32) + b_ref[...]


def _ada_mod(cvec, w, b):
    R, D = cvec.shape
    L, _, N = w.shape
    tn = 1024
    out = pl.pallas_call(
        _ada_kernel,
        grid=(L, N // tn),
        in_specs=[pl.BlockSpec((R, D), lambda l, j: (0, 0)),
                  pl.BlockSpec((None, D, tn), lambda l, j: (l, 0, j)),
                  pl.BlockSpec((None, 1, tn), lambda l, j: (l, 0, j))],
        out_specs=pl.BlockSpec((None, R, tn), lambda l, j: (l, 0, j)),
        out_shape=jax.ShapeDtypeStruct((L, R, N), F32),
        compiler_params=_cparams(("arbitrary", "arbitrary")),
        name="ada_mod",
    )(cvec, w, b.reshape(L, 1, N))
    return out.reshape(L, R, N_MOD, 1, D)


def _mod_sel(nct, nb):
    if nct == 0:
        return lambda b, i: b
    return lambda b, i: jnp.where(i < nct, nb, b)


def _normmod(x, g, sh, sc):
    y = x * lax.rsqrt(jnp.mean(x * x, axis=-1, keepdims=True) + EPS) * g
    return y * (1.0 + sc) + sh


def _normmod_mm_kernel(u_ref, g_ref, sh_ref, sc_ref, w_ref, o_ref, *, tn):
    xn = _normmod(u_ref[...], g_ref[...], sh_ref[...], sc_ref[...]).astype(BF16)
    for n0 in range(0, o_ref.shape[-1], tn):
        o_ref[:, n0:n0 + tn] = jnp.dot(xn, w_ref[:, n0:n0 + tn], preferred_element_type=F32).astype(o_ref.dtype)


def _normmod_matmul(u, gamma, mods, k_shift, k_scale, w, *, n_ctx, tm, tn, out_dtype=F32):
    B, T, D = u.shape
    N = w.shape[1]
    sel = _mod_sel(n_ctx // tm, B)
    return pl.pallas_call(
        functools.partial(_normmod_mm_kernel, tn=tn),
        grid=(B, T // tm),
        in_specs=[pl.BlockSpec((None, tm, D), lambda b, i: (b, i, 0)),
                  pl.BlockSpec((1, D), lambda b, i: (0, 0)),
                  pl.BlockSpec((None, None, 1, D), lambda b, i: (sel(b, i), k_shift, 0, 0)),
                  pl.BlockSpec((None, None, 1, D), lambda b, i: (sel(b, i), k_scale, 0, 0)),
                  pl.BlockSpec((D, N), lambda b, i: (0, 0), pipeline_mode=pl.Buffered(1))],
        out_specs=pl.BlockSpec((None, tm, N), lambda b, i: (b, i, 0)),
        out_shape=jax.ShapeDtypeStruct((B, T, N), out_dtype),
        compiler_params=_cparams(("arbitrary", "arbitrary")),
        name="normmod_matmul",
    )(u, gamma.reshape(1, D), mods, mods, w)


def _mm_res_kernel(*refs, n_lhs, tn):
    a_refs, w_ref, r_ref, g_ref, o_ref = refs[:n_lhs], refs[n_lhs], refs[n_lhs + 1], refs[n_lhs + 2], refs[n_lhs + 3]
    lhs = [a[...] for a in a_refs]
    for n0 in range(0, o_ref.shape[-1], tn):
        acc, k0 = None, 0
        for a in lhs:
            part = jnp.dot(a, w_ref[k0:k0 + a.shape[1], n0:n0 + tn], preferred_element_type=F32)
            acc = part if acc is None else acc + part
            k0 += a.shape[1]
        o_ref[:, n0:n0 + tn] = r_ref[:, n0:n0 + tn] + g_ref[:, n0:n0 + tn] * acc


def _matmul_residual(lhs, w, res, mods, k_gate, *, n_ctx, row0, tm, tn):
    B, Ta, _ = lhs[0].shape
    K, N = w.shape
    sel = _mod_sel(n_ctx // tm, B)
    r0 = row0 // tm
    return pl.pallas_call(
        functools.partial(_mm_res_kernel, n_lhs=len(lhs), tn=tn),
        grid=(B, Ta // tm),
        in_specs=[pl.BlockSpec((None, tm, a.shape[2]), lambda b, i: (b, i, 0)) for a in lhs]
        + [pl.BlockSpec((K, N), lambda b, i: (0, 0), pipeline_mode=pl.Buffered(1)),
           pl.BlockSpec((None, tm, N), lambda b, i: (b, i + r0, 0)),
           pl.BlockSpec((None, None, 1, N), lambda b, i: (sel(b, i), k_gate, 0, 0))],
        out_specs=pl.BlockSpec((None, tm, N), lambda b, i: (b, i, 0)),
        out_shape=jax.ShapeDtypeStruct((B, Ta, N), F32),
        compiler_params=_cparams(("arbitrary", "arbitrary")),
        name="matmul_residual",
    )(*lhs, w, res, mods)


def _s5_operators(lam_re, lam_im, log_dt, b_re, b_im, c_re, c_im):
    T = S5_CHUNK
    lr = jnp.minimum(lam_re.astype(F32), -1e-4)
    li = lam_im.astype(F32)
    dt = jnp.exp(log_dt.astype(F32))[..., None]
    mag = jnp.exp(lr * dt)
    a_re = mag * jnp.cos(li * dt)
    a_im = mag * jnp.sin(li * dt)
    num_re = a_re - 1.0
    den = lr * lr + li * li
    f_re = (num_re * lr + a_im * li) / den
    f_im = (a_im * lr - num_re * li) / den
    bb_re = f_re[..., None] * b_re.astype(F32) - f_im[..., None] * b_im.astype(F32)
    bb_im = f_re[..., None] * b_im.astype(F32) + f_im[..., None] * b_re.astype(F32)
    cr, ci = c_re.astype(F32), c_im.astype(F32)

    def powers(step, n):
        m = (step * jnp.arange(n + 1, dtype=F32))[:, None, None, None]
        mg = jnp.exp(m * (lr * dt))
        return mg * jnp.cos(m * (li * dt)), mg * jnp.sin(m * (li * dt))

    pr, pi = powers(1, T)
    pr16, pi16 = powers(T, S5_SUPER)
    ca_re = cr[None] * pr[:T, :, :, None, :] - ci[None] * pi[:T, :, :, None, :]
    ca_im = cr[None] * pi[:T, :, :, None, :] + ci[None] * pr[:T, :, :, None, :]
    kk = (jnp.einsum('kdgip,dgpj->dkgij', ca_re, bb_re) - jnp.einsum('kdgip,dgpj->dkgij', ca_im, bb_im))
    G = lr.shape[1]
    I = S5_GROUP
    s_idx = jnp.arange(T)[:, None]
    t_idx = jnp.arange(T)[None, :]

    def toeplitz(kd, lag):
        m = jnp.where((lag >= 0)[:, :, None, None, None], kd[jnp.clip(lag, 0, T - 1)], 0.0)
        return jnp.transpose(m, (2, 0, 4, 1, 3)).reshape(G, T * I, T * I)

    mt = toeplitz(kk[0], t_idx - s_idx) + toeplitz(kk[1], s_idx - t_idx)

    def v_op(d, qr, qi):
        vr = qr[:, :, :, None] * bb_re[d][None] - qi[:, :, :, None] * bb_im[d][None]
        vi = qr[:, :, :, None] * bb_im[d][None] + qi[:, :, :, None] * bb_re[d][None]
        vr = jnp.transpose(vr, (1, 0, 3, 2)).reshape(G, T * I, S5_STATE)
        vi = jnp.transpose(vi, (1, 0, 3, 2)).reshape(G, T * I, S5_STATE)
        return jnp.concatenate([vr, vi, -vi, vr], axis=-1)

    vt = jnp.concatenate([v_op(0, pr[:T, 0][::-1], pi[:T, 0][::-1]), v_op(1, pr[:T, 1], pi[:T, 1])], axis=-1)

    def w_op(d, qr, qi):
        wr = cr[d][None] * qr[:, :, None, :] - ci[d][None] * qi[:, :, None, :]
        wi = cr[d][None] * qi[:, :, None, :] + ci[d][None] * qr[:, :, None, :]
        wr = jnp.transpose(wr, (1, 3, 0, 2)).reshape(G, S5_STATE, T * I)
        wi = jnp.transpose(wi, (1, 3, 0, 2)).reshape(G, S5_STATE, T * I)
        return jnp.concatenate([wr, -wi], axis=1)

    w = jnp.concatenate([w_op(0, pr[1:T + 1, 0], pi[1:T + 1, 0]),
                         w_op(1, pr[1:T + 1, 1][::-1], pi[1:T + 1, 1][::-1])], axis=1)

    def dup(z):
        return jnp.concatenate([z, z], -1)

    rows = []
    for d in range(2):
        rows += [dup(pr16[1, d]), dup(pi16[1, d]), dup(pr16[S5_SUPER, d]), dup(pi16[S5_SUPER, d])]
    dec = jnp.stack(rows, axis=1)
    prs = []
    for d in range(2):
        prs += [dup(jnp.transpose(pr16[:S5_SUPER, d], (1, 0, 2))), dup(jnp.transpose(pi16[:S5_SUPER, d], (1, 0, 2)))]
    pwt = jnp.concatenate(prs, axis=1)
    return mt.astype(BF16), vt.astype(BF16), w.astype(BF16), dec, pwt


S5_LANE_GROUPS = LANES // S5_GROUP


def _piece_transpose(arrs, lane):
    for d in (4, 2, 1):
        bit = ((lane // S5_GROUP) & d) != 0
        new = list(arrs)
        for r in range(S5_LANE_GROUPS):
            if r & d == 0:
                a, b = arrs[r], arrs[r + d]
                new[r] = jnp.where(bit, pltpu.roll(b, S5_GROUP * d, 1), a)
                new[r + d] = jnp.where(bit, b, pltpu.roll(a, LANES - S5_GROUP * d, 1))
        arrs = new
    return arrs


def _s5_kernel(u_ref, mt_ref, vt_ref, w_ref, dec_ref, pw_ref, dsk_ref, y_ref, xg_ref, yg_ref, vv_ref, hin_ref,
               hs_ref, *, nsc):
    nj = S5_SUPER
    nch = u_ref.shape[0] // S5_CHUNK
    ng = S5_LANE_GROUPS
    lane = lax.broadcasted_iota(jnp.int32, (nch, LANES), 1)
    for h in range(2):
        zs = [u_ref[pl.ds(ng * h + t, nch, stride=S5_CHUNK), :] for t in range(ng)]
        for g, a in enumerate(_piece_transpose(zs, lane)):
            xg_ref[g, :, LANES * h:LANES * (h + 1)] = a
    hs_ref[...] = jnp.zeros_like(hs_ref)

    def group(g, carry):
        x = xg_ref[g]
        xb = x.astype(BF16)
        yg_ref[g] = jnp.dot(xb, mt_ref[g], preferred_element_type=F32) + x * dsk_ref[g]
        vv = jnp.dot(xb, vt_ref[g], preferred_element_type=F32)
        for n in range(4):
            vv_ref[n] = vv[:, LANES * n:LANES * (n + 1)]
        dec = dec_ref[g]
        pw = pw_ref[g]
        for d in range(2):
            ar16, ai16, ar256, ai256 = (dec[4 * d + n:4 * d + n + 1] for n in range(4))

            def v(j, d=d):
                return vv_ref[2 * d, pl.ds(j, nsc, stride=nj), :]

            def jv(j, d=d):
                return vv_ref[2 * d + 1, pl.ds(j, nsc, stride=nj), :]

            order = list(range(nj)) if d == 0 else list(range(nj - 1, -1, -1))
            loc = [None] * nj
            l = jnp.zeros((nsc, LANES), F32)
            gg = jnp.zeros((nsc, LANES), F32)
            for n, j in enumerate(order):
                if n > 0:
                    p = order[n - 1]
                    l, gg = ar16 * l + ai16 * gg + v(p), ar16 * gg - ai16 * l + jv(p)
                loc[j] = l
            p = order[-1]
            e, je = ar16 * l + ai16 * gg + v(p), ar16 * gg - ai16 * l + jv(p)
            seq = list(range(nsc)) if d == 0 else [0] + list(range(nsc - 1, 0, -1))
            cur = jnp.zeros((1, LANES), F32)
            jcur = jnp.zeros((1, LANES), F32)
            for m in seq:
                hs_ref[2 * d, m:m + 1, :] = cur
                hs_ref[2 * d + 1, m:m + 1, :] = jcur
                cur, jcur = (ar256 * cur + ai256 * jcur + e[m:m + 1], ar256 * jcur - ai256 * cur + je[m:m + 1])
            hs = hs_ref[2 * d, 0:nsc, :]
            jhs = hs_ref[2 * d + 1, 0:nsc, :]
            for n, j in enumerate(order):
                pr = pw[32 * d + n:32 * d + n + 1]
                pi = pw[32 * d + 16 + n:32 * d + 16 + n + 1]
                hin_ref[d, pl.ds(j, nsc, stride=nj), :] = loc[j] + pr * hs + pi * jhs
        hin = jnp.concatenate([hin_ref[0], hin_ref[1]], axis=1).astype(BF16)
        yg_ref[g] += jnp.dot(hin, w_ref[g], preferred_element_type=F32)
        return carry

    lax.fori_loop(0, ng, group, 0)
    for h in range(2):
        ys = [yg_ref[g, :, LANES * h:LANES * (h + 1)] for g in range(ng)]
        for t, a in enumerate(_piece_transpose(ys, lane)):
            y_ref[pl.ds(ng * h + t, nch, stride=S5_CHUNK), :] = a


def _s5_scan(px, ops, dsk, *, a_width):
    B, T, _ = px.shape
    mt, vt, w, dec, pw = ops
    ng = S5_LANE_GROUPS
    C = S5_CHUNK * S5_GROUP
    nch = T // S5_CHUNK
    nsc = nch // S5_SUPER
    nsp = (nsc + 7) // 8 * 8
    return pl.pallas_call(
        functools.partial(_s5_kernel, nsc=nsc),
        grid=(a_width // LANES, B),
        in_specs=[pl.BlockSpec((None, T, LANES), lambda gb, b: (b, 0, gb)),
                  pl.BlockSpec((ng, C, C), lambda gb, b: (gb, 0, 0)),
                  pl.BlockSpec((ng, C, 2 * C), lambda gb, b: (gb, 0, 0)),
                  pl.BlockSpec((ng, C, C), lambda gb, b: (gb, 0, 0)),
                  pl.BlockSpec((ng, 8, LANES), lambda gb, b: (gb, 0, 0)),
                  pl.BlockSpec((ng, 64, LANES), lambda gb, b: (gb, 0, 0)),
                  pl.BlockSpec((ng, 1, C), lambda gb, b: (gb, 0, 0))],
        out_specs=pl.BlockSpec((None, T, LANES), lambda gb, b: (b, 0, gb)),
        out_shape=jax.ShapeDtypeStruct((B, T, a_width), F32),
        scratch_shapes=[pltpu.VMEM((ng, nch, C), F32), pltpu.VMEM((ng, nch, C), F32),
                        pltpu.VMEM((4, nch, LANES), F32), pltpu.VMEM((2, nch, LANES), F32),
                        pltpu.VMEM((4, nsp, LANES), F32)],
        compiler_params=_cparams(("arbitrary", "arbitrary")),
        name="s5_scan",
    )(px, mt, vt, w, dec, pw, dsk)


def _s5_glu_kernel(y_ref, w_ref, o_ref):
    g = jax.nn.gelu(y_ref[...])
    z = jnp.dot(g.astype(BF16), w_ref[...], preferred_element_type=F32)
    o_ref[...] = (g * jax.nn.sigmoid(z)).astype(o_ref.dtype)


def _s5_glu(y, w, *, tm):
    B, T, A = y.shape
    return pl.pallas_call(
        _s5_glu_kernel,
        grid=(B, T // tm),
        in_specs=[pl.BlockSpec((None, tm, A), lambda b, i: (b, i, 0)),
                  pl.BlockSpec((A, A), lambda b, i: (0, 0))],
        out_specs=pl.BlockSpec((None, tm, A), lambda b, i: (b, i, 0)),
        out_shape=jax.ShapeDtypeStruct((B, T, A), BF16),
        compiler_params=_cparams(("arbitrary", "arbitrary")),
        name="s5_glu",
    )(y, w)


def _log_sigmoid(z):
    return jnp.minimum(z, 0.0) - jnp.log(1.0 + jnp.exp(-jnp.abs(z)))


def _gla_kernel(q_ref, k_ref, v_ref, r_ref, lr_ref, cos_ref, sin_ref, wg_ref, bg_ref, ng_ref, o_ref, s_ref, of_ref,
                *, nblk, dk, dv, hp):
    p = pl.program_id(2)
    s = pl.program_id(3)
    C = GLA_CHUNK
    nch = GLA_BLOCK // C
    blk = jnp.where(p == 0, s, jnp.where(s == 0, 0, nblk - s))

    @pl.when(s == 0)
    def _():
        s_ref[...] = jnp.zeros_like(s_ref)

    NB = GLA_BLOCK
    lane = lax.broadcasted_iota(jnp.int32, (NB, dk), 1)
    lower = (lane & 63) < 32
    rpos = lax.broadcasted_iota(jnp.int32, (NB, dk), 0) & (C - 1)
    row = lax.broadcasted_iota(jnp.int32, (NB, NB), 0)
    col = lax.broadcasted_iota(jnp.int32, (NB, NB), 1)
    same_chunk = (row // C) == (col // C)
    eye = (lax.broadcasted_iota(jnp.int32, (dk, dk), 0) == lax.broadcasted_iota(jnp.int32, (dk, dk), 1)).astype(BF16)

    def rope(z, cos, sin):
        return z * cos + jnp.where(lower, pltpu.roll(z, dk - 32, 1), pltpu.roll(z, 32, 1)) * sin

    def run(reverse):
        for hd in range(hp):
            head(reverse, hd)

    def head(reverse, hd):
        mask = same_chunk & ((row <= col) if reverse else (row >= col))
        cos = cos_ref[...]
        sin = sin_ref[...]
        kcols = slice(hd * dk, (hd + 1) * dk)
        vcols = slice(hd * dv, (hd + 1) * dv)
        q = rope(q_ref[:, kcols], cos, sin) * (dk ** -0.5)
        k = rope(k_ref[:, kcols], cos, sin)
        vb = v_ref[:, vcols].astype(BF16)
        lrb = lr_ref[...].astype(BF16)
        la = _log_sigmoid(jnp.dot(lrb, wg_ref[hd], preferred_element_type=F32) + bg_ref[hd]) * (1.0 / GLA_TAU)
        b = la
        sh = 1
        while sh < C:
            if reverse:
                b = b + jnp.where(rpos < C - sh, pltpu.roll(b, NB - sh, 0), 0.0)
            else:
                b = b + jnp.where(rpos >= sh, pltpu.roll(b, sh, 0), 0.0)
            sh *= 2
        last = C - 1 if not reverse else 0
        tots = [b[c * C + last:c * C + last + 1] for c in range(nch)]
        btot = jnp.concatenate([jnp.broadcast_to(t, (C, dk)) for t in tots], 0)
        dmat = jnp.exp(jnp.concatenate(tots + [jnp.zeros((8 - nch, dk), F32)], 0).T)
        qi = (q * jnp.exp(b)).astype(BF16)
        ki = (k * jnp.exp(-b)).astype(BF16)
        ko = (k * jnp.exp(btot - b)).astype(BF16)
        att = jnp.where(mask, _nt(qi, ki), 0.0).astype(BF16)
        o = jnp.dot(att, vb, preferred_element_type=F32)
        upd, dcol = [], []
        for c in range(nch):
            kot = _nt(eye, ko[c * C:(c + 1) * C]).astype(BF16)
            upd.append(jnp.dot(kot, vb[c * C:(c + 1) * C], preferred_element_type=F32))
            dcol.append(dmat[:, c:c + 1])
        st = s_ref[hd]
        o_state = [None] * nch
        for c in (range(nch - 1, -1, -1) if reverse else range(nch)):
            o_state[c] = jnp.dot(qi[c * C:(c + 1) * C], st.astype(BF16), preferred_element_type=F32)
            st = st * dcol[c] + upd[c]
        s_ref[hd] = st
        o = o + jnp.concatenate(o_state, axis=0)
        tok = pl.ds(pl.multiple_of(blk * NB, NB), NB)
        if not reverse:
            of_ref[tok, vcols] = o
        else:
            o = o + of_ref[tok, vcols]
            o = o * lax.rsqrt(jnp.mean(o * o, axis=-1, keepdims=True) + EPS) * ng_ref[hd]
            r = r_ref[:, vcols]
            o_ref[:, vcols] = (o * (r * jax.nn.sigmoid(r))).astype(o_ref.dtype)

    @pl.when(p == 0)
    def _():
        run(False)

    @pl.when(p == 1)
    def _():
        run(True)


def _rope_tables(n_ctx, seq, dk):
    nf = dk // 4
    inv = ROPE_THETA ** (-jnp.arange(nf, dtype=F32) / nf)
    pos = jnp.arange(seq)
    ang_r = (pos // GRID_W).astype(F32)[:, None] * inv[None, :]
    ang_c = (pos % GRID_W).astype(F32)[:, None] * inv[None, :]
    cos = jnp.concatenate([jnp.cos(ang_r)] * 2 + [jnp.cos(ang_c)] * 2, axis=-1)
    sin = jnp.concatenate([-jnp.sin(ang_r), jnp.sin(ang_r), -jnp.sin(ang_c), jnp.sin(ang_c)], axis=-1)
    cos = jnp.concatenate([jnp.ones((n_ctx, dk), F32), cos], axis=0)
    sin = jnp.concatenate([jnp.zeros((n_ctx, dk), F32), sin], axis=0)
    return cos, sin


def _gla(px, w_gate2, b_gate, norm_g, *, n_ctx, a_width):
    B, T, _ = px.shape
    H = GLA_HEADS
    qk = w_gate2.shape[-1]
    dk = qk // H
    bw = norm_g.shape[-1]
    dv = bw // H
    nblk = T // GLA_BLOCK
    cos, sin = _rope_tables(n_ctx, T - n_ctx, dk)
    lr_col0 = a_width + 2 * qk + 2 * bw
    wg = jnp.zeros((2, H, LANES, dk), F32)
    for d in range(2):
        wg = wg.at[d, :, d * GLA_RANK:(d + 1) * GLA_RANK, :].set(
            jnp.transpose(w_gate2[d].reshape(GLA_RANK, H, dk), (1, 0, 2)))
    wg = wg.astype(BF16)
    bg = b_gate.astype(F32).reshape(2, H, 1, dk)
    ng = norm_g.astype(F32).reshape(H, 1, dv)
    hp = GLA_HEADS_PER_STEP

    def blk(p, s):
        return jnp.where(p == 0, s, jnp.where(s == 0, 0, nblk - s))

    kw, vw = hp * dk, hp * dv
    qb, kb = a_width // kw, (a_width + qk) // kw
    vbk, rbk = (a_width + 2 * qk) // vw, (a_width + 2 * qk + bw) // vw
    lb = lr_col0 // LANES
    return pl.pallas_call(
        functools.partial(_gla_kernel, nblk=nblk, dk=dk, dv=dv, hp=hp),
        grid=(B, H // hp, 2, nblk),
        in_specs=[pl.BlockSpec((None, GLA_BLOCK, kw), lambda b, g, p, s: (b, blk(p, s), qb + g)),
                  pl.BlockSpec((None, GLA_BLOCK, kw), lambda b, g, p, s: (b, blk(p, s), kb + g)),
                  pl.BlockSpec((None, GLA_BLOCK, vw), lambda b, g, p, s: (b, blk(p, s), vbk + g)),
                  pl.BlockSpec((None, GLA_BLOCK, vw), lambda b, g, p, s: (b, blk(p, s), rbk + g)),
                  pl.BlockSpec((None, GLA_BLOCK, LANES), lambda b, g, p, s: (b, blk(p, s), lb)),
                  pl.BlockSpec((GLA_BLOCK, dk), lambda b, g, p, s: (blk(p, s), 0)),
                  pl.BlockSpec((GLA_BLOCK, dk), lambda b, g, p, s: (blk(p, s), 0)),
                  pl.BlockSpec((None, hp, LANES, dk), lambda b, g, p, s: (p, g, 0, 0)),
                  pl.BlockSpec((None, hp, 1, dk), lambda b, g, p, s: (p, g, 0, 0)),
                  pl.BlockSpec((hp, 1, dv), lambda b, g, p, s: (g, 0, 0))],
        out_specs=pl.BlockSpec((None, GLA_BLOCK, vw), lambda b, g, p, s: (b, jnp.where(p == 0, 0, blk(p, s)), g)),
        out_shape=jax.ShapeDtypeStruct((B, T, bw), BF16),
        scratch_shapes=[pltpu.VMEM((hp, dk, dv), F32), pltpu.VMEM((T, vw), F32)],
        compiler_params=_cparams(("arbitrary", "arbitrary", "arbitrary", "arbitrary")),
        name="gla",
    )(px, px, px, px, px, cos, sin, wg, bg, ng)


def _na_kernel(q_ref, k_ref, v_ref, bias_ref, qn_ref, kn_ref, o_ref, kb_ref, vb_ref, *, n_ctx, rows):
    hd = NA_HEAD_DIM
    W = GRID_W
    T = k_ref.shape[0]
    step = n_ctx

    def prep(i, c):
        sl = pl.ds(pl.multiple_of(i * step, step), step)
        kk = k_ref[sl, :]
        kb_ref[sl, :] = (kk * lax.rsqrt(jnp.mean(kk * kk, axis=-1, keepdims=True) + EPS) * kn_ref[...]).astype(BF16)
        vb_ref[sl, :] = v_ref[sl, :].astype(BF16)
        return c

    lax.fori_loop(0, T // step, prep, 0)
    scale = hd ** -0.5
    ngrp = rows // NA_QROWS
    nq = NA_QROWS * W
    nk = NA_UROWS * W

    def group(g, c):
        r0 = g * NA_QROWS
        ustart = jnp.clip(r0 - NA_KH // 2, 0, rows - NA_UROWS)
        case = jnp.where(g == 0, 0, jnp.where(g == ngrp - 1, 2, 1))
        q = q_ref[pl.ds(pl.multiple_of(n_ctx + r0 * W, W), nq), :]
        q = (q * lax.rsqrt(jnp.mean(q * q, axis=-1, keepdims=True) + EPS) * qn_ref[...] * scale).astype(BF16)
        ws = pl.ds(pl.multiple_of(n_ctx + ustart * W, W), nk)
        s_lat = _nt(q, kb_ref[ws, :]) + bias_ref[case]
        s_ctx = _nt(q, kb_ref[0:n_ctx, :])
        m = jnp.maximum(jnp.max(s_lat, axis=-1, keepdims=True), jnp.max(s_ctx, axis=-1, keepdims=True))
        p_lat = jnp.exp(s_lat - m)
        p_ctx = jnp.exp(s_ctx - m)
        den = jnp.sum(p_lat, axis=-1, keepdims=True) + jnp.sum(p_ctx, axis=-1, keepdims=True)
        o = (jnp.dot(p_lat.astype(BF16), vb_ref[ws, :], preferred_element_type=F32)
             + jnp.dot(p_ctx.astype(BF16), vb_ref[0:n_ctx, :], preferred_element_type=F32))
        o_ref[pl.ds(pl.multiple_of(r0 * W, W), nq), :] = (o / den).astype(o_ref.dtype)
        return c

    lax.fori_loop(0, ngrp, group, 0)


def _na_bias(rpb, rows):
    H = rpb.shape[0]
    col = jnp.arange(GRID_W)
    cs = jnp.clip(col - NA_KW // 2, 0, GRID_W - NA_KW)
    col_ok = (col[None, :] >= cs[:, None]) & (col[None, :] < cs[:, None] + NA_KW)
    dc_idx = jnp.clip(col[None, :] - col[:, None] + NA_KW - 1, 0, 2 * NA_KW - 2)
    rpb_c = jnp.where(col_ok[None, None], rpb.astype(F32)[:, :, dc_idx], NEG_INF)
    ngrp = rows // NA_QROWS
    tables = []
    for g in (0, 1, ngrp - 1):
        r0 = g * NA_QROWS
        ustart = min(max(r0 - NA_KH // 2, 0), rows - NA_UROWS)
        qr = r0 + jnp.arange(NA_QROWS)
        start = jnp.clip(qr - NA_KH // 2, 0, rows - NA_KH)
        kr = ustart + jnp.arange(NA_UROWS)
        ok = (kr[None, :] >= start[:, None]) & (kr[None, :] < start[:, None] + NA_KH)
        idx = jnp.clip(kr[None, :] - qr[:, None] + NA_KH - 1, 0, 2 * NA_KH - 2)
        t = jnp.where(ok[None, :, :, None, None], rpb_c[:, idx], NEG_INF)
        tables.append(jnp.transpose(t, (0, 1, 3, 2, 4)).reshape(H, NA_QROWS * GRID_W, NA_UROWS * GRID_W))
    return jnp.stack(tables, axis=1)


def _na_attention(qkv, q_norm, k_norm, rpb, *, n_ctx):
    B, T, D3 = qkv.shape
    D = D3 // 3
    H = D // NA_HEAD_DIM
    seq = T - n_ctx
    rows = seq // GRID_W
    assert rows % NA_QROWS == 0 and rows >= NA_UROWS + NA_QROWS
    bias = _na_bias(rpb, rows)
    hd = NA_HEAD_DIM
    return pl.pallas_call(
        functools.partial(_na_kernel, n_ctx=n_ctx, rows=rows),
        grid=(B, H),
        in_specs=[pl.BlockSpec((None, T, hd), lambda b, h: (b, 0, h)),
                  pl.BlockSpec((None, T, hd), lambda b, h: (b, 0, H + h)),
                  pl.BlockSpec((None, T, hd), lambda b, h: (b, 0, 2 * H + h)),
                  pl.BlockSpec((None, 3, NA_QROWS * GRID_W, NA_UROWS * GRID_W), lambda b, h: (h, 0, 0, 0)),
                  pl.BlockSpec((1, hd), lambda b, h: (0, 0)),
                  pl.BlockSpec((1, hd), lambda b, h: (0, 0))],
        out_specs=pl.BlockSpec((None, seq, hd), lambda b, h: (b, 0, h)),
        out_shape=jax.ShapeDtypeStruct((B, seq, D), BF16),
        scratch_shapes=[pltpu.VMEM((T, hd), BF16), pltpu.VMEM((T, hd), BF16)],
        compiler_params=_cparams(("arbitrary", "arbitrary")),
        name="na_attention",
    )(qkv, qkv, qkv, bias, q_norm.astype(F32).reshape(1, hd), k_norm.astype(F32).reshape(1, hd))


def _router_kernel(u_ref, g_ref, sh_ref, sc_ref, wr_ref, h_ref, mf_ref, mi_ref):
    h = _normmod(u_ref[...], g_ref[...], sh_ref[...], sc_ref[...])
    h_ref[...] = h
    logits = jnp.dot(h, wr_ref[...], precision=HIGHEST, preferred_element_type=F32)
    lane = lax.broadcasted_iota(jnp.int32, logits.shape, 1).astype(F32)
    big = 1e9

    def first_max(vals):
        m = jnp.max(vals, axis=-1, keepdims=True)
        return m, jnp.min(jnp.where(vals == m, lane, big), axis=-1, keepdims=True)

    gl = jnp.where(lane < MOE_GROUPS, logits, -jnp.inf)
    gmax, gidx = first_max(gl)
    p_top = 1.0 / jnp.sum(jnp.exp(gl - gmax), axis=-1, keepdims=True)
    lo = MOE_GROUPS + MOE_PER_GROUP * gidx
    el = jnp.where((lane >= lo) & (lane < lo + MOE_PER_GROUP), logits, -jnp.inf)
    v1, i1 = first_max(el)
    v2, i2 = first_max(jnp.where(lane == i1, -jnp.inf, el))
    e2 = jnp.exp(v2 - v1)
    w1 = p_top / (1.0 + e2)
    w2 = p_top * e2 / (1.0 + e2)
    mf_ref[...] = jnp.where(lane == 0, w1, jnp.where(lane == 1, w2, 0.0))
    mi_ref[...] = jnp.where(lane == 0, i1 - MOE_GROUPS, jnp.where(lane == 1, i2 - MOE_GROUPS, 0.0)).astype(jnp.int32)


def _router(u, gamma, mods, k_shift, k_scale, wr, *, n_ctx, tm):
    B, T, D = u.shape
    sel = _mod_sel(n_ctx // tm, B)
    return pl.pallas_call(
        _router_kernel,
        grid=(B, T // tm),
        in_specs=[pl.BlockSpec((None, tm, D), lambda b, i: (b, i, 0)),
                  pl.BlockSpec((1, D), lambda b, i: (0, 0)),
                  pl.BlockSpec((None, None, 1, D), lambda b, i: (sel(b, i), k_shift, 0, 0)),
                  pl.BlockSpec((None, None, 1, D), lambda b, i: (sel(b, i), k_scale, 0, 0)),
                  pl.BlockSpec((D, LANES), lambda b, i: (0, 0))],
        out_specs=[pl.BlockSpec((None, tm, D), lambda b, i: (b, i, 0)),
                   pl.BlockSpec((None, tm, LANES), lambda b, i: (b, i, 0)),
                   pl.BlockSpec((None, tm, LANES), lambda b, i: (b, i, 0))],
        out_shape=[jax.ShapeDtypeStruct((B, T, D), F32),
                   jax.ShapeDtypeStruct((B, T, LANES), F32),
                   jax.ShapeDtypeStruct((B, T, LANES), jnp.int32)],
        compiler_params=_cparams(("arbitrary", "arbitrary")),
        name="moe_router",
    )(u, gamma.reshape(1, D), mods, mods, wr)


def _moe_plan(eid2):
    N = eid2.shape[0]
    M = 2 * N
    E = MOE_EXPERTS
    eid = eid2.reshape(-1)
    onehot = (eid[:, None] == jnp.arange(E, dtype=jnp.int32)[None, :]).astype(jnp.int32)
    csum = jnp.cumsum(onehot, axis=0)
    rank = jnp.sum(onehot * csum, axis=1) - 1
    counts = csum[-1]
    padded = (counts + MOE_ROWS - 1) // MOE_ROWS * MOE_ROWS
    pend = jnp.cumsum(padded)
    pstart = pend - padded
    dest = (jnp.sum(onehot * pstart[None, :], axis=1) + rank).astype(jnp.int32)
    P = (M + E * (MOE_ROWS - 1) + MOE_ROWS - 1) // MOE_ROWS * MOE_ROWS
    nb = P // MOE_ROWS
    slot_tok = jnp.zeros((P,), jnp.int32).at[dest].set(jnp.arange(M, dtype=jnp.int32) // 2)
    pos = dest.reshape(N, 2)
    blk_start = jnp.arange(nb, dtype=jnp.int32) * MOE_ROWS
    blk_e = jnp.minimum(jnp.sum((pend[None, :] <= blk_start[:, None]).astype(jnp.int32), axis=1), E - 1)
    n_active = (pend[-1] // MOE_ROWS).astype(jnp.int32).reshape(1)
    return slot_tok.reshape(nb, MOE_ROWS), pos, blk_e.astype(jnp.int32), n_active


def _row_copy(src_hbm, idx_smem, dst_ref, sem, r):
    return pltpu.make_async_copy(src_hbm.at[pl.ds(idx_smem[r], 1), :], dst_ref.at[pl.ds(r, 1), :], sem)


def _expert_kernel(blk_e_ref, nact_ref, idx_hbm, h_hbm, wg_ref, wu_ref, wd_ref, o_ref, idx0, idx1, xbuf, sem_idx,
                   sem_rows):
    i = pl.program_id(0)
    nact = nact_ref[0]
    R = MOE_ROWS
    slot = lax.rem(i, 2)
    idx_bufs = (idx0, idx1)

    def idx_copy(blk, s):
        return pltpu.make_async_copy(idx_hbm.at[blk], idx_bufs[s], sem_idx.at[s])

    def issue_rows(s):
        def body(r, c):
            _row_copy(h_hbm, idx_bufs[s], xbuf.at[s], sem_rows.at[s], r).start(priority=1)
            return c

        lax.fori_loop(0, R, body, 0, unroll=8)

    @pl.when(i == 0)
    def _():
        first = idx_copy(0, 0)
        first.start()
        first.wait()
        issue_rows(0)

        @pl.when(nact > 1)
        def _():
            idx_copy(1, 1).start()

    def prefetch(s):
        @pl.when(i + 1 < nact)
        def _():
            idx_copy(i + 1, 1 - s).wait()
            issue_rows(1 - s)

        @pl.when(i + 2 < nact)
        def _():
            idx_copy(i + 2, s).start()

    @pl.when(slot == 0)
    def _():
        prefetch(0)

    @pl.when(slot == 1)
    def _():
        prefetch(1)

    @pl.when(i < nact)
    def _():
        pltpu.make_async_copy(h_hbm.at[pl.ds(0, R), :], xbuf.at[slot], sem_rows.at[slot]).wait()
        xb = xbuf[slot].astype(BF16)
        g = jnp.dot(xb, wg_ref[...], preferred_element_type=F32)
        u = jnp.dot(xb, wu_ref[...], preferred_element_type=F32)
        hmid = (g * jax.nn.sigmoid(g) * u).astype(BF16)
        o_ref[...] = jnp.dot(hmid, wd_ref[...], preferred_element_type=F32)

    @pl.when(i >= nact)
    def _():
        o_ref[...] = jnp.zeros_like(o_ref)


def _expert_mlp(h, slot_tok, blk_e, n_active, wg, wu, wd, layer):
    nb, R = slot_tok.shape
    D = h.shape[1]
    F = wg.shape[3]
    grid_spec = pltpu.PrefetchScalarGridSpec(
        num_scalar_prefetch=2,
        grid=(nb,),
        in_specs=[pl.BlockSpec(memory_space=pl.ANY),
                  pl.BlockSpec(memory_space=pl.ANY),
                  pl.BlockSpec((None, None, D, F), lambda i, be, na: (layer, be[i], 0, 0)),
                  pl.BlockSpec((None, None, D, F), lambda i, be, na: (layer, be[i], 0, 0)),
                  pl.BlockSpec((None, None, F, D), lambda i, be, na: (layer, be[i], 0, 0))],
        out_specs=pl.BlockSpec((R, D), lambda i, be, na: (i, 0)),
        scratch_shapes=[pltpu.SMEM((R,), jnp.int32), pltpu.SMEM((R,), jnp.int32), pltpu.VMEM((2, R, D), F32),
                        pltpu.SemaphoreType.DMA((2,)), pltpu.SemaphoreType.DMA((2,))],
    )
    return pl.pallas_call(
        _expert_kernel,
        grid_spec=grid_spec,
        out_shape=jax.ShapeDtypeStruct((nb * R, D), F32),
        compiler_params=_cparams(("arbitrary",)),
        name="moe_experts",
    )(blk_e, n_active, slot_tok, h, wg, wu, wd)


def _combine_kernel(p1_hbm, p2_hbm, yb_hbm, mf_ref, r_ref, g_ref, o_ref, i1_smem, i2_smem, g1_ref, g2_ref,
                    sem_idx, sem_rows, *, tiles_per_batch):
    b = pl.program_id(0)
    i = pl.program_id(1)
    t = b * tiles_per_batch + i
    R = o_ref.shape[0]
    c1 = pltpu.make_async_copy(p1_hbm.at[t], i1_smem, sem_idx)
    c2 = pltpu.make_async_copy(p2_hbm.at[t], i2_smem, sem_idx)
    c1.start()
    c2.start()
    c1.wait()
    c2.wait()

    def issue(r, c):
        _row_copy(yb_hbm, i1_smem, g1_ref, sem_rows, r).start(priority=0)
        _row_copy(yb_hbm, i2_smem, g2_ref, sem_rows, r).start(priority=1)
        return c

    lax.fori_loop(0, R, issue, 0, unroll=8)
    pltpu.make_async_copy(yb_hbm.at[pl.ds(0, R), :], g1_ref, sem_rows).wait()
    pltpu.make_async_copy(yb_hbm.at[pl.ds(0, R), :], g2_ref, sem_rows).wait()
    mf = mf_ref[...]
    y = mf[:, 0:1] * g1_ref[...] + mf[:, 1:2] * g2_ref[...]
    o_ref[...] = r_ref[...] + g_ref[...] * y


def _moe_combine(yb, pos, mf, res, mods, k_gate, *, n_ctx, row0, n_rows):
    B, T, _ = mf.shape
    D = yb.shape[1]
    R = MOE_ROWS
    tpb = T // R
    r0 = row0 // R
    sel = _mod_sel((n_ctx - row0) // R if n_ctx > row0 else 0, B)
    p1 = pos[:, 0].reshape(B * tpb, R)
    p2 = pos[:, 1].reshape(B * tpb, R)
    return pl.pallas_call(
        functools.partial(_combine_kernel, tiles_per_batch=tpb),
        grid=(B, n_rows // R),
        in_specs=[pl.BlockSpec(memory_space=pl.ANY), pl.BlockSpec(memory_space=pl.ANY),
                  pl.BlockSpec(memory_space=pl.ANY),
                  pl.BlockSpec((None, R, LANES), lambda b, i: (b, i + r0, 0)),
                  pl.BlockSpec((None, R, D), lambda b, i: (b, i + r0, 0)),
                  pl.BlockSpec((None, None, 1, D), lambda b, i: (sel(b, i), k_gate, 0, 0))],
        out_specs=pl.BlockSpec((None, R, D), lambda b, i: (b, i, 0)),
        out_shape=jax.ShapeDtypeStruct((B, n_rows, D), F32),
        scratch_shapes=[pltpu.SMEM((R,), jnp.int32), pltpu.SMEM((R,), jnp.int32),
                        pltpu.VMEM((R, D), F32), pltpu.VMEM((R, D), F32),
                        pltpu.SemaphoreType.DMA, pltpu.SemaphoreType.DMA],
        compiler_params=_cparams(("arbitrary", "arbitrary")),
        name="moe_combine",
    )(p1, p2, yb, mf, res, mods)


def _hier_moe(u, gamma, mods, w_route_group, w_route_expert, w_gate, w_up, w_down, layer, *, n_ctx, row0, n_rows):
    B, T, D = u.shape
    wr = jnp.zeros((D, LANES), F32)
    wr = wr.at[:, :MOE_GROUPS].set(w_route_group.astype(F32))
    wr = wr.at[:, MOE_GROUPS:MOE_GROUPS + MOE_EXPERTS].set(w_route_expert.astype(F32))
    h, mf, mi = _router(u, gamma, mods, 3, 4, wr, n_ctx=n_ctx, tm=MOE_ROWS)
    slot_tok, pos, blk_e, n_active = _moe_plan(mi[:, :, :2].reshape(B * T, 2))
    yb = _expert_mlp(h.reshape(B * T, D), slot_tok, blk_e, n_active, w_gate.astype(BF16), w_up.astype(BF16),
                     w_down.astype(BF16), layer)
    return _moe_combine(yb, pos, mf, u, mods, 5, n_ctx=n_ctx, row0=row0, n_rows=n_rows)


def _ab_layer(u, mods, gamma, w_in, w_out, s5p, s5_d, s5_w_glu, gla_w_gate2, gla_b_gate, gla_norm_g, *, n_ctx):
    B, T, D = u.shape
    a_width = s5_w_glu.shape[0]
    n_in = w_in.shape[1]
    n_pad = (n_in + LANES - 1) // LANES * LANES
    w_in_b = jnp.pad(w_in, ((0, 0), (0, n_pad - n_in))).astype(BF16)
    px = _normmod_matmul(u, gamma, mods, 0, 1, w_in_b, n_ctx=n_ctx, tm=256, tn=n_pad // 3)
    G = a_width // S5_GROUP
    ops = _s5_operators(*s5p)
    dsk = jnp.tile(s5_d.astype(F32).reshape(G, 1, S5_GROUP), (1, S5_CHUNK, 1)).reshape(G, 1, S5_CHUNK * S5_GROUP)
    ya = _s5_glu(_s5_scan(px, ops, dsk, a_width=a_width), s5_w_glu.astype(BF16), tm=256)
    yb = _gla(px, gla_w_gate2, gla_b_gate, gla_norm_g, n_ctx=n_ctx, a_width=a_width)
    return _matmul_residual([ya, yb], w_out.astype(BF16), u, mods, 2, n_ctx=n_ctx, row0=0, tm=256, tn=1024)


def _na_layer(u, mods, gamma, w_qkv, w_out, q_norm, k_norm, rpb, *, n_ctx):
    qkv = _normmod_matmul(u, gamma, mods, 0, 1, w_qkv.astype(BF16), n_ctx=n_ctx, tm=256, tn=2048)
    o = _na_attention(qkv, q_norm, k_norm, rpb, n_ctx=n_ctx)
    return _matmul_residual([o], w_out.astype(BF16), u, mods, 2, n_ctx=0, row0=n_ctx, tm=256, tn=1024)


def kernel(x, c, ctx, c_ctx, ada_w, ada_b, norm1_g, norm2_g, ab_w_in, ab_w_out, s5_lam_re, s5_lam_im, s5_log_dt,
           s5_b_re, s5_b_im, s5_c_re, s5_c_im, s5_d, s5_w_glu, gla_w_gate2, gla_b_gate, gla_norm_g, na_w_qkv,
           na_w_out, na_q_norm, na_k_norm, na_rpb, moe_w_route_group, moe_w_route_expert, moe_w_gate, moe_w_up,
           moe_w_down):
    B, seq, D = x.shape
    n_ctx = ctx.shape[1]
    depth = ada_w.shape[0]
    assert depth == 2 and n_ctx == S5_CHUNK * S5_SUPER and seq % (GRID_W * 4) == 0
    u = jnp.concatenate([ctx, x], axis=1)
    cvec = jnp.zeros((8, D), F32).at[:B].set(c).at[B].set(c_ctx)
    mods_all = _ada_mod(cvec, ada_w, ada_b)[:, :B + 1]
    mods = mods_all[0]
    s5p = (s5_lam_re[0], s5_lam_im[0], s5_log_dt[0], s5_b_re[0], s5_b_im[0], s5_c_re[0], s5_c_im[0])
    u = _ab_layer(u, mods, norm1_g[0], ab_w_in[0], ab_w_out[0], s5p, s5_d[0], s5_w_glu[0], gla_w_gate2[0],
                  gla_b_gate[0], gla_norm_g[0], n_ctx=n_ctx)
    u = _hier_moe(u, norm2_g[0], mods, moe_w_route_group[0], moe_w_route_expert[0], moe_w_gate, moe_w_up,
                  moe_w_down, 0, n_ctx=n_ctx, row0=0, n_rows=n_ctx + seq)
    mods = mods_all[1]
    xl = _na_layer(u, mods, norm1_g[1], na_w_qkv[0], na_w_out[0], na_q_norm[0], na_k_norm[0], na_rpb[0], n_ctx=n_ctx)
    return _hier_moe(xl, norm2_g[1], mods, moe_w_route_group[1], moe_w_route_expert[1], moe_w_gate, moe_w_up,
                     moe_w_down, 1, n_ctx=0, row0=0, n_rows=seq)
```

```python
import functools
import math

import jax
import jax.numpy as jnp
from jax import lax
from jax.experimental import pallas as pl
from jax.experimental.pallas import tpu as pltpu

F32 = jnp.float32
BF16 = jnp.bfloat16
HIGHEST = lax.Precision.HIGHEST

V7X_VMEM_BYTES = 64 * 1024 * 1024
VMEM_LIMIT = V7X_VMEM_BYTES - 12 * 1024 * 1024
LANES = 128

EPS = 1e-6
NEG_INF = -1e30
GRID_W = 64
N_MOD = 6
S5_GROUP = 16
S5_STATE = 64
S5_CHUNK = 16
S5_SUPER = 16
GLA_HEADS = 4
GLA_RANK = 16
GLA_TAU = 16.0
GLA_CHUNK = 64
GLA_BLOCK = 256
GLA_HEADS_PER_STEP = 4
NA_HEAD_DIM = 128
NA_KH = 8
NA_KW = 16
NA_QROWS = 4
NA_UROWS = NA_KH + NA_QROWS - 1
ROPE_THETA = 10000.0
MOE_GROUPS = 4
MOE_PER_GROUP = 8
MOE_EXPERTS = MOE_GROUPS * MOE_PER_GROUP
MOE_ROWS = 256
MOE_PIECES = 4


def _cparams(sem):
    return pltpu.CompilerParams(dimension_semantics=sem, vmem_limit_bytes=VMEM_LIMIT)


def _nt(a, b):
    return lax.dot_general(a, b, (((1,), (1,)), ((), ())), preferred_element_type=F32)


def _ada_kernel(c_ref, w_ref, b_ref, o_ref):
    c = c_ref[...]
    s = c * jax.nn.sigmoid(c)
    o_ref[...] = jnp.dot(s.astype(BF16), w_ref[...].astype(BF16), preferred_element_type=F32) + b_ref[...]


def _ada_mod(cvec, w, b):
    R, D = cvec.shape
    L, _, N = w.shape
    tn = 1024
    out = pl.pallas_call(
        _ada_kernel,
        grid=(L, N // tn),
        in_specs=[pl.BlockSpec((R, D), lambda l, j: (0, 0)),
                  pl.BlockSpec((None, D, tn), lambda l, j: (l, 0, j)),
                  pl.BlockSpec((None, 1, tn), lambda l, j: (l, 0, j))],
        out_specs=pl.BlockSpec((None, R, tn), lambda l, j: (l, 0, j)),
        out_shape=jax.ShapeDtypeStruct((L, R, N), F32),
        compiler_params=_cparams(("arbitrary", "arbitrary")),
        name="ada_mod",
    )(cvec, w, b.reshape(L, 1, N))
    return out.reshape(L, R, N_MOD, 1, D)


def _mod_sel(nct, nb):
    if nct == 0:
        return lambda b, i: b
    return lambda b, i: jnp.where(i < nct, nb, b)


def _normmod(x, g, sh, sc):
    y = x * lax.rsqrt(jnp.mean(x * x, axis=-1, keepdims=True) + EPS) * g
    return y * (1.0 + sc) + sh


def _normmod_mm_kernel(u_ref, g_ref, sh_ref, sc_ref, w_ref, o_ref, *, tn):
    xn = _normmod(u_ref[...], g_ref[...], sh_ref[...], sc_ref[...]).astype(BF16)
    for n0 in range(0, o_ref.shape[-1], tn):
        o_ref[:, n0:n0 + tn] = jnp.dot(xn, w_ref[:, n0:n0 + tn], preferred_element_type=F32).astype(o_ref.dtype)


def _normmod_matmul(u, gamma, mods, k_shift, k_scale, w, *, n_ctx, tm, tn, out_dtype=F32):
    B, T, D = u.shape
    N = w.shape[1]
    sel = _mod_sel(n_ctx // tm, B)
    return pl.pallas_call(
        functools.partial(_normmod_mm_kernel, tn=tn),
        grid=(B, T // tm),
        in_specs=[pl.BlockSpec((None, tm, D), lambda b, i: (b, i, 0)),
                  pl.BlockSpec((1, D), lambda b, i: (0, 0)),
                  pl.BlockSpec((None, None, 1, D), lambda b, i: (sel(b, i), k_shift, 0, 0)),
                  pl.BlockSpec((None, None, 1, D), lambda b, i: (sel(b, i), k_scale, 0, 0)),
                  pl.BlockSpec((D, N), lambda b, i: (0, 0), pipeline_mode=pl.Buffered(1))],
        out_specs=pl.BlockSpec((None, tm, N), lambda b, i: (b, i, 0)),
        out_shape=jax.ShapeDtypeStruct((B, T, N), out_dtype),
        compiler_params=_cparams(("arbitrary", "arbitrary")),
        name="normmod_matmul",
    )(u, gamma.reshape(1, D), mods, mods, w)


def _mm_res_kernel(*refs, n_lhs, tn):
    a_refs = refs[:n_lhs]
    w_ref, r_ref, g_ref, g2_ref, sh_ref, sc_ref, wr_ref, o_ref, h_ref, mf_ref, mi_ref = refs[n_lhs:]
    lhs = [a[...] for a in a_refs]
    for n0 in range(0, o_ref.shape[-1], tn):
        acc, k0 = None, 0
        for a in lhs:
            part = jnp.dot(a, w_ref[k0:k0 + a.shape[1], n0:n0 + tn], preferred_element_type=F32)
            acc = part if acc is None else acc + part
            k0 += a.shape[1]
        o_ref[:, n0:n0 + tn] = r_ref[:, n0:n0 + tn] + g_ref[:, n0:n0 + tn] * acc
    h = _normmod(o_ref[...], g2_ref[...], sh_ref[...], sc_ref[...])
    h_ref[...] = h
    _route(h, wr_ref, mf_ref, mi_ref)


def _matmul_residual_route(lhs, w, res, mods, gamma2, wr, *, n_ctx, row0, tm, tn):
    B, Ta, _ = lhs[0].shape
    K, N = w.shape
    sel = _mod_sel(n_ctx // tm, B)
    r0 = row0 // tm

    def mod_spec(k):
        return pl.BlockSpec((None, None, 1, N), lambda b, i: (sel(b, i), k, 0, 0))

    row_spec = pl.BlockSpec((None, tm, N), lambda b, i: (b, i, 0))
    lane_spec = pl.BlockSpec((None, tm, LANES), lambda b, i: (b, i, 0))
    return pl.pallas_call(
        functools.partial(_mm_res_kernel, n_lhs=len(lhs), tn=tn),
        grid=(B, Ta // tm),
        in_specs=[pl.BlockSpec((None, tm, a.shape[2]), lambda b, i: (b, i, 0)) for a in lhs]
        + [pl.BlockSpec((K, N), lambda b, i: (0, 0), pipeline_mode=pl.Buffered(1)),
           pl.BlockSpec((None, tm, N), lambda b, i: (b, i + r0, 0)),
           mod_spec(2),
           pl.BlockSpec((1, N), lambda b, i: (0, 0)),
           mod_spec(3), mod_spec(4),
           pl.BlockSpec((N, LANES), lambda b, i: (0, 0))],
        out_specs=[row_spec, row_spec, lane_spec, lane_spec],
        out_shape=[jax.ShapeDtypeStruct((B, Ta, N), F32), jax.ShapeDtypeStruct((B, Ta, N), F32),
                   jax.ShapeDtypeStruct((B, Ta, LANES), F32), jax.ShapeDtypeStruct((B, Ta, LANES), jnp.int32)],
        compiler_params=_cparams(("arbitrary", "arbitrary")),
        name="matmul_residual_route",
    )(*lhs, w, res, mods, gamma2.reshape(1, N), mods, mods, wr)


def _s5_operators(lam_re, lam_im, log_dt, b_re, b_im, c_re, c_im):
    T = S5_CHUNK
    lr = jnp.minimum(lam_re.astype(F32), -1e-4)
    li = lam_im.astype(F32)
    dt = jnp.exp(log_dt.astype(F32))[..., None]
    mag = jnp.exp(lr * dt)
    a_re = mag * jnp.cos(li * dt)
    a_im = mag * jnp.sin(li * dt)
    num_re = a_re - 1.0
    den = lr * lr + li * li
    f_re = (num_re * lr + a_im * li) / den
    f_im = (a_im * lr - num_re * li) / den
    bb_re = f_re[..., None] * b_re.astype(F32) - f_im[..., None] * b_im.astype(F32)
    bb_im = f_re[..., None] * b_im.astype(F32) + f_im[..., None] * b_re.astype(F32)
    cr, ci = c_re.astype(F32), c_im.astype(F32)

    def powers(step, n):
        m = (step * jnp.arange(n + 1, dtype=F32))[:, None, None, None]
        mg = jnp.exp(m * (lr * dt))
        return mg * jnp.cos(m * (li * dt)), mg * jnp.sin(m * (li * dt))

    pr, pi = powers(1, T)
    pr16, pi16 = powers(T, S5_SUPER)
    ca_re = cr[None] * pr[:T, :, :, None, :] - ci[None] * pi[:T, :, :, None, :]
    ca_im = cr[None] * pi[:T, :, :, None, :] + ci[None] * pr[:T, :, :, None, :]
    kk = (jnp.einsum('kdgip,dgpj->dkgij', ca_re, bb_re) - jnp.einsum('kdgip,dgpj->dkgij', ca_im, bb_im))
    G = lr.shape[1]
    I = S5_GROUP
    s_idx = jnp.arange(T)[:, None]
    t_idx = jnp.arange(T)[None, :]

    def toeplitz(kd, lag):
        m = jnp.where((lag >= 0)[:, :, None, None, None], kd[jnp.clip(lag, 0, T - 1)], 0.0)
        return jnp.transpose(m, (2, 0, 4, 1, 3)).reshape(G, T * I, T * I)

    mt = toeplitz(kk[0], t_idx - s_idx) + toeplitz(kk[1], s_idx - t_idx)

    def v_op(d, qr, qi):
        vr = qr[:, :, :, None] * bb_re[d][None] - qi[:, :, :, None] * bb_im[d][None]
        vi = qr[:, :, :, None] * bb_im[d][None] + qi[:, :, :, None] * bb_re[d][None]
        vr = jnp.transpose(vr, (1, 0, 3, 2)).reshape(G, T * I, S5_STATE)
        vi = jnp.transpose(vi, (1, 0, 3, 2)).reshape(G, T * I, S5_STATE)
        return jnp.concatenate([vr, vi, -vi, vr], axis=-1)

    vt = jnp.concatenate([v_op(0, pr[:T, 0][::-1], pi[:T, 0][::-1]), v_op(1, pr[:T, 1], pi[:T, 1])], axis=-1)

    def w_op(d, qr, qi):
        wr = cr[d][None] * qr[:, :, None, :] - ci[d][None] * qi[:, :, None, :]
        wi = cr[d][None] * qi[:, :, None, :] + ci[d][None] * qr[:, :, None, :]
        wr = jnp.transpose(wr, (1, 3, 0, 2)).reshape(G, S5_STATE, T * I)
        wi = jnp.transpose(wi, (1, 3, 0, 2)).reshape(G, S5_STATE, T * I)
        return jnp.concatenate([wr, -wi], axis=1)

    w = jnp.concatenate([w_op(0, pr[1:T + 1, 0], pi[1:T + 1, 0]),
                         w_op(1, pr[1:T + 1, 1][::-1], pi[1:T + 1, 1][::-1])], axis=1)

    def dup(z):
        return jnp.concatenate([z, z], -1)

    rows = []
    for d in range(2):
        rows += [dup(pr16[1, d]), dup(pi16[1, d]), dup(pr16[S5_SUPER, d]), dup(pi16[S5_SUPER, d])]
    dec = jnp.stack(rows, axis=1)
    prs = []
    for d in range(2):
        prs += [dup(jnp.transpose(pr16[:S5_SUPER, d], (1, 0, 2))), dup(jnp.transpose(pi16[:S5_SUPER, d], (1, 0, 2)))]
    pwt = jnp.concatenate(prs, axis=1)
    return mt.astype(BF16), vt.astype(BF16), w.astype(BF16), dec, pwt


S5_LANE_GROUPS = LANES // S5_GROUP


def _piece_transpose(arrs, lane):
    for d in (4, 2, 1):
        bit = ((lane // S5_GROUP) & d) != 0
        new = list(arrs)
        for r in range(S5_LANE_GROUPS):
            if r & d == 0:
                a, b = arrs[r], arrs[r + d]
                new[r] = jnp.where(bit, pltpu.roll(b, S5_GROUP * d, 1), a)
                new[r + d] = jnp.where(bit, b, pltpu.roll(a, LANES - S5_GROUP * d, 1))
        arrs = new
    return arrs


def _s5_kernel(u_ref, mt_ref, vt_ref, w_ref, dec_ref, pw_ref, dsk_ref, y_ref, xg_ref, yg_ref, vv_ref, hin_ref,
               hs_ref, *, nsc):
    nj = S5_SUPER
    nch = u_ref.shape[0] // S5_CHUNK
    ng = S5_LANE_GROUPS
    lane = lax.broadcasted_iota(jnp.int32, (nch, LANES), 1)
    for h in range(2):
        zs = [u_ref[pl.ds(ng * h + t, nch, stride=S5_CHUNK), :] for t in range(ng)]
        for g, a in enumerate(_piece_transpose(zs, lane)):
            xg_ref[g, :, LANES * h:LANES * (h + 1)] = a
    hs_ref[...] = jnp.zeros_like(hs_ref)

    def group(g, carry):
        x = xg_ref[g]
        xb = x.astype(BF16)
        yg_ref[g] = jnp.dot(xb, mt_ref[g], preferred_element_type=F32) + x * dsk_ref[g]
        vv = jnp.dot(xb, vt_ref[g], preferred_element_type=F32)
        for n in range(4):
            vv_ref[n] = vv[:, LANES * n:LANES * (n + 1)]
        dec = dec_ref[g]
        pw = pw_ref[g]
        for d in range(2):
            ar16, ai16, ar256, ai256 = (dec[4 * d + n:4 * d + n + 1] for n in range(4))

            def v(j, d=d):
                return vv_ref[2 * d, pl.ds(j, nsc, stride=nj), :]

            def jv(j, d=d):
                return vv_ref[2 * d + 1, pl.ds(j, nsc, stride=nj), :]

            order = list(range(nj)) if d == 0 else list(range(nj - 1, -1, -1))
            loc = [None] * nj
            l = jnp.zeros((nsc, LANES), F32)
            gg = jnp.zeros((nsc, LANES), F32)
            for n, j in enumerate(order):
                if n > 0:
                    p = order[n - 1]
                    l, gg = ar16 * l + ai16 * gg + v(p), ar16 * gg - ai16 * l + jv(p)
                loc[j] = l
            p = order[-1]
            e, je = ar16 * l + ai16 * gg + v(p), ar16 * gg - ai16 * l + jv(p)
            seq = list(range(nsc)) if d == 0 else [0] + list(range(nsc - 1, 0, -1))
            cur = jnp.zeros((1, LANES), F32)
            jcur = jnp.zeros((1, LANES), F32)
            for m in seq:
                hs_ref[2 * d, m:m + 1, :] = cur
                hs_ref[2 * d + 1, m:m + 1, :] = jcur
                cur, jcur = (ar256 * cur + ai256 * jcur + e[m:m + 1], ar256 * jcur - ai256 * cur + je[m:m + 1])
            hs = hs_ref[2 * d, 0:nsc, :]
            jhs = hs_ref[2 * d + 1, 0:nsc, :]
            for n, j in enumerate(order):
                pr = pw[32 * d + n:32 * d + n + 1]
                pi = pw[32 * d + 16 + n:32 * d + 16 + n + 1]
                hin_ref[d, pl.ds(j, nsc, stride=nj), :] = loc[j] + pr * hs + pi * jhs
        hin = jnp.concatenate([hin_ref[0], hin_ref[1]], axis=1).astype(BF16)
        yg_ref[g] += jnp.dot(hin, w_ref[g], preferred_element_type=F32)
        return carry

    lax.fori_loop(0, ng, group, 0)
    for h in range(2):
        ys = [yg_ref[g, :, LANES * h:LANES * (h + 1)] for g in range(ng)]
        for t, a in enumerate(_piece_transpose(ys, lane)):
            y_ref[pl.ds(ng * h + t, nch, stride=S5_CHUNK), :] = a


def _s5_scan(px, ops, dsk, *, a_width):
    B, T, _ = px.shape
    mt, vt, w, dec, pw = ops
    ng = S5_LANE_GROUPS
    C = S5_CHUNK * S5_GROUP
    nch = T // S5_CHUNK
    nsc = nch // S5_SUPER
    nsp = (nsc + 7) // 8 * 8
    return pl.pallas_call(
        functools.partial(_s5_kernel, nsc=nsc),
        grid=(a_width // LANES, B),
        in_specs=[pl.BlockSpec((None, T, LANES), lambda gb, b: (b, 0, gb)),
                  pl.BlockSpec((ng, C, C), lambda gb, b: (gb, 0, 0)),
                  pl.BlockSpec((ng, C, 2 * C), lambda gb, b: (gb, 0, 0)),
                  pl.BlockSpec((ng, C, C), lambda gb, b: (gb, 0, 0)),
                  pl.BlockSpec((ng, 8, LANES), lambda gb, b: (gb, 0, 0)),
                  pl.BlockSpec((ng, 64, LANES), lambda gb, b: (gb, 0, 0)),
                  pl.BlockSpec((ng, 1, C), lambda gb, b: (gb, 0, 0))],
        out_specs=pl.BlockSpec((None, T, LANES), lambda gb, b: (b, 0, gb)),
        out_shape=jax.ShapeDtypeStruct((B, T, a_width), F32),
        scratch_shapes=[pltpu.VMEM((ng, nch, C), F32), pltpu.VMEM((ng, nch, C), F32),
                        pltpu.VMEM((4, nch, LANES), F32), pltpu.VMEM((2, nch, LANES), F32),
                        pltpu.VMEM((4, nsp, LANES), F32)],
        compiler_params=_cparams(("arbitrary", "arbitrary")),
        name="s5_scan",
    )(px, mt, vt, w, dec, pw, dsk)


def _s5_glu_kernel(y_ref, w_ref, o_ref):
    g = jax.nn.gelu(y_ref[...])
    z = jnp.dot(g.astype(BF16), w_ref[...], preferred_element_type=F32)
    o_ref[...] = (g * jax.nn.sigmoid(z)).astype(o_ref.dtype)


def _s5_glu(y, w, *, tm):
    B, T, A = y.shape
    return pl.pallas_call(
        _s5_glu_kernel,
        grid=(B, T // tm),
        in_specs=[pl.BlockSpec((None, tm, A), lambda b, i: (b, i, 0)),
                  pl.BlockSpec((A, A), lambda b, i: (0, 0))],
        out_specs=pl.BlockSpec((None, tm, A), lambda b, i: (b, i, 0)),
        out_shape=jax.ShapeDtypeStruct((B, T, A), BF16),
        compiler_params=_cparams(("arbitrary", "arbitrary")),
        name="s5_glu",
    )(y, w)


def _log_sigmoid(z):
    return jnp.minimum(z, 0.0) - jnp.log(1.0 + jnp.exp(-jnp.abs(z)))


def _gla_kernel(q_ref, k_ref, v_ref, r_ref, lr_ref, cos_ref, sin_ref, wg_ref, bg_ref, ng_ref, o_ref, s_ref, of_ref,
                *, nblk, dk, dv, hp):
    p = pl.program_id(2)
    s = pl.program_id(3)
    C = GLA_CHUNK
    nch = GLA_BLOCK // C
    blk = jnp.where(p == 0, s, jnp.where(s == 0, 0, nblk - s))

    @pl.when(s == 0)
    def _():
        s_ref[...] = jnp.zeros_like(s_ref)

    NB = GLA_BLOCK
    lane = lax.broadcasted_iota(jnp.int32, (NB, dk), 1)
    lower = (lane & 63) < 32
    rpos = lax.broadcasted_iota(jnp.int32, (NB, dk), 0) & (C - 1)
    row = lax.broadcasted_iota(jnp.int32, (NB, NB), 0)
    col = lax.broadcasted_iota(jnp.int32, (NB, NB), 1)
    same_chunk = (row // C) == (col // C)
    eye = (lax.broadcasted_iota(jnp.int32, (dk, dk), 0) == lax.broadcasted_iota(jnp.int32, (dk, dk), 1)).astype(BF16)

    def rope(z, cos, sin):
        return z * cos + jnp.where(lower, pltpu.roll(z, dk - 32, 1), pltpu.roll(z, 32, 1)) * sin

    def run(reverse):
        for hd in range(hp):
            head(reverse, hd)

    def head(reverse, hd):
        mask = same_chunk & ((row <= col) if reverse else (row >= col))
        cos = cos_ref[...]
        sin = sin_ref[...]
        kcols = slice(hd * dk, (hd + 1) * dk)
        vcols = slice(hd * dv, (hd + 1) * dv)
        q = rope(q_ref[:, kcols], cos, sin) * (dk ** -0.5)
        k = rope(k_ref[:, kcols], cos, sin)
        vb = v_ref[:, vcols].astype(BF16)
        lrb = lr_ref[...].astype(BF16)
        la = _log_sigmoid(jnp.dot(lrb, wg_ref[hd], preferred_element_type=F32) + bg_ref[hd]) * (1.0 / GLA_TAU)
        b = la
        sh = 1
        while sh < C:
            if reverse:
                b = b + jnp.where(rpos < C - sh, pltpu.roll(b, NB - sh, 0), 0.0)
            else:
                b = b + jnp.where(rpos >= sh, pltpu.roll(b, sh, 0), 0.0)
            sh *= 2
        last = C - 1 if not reverse else 0
        tots = [b[c * C + last:c * C + last + 1] for c in range(nch)]
        btot = jnp.concatenate([jnp.broadcast_to(t, (C, dk)) for t in tots], 0)
        dmat = jnp.exp(jnp.concatenate(tots + [jnp.zeros((8 - nch, dk), F32)], 0).T)
        qi = (q * jnp.exp(b)).astype(BF16)
        ki = (k * jnp.exp(-b)).astype(BF16)
        ko = (k * jnp.exp(btot - b)).astype(BF16)
        att = jnp.where(mask, _nt(qi, ki), 0.0).astype(BF16)
        o = jnp.dot(att, vb, preferred_element_type=F32)
        upd, dcol = [], []
        for c in range(nch):
            kot = _nt(eye, ko[c * C:(c + 1) * C]).astype(BF16)
            upd.append(jnp.dot(kot, vb[c * C:(c + 1) * C], preferred_element_type=F32))
            dcol.append(dmat[:, c:c + 1])
        st = s_ref[hd]
        o_state = [None] * nch
        for c in (range(nch - 1, -1, -1) if reverse else range(nch)):
            o_state[c] = jnp.dot(qi[c * C:(c + 1) * C], st.astype(BF16), preferred_element_type=F32)
            st = st * dcol[c] + upd[c]
        s_ref[hd] = st
        o = o + jnp.concatenate(o_state, axis=0)
        tok = pl.ds(pl.multiple_of(blk * NB, NB), NB)
        if not reverse:
            of_ref[tok, vcols] = o
        else:
            o = o + of_ref[tok, vcols]
            o = o * lax.rsqrt(jnp.mean(o * o, axis=-1, keepdims=True) + EPS) * ng_ref[hd]
            r = r_ref[:, vcols]
            o_ref[:, vcols] = (o * (r * jax.nn.sigmoid(r))).astype(o_ref.dtype)

    @pl.when(p == 0)
    def _():
        run(False)

    @pl.when(p == 1)
    def _():
        run(True)


def _rope_tables(n_ctx, seq, dk):
    nf = dk // 4
    inv = ROPE_THETA ** (-jnp.arange(nf, dtype=F32) / nf)
    pos = jnp.arange(seq)
    ang_r = (pos // GRID_W).astype(F32)[:, None] * inv[None, :]
    ang_c = (pos % GRID_W).astype(F32)[:, None] * inv[None, :]
    cos = jnp.concatenate([jnp.cos(ang_r)] * 2 + [jnp.cos(ang_c)] * 2, axis=-1)
    sin = jnp.concatenate([-jnp.sin(ang_r), jnp.sin(ang_r), -jnp.sin(ang_c), jnp.sin(ang_c)], axis=-1)
    cos = jnp.concatenate([jnp.ones((n_ctx, dk), F32), cos], axis=0)
    sin = jnp.concatenate([jnp.zeros((n_ctx, dk), F32), sin], axis=0)
    return cos, sin


def _gla(px, w_gate2, b_gate, norm_g, *, n_ctx, a_width):
    B, T, _ = px.shape
    H = GLA_HEADS
    qk = w_gate2.shape[-1]
    dk = qk // H
    bw = norm_g.shape[-1]
    dv = bw // H
    nblk = T // GLA_BLOCK
    cos, sin = _rope_tables(n_ctx, T - n_ctx, dk)
    lr_col0 = a_width + 2 * qk + 2 * bw
    wg = jnp.zeros((2, H, LANES, dk), F32)
    for d in range(2):
        wg = wg.at[d, :, d * GLA_RANK:(d + 1) * GLA_RANK, :].set(
            jnp.transpose(w_gate2[d].reshape(GLA_RANK, H, dk), (1, 0, 2)))
    wg = wg.astype(BF16)
    bg = b_gate.astype(F32).reshape(2, H, 1, dk)
    ng = norm_g.astype(F32).reshape(H, 1, dv)
    hp = GLA_HEADS_PER_STEP

    def blk(p, s):
        return jnp.where(p == 0, s, jnp.where(s == 0, 0, nblk - s))

    kw, vw = hp * dk, hp * dv
    qb, kb = a_width // kw, (a_width + qk) // kw
    vbk, rbk = (a_width + 2 * qk) // vw, (a_width + 2 * qk + bw) // vw
    lb = lr_col0 // LANES
    return pl.pallas_call(
        functools.partial(_gla_kernel, nblk=nblk, dk=dk, dv=dv, hp=hp),
        grid=(B, H // hp, 2, nblk),
        in_specs=[pl.BlockSpec((None, GLA_BLOCK, kw), lambda b, g, p, s: (b, blk(p, s), qb + g)),
                  pl.BlockSpec((None, GLA_BLOCK, kw), lambda b, g, p, s: (b, blk(p, s), kb + g)),
                  pl.BlockSpec((None, GLA_BLOCK, vw), lambda b, g, p, s: (b, blk(p, s), vbk + g)),
                  pl.BlockSpec((None, GLA_BLOCK, vw), lambda b, g, p, s: (b, blk(p, s), rbk + g)),
                  pl.BlockSpec((None, GLA_BLOCK, LANES), lambda b, g, p, s: (b, blk(p, s), lb)),
                  pl.BlockSpec((GLA_BLOCK, dk), lambda b, g, p, s: (blk(p, s), 0)),
                  pl.BlockSpec((GLA_BLOCK, dk), lambda b, g, p, s: (blk(p, s), 0)),
                  pl.BlockSpec((None, hp, LANES, dk), lambda b, g, p, s: (p, g, 0, 0)),
                  pl.BlockSpec((None, hp, 1, dk), lambda b, g, p, s: (p, g, 0, 0)),
                  pl.BlockSpec((hp, 1, dv), lambda b, g, p, s: (g, 0, 0))],
        out_specs=pl.BlockSpec((None, GLA_BLOCK, vw), lambda b, g, p, s: (b, jnp.where(p == 0, 0, blk(p, s)), g)),
        out_shape=jax.ShapeDtypeStruct((B, T, bw), BF16),
        scratch_shapes=[pltpu.VMEM((hp, dk, dv), F32), pltpu.VMEM((T, vw), F32)],
        compiler_params=_cparams(("arbitrary", "arbitrary", "arbitrary", "arbitrary")),
        name="gla",
    )(px, px, px, px, px, cos, sin, wg, bg, ng)


def _na_kernel(q_ref, k_ref, v_ref, bias_ref, qn_ref, kn_ref, o_ref, kb_ref, *, n_ctx, rows):
    hd = NA_HEAD_DIM
    W = GRID_W
    T = k_ref.shape[0]
    step = n_ctx

    def prep(i, c):
        sl = pl.ds(pl.multiple_of(i * step, step), step)
        kk = k_ref[sl, :].astype(F32)
        kb_ref[sl, :] = (kk * lax.rsqrt(jnp.mean(kk * kk, axis=-1, keepdims=True) + EPS) * kn_ref[...]).astype(BF16)
        return c

    lax.fori_loop(0, T // step, prep, 0)
    scale = hd ** -0.5
    ngrp = rows // NA_QROWS
    nq = NA_QROWS * W
    nk = NA_UROWS * W

    def group(g, c):
        r0 = g * NA_QROWS
        ustart = jnp.clip(r0 - NA_KH // 2, 0, rows - NA_UROWS)
        case = jnp.where(g == 0, 0, jnp.where(g == ngrp - 1, 2, 1))
        q = q_ref[pl.ds(pl.multiple_of(n_ctx + r0 * W, W), nq), :].astype(F32)
        q = (q * lax.rsqrt(jnp.mean(q * q, axis=-1, keepdims=True) + EPS) * qn_ref[...] * scale).astype(BF16)
        ws = pl.ds(pl.multiple_of(n_ctx + ustart * W, W), nk)
        s_lat = _nt(q, kb_ref[ws, :]) + bias_ref[case]
        s_ctx = _nt(q, kb_ref[0:n_ctx, :])
        m = jnp.maximum(jnp.max(s_lat, axis=-1, keepdims=True), jnp.max(s_ctx, axis=-1, keepdims=True))
        p_lat = jnp.exp(s_lat - m)
        p_ctx = jnp.exp(s_ctx - m)
        den = jnp.sum(p_lat, axis=-1, keepdims=True) + jnp.sum(p_ctx, axis=-1, keepdims=True)
        o = (jnp.dot(p_lat.astype(BF16), v_ref[ws, :], preferred_element_type=F32)
             + jnp.dot(p_ctx.astype(BF16), v_ref[0:n_ctx, :], preferred_element_type=F32))
        o_ref[pl.ds(pl.multiple_of(r0 * W, W), nq), :] = (o / den).astype(o_ref.dtype)
        return c

    lax.fori_loop(0, ngrp, group, 0)


def _na_bias(rpb, rows):
    H = rpb.shape[0]
    col = jnp.arange(GRID_W)
    cs = jnp.clip(col - NA_KW // 2, 0, GRID_W - NA_KW)
    col_ok = (col[None, :] >= cs[:, None]) & (col[None, :] < cs[:, None] + NA_KW)
    dc_idx = jnp.clip(col[None, :] - col[:, None] + NA_KW - 1, 0, 2 * NA_KW - 2)
    rpb_c = jnp.where(col_ok[None, None], rpb.astype(F32)[:, :, dc_idx], NEG_INF)
    ngrp = rows // NA_QROWS
    tables = []
    for g in (0, 1, ngrp - 1):
        r0 = g * NA_QROWS
        ustart = min(max(r0 - NA_KH // 2, 0), rows - NA_UROWS)
        qr = r0 + jnp.arange(NA_QROWS)
        start = jnp.clip(qr - NA_KH // 2, 0, rows - NA_KH)
        kr = ustart + jnp.arange(NA_UROWS)
        ok = (kr[None, :] >= start[:, None]) & (kr[None, :] < start[:, None] + NA_KH)
        idx = jnp.clip(kr[None, :] - qr[:, None] + NA_KH - 1, 0, 2 * NA_KH - 2)
        t = jnp.where(ok[None, :, :, None, None], rpb_c[:, idx], NEG_INF)
        tables.append(jnp.transpose(t, (0, 1, 3, 2, 4)).reshape(H, NA_QROWS * GRID_W, NA_UROWS * GRID_W))
    return jnp.stack(tables, axis=1)


def _na_attention(qkv, q_norm, k_norm, rpb, *, n_ctx):
    assert qkv.dtype == BF16
    B, T, D3 = qkv.shape
    D = D3 // 3
    H = D // NA_HEAD_DIM
    seq = T - n_ctx
    rows = seq // GRID_W
    assert rows % NA_QROWS == 0 and rows >= NA_UROWS + NA_QROWS
    bias = _na_bias(rpb, rows)
    hd = NA_HEAD_DIM
    return pl.pallas_call(
        functools.partial(_na_kernel, n_ctx=n_ctx, rows=rows),
        grid=(H, B),
        in_specs=[pl.BlockSpec((None, T, hd), lambda h, b: (b, 0, h)),
                  pl.BlockSpec((None, T, hd), lambda h, b: (b, 0, H + h)),
                  pl.BlockSpec((None, T, hd), lambda h, b: (b, 0, 2 * H + h)),
                  pl.BlockSpec((None, 3, NA_QROWS * GRID_W, NA_UROWS * GRID_W), lambda h, b: (h, 0, 0, 0)),
                  pl.BlockSpec((1, hd), lambda h, b: (0, 0)),
                  pl.BlockSpec((1, hd), lambda h, b: (0, 0))],
        out_specs=pl.BlockSpec((None, seq, hd), lambda h, b: (b, 0, h)),
        out_shape=jax.ShapeDtypeStruct((B, seq, D), BF16),
        scratch_shapes=[pltpu.VMEM((T, hd), BF16)],
        compiler_params=_cparams(("arbitrary", "arbitrary")),
        name="na_attention",
    )(qkv, qkv, qkv, bias, q_norm.astype(F32).reshape(1, hd), k_norm.astype(F32).reshape(1, hd))


def _route(h, wr_ref, mf_ref, mi_ref):
    logits = jnp.dot(h, wr_ref[...], precision=HIGHEST, preferred_element_type=F32)
    lane = lax.broadcasted_iota(jnp.int32, logits.shape, 1).astype(F32)
    big = 1e9

    def first_max(vals):
        m = jnp.max(vals, axis=-1, keepdims=True)
        return m, jnp.min(jnp.where(vals == m, lane, big), axis=-1, keepdims=True)

    gl = jnp.where(lane < MOE_GROUPS, logits, -jnp.inf)
    gmax, gidx = first_max(gl)
    p_top = 1.0 / jnp.sum(jnp.exp(gl - gmax), axis=-1, keepdims=True)
    lo = MOE_GROUPS + MOE_PER_GROUP * gidx
    el = jnp.where((lane >= lo) & (lane < lo + MOE_PER_GROUP), logits, -jnp.inf)
    v1, i1 = first_max(el)
    v2, i2 = first_max(jnp.where(lane == i1, -jnp.inf, el))
    e2 = jnp.exp(v2 - v1)
    w1 = p_top / (1.0 + e2)
    w2 = p_top * e2 / (1.0 + e2)
    mf_ref[...] = jnp.where(lane == 0, w1, jnp.where(lane == 1, w2, 0.0))
    mi_ref[...] = jnp.where(lane == 0, i1 - MOE_GROUPS, jnp.where(lane == 1, i2 - MOE_GROUPS, 0.0)).astype(jnp.int32)


def _router_weights(w_route_group, w_route_expert):
    D = w_route_group.shape[0]
    wr = jnp.zeros((D, LANES), F32)
    wr = wr.at[:, :MOE_GROUPS].set(w_route_group.astype(F32))
    return wr.at[:, MOE_GROUPS:MOE_GROUPS + MOE_EXPERTS].set(w_route_expert.astype(F32))


def _moe_plan(eid2):
    N = eid2.shape[0]
    M = 2 * N
    E = MOE_EXPERTS
    eid = eid2.reshape(-1)
    onehot = (eid[:, None] == jnp.arange(E, dtype=jnp.int32)[None, :]).astype(jnp.int32)
    csum = jnp.cumsum(onehot, axis=0)
    rank = jnp.sum(onehot * csum, axis=1) - 1
    counts = csum[-1]
    padded = (counts + MOE_ROWS - 1) // MOE_ROWS * MOE_ROWS
    pend = jnp.cumsum(padded)
    pstart = pend - padded
    dest = (jnp.sum(onehot * pstart[None, :], axis=1) + rank).astype(jnp.int32)
    P = (M + E * (MOE_ROWS - 1) + MOE_ROWS - 1) // MOE_ROWS * MOE_ROWS
    nb = P // MOE_ROWS
    slot_tok = jnp.zeros((P,), jnp.int32).at[dest].set(jnp.arange(M, dtype=jnp.int32) // 2)
    pos = dest.reshape(N, 2)
    blk_start = jnp.arange(nb, dtype=jnp.int32) * MOE_ROWS
    blk_e = jnp.minimum(jnp.sum((pend[None, :] <= blk_start[:, None]).astype(jnp.int32), axis=1), E - 1)
    n_active = (pend[-1] // MOE_ROWS).astype(jnp.int32).reshape(1)
    return slot_tok.reshape(nb, MOE_ROWS), pos, blk_e.astype(jnp.int32), n_active


def _row_copy(src_hbm, idx_smem, dst_ref, sem, r):
    return pltpu.make_async_copy(src_hbm.at[pl.ds(idx_smem[r], 1), :], dst_ref.at[pl.ds(r, 1), :], sem)


def _expert_kernel(blk_e_ref, nact_ref, idx_hbm, h_hbm, wg_ref, wu_ref, wd_ref, o_ref, idx0, idx1, xbuf, sem_idx,
                   sem_rows):
    i = pl.program_id(0)
    nb = pl.num_programs(0)
    nact = nact_ref[0]
    R = MOE_ROWS
    F = wg_ref.shape[1]
    slot = lax.rem(i, 2)
    idx_bufs = (idx0, idx1)
    pieces = MOE_PIECES
    rows_per_piece = R // pieces
    fw = F // pieces

    def idx_copy(blk, s):
        return pltpu.make_async_copy(idx_hbm.at[blk], idx_bufs[s], sem_idx.at[s])

    def rows_done(s):
        return pltpu.make_async_copy(h_hbm.at[pl.ds(0, R), :], xbuf.at[s], sem_rows.at[s])

    @pl.when(i == 0)
    def _():
        first = idx_copy(0, 0)
        first.start()
        first.wait()

        def body(r, c):
            _row_copy(h_hbm, idx0, xbuf.at[0], sem_rows.at[0], r).start(priority=1)
            return c

        lax.fori_loop(0, R, body, 0, unroll=8)
        idx_copy(jnp.minimum(1, nb - 1), 1).start()

    def step(s):
        o = 1 - s
        idx_copy(0, o).wait()
        idx_copy(jnp.minimum(i + 2, nb - 1), s).start()
        rows_done(s).wait()

        def issue_next_rows(piece):
            def body(r, c):
                _row_copy(h_hbm, idx_bufs[o], xbuf.at[o], sem_rows.at[o], r).start(priority=1)
                return c

            lax.fori_loop(piece * rows_per_piece, (piece + 1) * rows_per_piece, body, 0, unroll=8)

        @pl.when(i < nact)
        def _():
            xb = xbuf[s].astype(BF16)
            acc = None
            for c in range(pieces):
                issue_next_rows(c)
                cols = slice(c * fw, (c + 1) * fw)
                g = jnp.dot(xb, wg_ref[:, cols], preferred_element_type=F32)
                u = jnp.dot(xb, wu_ref[:, cols], preferred_element_type=F32)
                hmid = (g * jax.nn.sigmoid(g) * u).astype(BF16)
                y = jnp.dot(hmid, wd_ref[cols, :], preferred_element_type=F32)
                acc = y if acc is None else acc + y
            o_ref[...] = acc

        @pl.when(i >= nact)
        def _():
            for c in range(pieces):
                issue_next_rows(c)
            o_ref[...] = jnp.zeros_like(o_ref)

        @pl.when(i == nb - 1)
        def _():
            rows_done(o).wait()
            idx_copy(0, s).wait()

    @pl.when(slot == 0)
    def _():
        step(0)

    @pl.when(slot == 1)
    def _():
        step(1)


def _expert_mlp(h, slot_tok, blk_e, n_active, wg, wu, wd, layer):
    nb, R = slot_tok.shape
    D = h.shape[1]
    F = wg.shape[3]
    grid_spec = pltpu.PrefetchScalarGridSpec(
        num_scalar_prefetch=2,
        grid=(nb,),
        in_specs=[pl.BlockSpec(memory_space=pl.ANY),
                  pl.BlockSpec(memory_space=pl.ANY),
                  pl.BlockSpec((None, None, D, F), lambda i, be, na: (layer, be[i], 0, 0)),
                  pl.BlockSpec((None, None, D, F), lambda i, be, na: (layer, be[i], 0, 0)),
                  pl.BlockSpec((None, None, F, D), lambda i, be, na: (layer, be[i], 0, 0))],
        out_specs=pl.BlockSpec((R, D), lambda i, be, na: (i, 0)),
        scratch_shapes=[pltpu.SMEM((R,), jnp.int32), pltpu.SMEM((R,), jnp.int32), pltpu.VMEM((2, R, D), F32),
                        pltpu.SemaphoreType.DMA((2,)), pltpu.SemaphoreType.DMA((2,))],
    )
    return pl.pallas_call(
        _expert_kernel,
        grid_spec=grid_spec,
        out_shape=jax.ShapeDtypeStruct((nb * R, D), F32),
        compiler_params=_cparams(("arbitrary",)),
        name="moe_experts",
    )(blk_e, n_active, slot_tok, h, wg, wu, wd)


def _combine_kernel(p1_hbm, p2_hbm, yb_hbm, mf_ref, r_ref, g_ref, o_ref, i1_smem, i2_smem, g1_ref, g2_ref,
                    sem_idx, sem_rows, *, tiles_per_batch):
    b = pl.program_id(0)
    i = pl.program_id(1)
    t = b * tiles_per_batch + i
    R = o_ref.shape[0]
    c1 = pltpu.make_async_copy(p1_hbm.at[t], i1_smem, sem_idx)
    c2 = pltpu.make_async_copy(p2_hbm.at[t], i2_smem, sem_idx)
    c1.start()
    c2.start()
    c1.wait()
    c2.wait()

    def issue(r, c):
        _row_copy(yb_hbm, i1_smem, g1_ref, sem_rows, r).start(priority=0)
        _row_copy(yb_hbm, i2_smem, g2_ref, sem_rows, r).start(priority=1)
        return c

    lax.fori_loop(0, R, issue, 0, unroll=8)
    pltpu.make_async_copy(yb_hbm.at[pl.ds(0, R), :], g1_ref, sem_rows).wait()
    pltpu.make_async_copy(yb_hbm.at[pl.ds(0, R), :], g2_ref, sem_rows).wait()
    mf = mf_ref[...]
    y = mf[:, 0:1] * g1_ref[...] + mf[:, 1:2] * g2_ref[...]
    o_ref[...] = r_ref[...] + g_ref[...] * y


def _moe_combine(yb, pos, mf, res, mods, k_gate, *, n_ctx, row0, n_rows):
    B, T, _ = mf.shape
    D = yb.shape[1]
    R = MOE_ROWS
    tpb = T // R
    r0 = row0 // R
    sel = _mod_sel((n_ctx - row0) // R if n_ctx > row0 else 0, B)
    p1 = pos[:, 0].reshape(B * tpb, R)
    p2 = pos[:, 1].reshape(B * tpb, R)
    return pl.pallas_call(
        functools.partial(_combine_kernel, tiles_per_batch=tpb),
        grid=(B, n_rows // R),
        in_specs=[pl.BlockSpec(memory_space=pl.ANY), pl.BlockSpec(memory_space=pl.ANY),
                  pl.BlockSpec(memory_space=pl.ANY),
                  pl.BlockSpec((None, R, LANES), lambda b, i: (b, i + r0, 0)),
                  pl.BlockSpec((None, R, D), lambda b, i: (b, i + r0, 0)),
                  pl.BlockSpec((None, None, 1, D), lambda b, i: (sel(b, i), k_gate, 0, 0))],
        out_specs=pl.BlockSpec((None, R, D), lambda b, i: (b, i, 0)),
        out_shape=jax.ShapeDtypeStruct((B, n_rows, D), F32),
        scratch_shapes=[pltpu.SMEM((R,), jnp.int32), pltpu.SMEM((R,), jnp.int32),
                        pltpu.VMEM((R, D), F32), pltpu.VMEM((R, D), F32),
                        pltpu.SemaphoreType.DMA, pltpu.SemaphoreType.DMA],
        compiler_params=_cparams(("arbitrary", "arbitrary")),
        name="moe_combine",
    )(p1, p2, yb, mf, res, mods)


def _hier_moe(routed, mods, w_gate, w_up, w_down, layer, *, n_ctx):
    u, h, mf, mi = routed
    B, T, D = u.shape
    slot_tok, pos, blk_e, n_active = _moe_plan(mi[:, :, :2].reshape(B * T, 2))
    yb = _expert_mlp(h.reshape(B * T, D), slot_tok, blk_e, n_active, w_gate.astype(BF16), w_up.astype(BF16),
                     w_down.astype(BF16), layer)
    return _moe_combine(yb, pos, mf, u, mods, 5, n_ctx=n_ctx, row0=0, n_rows=T)


def _ab_layer(u, mods, gamma, gamma2, wr, w_in, w_out, s5p, s5_d, s5_w_glu, gla_w_gate2, gla_b_gate, gla_norm_g, *,
              n_ctx):
    a_width = s5_w_glu.shape[0]
    n_in = w_in.shape[1]
    n_pad = (n_in + LANES - 1) // LANES * LANES
    w_in_b = jnp.pad(w_in, ((0, 0), (0, n_pad - n_in))).astype(BF16)
    px = _normmod_matmul(u, gamma, mods, 0, 1, w_in_b, n_ctx=n_ctx, tm=256, tn=n_pad // 3)
    G = a_width // S5_GROUP
    ops = _s5_operators(*s5p)
    dsk = jnp.tile(s5_d.astype(F32).reshape(G, 1, S5_GROUP), (1, S5_CHUNK, 1)).reshape(G, 1, S5_CHUNK * S5_GROUP)
    ya = _s5_glu(_s5_scan(px, ops, dsk, a_width=a_width), s5_w_glu.astype(BF16), tm=256)
    yb = _gla(px, gla_w_gate2, gla_b_gate, gla_norm_g, n_ctx=n_ctx, a_width=a_width)
    return _matmul_residual_route([ya, yb], w_out.astype(BF16), u, mods, gamma2, wr, n_ctx=n_ctx, row0=0, tm=256,
                                  tn=1024)


def _na_layer(u, mods, gamma, gamma2, wr, w_qkv, w_out, q_norm, k_norm, rpb, *, n_ctx):
    qkv = _normmod_matmul(u, gamma, mods, 0, 1, w_qkv.astype(BF16), n_ctx=n_ctx, tm=256, tn=2048, out_dtype=BF16)
    o = _na_attention(qkv, q_norm, k_norm, rpb, n_ctx=n_ctx)
    return _matmul_residual_route([o], w_out.astype(BF16), u, mods, gamma2, wr, n_ctx=0, row0=n_ctx, tm=256, tn=1024)


def kernel(x, c, ctx, c_ctx, ada_w, ada_b, norm1_g, norm2_g, ab_w_in, ab_w_out, s5_lam_re, s5_lam_im, s5_log_dt,
           s5_b_re, s5_b_im, s5_c_re, s5_c_im, s5_d, s5_w_glu, gla_w_gate2, gla_b_gate, gla_norm_g, na_w_qkv,
           na_w_out, na_q_norm, na_k_norm, na_rpb, moe_w_route_group, moe_w_route_expert, moe_w_gate, moe_w_up,
           moe_w_down):
    B, seq, D = x.shape
    n_ctx = ctx.shape[1]
    depth = ada_w.shape[0]
    assert depth == 2 and n_ctx == S5_CHUNK * S5_SUPER and seq % (GRID_W * 4) == 0
    u = jnp.concatenate([ctx, x], axis=1)
    cvec = jnp.zeros((8, D), F32).at[:B].set(c).at[B].set(c_ctx)
    mods_all = _ada_mod(cvec, ada_w, ada_b)[:, :B + 1]
    wr = [_router_weights(moe_w_route_group[i], moe_w_route_expert[i]) for i in range(depth)]
    mods = mods_all[0]
    s5p = (s5_lam_re[0], s5_lam_im[0], s5_log_dt[0], s5_b_re[0], s5_b_im[0], s5_c_re[0], s5_c_im[0])
    routed = _ab_layer(u, mods, norm1_g[0], norm2_g[0], wr[0], ab_w_in[0], ab_w_out[0], s5p, s5_d[0], s5_w_glu[0],
                       gla_w_gate2[0], gla_b_gate[0], gla_norm_g[0], n_ctx=n_ctx)
    u = _hier_moe(routed, mods, moe_w_gate, moe_w_up, moe_w_down, 0, n_ctx=n_ctx)
    mods = mods_all[1]
    routed = _na_layer(u, mods, norm1_g[1], norm2_g[1], wr[1], na_w_qkv[0], na_w_out[0], na_q_norm[0], na_k_norm[0],
                       na_rpb[0], n_ctx=n_ctx)
    return _hier_moe(routed, mods, moe_w_gate, moe_w_up, moe_w_down, 1, n_ctx=0)
```

```python
import functools
import math

import jax
import jax.numpy as jnp
from jax import lax
from jax.experimental import pallas as pl
from jax.experimental.pallas import tpu as pltpu

F32 = jnp.float32
BF16 = jnp.bfloat16
HIGHEST = lax.Precision.HIGHEST

V7X_VMEM_BYTES = 64 * 1024 * 1024
VMEM_LIMIT = V7X_VMEM_BYTES - 12 * 1024 * 1024
LANES = 128

EPS = 1e-6
NEG_INF = -1e30
GRID_W = 64
N_MOD = 6
S5_GROUP = 16
S5_STATE = 64
S5_CHUNK = 16
S5_SUPER = 16
GLA_HEADS = 4
GLA_RANK = 16
GLA_TAU = 16.0
GLA_CHUNK = 64
GLA_BLOCK = 256
GLA_HEADS_PER_STEP = 4
NA_HEAD_DIM = 128
NA_KH = 8
NA_KW = 16
NA_QROWS = 4
NA_UROWS = NA_KH + NA_QROWS - 1
ROPE_THETA = 10000.0
MOE_GROUPS = 4
MOE_PER_GROUP = 8
MOE_EXPERTS = MOE_GROUPS * MOE_PER_GROUP
MOE_ROWS = 256


def _cparams(sem):
    return pltpu.CompilerParams(dimension_semantics=sem, vmem_limit_bytes=VMEM_LIMIT)


def _nt(a, b):
    return lax.dot_general(a, b, (((1,), (1,)), ((), ())), preferred_element_type=F32)


CAST_COLS = 1024


def _side_cast_specs(tensors, n_steps, lin):
    if not tensors:
        return [], [], [], []
    views = [t.reshape(t.shape[0], -1, CAST_COLS) for t, _ in tensors]
    rows = views[0].shape[1]
    nblocks = 1 << (n_steps.bit_length() - 1)
    assert all(v.shape[1] == rows for v in views) and rows % (16 * nblocks) == 0

    def slab(*ids):
        return jnp.minimum(lin(*ids), nblocks - 1)

    in_specs = [pl.BlockSpec((None, rows // nblocks, CAST_COLS), functools.partial(
        lambda layer, *ids: (layer, slab(*ids), 0), layer)) for _, layer in tensors]
    out_spec = pl.BlockSpec((rows // nblocks, CAST_COLS), lambda *ids: (slab(*ids), 0))
    return (views, in_specs, [out_spec] * len(views), [jax.ShapeDtypeStruct((rows, CAST_COLS), BF16)] * len(views))


def _side_cast(in_refs, out_refs):
    for i_ref, o_ref in zip(in_refs, out_refs):
        o_ref[...] = i_ref[...].astype(BF16)


def _ada_kernel(c_ref, w_ref, b_ref, o_ref):
    c = c_ref[...]
    s = c * jax.nn.sigmoid(c)
    o_ref[...] = jnp.dot(s.astype(BF16), w_ref[...].astype(BF16), preferred_element_type=F32) + b_ref[...]


def _ada_mod(cvec, w, b):
    R, D = cvec.shape
    L, _, N = w.shape
    tn = 1024
    out = pl.pallas_call(
        _ada_kernel,
        grid=(L, N // tn),
        in_specs=[pl.BlockSpec((R, D), lambda l, j: (0, 0)),
                  pl.BlockSpec((None, D, tn), lambda l, j: (l, 0, j)),
                  pl.BlockSpec((None, 1, tn), lambda l, j: (l, 0, j))],
        out_specs=pl.BlockSpec((None, R, tn), lambda l, j: (l, 0, j)),
        out_shape=jax.ShapeDtypeStruct((L, R, N), F32),
        compiler_params=_cparams(("arbitrary", "arbitrary")),
        name="ada_mod",
    )(cvec, w, b.reshape(L, 1, N))
    return out.reshape(L, R, N_MOD, 1, D)


def _mod_sel(nct, nb):
    if nct == 0:
        return lambda b, i: b
    return lambda b, i: jnp.where(i < nct, nb, b)


def _normmod(x, g, sh, sc):
    y = x * lax.rsqrt(jnp.mean(x * x, axis=-1, keepdims=True) + EPS) * g
    return y * (1.0 + sc) + sh


def _normmod_mm_kernel(u_ref, g_ref, sh_ref, sc_ref, w_ref, o_ref, *, tn):
    xn = _normmod(u_ref[...], g_ref[...], sh_ref[...], sc_ref[...]).astype(BF16)
    for n0 in range(0, o_ref.shape[-1], tn):
        o_ref[:, n0:n0 + tn] = jnp.dot(xn, w_ref[:, n0:n0 + tn], preferred_element_type=F32).astype(o_ref.dtype)


def _normmod_matmul(u, gamma, mods, k_shift, k_scale, w, *, n_ctx, tm, tn, out_dtype=F32):
    B, T, D = u.shape
    N = w.shape[1]
    sel = _mod_sel(n_ctx // tm, B)
    return pl.pallas_call(
        functools.partial(_normmod_mm_kernel, tn=tn),
        grid=(B, T // tm),
        in_specs=[pl.BlockSpec((None, tm, D), lambda b, i: (b, i, 0)),
                  pl.BlockSpec((1, D), lambda b, i: (0, 0)),
                  pl.BlockSpec((None, None, 1, D), lambda b, i: (sel(b, i), k_shift, 0, 0)),
                  pl.BlockSpec((None, None, 1, D), lambda b, i: (sel(b, i), k_scale, 0, 0)),
                  pl.BlockSpec((D, N), lambda b, i: (0, 0), pipeline_mode=pl.Buffered(1))],
        out_specs=pl.BlockSpec((None, tm, N), lambda b, i: (b, i, 0)),
        out_shape=jax.ShapeDtypeStruct((B, T, N), out_dtype),
        compiler_params=_cparams(("arbitrary", "arbitrary")),
        name="normmod_matmul",
    )(u, gamma.reshape(1, D), mods, mods, w)


def _mm_res_kernel(*refs, n_lhs, tn):
    a_refs, w_ref, r_ref, g_ref, o_ref = refs[:n_lhs], refs[n_lhs], refs[n_lhs + 1], refs[n_lhs + 2], refs[n_lhs + 3]
    lhs = [a[...] for a in a_refs]
    for n0 in range(0, o_ref.shape[-1], tn):
        acc, k0 = None, 0
        for a in lhs:
            part = jnp.dot(a, w_ref[k0:k0 + a.shape[1], n0:n0 + tn], preferred_element_type=F32)
            acc = part if acc is None else acc + part
            k0 += a.shape[1]
        o_ref[:, n0:n0 + tn] = r_ref[:, n0:n0 + tn] + g_ref[:, n0:n0 + tn] * acc


def _matmul_residual(lhs, w, res, mods, k_gate, *, n_ctx, row0, tm, tn):
    B, Ta, _ = lhs[0].shape
    K, N = w.shape
    sel = _mod_sel(n_ctx // tm, B)
    r0 = row0 // tm
    return pl.pallas_call(
        functools.partial(_mm_res_kernel, n_lhs=len(lhs), tn=tn),
        grid=(B, Ta // tm),
        in_specs=[pl.BlockSpec((None, tm, a.shape[2]), lambda b, i: (b, i, 0)) for a in lhs]
        + [pl.BlockSpec((K, N), lambda b, i: (0, 0), pipeline_mode=pl.Buffered(1)),
           pl.BlockSpec((None, tm, N), lambda b, i: (b, i + r0, 0)),
           pl.BlockSpec((None, None, 1, N), lambda b, i: (sel(b, i), k_gate, 0, 0))],
        out_specs=pl.BlockSpec((None, tm, N), lambda b, i: (b, i, 0)),
        out_shape=jax.ShapeDtypeStruct((B, Ta, N), F32),
        compiler_params=_cparams(("arbitrary", "arbitrary")),
        name="matmul_residual",
    )(*lhs, w, res, mods)


def _s5_operators(lam_re, lam_im, log_dt, b_re, b_im, c_re, c_im):
    T = S5_CHUNK
    lr = jnp.minimum(lam_re.astype(F32), -1e-4)
    li = lam_im.astype(F32)
    dt = jnp.exp(log_dt.astype(F32))[..., None]
    mag = jnp.exp(lr * dt)
    a_re = mag * jnp.cos(li * dt)
    a_im = mag * jnp.sin(li * dt)
    num_re = a_re - 1.0
    den = lr * lr + li * li
    f_re = (num_re * lr + a_im * li) / den
    f_im = (a_im * lr - num_re * li) / den
    bb_re = f_re[..., None] * b_re.astype(F32) - f_im[..., None] * b_im.astype(F32)
    bb_im = f_re[..., None] * b_im.astype(F32) + f_im[..., None] * b_re.astype(F32)
    cr, ci = c_re.astype(F32), c_im.astype(F32)

    def powers(step, n):
        m = (step * jnp.arange(n + 1, dtype=F32))[:, None, None, None]
        mg = jnp.exp(m * (lr * dt))
        return mg * jnp.cos(m * (li * dt)), mg * jnp.sin(m * (li * dt))

    pr, pi = powers(1, T)
    pr16, pi16 = powers(T, S5_SUPER)
    ca_re = cr[None] * pr[:T, :, :, None, :] - ci[None] * pi[:T, :, :, None, :]
    ca_im = cr[None] * pi[:T, :, :, None, :] + ci[None] * pr[:T, :, :, None, :]
    kk = (jnp.einsum('kdgip,dgpj->dkgij', ca_re, bb_re) - jnp.einsum('kdgip,dgpj->dkgij', ca_im, bb_im))
    G = lr.shape[1]
    I = S5_GROUP
    s_idx = jnp.arange(T)[:, None]
    t_idx = jnp.arange(T)[None, :]

    def toeplitz(kd, lag):
        m = jnp.where((lag >= 0)[:, :, None, None, None], kd[jnp.clip(lag, 0, T - 1)], 0.0)
        return jnp.transpose(m, (2, 0, 4, 1, 3)).reshape(G, T * I, T * I)

    mt = toeplitz(kk[0], t_idx - s_idx) + toeplitz(kk[1], s_idx - t_idx)

    def v_op(d, qr, qi):
        vr = qr[:, :, :, None] * bb_re[d][None] - qi[:, :, :, None] * bb_im[d][None]
        vi = qr[:, :, :, None] * bb_im[d][None] + qi[:, :, :, None] * bb_re[d][None]
        vr = jnp.transpose(vr, (1, 0, 3, 2)).reshape(G, T * I, S5_STATE)
        vi = jnp.transpose(vi, (1, 0, 3, 2)).reshape(G, T * I, S5_STATE)
        return jnp.concatenate([vr, vi, -vi, vr], axis=-1)

    vt = jnp.concatenate([v_op(0, pr[:T, 0][::-1], pi[:T, 0][::-1]), v_op(1, pr[:T, 1], pi[:T, 1])], axis=-1)

    def w_op(d, qr, qi):
        wr = cr[d][None] * qr[:, :, None, :] - ci[d][None] * qi[:, :, None, :]
        wi = cr[d][None] * qi[:, :, None, :] + ci[d][None] * qr[:, :, None, :]
        wr = jnp.transpose(wr, (1, 3, 0, 2)).reshape(G, S5_STATE, T * I)
        wi = jnp.transpose(wi, (1, 3, 0, 2)).reshape(G, S5_STATE, T * I)
        return jnp.concatenate([wr, -wi], axis=1)

    w = jnp.concatenate([w_op(0, pr[1:T + 1, 0], pi[1:T + 1, 0]),
                         w_op(1, pr[1:T + 1, 1][::-1], pi[1:T + 1, 1][::-1])], axis=1)

    def dup(z):
        return jnp.concatenate([z, z], -1)

    rows = []
    for d in range(2):
        rows += [dup(pr16[1, d]), dup(pi16[1, d]), dup(pr16[S5_SUPER, d]), dup(pi16[S5_SUPER, d])]
    dec = jnp.stack(rows, axis=1)
    prs = []
    for d in range(2):
        prs += [dup(jnp.transpose(pr16[:S5_SUPER, d], (1, 0, 2))), dup(jnp.transpose(pi16[:S5_SUPER, d], (1, 0, 2)))]
    pwt = jnp.concatenate(prs, axis=1)
    return mt.astype(BF16), vt.astype(BF16), w.astype(BF16), dec, pwt


S5_LANE_GROUPS = LANES // S5_GROUP


def _piece_transpose(arrs, lane):
    for d in (4, 2, 1):
        bit = ((lane // S5_GROUP) & d) != 0
        new = list(arrs)
        for r in range(S5_LANE_GROUPS):
            if r & d == 0:
                a, b = arrs[r], arrs[r + d]
                new[r] = jnp.where(bit, pltpu.roll(b, S5_GROUP * d, 1), a)
                new[r + d] = jnp.where(bit, b, pltpu.roll(a, LANES - S5_GROUP * d, 1))
        arrs = new
    return arrs


def _s5_kernel(u_ref, mt_ref, vt_ref, w_ref, dec_ref, pw_ref, dsk_ref, y_ref, xg_ref, yg_ref, vv_ref, hin_ref,
               hs_ref, *, nsc):
    nj = S5_SUPER
    nch = u_ref.shape[0] // S5_CHUNK
    ng = S5_LANE_GROUPS
    lane = lax.broadcasted_iota(jnp.int32, (nch, LANES), 1)
    for h in range(2):
        zs = [u_ref[pl.ds(ng * h + t, nch, stride=S5_CHUNK), :] for t in range(ng)]
        for g, a in enumerate(_piece_transpose(zs, lane)):
            xg_ref[g, :, LANES * h:LANES * (h + 1)] = a
    hs_ref[...] = jnp.zeros_like(hs_ref)

    def group(g, carry):
        x = xg_ref[g]
        xb = x.astype(BF16)
        yg_ref[g] = jnp.dot(xb, mt_ref[g], preferred_element_type=F32) + x * dsk_ref[g]
        vv = jnp.dot(xb, vt_ref[g], preferred_element_type=F32)
        for n in range(4):
            vv_ref[n] = vv[:, LANES * n:LANES * (n + 1)]
        dec = dec_ref[g]
        pw = pw_ref[g]
        for d in range(2):
            ar16, ai16, ar256, ai256 = (dec[4 * d + n:4 * d + n + 1] for n in range(4))

            def v(j, d=d):
                return vv_ref[2 * d, pl.ds(j, nsc, stride=nj), :]

            def jv(j, d=d):
                return vv_ref[2 * d + 1, pl.ds(j, nsc, stride=nj), :]

            order = list(range(nj)) if d == 0 else list(range(nj - 1, -1, -1))
            loc = [None] * nj
            l = jnp.zeros((nsc, LANES), F32)
            gg = jnp.zeros((nsc, LANES), F32)
            for n, j in enumerate(order):
                if n > 0:
                    p = order[n - 1]
                    l, gg = ar16 * l + ai16 * gg + v(p), ar16 * gg - ai16 * l + jv(p)
                loc[j] = l
            p = order[-1]
            e, je = ar16 * l + ai16 * gg + v(p), ar16 * gg - ai16 * l + jv(p)
            seq = list(range(nsc)) if d == 0 else [0] + list(range(nsc - 1, 0, -1))
            cur = jnp.zeros((1, LANES), F32)
            jcur = jnp.zeros((1, LANES), F32)
            for m in seq:
                hs_ref[2 * d, m:m + 1, :] = cur
                hs_ref[2 * d + 1, m:m + 1, :] = jcur
                cur, jcur = (ar256 * cur + ai256 * jcur + e[m:m + 1], ar256 * jcur - ai256 * cur + je[m:m + 1])
            hs = hs_ref[2 * d, 0:nsc, :]
            jhs = hs_ref[2 * d + 1, 0:nsc, :]
            for n, j in enumerate(order):
                pr = pw[32 * d + n:32 * d + n + 1]
                pi = pw[32 * d + 16 + n:32 * d + 16 + n + 1]
                hin_ref[d, pl.ds(j, nsc, stride=nj), :] = loc[j] + pr * hs + pi * jhs
        hin = jnp.concatenate([hin_ref[0], hin_ref[1]], axis=1).astype(BF16)
        yg_ref[g] += jnp.dot(hin, w_ref[g], preferred_element_type=F32)
        return carry

    lax.fori_loop(0, ng, group, 0, unroll=2)
    for h in range(2):
        ys = [yg_ref[g, :, LANES * h:LANES * (h + 1)] for g in range(ng)]
        for t, a in enumerate(_piece_transpose(ys, lane)):
            y_ref[pl.ds(ng * h + t, nch, stride=S5_CHUNK), :] = a


def _s5_scan(px, ops, dsk, *, a_width):
    B, T, _ = px.shape
    mt, vt, w, dec, pw = ops
    ng = S5_LANE_GROUPS
    C = S5_CHUNK * S5_GROUP
    nch = T // S5_CHUNK
    nsc = nch // S5_SUPER
    nsp = (nsc + 7) // 8 * 8
    return pl.pallas_call(
        functools.partial(_s5_kernel, nsc=nsc),
        grid=(a_width // LANES, B),
        in_specs=[pl.BlockSpec((None, T, LANES), lambda gb, b: (b, 0, gb)),
                  pl.BlockSpec((ng, C, C), lambda gb, b: (gb, 0, 0)),
                  pl.BlockSpec((ng, C, 2 * C), lambda gb, b: (gb, 0, 0)),
                  pl.BlockSpec((ng, C, C), lambda gb, b: (gb, 0, 0)),
                  pl.BlockSpec((ng, 8, LANES), lambda gb, b: (gb, 0, 0)),
                  pl.BlockSpec((ng, 64, LANES), lambda gb, b: (gb, 0, 0)),
                  pl.BlockSpec((ng, 1, C), lambda gb, b: (gb, 0, 0))],
        out_specs=pl.BlockSpec((None, T, LANES), lambda gb, b: (b, 0, gb)),
        out_shape=jax.ShapeDtypeStruct((B, T, a_width), F32),
        scratch_shapes=[pltpu.VMEM((ng, nch, C), F32), pltpu.VMEM((ng, nch, C), F32),
                        pltpu.VMEM((4, nch, LANES), F32), pltpu.VMEM((2, nch, LANES), F32),
                        pltpu.VMEM((4, nsp, LANES), F32)],
        compiler_params=_cparams(("arbitrary", "arbitrary")),
        name="s5_scan",
    )(px, mt, vt, w, dec, pw, dsk)


def _s5_glu_kernel(y_ref, w_ref, o_ref):
    g = jax.nn.gelu(y_ref[...])
    z = jnp.dot(g.astype(BF16), w_ref[...], preferred_element_type=F32)
    o_ref[...] = (g * jax.nn.sigmoid(z)).astype(o_ref.dtype)


def _s5_glu(y, w, *, tm):
    B, T, A = y.shape
    return pl.pallas_call(
        _s5_glu_kernel,
        grid=(B, T // tm),
        in_specs=[pl.BlockSpec((None, tm, A), lambda b, i: (b, i, 0)),
                  pl.BlockSpec((A, A), lambda b, i: (0, 0))],
        out_specs=pl.BlockSpec((None, tm, A), lambda b, i: (b, i, 0)),
        out_shape=jax.ShapeDtypeStruct((B, T, A), BF16),
        compiler_params=_cparams(("arbitrary", "arbitrary")),
        name="s5_glu",
    )(y, w)


def _log_sigmoid(z):
    return jnp.minimum(z, 0.0) - jnp.log(1.0 + jnp.exp(-jnp.abs(z)))


def _gla_kernel(*refs, nblk, dk, dv, hp, n_cast):
    q_ref, k_ref, v_ref, r_ref, lr_ref, cos_ref, sin_ref, wg_ref, bg_ref, ng_ref = refs[:10]
    cast_in = refs[10:10 + n_cast]
    o_ref = refs[10 + n_cast]
    cast_out = refs[11 + n_cast:11 + 2 * n_cast]
    s_ref, of_ref = refs[11 + 2 * n_cast:]
    p = pl.program_id(2)
    s = pl.program_id(3)
    C = GLA_CHUNK
    nch = GLA_BLOCK // C
    blk = jnp.where(p == 0, s, jnp.where(s == 0, 0, nblk - s))

    @pl.when(s == 0)
    def _():
        s_ref[...] = jnp.zeros_like(s_ref)

    NB = GLA_BLOCK
    lane = lax.broadcasted_iota(jnp.int32, (NB, dk), 1)
    lower = (lane & 63) < 32
    rpos = lax.broadcasted_iota(jnp.int32, (NB, dk), 0) & (C - 1)
    row = lax.broadcasted_iota(jnp.int32, (NB, NB), 0)
    col = lax.broadcasted_iota(jnp.int32, (NB, NB), 1)
    same_chunk = (row // C) == (col // C)
    eye = (lax.broadcasted_iota(jnp.int32, (dk, dk), 0) == lax.broadcasted_iota(jnp.int32, (dk, dk), 1)).astype(BF16)

    def rope(z, cos, sin):
        return z * cos + jnp.where(lower, pltpu.roll(z, dk - 32, 1), pltpu.roll(z, 32, 1)) * sin

    def run(reverse):
        _side_cast(cast_in, cast_out)
        for hd in range(hp):
            head(reverse, hd)

    def head(reverse, hd):
        mask = same_chunk & ((row <= col) if reverse else (row >= col))
        cos = cos_ref[...]
        sin = sin_ref[...]
        kcols = slice(hd * dk, (hd + 1) * dk)
        vcols = slice(hd * dv, (hd + 1) * dv)
        q = rope(q_ref[:, kcols], cos, sin) * (dk ** -0.5)
        k = rope(k_ref[:, kcols], cos, sin)
        vb = v_ref[:, vcols].astype(BF16)
        lrb = lr_ref[...].astype(BF16)
        la = _log_sigmoid(jnp.dot(lrb, wg_ref[hd], preferred_element_type=F32) + bg_ref[hd]) * (1.0 / GLA_TAU)
        b = la
        sh = 1
        while sh < C:
            if reverse:
                b = b + jnp.where(rpos < C - sh, pltpu.roll(b, NB - sh, 0), 0.0)
            else:
                b = b + jnp.where(rpos >= sh, pltpu.roll(b, sh, 0), 0.0)
            sh *= 2
        last = C - 1 if not reverse else 0
        tots = [b[c * C + last:c * C + last + 1] for c in range(nch)]
        btot = jnp.concatenate([jnp.broadcast_to(t, (C, dk)) for t in tots], 0)
        dmat = jnp.exp(jnp.concatenate(tots + [jnp.zeros((8 - nch, dk), F32)], 0).T)
        qi = (q * jnp.exp(b)).astype(BF16)
        ki = (k * jnp.exp(-b)).astype(BF16)
        ko = (k * jnp.exp(btot - b)).astype(BF16)
        att = jnp.where(mask, _nt(qi, ki), 0.0).astype(BF16)
        o = jnp.dot(att, vb, preferred_element_type=F32)
        upd, dcol = [], []
        for c in range(nch):
            kot = _nt(eye, ko[c * C:(c + 1) * C]).astype(BF16)
            upd.append(jnp.dot(kot, vb[c * C:(c + 1) * C], preferred_element_type=F32))
            dcol.append(dmat[:, c:c + 1])
        st = s_ref[hd]
        o_state = [None] * nch
        for c in (range(nch - 1, -1, -1) if reverse else range(nch)):
            o_state[c] = jnp.dot(qi[c * C:(c + 1) * C], st.astype(BF16), preferred_element_type=F32)
            st = st * dcol[c] + upd[c]
        s_ref[hd] = st
        o = o + jnp.concatenate(o_state, axis=0)
        tok = pl.ds(pl.multiple_of(blk * NB, NB), NB)
        if not reverse:
            of_ref[tok, vcols] = o
        else:
            o = o + of_ref[tok, vcols]
            o = o * lax.rsqrt(jnp.mean(o * o, axis=-1, keepdims=True) + EPS) * ng_ref[hd]
            r = r_ref[:, vcols]
            o_ref[:, vcols] = (o * (r * jax.nn.sigmoid(r))).astype(o_ref.dtype)

    @pl.when(p == 0)
    def _():
        run(False)

    @pl.when(p == 1)
    def _():
        run(True)


def _rope_tables(n_ctx, seq, dk):
    nf = dk // 4
    inv = ROPE_THETA ** (-jnp.arange(nf, dtype=F32) / nf)
    pos = jnp.arange(seq)
    ang_r = (pos // GRID_W).astype(F32)[:, None] * inv[None, :]
    ang_c = (pos % GRID_W).astype(F32)[:, None] * inv[None, :]
    cos = jnp.concatenate([jnp.cos(ang_r)] * 2 + [jnp.cos(ang_c)] * 2, axis=-1)
    sin = jnp.concatenate([-jnp.sin(ang_r), jnp.sin(ang_r), -jnp.sin(ang_c), jnp.sin(ang_c)], axis=-1)
    cos = jnp.concatenate([jnp.ones((n_ctx, dk), F32), cos], axis=0)
    sin = jnp.concatenate([jnp.zeros((n_ctx, dk), F32), sin], axis=0)
    return cos, sin


def _gla(px, w_gate2, b_gate, norm_g, cast, *, n_ctx, a_width):
    B, T, _ = px.shape
    H = GLA_HEADS
    qk = w_gate2.shape[-1]
    dk = qk // H
    bw = norm_g.shape[-1]
    dv = bw // H
    nblk = T // GLA_BLOCK
    cos, sin = _rope_tables(n_ctx, T - n_ctx, dk)
    lr_col0 = a_width + 2 * qk + 2 * bw
    wg = jnp.zeros((2, H, LANES, dk), F32)
    for d in range(2):
        wg = wg.at[d, :, d * GLA_RANK:(d + 1) * GLA_RANK, :].set(
            jnp.transpose(w_gate2[d].reshape(GLA_RANK, H, dk), (1, 0, 2)))
    wg = wg.astype(BF16)
    bg = b_gate.astype(F32).reshape(2, H, 1, dk)
    ng = norm_g.astype(F32).reshape(H, 1, dv)
    hp = GLA_HEADS_PER_STEP

    def blk(p, s):
        return jnp.where(p == 0, s, jnp.where(s == 0, 0, nblk - s))

    kw, vw = hp * dk, hp * dv
    qb, kb = a_width // kw, (a_width + qk) // kw
    vbk, rbk = (a_width + 2 * qk) // vw, (a_width + 2 * qk + bw) // vw
    lb = lr_col0 // LANES
    ngrp = H // hp
    views, c_in, c_out, c_shapes = _side_cast_specs(
        cast, B * ngrp * 2 * nblk, lambda b, g, p, s: ((b * ngrp + g) * 2 + p) * nblk + s)
    outs = pl.pallas_call(
        functools.partial(_gla_kernel, nblk=nblk, dk=dk, dv=dv, hp=hp, n_cast=len(views)),
        grid=(B, ngrp, 2, nblk),
        in_specs=[pl.BlockSpec((None, GLA_BLOCK, kw), lambda b, g, p, s: (b, blk(p, s), qb + g)),
                  pl.BlockSpec((None, GLA_BLOCK, kw), lambda b, g, p, s: (b, blk(p, s), kb + g)),
                  pl.BlockSpec((None, GLA_BLOCK, vw), lambda b, g, p, s: (b, blk(p, s), vbk + g)),
                  pl.BlockSpec((None, GLA_BLOCK, vw), lambda b, g, p, s: (b, blk(p, s), rbk + g)),
                  pl.BlockSpec((None, GLA_BLOCK, LANES), lambda b, g, p, s: (b, blk(p, s), lb)),
                  pl.BlockSpec((GLA_BLOCK, dk), lambda b, g, p, s: (blk(p, s), 0)),
                  pl.BlockSpec((GLA_BLOCK, dk), lambda b, g, p, s: (blk(p, s), 0)),
                  pl.BlockSpec((None, hp, LANES, dk), lambda b, g, p, s: (p, g, 0, 0)),
                  pl.BlockSpec((None, hp, 1, dk), lambda b, g, p, s: (p, g, 0, 0)),
                  pl.BlockSpec((hp, 1, dv), lambda b, g, p, s: (g, 0, 0))] + c_in,
        out_specs=[pl.BlockSpec((None, GLA_BLOCK, vw),
                                lambda b, g, p, s: (b, jnp.where(p == 0, 0, blk(p, s)), g))] + c_out,
        out_shape=[jax.ShapeDtypeStruct((B, T, bw), BF16)] + c_shapes,
        scratch_shapes=[pltpu.VMEM((hp, dk, dv), F32), pltpu.VMEM((T, vw), F32)],
        compiler_params=_cparams(("arbitrary", "arbitrary", "arbitrary", "arbitrary")),
        name="gla",
    )(px, px, px, px, px, cos, sin, wg, bg, ng, *views)
    return outs[0], outs[1:]


def _na_kernel(*refs, n_ctx, rows, n_cast):
    q_ref, k_ref, v_ref, bias_ref, qn_ref, kn_ref = refs[:6]
    o_ref = refs[6 + n_cast]
    kb_ref, vb_ref = refs[7 + 2 * n_cast:]
    _side_cast(refs[6:6 + n_cast], refs[7 + n_cast:7 + 2 * n_cast])
    hd = NA_HEAD_DIM
    W = GRID_W
    T = k_ref.shape[0]
    step = n_ctx

    def prep(i, c):
        sl = pl.ds(pl.multiple_of(i * step, step), step)
        kk = k_ref[sl, :]
        kb_ref[sl, :] = (kk * lax.rsqrt(jnp.mean(kk * kk, axis=-1, keepdims=True) + EPS) * kn_ref[...]).astype(BF16)
        vb_ref[sl, :] = v_ref[sl, :].astype(BF16)
        return c

    lax.fori_loop(0, T // step, prep, 0)
    scale = hd ** -0.5
    ngrp = rows // NA_QROWS
    nq = NA_QROWS * W
    nk = NA_UROWS * W

    def group(g, c):
        r0 = g * NA_QROWS
        ustart = jnp.clip(r0 - NA_KH // 2, 0, rows - NA_UROWS)
        case = jnp.where(g == 0, 0, jnp.where(g == ngrp - 1, 2, 1))
        q = q_ref[pl.ds(pl.multiple_of(n_ctx + r0 * W, W), nq), :]
        q = (q * lax.rsqrt(jnp.mean(q * q, axis=-1, keepdims=True) + EPS) * qn_ref[...] * scale).astype(BF16)
        ws = pl.ds(pl.multiple_of(n_ctx + ustart * W, W), nk)
        s_lat = _nt(q, kb_ref[ws, :]) + bias_ref[case]
        s_ctx = _nt(q, kb_ref[0:n_ctx, :])
        m = jnp.maximum(jnp.max(s_lat, axis=-1, keepdims=True), jnp.max(s_ctx, axis=-1, keepdims=True))
        p_lat = jnp.exp(s_lat - m)
        p_ctx = jnp.exp(s_ctx - m)
        den = jnp.sum(p_lat, axis=-1, keepdims=True) + jnp.sum(p_ctx, axis=-1, keepdims=True)
        o = (jnp.dot(p_lat.astype(BF16), vb_ref[ws, :], preferred_element_type=F32)
             + jnp.dot(p_ctx.astype(BF16), vb_ref[0:n_ctx, :], preferred_element_type=F32))
        o_ref[pl.ds(pl.multiple_of(r0 * W, W), nq), :] = (o / den).astype(o_ref.dtype)
        return c

    lax.fori_loop(0, ngrp, group, 0, unroll=4)


def _na_bias(rpb, rows):
    H = rpb.shape[0]
    col = jnp.arange(GRID_W)
    cs = jnp.clip(col - NA_KW // 2, 0, GRID_W - NA_KW)
    col_ok = (col[None, :] >= cs[:, None]) & (col[None, :] < cs[:, None] + NA_KW)
    dc_idx = jnp.clip(col[None, :] - col[:, None] + NA_KW - 1, 0, 2 * NA_KW - 2)
    rpb_c = jnp.where(col_ok[None, None], rpb.astype(F32)[:, :, dc_idx], NEG_INF)
    ngrp = rows // NA_QROWS
    tables = []
    for g in (0, 1, ngrp - 1):
        r0 = g * NA_QROWS
        ustart = min(max(r0 - NA_KH // 2, 0), rows - NA_UROWS)
        qr = r0 + jnp.arange(NA_QROWS)
        start = jnp.clip(qr - NA_KH // 2, 0, rows - NA_KH)
        kr = ustart + jnp.arange(NA_UROWS)
        ok = (kr[None, :] >= start[:, None]) & (kr[None, :] < start[:, None] + NA_KH)
        idx = jnp.clip(kr[None, :] - qr[:, None] + NA_KH - 1, 0, 2 * NA_KH - 2)
        t = jnp.where(ok[None, :, :, None, None], rpb_c[:, idx], NEG_INF)
        tables.append(jnp.transpose(t, (0, 1, 3, 2, 4)).reshape(H, NA_QROWS * GRID_W, NA_UROWS * GRID_W))
    return jnp.stack(tables, axis=1)


def _na_attention(qkv, q_norm, k_norm, rpb, cast, *, n_ctx):
    B, T, D3 = qkv.shape
    D = D3 // 3
    H = D // NA_HEAD_DIM
    seq = T - n_ctx
    rows = seq // GRID_W
    assert rows % NA_QROWS == 0 and rows >= NA_UROWS + NA_QROWS
    bias = _na_bias(rpb, rows)
    hd = NA_HEAD_DIM
    views, c_in, c_out, c_shapes = _side_cast_specs(cast, B * H, lambda b, h: b * H + h)
    outs = pl.pallas_call(
        functools.partial(_na_kernel, n_ctx=n_ctx, rows=rows, n_cast=len(views)),
        grid=(B, H),
        in_specs=[pl.BlockSpec((None, T, hd), lambda b, h: (b, 0, h)),
                  pl.BlockSpec((None, T, hd), lambda b, h: (b, 0, H + h)),
                  pl.BlockSpec((None, T, hd), lambda b, h: (b, 0, 2 * H + h)),
                  pl.BlockSpec((None, 3, NA_QROWS * GRID_W, NA_UROWS * GRID_W), lambda b, h: (h, 0, 0, 0)),
                  pl.BlockSpec((1, hd), lambda b, h: (0, 0)),
                  pl.BlockSpec((1, hd), lambda b, h: (0, 0))] + c_in,
        out_specs=[pl.BlockSpec((None, seq, hd), lambda b, h: (b, 0, h))] + c_out,
        out_shape=[jax.ShapeDtypeStruct((B, seq, D), BF16)] + c_shapes,
        scratch_shapes=[pltpu.VMEM((T, hd), BF16), pltpu.VMEM((T, hd), BF16)],
        compiler_params=_cparams(("arbitrary", "arbitrary")),
        name="na_attention",
    )(qkv, qkv, qkv, bias, q_norm.astype(F32).reshape(1, hd), k_norm.astype(F32).reshape(1, hd), *views)
    return outs[0], outs[1:]


def _router_kernel(u_ref, g_ref, sh_ref, sc_ref, wr_ref, h_ref, mf_ref, mi_ref):
    h = _normmod(u_ref[...], g_ref[...], sh_ref[...], sc_ref[...])
    h_ref[...] = h
    logits = jnp.dot(h, wr_ref[...], precision=HIGHEST, preferred_element_type=F32)
    lane = lax.broadcasted_iota(jnp.int32, logits.shape, 1).astype(F32)
    big = 1e9

    def first_max(vals):
        m = jnp.max(vals, axis=-1, keepdims=True)
        return m, jnp.min(jnp.where(vals == m, lane, big), axis=-1, keepdims=True)

    gl = jnp.where(lane < MOE_GROUPS, logits, -jnp.inf)
    gmax, gidx = first_max(gl)
    p_top = 1.0 / jnp.sum(jnp.exp(gl - gmax), axis=-1, keepdims=True)
    lo = MOE_GROUPS + MOE_PER_GROUP * gidx
    el = jnp.where((lane >= lo) & (lane < lo + MOE_PER_GROUP), logits, -jnp.inf)
    v1, i1 = first_max(el)
    v2, i2 = first_max(jnp.where(lane == i1, -jnp.inf, el))
    e2 = jnp.exp(v2 - v1)
    w1 = p_top / (1.0 + e2)
    w2 = p_top * e2 / (1.0 + e2)
    mf_ref[...] = jnp.where(lane == 0, w1, jnp.where(lane == 1, w2, 0.0))
    mi_ref[...] = jnp.where(lane == 0, i1 - MOE_GROUPS, jnp.where(lane == 1, i2 - MOE_GROUPS, 0.0)).astype(jnp.int32)


def _router(u, gamma, mods, k_shift, k_scale, wr, *, n_ctx, tm):
    B, T, D = u.shape
    sel = _mod_sel(n_ctx // tm, B)
    return pl.pallas_call(
        _router_kernel,
        grid=(B, T // tm),
        in_specs=[pl.BlockSpec((None, tm, D), lambda b, i: (b, i, 0)),
                  pl.BlockSpec((1, D), lambda b, i: (0, 0)),
                  pl.BlockSpec((None, None, 1, D), lambda b, i: (sel(b, i), k_shift, 0, 0)),
                  pl.BlockSpec((None, None, 1, D), lambda b, i: (sel(b, i), k_scale, 0, 0)),
                  pl.BlockSpec((D, LANES), lambda b, i: (0, 0))],
        out_specs=[pl.BlockSpec((None, tm, D), lambda b, i: (b, i, 0)),
                   pl.BlockSpec((None, tm, LANES), lambda b, i: (b, i, 0)),
                   pl.BlockSpec((None, tm, LANES), lambda b, i: (b, i, 0))],
        out_shape=[jax.ShapeDtypeStruct((B, T, D), F32),
                   jax.ShapeDtypeStruct((B, T, LANES), F32),
                   jax.ShapeDtypeStruct((B, T, LANES), jnp.int32)],
        compiler_params=_cparams(("arbitrary", "arbitrary")),
        name="moe_router",
    )(u, gamma.reshape(1, D), mods, mods, wr)


def _moe_plan(eid2):
    N = eid2.shape[0]
    M = 2 * N
    E = MOE_EXPERTS
    eid = eid2.reshape(-1)
    onehot = (eid[:, None] == jnp.arange(E, dtype=jnp.int32)[None, :]).astype(jnp.int32)
    csum = jnp.cumsum(onehot, axis=0)
    rank = jnp.sum(onehot * csum, axis=1) - 1
    counts = csum[-1]
    padded = (counts + MOE_ROWS - 1) // MOE_ROWS * MOE_ROWS
    pend = jnp.cumsum(padded)
    pstart = pend - padded
    dest = (jnp.sum(onehot * pstart[None, :], axis=1) + rank).astype(jnp.int32)
    P = (M + E * (MOE_ROWS - 1) + MOE_ROWS - 1) // MOE_ROWS * MOE_ROWS
    nb = P // MOE_ROWS
    slot_tok = jnp.zeros((P,), jnp.int32).at[dest].set(jnp.arange(M, dtype=jnp.int32) // 2)
    pos = dest.reshape(N, 2)
    blk_start = jnp.arange(nb, dtype=jnp.int32) * MOE_ROWS
    blk_e = jnp.minimum(jnp.sum((pend[None, :] <= blk_start[:, None]).astype(jnp.int32), axis=1), E - 1)
    n_active = (pend[-1] // MOE_ROWS).astype(jnp.int32).reshape(1)
    return slot_tok.reshape(nb, MOE_ROWS), pos, blk_e.astype(jnp.int32), n_active


def _row_copy(src_hbm, idx_smem, dst_ref, sem, r):
    return pltpu.make_async_copy(src_hbm.at[pl.ds(idx_smem[r], 1), :], dst_ref.at[pl.ds(r, 1), :], sem)


def _expert_kernel(blk_e_ref, nact_ref, idx_hbm, h_hbm, wg_ref, wu_ref, wd_ref, o_ref, idx0, idx1, xbuf, sem_idx,
                   sem_rows):
    i = pl.program_id(0)
    nact = nact_ref[0]
    R = MOE_ROWS
    slot = lax.rem(i, 2)
    idx_bufs = (idx0, idx1)

    def idx_copy(blk, s):
        return pltpu.make_async_copy(idx_hbm.at[blk], idx_bufs[s], sem_idx.at[s])

    def issue_rows(s):
        def body(r, c):
            _row_copy(h_hbm, idx_bufs[s], xbuf.at[s], sem_rows.at[s], r).start(priority=1)
            return c

        lax.fori_loop(0, R, body, 0, unroll=8)

    @pl.when(i == 0)
    def _():
        first = idx_copy(0, 0)
        first.start()
        first.wait()
        issue_rows(0)

        @pl.when(nact > 1)
        def _():
            idx_copy(1, 1).start()

    def prefetch(s):
        @pl.when(i + 1 < nact)
        def _():
            idx_copy(i + 1, 1 - s).wait()
            issue_rows(1 - s)

        @pl.when(i + 2 < nact)
        def _():
            idx_copy(i + 2, s).start()

    @pl.when(slot == 0)
    def _():
        prefetch(0)

    @pl.when(slot == 1)
    def _():
        prefetch(1)

    @pl.when(i < nact)
    def _():
        pltpu.make_async_copy(h_hbm.at[pl.ds(0, R), :], xbuf.at[slot], sem_rows.at[slot]).wait()
        xb = xbuf[slot].astype(BF16)
        g = jnp.dot(xb, wg_ref[...], preferred_element_type=F32)
        u = jnp.dot(xb, wu_ref[...], preferred_element_type=F32)
        hmid = (g * jax.nn.sigmoid(g) * u).astype(BF16)
        o_ref[...] = jnp.dot(hmid, wd_ref[...], preferred_element_type=F32)

    @pl.when(i >= nact)
    def _():
        o_ref[...] = jnp.zeros_like(o_ref)


def _expert_mlp(h, slot_tok, blk_e, n_active, wg, wu, wd):
    nb, R = slot_tok.shape
    D = h.shape[1]
    F = wg.shape[2]
    grid_spec = pltpu.PrefetchScalarGridSpec(
        num_scalar_prefetch=2,
        grid=(nb,),
        in_specs=[pl.BlockSpec(memory_space=pl.ANY),
                  pl.BlockSpec(memory_space=pl.ANY),
                  pl.BlockSpec((None, D, F), lambda i, be, na: (be[i], 0, 0)),
                  pl.BlockSpec((None, D, F), lambda i, be, na: (be[i], 0, 0)),
                  pl.BlockSpec((None, F, D), lambda i, be, na: (be[i], 0, 0))],
        out_specs=pl.BlockSpec((R, D), lambda i, be, na: (i, 0)),
        scratch_shapes=[pltpu.SMEM((R,), jnp.int32), pltpu.SMEM((R,), jnp.int32), pltpu.VMEM((2, R, D), F32),
                        pltpu.SemaphoreType.DMA((2,)), pltpu.SemaphoreType.DMA((2,))],
    )
    return pl.pallas_call(
        _expert_kernel,
        grid_spec=grid_spec,
        out_shape=jax.ShapeDtypeStruct((nb * R, D), F32),
        compiler_params=_cparams(("arbitrary",)),
        name="moe_experts",
    )(blk_e, n_active, slot_tok, h, wg, wu, wd)


def _combine_kernel(*refs, tiles_per_batch, n_cast):
    p1_hbm, p2_hbm, yb_hbm, mf_ref, r_ref, g_ref = refs[:6]
    o_ref = refs[6 + n_cast]
    i1_smem, i2_smem, g1_ref, g2_ref, sem_idx, sem_rows = refs[7 + 2 * n_cast:]
    b = pl.program_id(0)
    i = pl.program_id(1)
    t = b * tiles_per_batch + i
    R = o_ref.shape[0]
    c1 = pltpu.make_async_copy(p1_hbm.at[t], i1_smem, sem_idx)
    c2 = pltpu.make_async_copy(p2_hbm.at[t], i2_smem, sem_idx)
    c1.start()
    c2.start()
    c1.wait()
    c2.wait()

    def issue(r, c):
        _row_copy(yb_hbm, i1_smem, g1_ref, sem_rows, r).start(priority=0)
        _row_copy(yb_hbm, i2_smem, g2_ref, sem_rows, r).start(priority=1)
        return c

    lax.fori_loop(0, R, issue, 0, unroll=8)
    _side_cast(refs[6:6 + n_cast], refs[7 + n_cast:7 + 2 * n_cast])
    pltpu.make_async_copy(yb_hbm.at[pl.ds(0, R), :], g1_ref, sem_rows).wait()
    pltpu.make_async_copy(yb_hbm.at[pl.ds(0, R), :], g2_ref, sem_rows).wait()
    mf = mf_ref[...]
    y = mf[:, 0:1] * g1_ref[...] + mf[:, 1:2] * g2_ref[...]
    o_ref[...] = r_ref[...] + g_ref[...] * y


def _moe_combine(yb, pos, mf, res, mods, k_gate, cast, *, n_ctx, row0, n_rows):
    B, T, _ = mf.shape
    D = yb.shape[1]
    R = MOE_ROWS
    tpb = T // R
    r0 = row0 // R
    sel = _mod_sel((n_ctx - row0) // R if n_ctx > row0 else 0, B)
    p1 = pos[:, 0].reshape(B * tpb, R)
    p2 = pos[:, 1].reshape(B * tpb, R)
    nt = n_rows // R
    views, c_in, c_out, c_shapes = _side_cast_specs(cast, B * nt, lambda b, i: b * nt + i)
    outs = pl.pallas_call(
        functools.partial(_combine_kernel, tiles_per_batch=tpb, n_cast=len(views)),
        grid=(B, nt),
        in_specs=[pl.BlockSpec(memory_space=pl.ANY), pl.BlockSpec(memory_space=pl.ANY),
                  pl.BlockSpec(memory_space=pl.ANY),
                  pl.BlockSpec((None, R, LANES), lambda b, i: (b, i + r0, 0)),
                  pl.BlockSpec((None, R, D), lambda b, i: (b, i + r0, 0)),
                  pl.BlockSpec((None, None, 1, D), lambda b, i: (sel(b, i), k_gate, 0, 0))] + c_in,
        out_specs=[pl.BlockSpec((None, R, D), lambda b, i: (b, i, 0))] + c_out,
        out_shape=[jax.ShapeDtypeStruct((B, n_rows, D), F32)] + c_shapes,
        scratch_shapes=[pltpu.SMEM((R,), jnp.int32), pltpu.SMEM((R,), jnp.int32),
                        pltpu.VMEM((R, D), F32), pltpu.VMEM((R, D), F32),
                        pltpu.SemaphoreType.DMA, pltpu.SemaphoreType.DMA],
        compiler_params=_cparams(("arbitrary", "arbitrary")),
        name="moe_combine",
    )(p1, p2, yb, mf, res, mods, *views)
    return outs[0], outs[1:]


def _hier_moe(u, gamma, mods, w_route_group, w_route_expert, wg, wu, wd, cast, *, n_ctx, row0, n_rows):
    B, T, D = u.shape
    E = MOE_EXPERTS
    wr = jnp.zeros((D, LANES), F32)
    wr = wr.at[:, :MOE_GROUPS].set(w_route_group.astype(F32))
    wr = wr.at[:, MOE_GROUPS:MOE_GROUPS + MOE_EXPERTS].set(w_route_expert.astype(F32))
    h, mf, mi = _router(u, gamma, mods, 3, 4, wr, n_ctx=n_ctx, tm=MOE_ROWS)
    slot_tok, pos, blk_e, n_active = _moe_plan(mi[:, :, :2].reshape(B * T, 2))
    yb = _expert_mlp(h.reshape(B * T, D), slot_tok, blk_e, n_active, wg.reshape(E, D, -1), wu.reshape(E, D, -1),
                     wd.reshape(E, -1, D))
    return _moe_combine(yb, pos, mf, u, mods, 5, cast, n_ctx=n_ctx, row0=row0, n_rows=n_rows)


def _ab_layer(u, mods, gamma, w_in, w_out, s5p, s5_d, s5_w_glu, gla_w_gate2, gla_b_gate, gla_norm_g, cast, *, n_ctx):
    a_width = s5_w_glu.shape[0]
    n_in = w_in.shape[1]
    n_pad = (n_in + LANES - 1) // LANES * LANES
    w_in_b = jnp.pad(w_in, ((0, 0), (0, n_pad - n_in))).astype(BF16)
    px = _normmod_matmul(u, gamma, mods, 0, 1, w_in_b, n_ctx=n_ctx, tm=256, tn=n_pad // 3)
    G = a_width // S5_GROUP
    ops = _s5_operators(*s5p)
    dsk = jnp.tile(s5_d.astype(F32).reshape(G, 1, S5_GROUP), (1, S5_CHUNK, 1)).reshape(G, 1, S5_CHUNK * S5_GROUP)
    ya = _s5_glu(_s5_scan(px, ops, dsk, a_width=a_width), s5_w_glu.astype(BF16), tm=256)
    yb, casted = _gla(px, gla_w_gate2, gla_b_gate, gla_norm_g, cast, n_ctx=n_ctx, a_width=a_width)
    out = _matmul_residual([ya, yb], w_out.astype(BF16), u, mods, 2, n_ctx=n_ctx, row0=0, tm=256, tn=1024)
    return out, casted


def _na_layer(u, mods, gamma, w_qkv, w_out, q_norm, k_norm, rpb, cast, *, n_ctx):
    qkv = _normmod_matmul(u, gamma, mods, 0, 1, w_qkv.astype(BF16), n_ctx=n_ctx, tm=256, tn=2048)
    o, casted = _na_attention(qkv, q_norm, k_norm, rpb, cast, n_ctx=n_ctx)
    out = _matmul_residual([o], w_out.astype(BF16), u, mods, 2, n_ctx=0, row0=n_ctx, tm=256, tn=1024)
    return out, casted


def kernel(x, c, ctx, c_ctx, ada_w, ada_b, norm1_g, norm2_g, ab_w_in, ab_w_out, s5_lam_re, s5_lam_im, s5_log_dt,
           s5_b_re, s5_b_im, s5_c_re, s5_c_im, s5_d, s5_w_glu, gla_w_gate2, gla_b_gate, gla_norm_g, na_w_qkv,
           na_w_out, na_q_norm, na_k_norm, na_rpb, moe_w_route_group, moe_w_route_expert, moe_w_gate, moe_w_up,
           moe_w_down):
    B, seq, D = x.shape
    n_ctx = ctx.shape[1]
    depth = ada_w.shape[0]
    assert depth == 2 and n_ctx == S5_CHUNK * S5_SUPER and seq % (GRID_W * 4) == 0
    u = jnp.concatenate([ctx, x], axis=1)
    cvec = jnp.zeros((8, D), F32).at[:B].set(c).at[B].set(c_ctx)
    mods_all = _ada_mod(cvec, ada_w, ada_b)[:, :B + 1]
    mods = mods_all[0]
    s5p = (s5_lam_re[0], s5_lam_im[0], s5_log_dt[0], s5_b_re[0], s5_b_im[0], s5_c_re[0], s5_c_im[0])
    u, (wg0, wu0, wd0) = _ab_layer(u, mods, norm1_g[0], ab_w_in[0], ab_w_out[0], s5p, s5_d[0], s5_w_glu[0],
                                   gla_w_gate2[0], gla_b_gate[0], gla_norm_g[0],
                                   [(moe_w_gate, 0), (moe_w_up, 0), (moe_w_down, 0)], n_ctx=n_ctx)
    u, (wd1,) = _hier_moe(u, norm2_g[0], mods, moe_w_route_group[0], moe_w_route_expert[0], wg0, wu0, wd0,
                          [(moe_w_down, 1)], n_ctx=n_ctx, row0=0, n_rows=n_ctx + seq)
    mods = mods_all[1]
    xl, (wg1, wu1) = _na_layer(u, mods, norm1_g[1], na_w_qkv[0], na_w_out[0], na_q_norm[0], na_k_norm[0], na_rpb[0],
                               [(moe_w_gate, 1), (moe_w_up, 1)], n_ctx=n_ctx)
    out, _ = _hier_moe(xl, norm2_g[1], mods, moe_w_route_group[1], moe_w_route_expert[1], wg1, wu1, wd1, [],
                       n_ctx=0, row0=0, n_rows=seq)
    return out
```

```python
import functools
import math

import jax
import jax.numpy as jnp
from jax import lax
from jax.experimental import pallas as pl
from jax.experimental.pallas import tpu as pltpu

F32 = jnp.float32
BF16 = jnp.bfloat16
HIGHEST = lax.Precision.HIGHEST

V7X_VMEM_BYTES = 64 * 1024 * 1024
VMEM_LIMIT = V7X_VMEM_BYTES - 12 * 1024 * 1024
LANES = 128

EPS = 1e-6
NEG_INF = -1e30
GRID_W = 64
N_MOD = 6
S5_GROUP = 16
S5_STATE = 64
S5_CHUNK = 16
S5_SUPER = 16
GLA_HEADS = 4
GLA_RANK = 16
GLA_TAU = 16.0
GLA_CHUNK = 64
GLA_BLOCK = 256
GLA_HEADS_PER_STEP = 4
NA_HEAD_DIM = 128
NA_KH = 8
NA_KW = 16
NA_QROWS = 4
NA_UROWS = NA_KH + NA_QROWS - 1
ROPE_THETA = 10000.0
MOE_GROUPS = 4
MOE_PER_GROUP = 8
MOE_EXPERTS = MOE_GROUPS * MOE_PER_GROUP
MOE_ROWS = 256


def _cparams(sem):
    return pltpu.CompilerParams(dimension_semantics=sem, vmem_limit_bytes=VMEM_LIMIT)


def _nt(a, b):
    return lax.dot_general(a, b, (((1,), (1,)), ((), ())), preferred_element_type=F32)


def _side_cast_specs(tensors, n_steps, lin):
    nblocks = 1 << (n_steps.bit_length() - 1)

    def slab(*ids):
        return jnp.minimum(lin(*ids), nblocks - 1)

    views, in_specs, out_specs, out_shapes = [], [], [], []
    for t, layer in tensors:
        v = t.reshape(t.shape[0], -1, t.shape[-1])
        rows, cols = v.shape[1:]
        assert rows % (16 * nblocks) == 0
        views.append(v)
        in_specs.append(pl.BlockSpec((None, rows // nblocks, cols), functools.partial(
            lambda layer, *ids: (layer, slab(*ids), 0), layer)))
        out_specs.append(pl.BlockSpec((rows // nblocks, cols), lambda *ids: (slab(*ids), 0)))
        out_shapes.append(jax.ShapeDtypeStruct((rows, cols), BF16))
    return views, in_specs, out_specs, out_shapes


def _side_cast(in_refs, out_refs):
    for i_ref, o_ref in zip(in_refs, out_refs):
        o_ref[...] = i_ref[...].astype(BF16)


def _ada_kernel(c_ref, w_ref, b_ref, o_ref):
    c = c_ref[...]
    s = c * jax.nn.sigmoid(c)
    o_ref[...] = jnp.dot(s.astype(BF16), w_ref[...].astype(BF16), preferred_element_type=F32) + b_ref[...]


def _ada_mod(cvec, w, b):
    R, D = cvec.shape
    L, _, N = w.shape
    tn = 1024
    out = pl.pallas_call(
        _ada_kernel,
        grid=(L, N // tn),
        in_specs=[pl.BlockSpec((R, D), lambda l, j: (0, 0)),
                  pl.BlockSpec((None, D, tn), lambda l, j: (l, 0, j)),
                  pl.BlockSpec((None, 1, tn), lambda l, j: (l, 0, j))],
        out_specs=pl.BlockSpec((None, R, tn), lambda l, j: (l, 0, j)),
        out_shape=jax.ShapeDtypeStruct((L, R, N), F32),
        compiler_params=_cparams(("arbitrary", "arbitrary")),
        name="ada_mod",
    )(cvec, w, b.reshape(L, 1, N))
    return out.reshape(L, R, N_MOD, 1, D)


def _mod_sel(nct, nb):
    if nct == 0:
        return lambda b, i: b
    return lambda b, i: jnp.where(i < nct, nb, b)


def _normmod(x, g, sh, sc):
    y = x * lax.rsqrt(jnp.mean(x * x, axis=-1, keepdims=True) + EPS) * g
    return y * (1.0 + sc) + sh


def _normmod_mm_kernel(u_ref, g_ref, sh_ref, sc_ref, w_ref, o_ref, *, tn):
    xn = _normmod(u_ref[...], g_ref[...], sh_ref[...], sc_ref[...]).astype(BF16)
    for n0 in range(0, o_ref.shape[-1], tn):
        o_ref[:, n0:n0 + tn] = jnp.dot(xn, w_ref[:, n0:n0 + tn], preferred_element_type=F32).astype(o_ref.dtype)


def _normmod_matmul(u, gamma, mods, k_shift, k_scale, w, *, n_ctx, tm, tn, out_dtype=F32):
    B, T, D = u.shape
    N = w.shape[1]
    sel = _mod_sel(n_ctx // tm, B)
    return pl.pallas_call(
        functools.partial(_normmod_mm_kernel, tn=tn),
        grid=(B, T // tm),
        in_specs=[pl.BlockSpec((None, tm, D), lambda b, i: (b, i, 0)),
                  pl.BlockSpec((1, D), lambda b, i: (0, 0)),
                  pl.BlockSpec((None, None, 1, D), lambda b, i: (sel(b, i), k_shift, 0, 0)),
                  pl.BlockSpec((None, None, 1, D), lambda b, i: (sel(b, i), k_scale, 0, 0)),
                  pl.BlockSpec((D, N), lambda b, i: (0, 0), pipeline_mode=pl.Buffered(1))],
        out_specs=pl.BlockSpec((None, tm, N), lambda b, i: (b, i, 0)),
        out_shape=jax.ShapeDtypeStruct((B, T, N), out_dtype),
        compiler_params=_cparams(("arbitrary", "arbitrary")),
        name="normmod_matmul",
    )(u, gamma.reshape(1, D), mods, mods, w)


def _mm_res_kernel(*refs, n_lhs, tn):
    a_refs, w_ref, r_ref, g_ref, o_ref = refs[:n_lhs], refs[n_lhs], refs[n_lhs + 1], refs[n_lhs + 2], refs[n_lhs + 3]
    lhs = [a[...] for a in a_refs]
    for n0 in range(0, o_ref.shape[-1], tn):
        acc, k0 = None, 0
        for a in lhs:
            part = jnp.dot(a, w_ref[k0:k0 + a.shape[1], n0:n0 + tn], preferred_element_type=F32)
            acc = part if acc is None else acc + part
            k0 += a.shape[1]
        o_ref[:, n0:n0 + tn] = r_ref[:, n0:n0 + tn] + g_ref[:, n0:n0 + tn] * acc


def _matmul_residual(lhs, w, res, mods, k_gate, *, n_ctx, row0, tm, tn):
    B, Ta, _ = lhs[0].shape
    K, N = w.shape
    sel = _mod_sel(n_ctx // tm, B)
    r0 = row0 // tm
    return pl.pallas_call(
        functools.partial(_mm_res_kernel, n_lhs=len(lhs), tn=tn),
        grid=(B, Ta // tm),
        in_specs=[pl.BlockSpec((None, tm, a.shape[2]), lambda b, i: (b, i, 0)) for a in lhs]
        + [pl.BlockSpec((K, N), lambda b, i: (0, 0), pipeline_mode=pl.Buffered(1)),
           pl.BlockSpec((None, tm, N), lambda b, i: (b, i + r0, 0)),
           pl.BlockSpec((None, None, 1, N), lambda b, i: (sel(b, i), k_gate, 0, 0))],
        out_specs=pl.BlockSpec((None, tm, N), lambda b, i: (b, i, 0)),
        out_shape=jax.ShapeDtypeStruct((B, Ta, N), F32),
        compiler_params=_cparams(("arbitrary", "arbitrary")),
        name="matmul_residual",
    )(*lhs, w, res, mods)


def _s5_operators(lam_re, lam_im, log_dt, b_re, b_im, c_re, c_im):
    T = S5_CHUNK
    lr = jnp.minimum(lam_re.astype(F32), -1e-4)
    li = lam_im.astype(F32)
    dt = jnp.exp(log_dt.astype(F32))[..., None]
    mag = jnp.exp(lr * dt)
    a_re = mag * jnp.cos(li * dt)
    a_im = mag * jnp.sin(li * dt)
    num_re = a_re - 1.0
    den = lr * lr + li * li
    f_re = (num_re * lr + a_im * li) / den
    f_im = (a_im * lr - num_re * li) / den
    bb_re = f_re[..., None] * b_re.astype(F32) - f_im[..., None] * b_im.astype(F32)
    bb_im = f_re[..., None] * b_im.astype(F32) + f_im[..., None] * b_re.astype(F32)
    cr, ci = c_re.astype(F32), c_im.astype(F32)

    def powers(step, n):
        m = (step * jnp.arange(n + 1, dtype=F32))[:, None, None, None]
        mg = jnp.exp(m * (lr * dt))
        return mg * jnp.cos(m * (li * dt)), mg * jnp.sin(m * (li * dt))

    pr, pi = powers(1, T)
    pr16, pi16 = powers(T, S5_SUPER)
    ca_re = cr[None] * pr[:T, :, :, None, :] - ci[None] * pi[:T, :, :, None, :]
    ca_im = cr[None] * pi[:T, :, :, None, :] + ci[None] * pr[:T, :, :, None, :]
    kk = (jnp.einsum('kdgip,dgpj->dkgij', ca_re, bb_re) - jnp.einsum('kdgip,dgpj->dkgij', ca_im, bb_im))
    G = lr.shape[1]
    I = S5_GROUP
    s_idx = jnp.arange(T)[:, None]
    t_idx = jnp.arange(T)[None, :]

    def toeplitz(kd, lag):
        m = jnp.where((lag >= 0)[:, :, None, None, None], kd[jnp.clip(lag, 0, T - 1)], 0.0)
        return jnp.transpose(m, (2, 0, 4, 1, 3)).reshape(G, T * I, T * I)

    mt = toeplitz(kk[0], t_idx - s_idx) + toeplitz(kk[1], s_idx - t_idx)

    def v_op(d, qr, qi):
        vr = qr[:, :, :, None] * bb_re[d][None] - qi[:, :, :, None] * bb_im[d][None]
        vi = qr[:, :, :, None] * bb_im[d][None] + qi[:, :, :, None] * bb_re[d][None]
        vr = jnp.transpose(vr, (1, 0, 3, 2)).reshape(G, T * I, S5_STATE)
        vi = jnp.transpose(vi, (1, 0, 3, 2)).reshape(G, T * I, S5_STATE)
        return jnp.concatenate([vr, vi, -vi, vr], axis=-1)

    vt = jnp.concatenate([v_op(0, pr[:T, 0][::-1], pi[:T, 0][::-1]), v_op(1, pr[:T, 1], pi[:T, 1])], axis=-1)

    def w_op(d, qr, qi):
        wr = cr[d][None] * qr[:, :, None, :] - ci[d][None] * qi[:, :, None, :]
        wi = cr[d][None] * qi[:, :, None, :] + ci[d][None] * qr[:, :, None, :]
        wr = jnp.transpose(wr, (1, 3, 0, 2)).reshape(G, S5_STATE, T * I)
        wi = jnp.transpose(wi, (1, 3, 0, 2)).reshape(G, S5_STATE, T * I)
        return jnp.concatenate([wr, -wi], axis=1)

    w = jnp.concatenate([w_op(0, pr[1:T + 1, 0], pi[1:T + 1, 0]),
                         w_op(1, pr[1:T + 1, 1][::-1], pi[1:T + 1, 1][::-1])], axis=1)

    def dup(z):
        return jnp.concatenate([z, z], -1)

    rows = []
    for d in range(2):
        rows += [dup(pr16[1, d]), dup(pi16[1, d]), dup(pr16[S5_SUPER, d]), dup(pi16[S5_SUPER, d])]
    dec = jnp.stack(rows, axis=1)
    prs = []
    for d in range(2):
        prs += [dup(jnp.transpose(pr16[:S5_SUPER, d], (1, 0, 2))), dup(jnp.transpose(pi16[:S5_SUPER, d], (1, 0, 2)))]
    pwt = jnp.concatenate(prs, axis=1)
    return mt.astype(BF16), vt.astype(BF16), w.astype(BF16), dec, pwt


S5_LANE_GROUPS = LANES // S5_GROUP


def _piece_transpose(arrs, lane):
    for d in (4, 2, 1):
        bit = ((lane // S5_GROUP) & d) != 0
        new = list(arrs)
        for r in range(S5_LANE_GROUPS):
            if r & d == 0:
                a, b = arrs[r], arrs[r + d]
                new[r] = jnp.where(bit, pltpu.roll(b, S5_GROUP * d, 1), a)
                new[r + d] = jnp.where(bit, b, pltpu.roll(a, LANES - S5_GROUP * d, 1))
        arrs = new
    return arrs


def _s5_kernel(u_ref, mt_ref, vt_ref, w_ref, dec_ref, pw_ref, dsk_ref, y_ref, xg_ref, yg_ref, vv_ref, hin_ref,
               hs_ref, *, nsc):
    nj = S5_SUPER
    nch = u_ref.shape[0] // S5_CHUNK
    ng = S5_LANE_GROUPS
    lane = lax.broadcasted_iota(jnp.int32, (nch, LANES), 1)
    for h in range(2):
        zs = [u_ref[pl.ds(ng * h + t, nch, stride=S5_CHUNK), :] for t in range(ng)]
        for g, a in enumerate(_piece_transpose(zs, lane)):
            xg_ref[g, :, LANES * h:LANES * (h + 1)] = a
    hs_ref[...] = jnp.zeros_like(hs_ref)

    def group(g, carry):
        x = xg_ref[g]
        xb = x.astype(BF16)
        yg_ref[g] = jnp.dot(xb, mt_ref[g], preferred_element_type=F32) + x * dsk_ref[g]
        vv = jnp.dot(xb, vt_ref[g], preferred_element_type=F32)
        for n in range(4):
            vv_ref[n] = vv[:, LANES * n:LANES * (n + 1)]
        dec = dec_ref[g]
        pw = pw_ref[g]
        for d in range(2):
            ar16, ai16, ar256, ai256 = (dec[4 * d + n:4 * d + n + 1] for n in range(4))

            def v(j, d=d):
                return vv_ref[2 * d, pl.ds(j, nsc, stride=nj), :]

            def jv(j, d=d):
                return vv_ref[2 * d + 1, pl.ds(j, nsc, stride=nj), :]

            order = list(range(nj)) if d == 0 else list(range(nj - 1, -1, -1))
            loc = [None] * nj
            l = jnp.zeros((nsc, LANES), F32)
            gg = jnp.zeros((nsc, LANES), F32)
            for n, j in enumerate(order):
                if n > 0:
                    p = order[n - 1]
                    l, gg = ar16 * l + ai16 * gg + v(p), ar16 * gg - ai16 * l + jv(p)
                loc[j] = l
            p = order[-1]
            e, je = ar16 * l + ai16 * gg + v(p), ar16 * gg - ai16 * l + jv(p)
            seq = list(range(nsc)) if d == 0 else [0] + list(range(nsc - 1, 0, -1))
            cur = jnp.zeros((1, LANES), F32)
            jcur = jnp.zeros((1, LANES), F32)
            for m in seq:
                hs_ref[2 * d, m:m + 1, :] = cur
                hs_ref[2 * d + 1, m:m + 1, :] = jcur
                cur, jcur = (ar256 * cur + ai256 * jcur + e[m:m + 1], ar256 * jcur - ai256 * cur + je[m:m + 1])
            hs = hs_ref[2 * d, 0:nsc, :]
            jhs = hs_ref[2 * d + 1, 0:nsc, :]
            for n, j in enumerate(order):
                pr = pw[32 * d + n:32 * d + n + 1]
                pi = pw[32 * d + 16 + n:32 * d + 16 + n + 1]
                hin_ref[d, pl.ds(j, nsc, stride=nj), :] = loc[j] + pr * hs + pi * jhs
        hin = jnp.concatenate([hin_ref[0], hin_ref[1]], axis=1).astype(BF16)
        yg_ref[g] += jnp.dot(hin, w_ref[g], preferred_element_type=F32)
        return carry

    lax.fori_loop(0, ng, group, 0, unroll=2)
    for h in range(2):
        ys = [yg_ref[g, :, LANES * h:LANES * (h + 1)] for g in range(ng)]
        for t, a in enumerate(_piece_transpose(ys, lane)):
            y_ref[pl.ds(ng * h + t, nch, stride=S5_CHUNK), :] = a


def _s5_scan(px, ops, dsk, *, a_width):
    B, T, _ = px.shape
    mt, vt, w, dec, pw = ops
    ng = S5_LANE_GROUPS
    C = S5_CHUNK * S5_GROUP
    nch = T // S5_CHUNK
    nsc = nch // S5_SUPER
    nsp = (nsc + 7) // 8 * 8
    return pl.pallas_call(
        functools.partial(_s5_kernel, nsc=nsc),
        grid=(a_width // LANES, B),
        in_specs=[pl.BlockSpec((None, T, LANES), lambda gb, b: (b, 0, gb)),
                  pl.BlockSpec((ng, C, C), lambda gb, b: (gb, 0, 0)),
                  pl.BlockSpec((ng, C, 2 * C), lambda gb, b: (gb, 0, 0)),
                  pl.BlockSpec((ng, C, C), lambda gb, b: (gb, 0, 0)),
                  pl.BlockSpec((ng, 8, LANES), lambda gb, b: (gb, 0, 0)),
                  pl.BlockSpec((ng, 64, LANES), lambda gb, b: (gb, 0, 0)),
                  pl.BlockSpec((ng, 1, C), lambda gb, b: (gb, 0, 0))],
        out_specs=pl.BlockSpec((None, T, LANES), lambda gb, b: (b, 0, gb)),
        out_shape=jax.ShapeDtypeStruct((B, T, a_width), F32),
        scratch_shapes=[pltpu.VMEM((ng, nch, C), F32), pltpu.VMEM((ng, nch, C), F32),
                        pltpu.VMEM((4, nch, LANES), F32), pltpu.VMEM((2, nch, LANES), F32),
                        pltpu.VMEM((4, nsp, LANES), F32)],
        compiler_params=_cparams(("arbitrary", "arbitrary")),
        name="s5_scan",
    )(px, mt, vt, w, dec, pw, dsk)


def _s5_glu_kernel(y_ref, w_ref, o_ref):
    g = jax.nn.gelu(y_ref[...])
    z = jnp.dot(g.astype(BF16), w_ref[...], preferred_element_type=F32)
    o_ref[...] = (g * jax.nn.sigmoid(z)).astype(o_ref.dtype)


def _s5_glu(y, w, *, tm):
    B, T, A = y.shape
    return pl.pallas_call(
        _s5_glu_kernel,
        grid=(B, T // tm),
        in_specs=[pl.BlockSpec((None, tm, A), lambda b, i: (b, i, 0)),
                  pl.BlockSpec((A, A), lambda b, i: (0, 0))],
        out_specs=pl.BlockSpec((None, tm, A), lambda b, i: (b, i, 0)),
        out_shape=jax.ShapeDtypeStruct((B, T, A), BF16),
        compiler_params=_cparams(("arbitrary", "arbitrary")),
        name="s5_glu",
    )(y, w)


def _log_sigmoid(z):
    return jnp.minimum(z, 0.0) - jnp.log(1.0 + jnp.exp(-jnp.abs(z)))


def _gla_kernel(*refs, nblk, dk, dv, hp, n_cast):
    q_ref, k_ref, v_ref, r_ref, lr_ref, cos_ref, sin_ref, wg_ref, bg_ref, ng_ref = refs[:10]
    cast_in = refs[10:10 + n_cast]
    o_ref = refs[10 + n_cast]
    cast_out = refs[11 + n_cast:11 + 2 * n_cast]
    s_ref, of_ref = refs[11 + 2 * n_cast:]
    p = pl.program_id(2)
    s = pl.program_id(3)
    C = GLA_CHUNK
    nch = GLA_BLOCK // C
    blk = jnp.where(p == 0, s, jnp.where(s == 0, 0, nblk - s))

    @pl.when(s == 0)
    def _():
        s_ref[...] = jnp.zeros_like(s_ref)

    NB = GLA_BLOCK
    lane = lax.broadcasted_iota(jnp.int32, (NB, dk), 1)
    lower = (lane & 63) < 32
    rpos = lax.broadcasted_iota(jnp.int32, (NB, dk), 0) & (C - 1)
    row = lax.broadcasted_iota(jnp.int32, (NB, NB), 0)
    col = lax.broadcasted_iota(jnp.int32, (NB, NB), 1)
    same_chunk = (row // C) == (col // C)
    eye = (lax.broadcasted_iota(jnp.int32, (dk, dk), 0) == lax.broadcasted_iota(jnp.int32, (dk, dk), 1)).astype(BF16)

    def rope(z, cos, sin):
        return z * cos + jnp.where(lower, pltpu.roll(z, dk - 32, 1), pltpu.roll(z, 32, 1)) * sin

    def run(reverse):
        _side_cast(cast_in, cast_out)
        for hd in range(hp):
            head(reverse, hd)

    def head(reverse, hd):
        mask = same_chunk & ((row <= col) if reverse else (row >= col))
        cos = cos_ref[...]
        sin = sin_ref[...]
        kcols = slice(hd * dk, (hd + 1) * dk)
        vcols = slice(hd * dv, (hd + 1) * dv)
        q = rope(q_ref[:, kcols], cos, sin) * (dk ** -0.5)
        k = rope(k_ref[:, kcols], cos, sin)
        vb = v_ref[:, vcols].astype(BF16)
        lrb = lr_ref[...].astype(BF16)
        la = _log_sigmoid(jnp.dot(lrb, wg_ref[hd], preferred_element_type=F32) + bg_ref[hd]) * (1.0 / GLA_TAU)
        b = la
        sh = 1
        while sh < C:
            if reverse:
                b = b + jnp.where(rpos < C - sh, pltpu.roll(b, NB - sh, 0), 0.0)
            else:
                b = b + jnp.where(rpos >= sh, pltpu.roll(b, sh, 0), 0.0)
            sh *= 2
        last = C - 1 if not reverse else 0
        tots = [b[c * C + last:c * C + last + 1] for c in range(nch)]
        btot = jnp.concatenate([jnp.broadcast_to(t, (C, dk)) for t in tots], 0)
        dmat = jnp.exp(jnp.concatenate(tots + [jnp.zeros((8 - nch, dk), F32)], 0).T)
        qi = (q * jnp.exp(b)).astype(BF16)
        ki = (k * jnp.exp(-b)).astype(BF16)
        ko = (k * jnp.exp(btot - b)).astype(BF16)
        att = jnp.where(mask, _nt(qi, ki), 0.0).astype(BF16)
        o = jnp.dot(att, vb, preferred_element_type=F32)
        upd, dcol = [], []
        for c in range(nch):
            kot = _nt(eye, ko[c * C:(c + 1) * C]).astype(BF16)
            upd.append(jnp.dot(kot, vb[c * C:(c + 1) * C], preferred_element_type=F32))
            dcol.append(dmat[:, c:c + 1])
        st = s_ref[hd]
        o_state = [None] * nch
        for c in (range(nch - 1, -1, -1) if reverse else range(nch)):
            o_state[c] = jnp.dot(qi[c * C:(c + 1) * C], st.astype(BF16), preferred_element_type=F32)
            st = st * dcol[c] + upd[c]
        s_ref[hd] = st
        o = o + jnp.concatenate(o_state, axis=0)
        tok = pl.ds(pl.multiple_of(blk * NB, NB), NB)
        if not reverse:
            of_ref[tok, vcols] = o
        else:
            o = o + of_ref[tok, vcols]
            o = o * lax.rsqrt(jnp.mean(o * o, axis=-1, keepdims=True) + EPS) * ng_ref[hd]
            r = r_ref[:, vcols]
            o_ref[:, vcols] = (o * (r * jax.nn.sigmoid(r))).astype(o_ref.dtype)

    @pl.when(p == 0)
    def _():
        run(False)

    @pl.when(p == 1)
    def _():
        run(True)


def _rope_tables(n_ctx, seq, dk):
    nf = dk // 4
    inv = ROPE_THETA ** (-jnp.arange(nf, dtype=F32) / nf)
    pos = jnp.arange(seq)
    ang_r = (pos // GRID_W).astype(F32)[:, None] * inv[None, :]
    ang_c = (pos % GRID_W).astype(F32)[:, None] * inv[None, :]
    cos = jnp.concatenate([jnp.cos(ang_r)] * 2 + [jnp.cos(ang_c)] * 2, axis=-1)
    sin = jnp.concatenate([-jnp.sin(ang_r), jnp.sin(ang_r), -jnp.sin(ang_c), jnp.sin(ang_c)], axis=-1)
    cos = jnp.concatenate([jnp.ones((n_ctx, dk), F32), cos], axis=0)
    sin = jnp.concatenate([jnp.zeros((n_ctx, dk), F32), sin], axis=0)
    return cos, sin


def _gla(px, w_gate2, b_gate, norm_g, cast, *, n_ctx, a_width):
    B, T, _ = px.shape
    H = GLA_HEADS
    qk = w_gate2.shape[-1]
    dk = qk // H
    bw = norm_g.shape[-1]
    dv = bw // H
    nblk = T // GLA_BLOCK
    cos, sin = _rope_tables(n_ctx, T - n_ctx, dk)
    lr_col0 = a_width + 2 * qk + 2 * bw
    wg = jnp.zeros((2, H, LANES, dk), F32)
    for d in range(2):
        wg = wg.at[d, :, d * GLA_RANK:(d + 1) * GLA_RANK, :].set(
            jnp.transpose(w_gate2[d].reshape(GLA_RANK, H, dk), (1, 0, 2)))
    wg = wg.astype(BF16)
    bg = b_gate.astype(F32).reshape(2, H, 1, dk)
    ng = norm_g.astype(F32).reshape(H, 1, dv)
    hp = GLA_HEADS_PER_STEP

    def blk(p, s):
        return jnp.where(p == 0, s, jnp.where(s == 0, 0, nblk - s))

    kw, vw = hp * dk, hp * dv
    qb, kb = a_width // kw, (a_width + qk) // kw
    vbk, rbk = (a_width + 2 * qk) // vw, (a_width + 2 * qk + bw) // vw
    lb = lr_col0 // LANES
    ngrp = H // hp
    views, c_in, c_out, c_shapes = _side_cast_specs(
        cast, B * ngrp * 2 * nblk, lambda b, g, p, s: ((b * ngrp + g) * 2 + p) * nblk + s)
    outs = pl.pallas_call(
        functools.partial(_gla_kernel, nblk=nblk, dk=dk, dv=dv, hp=hp, n_cast=len(views)),
        grid=(B, ngrp, 2, nblk),
        in_specs=[pl.BlockSpec((None, GLA_BLOCK, kw), lambda b, g, p, s: (b, blk(p, s), qb + g)),
                  pl.BlockSpec((None, GLA_BLOCK, kw), lambda b, g, p, s: (b, blk(p, s), kb + g)),
                  pl.BlockSpec((None, GLA_BLOCK, vw), lambda b, g, p, s: (b, blk(p, s), vbk + g)),
                  pl.BlockSpec((None, GLA_BLOCK, vw), lambda b, g, p, s: (b, blk(p, s), rbk + g)),
                  pl.BlockSpec((None, GLA_BLOCK, LANES), lambda b, g, p, s: (b, blk(p, s), lb)),
                  pl.BlockSpec((GLA_BLOCK, dk), lambda b, g, p, s: (blk(p, s), 0)),
                  pl.BlockSpec((GLA_BLOCK, dk), lambda b, g, p, s: (blk(p, s), 0)),
                  pl.BlockSpec((None, hp, LANES, dk), lambda b, g, p, s: (p, g, 0, 0)),
                  pl.BlockSpec((None, hp, 1, dk), lambda b, g, p, s: (p, g, 0, 0)),
                  pl.BlockSpec((hp, 1, dv), lambda b, g, p, s: (g, 0, 0))] + c_in,
        out_specs=[pl.BlockSpec((None, GLA_BLOCK, vw),
                                lambda b, g, p, s: (b, jnp.where(p == 0, 0, blk(p, s)), g))] + c_out,
        out_shape=[jax.ShapeDtypeStruct((B, T, bw), BF16)] + c_shapes,
        scratch_shapes=[pltpu.VMEM((hp, dk, dv), F32), pltpu.VMEM((T, vw), F32)],
        compiler_params=_cparams(("arbitrary", "arbitrary", "arbitrary", "arbitrary")),
        name="gla",
    )(px, px, px, px, px, cos, sin, wg, bg, ng, *views)
    return outs[0], outs[1:]


def _na_kernel(*refs, n_ctx, rows, n_cast):
    q_ref, k_ref, v_ref, bias_ref, qn_ref, kn_ref = refs[:6]
    o_ref = refs[6 + n_cast]
    kb_ref, vb_ref = refs[7 + 2 * n_cast:]
    _side_cast(refs[6:6 + n_cast], refs[7 + n_cast:7 + 2 * n_cast])
    hd = NA_HEAD_DIM
    W = GRID_W
    T = k_ref.shape[0]
    step = n_ctx

    def prep(i, c):
        sl = pl.ds(pl.multiple_of(i * step, step), step)
        kk = k_ref[sl, :]
        kb_ref[sl, :] = (kk * lax.rsqrt(jnp.mean(kk * kk, axis=-1, keepdims=True) + EPS) * kn_ref[...]).astype(BF16)
        vb_ref[sl, :] = v_ref[sl, :].astype(BF16)
        return c

    lax.fori_loop(0, T // step, prep, 0)
    scale = hd ** -0.5
    ngrp = rows // NA_QROWS
    nq = NA_QROWS * W
    nk = NA_UROWS * W

    def group(g, c):
        r0 = g * NA_QROWS
        ustart = jnp.clip(r0 - NA_KH // 2, 0, rows - NA_UROWS)
        case = jnp.where(g == 0, 0, jnp.where(g == ngrp - 1, 2, 1))
        q = q_ref[pl.ds(pl.multiple_of(n_ctx + r0 * W, W), nq), :]
        q = (q * lax.rsqrt(jnp.mean(q * q, axis=-1, keepdims=True) + EPS) * qn_ref[...] * scale).astype(BF16)
        ws = pl.ds(pl.multiple_of(n_ctx + ustart * W, W), nk)
        s_lat = _nt(q, kb_ref[ws, :]) + bias_ref[case]
        s_ctx = _nt(q, kb_ref[0:n_ctx, :])
        m = jnp.maximum(jnp.max(s_lat, axis=-1, keepdims=True), jnp.max(s_ctx, axis=-1, keepdims=True))
        p_lat = jnp.exp(s_lat - m)
        p_ctx = jnp.exp(s_ctx - m)
        den = jnp.sum(p_lat, axis=-1, keepdims=True) + jnp.sum(p_ctx, axis=-1, keepdims=True)
        o = (jnp.dot(p_lat.astype(BF16), vb_ref[ws, :], preferred_element_type=F32)
             + jnp.dot(p_ctx.astype(BF16), vb_ref[0:n_ctx, :], preferred_element_type=F32))
        o_ref[pl.ds(pl.multiple_of(r0 * W, W), nq), :] = (o / den).astype(o_ref.dtype)
        return c

    lax.fori_loop(0, ngrp, group, 0, unroll=4)


def _na_bias(rpb, rows):
    H = rpb.shape[0]
    col = jnp.arange(GRID_W)
    cs = jnp.clip(col - NA_KW // 2, 0, GRID_W - NA_KW)
    col_ok = (col[None, :] >= cs[:, None]) & (col[None, :] < cs[:, None] + NA_KW)
    dc_idx = jnp.clip(col[None, :] - col[:, None] + NA_KW - 1, 0, 2 * NA_KW - 2)
    rpb_c = jnp.where(col_ok[None, None], rpb.astype(F32)[:, :, dc_idx], NEG_INF)
    ngrp = rows // NA_QROWS
    tables = []
    for g in (0, 1, ngrp - 1):
        r0 = g * NA_QROWS
        ustart = min(max(r0 - NA_KH // 2, 0), rows - NA_UROWS)
        qr = r0 + jnp.arange(NA_QROWS)
        start = jnp.clip(qr - NA_KH // 2, 0, rows - NA_KH)
        kr = ustart + jnp.arange(NA_UROWS)
        ok = (kr[None, :] >= start[:, None]) & (kr[None, :] < start[:, None] + NA_KH)
        idx = jnp.clip(kr[None, :] - qr[:, None] + NA_KH - 1, 0, 2 * NA_KH - 2)
        t = jnp.where(ok[None, :, :, None, None], rpb_c[:, idx], NEG_INF)
        tables.append(jnp.transpose(t, (0, 1, 3, 2, 4)).reshape(H, NA_QROWS * GRID_W, NA_UROWS * GRID_W))
    return jnp.stack(tables, axis=1)


def _na_attention(qkv, q_norm, k_norm, rpb, cast, *, n_ctx):
    B, T, D3 = qkv.shape
    D = D3 // 3
    H = D // NA_HEAD_DIM
    seq = T - n_ctx
    rows = seq // GRID_W
    assert rows % NA_QROWS == 0 and rows >= NA_UROWS + NA_QROWS
    bias = _na_bias(rpb, rows)
    hd = NA_HEAD_DIM
    views, c_in, c_out, c_shapes = _side_cast_specs(cast, B * H, lambda b, h: b * H + h)
    outs = pl.pallas_call(
        functools.partial(_na_kernel, n_ctx=n_ctx, rows=rows, n_cast=len(views)),
        grid=(B, H),
        in_specs=[pl.BlockSpec((None, T, hd), lambda b, h: (b, 0, h)),
                  pl.BlockSpec((None, T, hd), lambda b, h: (b, 0, H + h)),
                  pl.BlockSpec((None, T, hd), lambda b, h: (b, 0, 2 * H + h)),
                  pl.BlockSpec((None, 3, NA_QROWS * GRID_W, NA_UROWS * GRID_W), lambda b, h: (h, 0, 0, 0)),
                  pl.BlockSpec((1, hd), lambda b, h: (0, 0)),
                  pl.BlockSpec((1, hd), lambda b, h: (0, 0))] + c_in,
        out_specs=[pl.BlockSpec((None, seq, hd), lambda b, h: (b, 0, h))] + c_out,
        out_shape=[jax.ShapeDtypeStruct((B, seq, D), BF16)] + c_shapes,
        scratch_shapes=[pltpu.VMEM((T, hd), BF16), pltpu.VMEM((T, hd), BF16)],
        compiler_params=_cparams(("arbitrary", "arbitrary")),
        name="na_attention",
    )(qkv, qkv, qkv, bias, q_norm.astype(F32).reshape(1, hd), k_norm.astype(F32).reshape(1, hd), *views)
    return outs[0], outs[1:]


def _router_kernel(u_ref, g_ref, sh_ref, sc_ref, wr_ref, h_ref, mf_ref, mi_ref):
    h = _normmod(u_ref[...], g_ref[...], sh_ref[...], sc_ref[...])
    h_ref[...] = h
    logits = jnp.dot(h, wr_ref[...], precision=HIGHEST, preferred_element_type=F32)
    lane = lax.broadcasted_iota(jnp.int32, logits.shape, 1).astype(F32)
    big = 1e9

    def first_max(vals):
        m = jnp.max(vals, axis=-1, keepdims=True)
        return m, jnp.min(jnp.where(vals == m, lane, big), axis=-1, keepdims=True)

    gl = jnp.where(lane < MOE_GROUPS, logits, -jnp.inf)
    gmax, gidx = first_max(gl)
    p_top = 1.0 / jnp.sum(jnp.exp(gl - gmax), axis=-1, keepdims=True)
    lo = MOE_GROUPS + MOE_PER_GROUP * gidx
    el = jnp.where((lane >= lo) & (lane < lo + MOE_PER_GROUP), logits, -jnp.inf)
    v1, i1 = first_max(el)
    v2, i2 = first_max(jnp.where(lane == i1, -jnp.inf, el))
    e2 = jnp.exp(v2 - v1)
    w1 = p_top / (1.0 + e2)
    w2 = p_top * e2 / (1.0 + e2)
    mf_ref[...] = jnp.where(lane == 0, w1, jnp.where(lane == 1, w2, 0.0))
    mi_ref[...] = jnp.where(lane == 0, i1 - MOE_GROUPS, jnp.where(lane == 1, i2 - MOE_GROUPS, 0.0)).astype(jnp.int32)


def _router(u, gamma, mods, k_shift, k_scale, wr, *, n_ctx, tm):
    B, T, D = u.shape
    sel = _mod_sel(n_ctx // tm, B)
    return pl.pallas_call(
        _router_kernel,
        grid=(B, T // tm),
        in_specs=[pl.BlockSpec((None, tm, D), lambda b, i: (b, i, 0)),
                  pl.BlockSpec((1, D), lambda b, i: (0, 0)),
                  pl.BlockSpec((None, None, 1, D), lambda b, i: (sel(b, i), k_shift, 0, 0)),
                  pl.BlockSpec((None, None, 1, D), lambda b, i: (sel(b, i), k_scale, 0, 0)),
                  pl.BlockSpec((D, LANES), lambda b, i: (0, 0))],
        out_specs=[pl.BlockSpec((None, tm, D), lambda b, i: (b, i, 0)),
                   pl.BlockSpec((None, tm, LANES), lambda b, i: (b, i, 0)),
                   pl.BlockSpec((None, tm, LANES), lambda b, i: (b, i, 0))],
        out_shape=[jax.ShapeDtypeStruct((B, T, D), F32),
                   jax.ShapeDtypeStruct((B, T, LANES), F32),
                   jax.ShapeDtypeStruct((B, T, LANES), jnp.int32)],
        compiler_params=_cparams(("arbitrary", "arbitrary")),
        name="moe_router",
    )(u, gamma.reshape(1, D), mods, mods, wr)


def _moe_plan(eid2):
    N = eid2.shape[0]
    M = 2 * N
    E = MOE_EXPERTS
    eid = eid2.reshape(-1)
    onehot = (eid[:, None] == jnp.arange(E, dtype=jnp.int32)[None, :]).astype(jnp.int32)
    csum = jnp.cumsum(onehot, axis=0)
    rank = jnp.sum(onehot * csum, axis=1) - 1
    counts = csum[-1]
    padded = (counts + MOE_ROWS - 1) // MOE_ROWS * MOE_ROWS
    pend = jnp.cumsum(padded)
    pstart = pend - padded
    dest = (jnp.sum(onehot * pstart[None, :], axis=1) + rank).astype(jnp.int32)
    P = (M + E * (MOE_ROWS - 1) + MOE_ROWS - 1) // MOE_ROWS * MOE_ROWS
    nb = P // MOE_ROWS
    slot_tok = jnp.zeros((P,), jnp.int32).at[dest].set(jnp.arange(M, dtype=jnp.int32) // 2)
    pos = dest.reshape(N, 2)
    blk_start = jnp.arange(nb, dtype=jnp.int32) * MOE_ROWS
    blk_e = jnp.minimum(jnp.sum((pend[None, :] <= blk_start[:, None]).astype(jnp.int32), axis=1), E - 1)
    n_active = (pend[-1] // MOE_ROWS).astype(jnp.int32).reshape(1)
    return slot_tok.reshape(nb, MOE_ROWS), pos, blk_e.astype(jnp.int32), n_active


def _row_copy(src_hbm, idx_smem, dst_ref, sem, r):
    return pltpu.make_async_copy(src_hbm.at[pl.ds(idx_smem[r], 1), :], dst_ref.at[pl.ds(r, 1), :], sem)


def _expert_kernel(blk_e_ref, nact_ref, idx_hbm, h_hbm, wg_ref, wu_ref, wd_ref, o_ref, idx0, idx1, xbuf, sem_idx,
                   sem_rows):
    i = pl.program_id(0)
    nact = nact_ref[0]
    R = MOE_ROWS
    slot = lax.rem(i, 2)
    idx_bufs = (idx0, idx1)

    def idx_copy(blk, s):
        return pltpu.make_async_copy(idx_hbm.at[blk], idx_bufs[s], sem_idx.at[s])

    def issue_rows(s):
        def body(r, c):
            _row_copy(h_hbm, idx_bufs[s], xbuf.at[s], sem_rows.at[s], r).start(priority=1)
            return c

        lax.fori_loop(0, R, body, 0, unroll=8)

    @pl.when(i == 0)
    def _():
        first = idx_copy(0, 0)
        first.start()
        first.wait()
        issue_rows(0)

        @pl.when(nact > 1)
        def _():
            idx_copy(1, 1).start()

    def prefetch(s):
        @pl.when(i + 1 < nact)
        def _():
            idx_copy(i + 1, 1 - s).wait()
            issue_rows(1 - s)

        @pl.when(i + 2 < nact)
        def _():
            idx_copy(i + 2, s).start()

    @pl.when(slot == 0)
    def _():
        prefetch(0)

    @pl.when(slot == 1)
    def _():
        prefetch(1)

    @pl.when(i < nact)
    def _():
        pltpu.make_async_copy(h_hbm.at[pl.ds(0, R), :], xbuf.at[slot], sem_rows.at[slot]).wait()
        xb = xbuf[slot].astype(BF16)
        g = jnp.dot(xb, wg_ref[...], preferred_element_type=F32)
        u = jnp.dot(xb, wu_ref[...], preferred_element_type=F32)
        hmid = (g * jax.nn.sigmoid(g) * u).astype(BF16)
        o_ref[...] = jnp.dot(hmid, wd_ref[...], preferred_element_type=F32)

    @pl.when(i >= nact)
    def _():
        o_ref[...] = jnp.zeros_like(o_ref)


def _expert_mlp(h, slot_tok, blk_e, n_active, wg, wu, wd):
    nb, R = slot_tok.shape
    D = h.shape[1]
    F = wg.shape[2]
    grid_spec = pltpu.PrefetchScalarGridSpec(
        num_scalar_prefetch=2,
        grid=(nb,),
        in_specs=[pl.BlockSpec(memory_space=pl.ANY),
                  pl.BlockSpec(memory_space=pl.ANY),
                  pl.BlockSpec((None, D, F), lambda i, be, na: (be[i], 0, 0)),
                  pl.BlockSpec((None, D, F), lambda i, be, na: (be[i], 0, 0)),
                  pl.BlockSpec((None, F, D), lambda i, be, na: (be[i], 0, 0))],
        out_specs=pl.BlockSpec((R, D), lambda i, be, na: (i, 0)),
        scratch_shapes=[pltpu.SMEM((R,), jnp.int32), pltpu.SMEM((R,), jnp.int32), pltpu.VMEM((2, R, D), F32),
                        pltpu.SemaphoreType.DMA((2,)), pltpu.SemaphoreType.DMA((2,))],
    )
    return pl.pallas_call(
        _expert_kernel,
        grid_spec=grid_spec,
        out_shape=jax.ShapeDtypeStruct((nb * R, D), F32),
        compiler_params=_cparams(("arbitrary",)),
        name="moe_experts",
    )(blk_e, n_active, slot_tok, h, wg, wu, wd)


def _combine_kernel(*refs, tiles_per_batch, n_cast):
    p1_hbm, p2_hbm, yb_hbm, mf_ref, r_ref, g_ref = refs[:6]
    o_ref = refs[6 + n_cast]
    i1_smem, i2_smem, g1_ref, g2_ref, sem_idx, sem_rows = refs[7 + 2 * n_cast:]
    b = pl.program_id(0)
    i = pl.program_id(1)
    t = b * tiles_per_batch + i
    R = o_ref.shape[0]
    c1 = pltpu.make_async_copy(p1_hbm.at[t], i1_smem, sem_idx)
    c2 = pltpu.make_async_copy(p2_hbm.at[t], i2_smem, sem_idx)
    c1.start()
    c2.start()
    c1.wait()
    c2.wait()

    def issue(r, c):
        _row_copy(yb_hbm, i1_smem, g1_ref, sem_rows, r).start(priority=0)
        _row_copy(yb_hbm, i2_smem, g2_ref, sem_rows, r).start(priority=1)
        return c

    lax.fori_loop(0, R, issue, 0, unroll=8)
    _side_cast(refs[6:6 + n_cast], refs[7 + n_cast:7 + 2 * n_cast])
    pltpu.make_async_copy(yb_hbm.at[pl.ds(0, R), :], g1_ref, sem_rows).wait()
    pltpu.make_async_copy(yb_hbm.at[pl.ds(0, R), :], g2_ref, sem_rows).wait()
    mf = mf_ref[...]
    y = mf[:, 0:1] * g1_ref[...] + mf[:, 1:2] * g2_ref[...]
    o_ref[...] = r_ref[...] + g_ref[...] * y


def _moe_combine(yb, pos, mf, res, mods, k_gate, cast, *, n_ctx, row0, n_rows):
    B, T, _ = mf.shape
    D = yb.shape[1]
    R = MOE_ROWS
    tpb = T // R
    r0 = row0 // R
    sel = _mod_sel((n_ctx - row0) // R if n_ctx > row0 else 0, B)
    p1 = pos[:, 0].reshape(B * tpb, R)
    p2 = pos[:, 1].reshape(B * tpb, R)
    nt = n_rows // R
    views, c_in, c_out, c_shapes = _side_cast_specs(cast, B * nt, lambda b, i: b * nt + i)
    outs = pl.pallas_call(
        functools.partial(_combine_kernel, tiles_per_batch=tpb, n_cast=len(views)),
        grid=(B, nt),
        in_specs=[pl.BlockSpec(memory_space=pl.ANY), pl.BlockSpec(memory_space=pl.ANY),
                  pl.BlockSpec(memory_space=pl.ANY),
                  pl.BlockSpec((None, R, LANES), lambda b, i: (b, i + r0, 0)),
                  pl.BlockSpec((None, R, D), lambda b, i: (b, i + r0, 0)),
                  pl.BlockSpec((None, None, 1, D), lambda b, i: (sel(b, i), k_gate, 0, 0))] + c_in,
        out_specs=[pl.BlockSpec((None, R, D), lambda b, i: (b, i, 0))] + c_out,
        out_shape=[jax.ShapeDtypeStruct((B, n_rows, D), F32)] + c_shapes,
        scratch_shapes=[pltpu.SMEM((R,), jnp.int32), pltpu.SMEM((R,), jnp.int32),
                        pltpu.VMEM((R, D), F32), pltpu.VMEM((R, D), F32),
                        pltpu.SemaphoreType.DMA, pltpu.SemaphoreType.DMA],
        compiler_params=_cparams(("arbitrary", "arbitrary")),
        name="moe_combine",
    )(p1, p2, yb, mf, res, mods, *views)
    return outs[0], outs[1:]


def _hier_moe(u, gamma, mods, w_route_group, w_route_expert, wg, wu, wd, cast, *, n_ctx, row0, n_rows):
    B, T, D = u.shape
    E = MOE_EXPERTS
    wr = jnp.zeros((D, LANES), F32)
    wr = wr.at[:, :MOE_GROUPS].set(w_route_group.astype(F32))
    wr = wr.at[:, MOE_GROUPS:MOE_GROUPS + MOE_EXPERTS].set(w_route_expert.astype(F32))
    h, mf, mi = _router(u, gamma, mods, 3, 4, wr, n_ctx=n_ctx, tm=MOE_ROWS)
    slot_tok, pos, blk_e, n_active = _moe_plan(mi[:, :, :2].reshape(B * T, 2))
    yb = _expert_mlp(h.reshape(B * T, D), slot_tok, blk_e, n_active, wg.reshape(E, D, -1), wu.reshape(E, D, -1),
                     wd.reshape(E, -1, D))
    return _moe_combine(yb, pos, mf, u, mods, 5, cast, n_ctx=n_ctx, row0=row0, n_rows=n_rows)


def _ab_layer(u, mods, gamma, w_in, w_out, s5p, s5_d, s5_w_glu, gla_w_gate2, gla_b_gate, gla_norm_g, cast, *, n_ctx):
    a_width = s5_w_glu.shape[0]
    n_in = w_in.shape[1]
    n_pad = (n_in + LANES - 1) // LANES * LANES
    w_in_b = jnp.pad(w_in, ((0, 0), (0, n_pad - n_in))).astype(BF16)
    px = _normmod_matmul(u, gamma, mods, 0, 1, w_in_b, n_ctx=n_ctx, tm=256, tn=n_pad // 3)
    G = a_width // S5_GROUP
    ops = _s5_operators(*s5p)
    dsk = jnp.tile(s5_d.astype(F32).reshape(G, 1, S5_GROUP), (1, S5_CHUNK, 1)).reshape(G, 1, S5_CHUNK * S5_GROUP)
    ya = _s5_glu(_s5_scan(px, ops, dsk, a_width=a_width), s5_w_glu.astype(BF16), tm=256)
    yb, casted = _gla(px, gla_w_gate2, gla_b_gate, gla_norm_g, cast, n_ctx=n_ctx, a_width=a_width)
    out = _matmul_residual([ya, yb], w_out.astype(BF16), u, mods, 2, n_ctx=n_ctx, row0=0, tm=256, tn=1024)
    return out, casted


def _na_layer(u, mods, gamma, w_qkv, w_out, q_norm, k_norm, rpb, cast, *, n_ctx):
    qkv = _normmod_matmul(u, gamma, mods, 0, 1, w_qkv.astype(BF16), n_ctx=n_ctx, tm=256, tn=2048)
    o, casted = _na_attention(qkv, q_norm, k_norm, rpb, cast, n_ctx=n_ctx)
    out = _matmul_residual([o], w_out.astype(BF16), u, mods, 2, n_ctx=0, row0=n_ctx, tm=256, tn=1024)
    return out, casted


def kernel(x, c, ctx, c_ctx, ada_w, ada_b, norm1_g, norm2_g, ab_w_in, ab_w_out, s5_lam_re, s5_lam_im, s5_log_dt,
           s5_b_re, s5_b_im, s5_c_re, s5_c_im, s5_d, s5_w_glu, gla_w_gate2, gla_b_gate, gla_norm_g, na_w_qkv,
           na_w_out, na_q_norm, na_k_norm, na_rpb, moe_w_route_group, moe_w_route_expert, moe_w_gate, moe_w_up,
           moe_w_down):
    B, seq, D = x.shape
    n_ctx = ctx.shape[1]
    depth = ada_w.shape[0]
    assert depth == 2 and n_ctx == S5_CHUNK * S5_SUPER and seq % (GRID_W * 4) == 0
    u = jnp.concatenate([ctx, x], axis=1)
    cvec = jnp.zeros((8, D), F32).at[:B].set(c).at[B].set(c_ctx)
    mods_all = _ada_mod(cvec, ada_w, ada_b)[:, :B + 1]
    mods = mods_all[0]
    s5p = (s5_lam_re[0], s5_lam_im[0], s5_log_dt[0], s5_b_re[0], s5_b_im[0], s5_c_re[0], s5_c_im[0])
    u, (wg0, wu0, wd0) = _ab_layer(u, mods, norm1_g[0], ab_w_in[0], ab_w_out[0], s5p, s5_d[0], s5_w_glu[0],
                                   gla_w_gate2[0], gla_b_gate[0], gla_norm_g[0],
                                   [(moe_w_gate, 0), (moe_w_up, 0), (moe_w_down, 0)], n_ctx=n_ctx)
    u, (wd1,) = _hier_moe(u, norm2_g[0], mods, moe_w_route_group[0], moe_w_route_expert[0], wg0, wu0, wd0,
                          [(moe_w_down, 1)], n_ctx=n_ctx, row0=0, n_rows=n_ctx + seq)
    mods = mods_all[1]
    xl, (wg1, wu1) = _na_layer(u, mods, norm1_g[1], na_w_qkv[0], na_w_out[0], na_q_norm[0], na_k_norm[0], na_rpb[0],
                               [(moe_w_gate, 1), (moe_w_up, 1)], n_ctx=n_ctx)
    out, _ = _hier_moe(xl, norm2_g[1], mods, moe_w_route_group[1], moe_w_route_expert[1], wg1, wu1, wd1, [],
                       n_ctx=0, row0=0, n_rows=seq)
    return out
```

```python
import functools

import jax
import jax.numpy as jnp
from jax import lax
from jax.experimental import pallas as pl
from jax.experimental.pallas import tpu as pltpu

F32 = jnp.float32
BF16 = jnp.bfloat16

V7X_VMEM_BYTES = 64 * 1024 * 1024
VMEM_LIMIT = V7X_VMEM_BYTES - 12 * 1024 * 1024
LANES = 128

EPS = 1e-6
NEG_INF = -1e30
GRID_W = 64
N_MOD = 6
S5_GROUP = 16
S5_STATE = 64
S5_CHUNK = 16
S5_SUPER = 16
GLA_HEADS = 4
GLA_RANK = 16
GLA_TAU = 16.0
GLA_CHUNK = 64
GLA_BLOCK = 256
GLA_HEADS_PER_STEP = 4
NA_HEAD_DIM = 128
NA_KH = 8
NA_KW = 16
NA_QROWS = 4
NA_UROWS = NA_KH + NA_QROWS - 1
ROPE_THETA = 10000.0
MOE_GROUPS = 4
MOE_PER_GROUP = 8
MOE_EXPERTS = MOE_GROUPS * MOE_PER_GROUP
MOE_ROWS = 256


def _cparams(sem):
    return pltpu.CompilerParams(dimension_semantics=sem, vmem_limit_bytes=VMEM_LIMIT)


def _nt(a, b):
    return lax.dot_general(a, b, (((1,), (1,)), ((), ())), preferred_element_type=F32)


def _side_cast_specs(tensors, n_steps, lin):
    nblocks = 1 << (n_steps.bit_length() - 1)

    def slab(*ids):
        return jnp.minimum(lin(*ids), nblocks - 1)

    views, in_specs, out_specs, out_shapes = [], [], [], []
    for t, layer in tensors:
        v = t.reshape(t.shape[0], -1, t.shape[-1])
        rows, cols = v.shape[1:]
        assert rows % (16 * nblocks) == 0
        views.append(v)
        in_specs.append(pl.BlockSpec((None, rows // nblocks, cols), functools.partial(
            lambda layer, *ids: (layer, slab(*ids), 0), layer)))
        out_specs.append(pl.BlockSpec((rows // nblocks, cols), lambda *ids: (slab(*ids), 0)))
        out_shapes.append(jax.ShapeDtypeStruct((rows, cols), BF16))
    return views, in_specs, out_specs, out_shapes


def _side_cast(in_refs, out_refs):
    for i_ref, o_ref in zip(in_refs, out_refs):
        o_ref[...] = i_ref[...].astype(BF16)


def _ada_kernel(c_ref, w_ref, b_ref, o_ref):
    c = c_ref[...]
    s = c * jax.nn.sigmoid(c)
    o_ref[...] = jnp.dot(s.astype(BF16), w_ref[...].astype(BF16), preferred_element_type=F32) + b_ref[...]


def _ada_mod(cvec, w, b):
    R, D = cvec.shape
    L, _, N = w.shape
    tn = 1024
    out = pl.pallas_call(
        _ada_kernel,
        grid=(L, N // tn),
        in_specs=[pl.BlockSpec((R, D), lambda l, j: (0, 0)),
                  pl.BlockSpec((None, D, tn), lambda l, j: (l, 0, j)),
                  pl.BlockSpec((None, 1, tn), lambda l, j: (l, 0, j))],
        out_specs=pl.BlockSpec((None, R, tn), lambda l, j: (l, 0, j)),
        out_shape=jax.ShapeDtypeStruct((L, R, N), F32),
        compiler_params=_cparams(("arbitrary", "arbitrary")),
        name="ada_mod",
    )(cvec, w, b.reshape(L, 1, N))
    return out.reshape(L, R, N_MOD, 1, D)


def _mod_sel(nct, nb):
    if nct == 0:
        return lambda b, i: b
    return lambda b, i: jnp.where(i < nct, nb, b)


def _normmod(x, g, sh, sc):
    y = x * lax.rsqrt(jnp.mean(x * x, axis=-1, keepdims=True) + EPS) * g
    return y * (1.0 + sc) + sh


def _normmod_mm_kernel(u_ref, g_ref, sh_ref, sc_ref, w_ref, o_ref, *, tn):
    xn = _normmod(u_ref[...], g_ref[...], sh_ref[...], sc_ref[...]).astype(BF16)
    for n0 in range(0, o_ref.shape[-1], tn):
        o_ref[:, n0:n0 + tn] = jnp.dot(xn, w_ref[:, n0:n0 + tn], preferred_element_type=F32).astype(o_ref.dtype)


def _normmod_matmul(u, gamma, mods, k_shift, k_scale, w, *, n_ctx, tm, tn, out_dtype=F32):
    B, T, D = u.shape
    N = w.shape[1]
    sel = _mod_sel(n_ctx // tm, B)
    return pl.pallas_call(
        functools.partial(_normmod_mm_kernel, tn=tn),
        grid=(B, T // tm),
        in_specs=[pl.BlockSpec((None, tm, D), lambda b, i: (b, i, 0)),
                  pl.BlockSpec((1, D), lambda b, i: (0, 0)),
                  pl.BlockSpec((None, None, 1, D), lambda b, i: (sel(b, i), k_shift, 0, 0)),
                  pl.BlockSpec((None, None, 1, D), lambda b, i: (sel(b, i), k_scale, 0, 0)),
                  pl.BlockSpec((D, N), lambda b, i: (0, 0), pipeline_mode=pl.Buffered(1))],
        out_specs=pl.BlockSpec((None, tm, N), lambda b, i: (b, i, 0)),
        out_shape=jax.ShapeDtypeStruct((B, T, N), out_dtype),
        compiler_params=_cparams(("arbitrary", "arbitrary")),
        name="normmod_matmul",
    )(u, gamma.reshape(1, D), mods, mods, w)


def _mm_res_kernel(*refs, n_lhs, tn):
    a_refs, w_ref, r_ref, g_ref, o_ref = refs[:n_lhs], refs[n_lhs], refs[n_lhs + 1], refs[n_lhs + 2], refs[n_lhs + 3]
    lhs = [a[...] for a in a_refs]
    for n0 in range(0, o_ref.shape[-1], tn):
        acc, k0 = None, 0
        for a in lhs:
            part = jnp.dot(a, w_ref[k0:k0 + a.shape[1], n0:n0 + tn], preferred_element_type=F32)
            acc = part if acc is None else acc + part
            k0 += a.shape[1]
        o_ref[:, n0:n0 + tn] = r_ref[:, n0:n0 + tn] + g_ref[:, n0:n0 + tn] * acc


def _matmul_residual(lhs, w, res, mods, k_gate, *, n_ctx, row0, tm, tn):
    B, Ta, _ = lhs[0].shape
    K, N = w.shape
    sel = _mod_sel(n_ctx // tm, B)
    r0 = row0 // tm
    return pl.pallas_call(
        functools.partial(_mm_res_kernel, n_lhs=len(lhs), tn=tn),
        grid=(B, Ta // tm),
        in_specs=[pl.BlockSpec((None, tm, a.shape[2]), lambda b, i: (b, i, 0)) for a in lhs]
        + [pl.BlockSpec((K, N), lambda b, i: (0, 0), pipeline_mode=pl.Buffered(1)),
           pl.BlockSpec((None, tm, N), lambda b, i: (b, i + r0, 0)),
           pl.BlockSpec((None, None, 1, N), lambda b, i: (sel(b, i), k_gate, 0, 0))],
        out_specs=pl.BlockSpec((None, tm, N), lambda b, i: (b, i, 0)),
        out_shape=jax.ShapeDtypeStruct((B, Ta, N), F32),
        compiler_params=_cparams(("arbitrary", "arbitrary")),
        name="matmul_residual",
    )(*lhs, w, res, mods)


def _s5_operators(lam_re, lam_im, log_dt, b_re, b_im, c_re, c_im):
    T = S5_CHUNK
    lr = jnp.minimum(lam_re.astype(F32), -1e-4)
    li = lam_im.astype(F32)
    dt = jnp.exp(log_dt.astype(F32))[..., None]
    mag = jnp.exp(lr * dt)
    a_re = mag * jnp.cos(li * dt)
    a_im = mag * jnp.sin(li * dt)
    num_re = a_re - 1.0
    den = lr * lr + li * li
    f_re = (num_re * lr + a_im * li) / den
    f_im = (a_im * lr - num_re * li) / den
    bb_re = f_re[..., None] * b_re.astype(F32) - f_im[..., None] * b_im.astype(F32)
    bb_im = f_re[..., None] * b_im.astype(F32) + f_im[..., None] * b_re.astype(F32)
    cr, ci = c_re.astype(F32), c_im.astype(F32)

    def powers(step, n):
        m = (step * jnp.arange(n + 1, dtype=F32))[:, None, None, None]
        mg = jnp.exp(m * (lr * dt))
        return mg * jnp.cos(m * (li * dt)), mg * jnp.sin(m * (li * dt))

    pr, pi = powers(1, T)
    pr16, pi16 = powers(T, S5_SUPER)
    ca_re = cr[None] * pr[:T, :, :, None, :] - ci[None] * pi[:T, :, :, None, :]
    ca_im = cr[None] * pi[:T, :, :, None, :] + ci[None] * pr[:T, :, :, None, :]
    kk = (jnp.einsum('kdgip,dgpj->dkgij', ca_re, bb_re) - jnp.einsum('kdgip,dgpj->dkgij', ca_im, bb_im))
    G = lr.shape[1]
    I = S5_GROUP
    s_idx = jnp.arange(T)[:, None]
    t_idx = jnp.arange(T)[None, :]

    def toeplitz(kd, lag):
        m = jnp.where((lag >= 0)[:, :, None, None, None], kd[jnp.clip(lag, 0, T - 1)], 0.0)
        return jnp.transpose(m, (2, 0, 4, 1, 3)).reshape(G, T * I, T * I)

    mt = toeplitz(kk[0], t_idx - s_idx) + toeplitz(kk[1], s_idx - t_idx)

    def v_op(d, qr, qi):
        vr = qr[:, :, :, None] * bb_re[d][None] - qi[:, :, :, None] * bb_im[d][None]
        vi = qr[:, :, :, None] * bb_im[d][None] + qi[:, :, :, None] * bb_re[d][None]
        vr = jnp.transpose(vr, (1, 0, 3, 2)).reshape(G, T * I, S5_STATE)
        vi = jnp.transpose(vi, (1, 0, 3, 2)).reshape(G, T * I, S5_STATE)
        return jnp.concatenate([vr, vi, -vi, vr], axis=-1)

    vt = jnp.concatenate([v_op(0, pr[:T, 0][::-1], pi[:T, 0][::-1]), v_op(1, pr[:T, 1], pi[:T, 1])], axis=-1)

    def w_op(d, qr, qi):
        wr = cr[d][None] * qr[:, :, None, :] - ci[d][None] * qi[:, :, None, :]
        wi = cr[d][None] * qi[:, :, None, :] + ci[d][None] * qr[:, :, None, :]
        wr = jnp.transpose(wr, (1, 3, 0, 2)).reshape(G, S5_STATE, T * I)
        wi = jnp.transpose(wi, (1, 3, 0, 2)).reshape(G, S5_STATE, T * I)
        return jnp.concatenate([wr, -wi], axis=1)

    w = jnp.concatenate([w_op(0, pr[1:T + 1, 0], pi[1:T + 1, 0]),
                         w_op(1, pr[1:T + 1, 1][::-1], pi[1:T + 1, 1][::-1])], axis=1)

    def dup(z):
        return jnp.concatenate([z, z], -1)

    rows = []
    for d in range(2):
        rows += [dup(pr16[1, d]), dup(pi16[1, d]), dup(pr16[S5_SUPER, d]), dup(pi16[S5_SUPER, d])]
    dec = jnp.stack(rows, axis=1)
    prs = []
    for d in range(2):
        prs += [dup(jnp.transpose(pr16[:S5_SUPER, d], (1, 0, 2))), dup(jnp.transpose(pi16[:S5_SUPER, d], (1, 0, 2)))]
    pwt = jnp.concatenate(prs, axis=1)
    return mt.astype(BF16), vt.astype(BF16), w.astype(BF16), dec, pwt


S5_LANE_GROUPS = LANES // S5_GROUP


def _piece_transpose(arrs, lane):
    for d in (4, 2, 1):
        bit = ((lane // S5_GROUP) & d) != 0
        new = list(arrs)
        for r in range(S5_LANE_GROUPS):
            if r & d == 0:
                a, b = arrs[r], arrs[r + d]
                new[r] = jnp.where(bit, pltpu.roll(b, S5_GROUP * d, 1), a)
                new[r + d] = jnp.where(bit, b, pltpu.roll(a, LANES - S5_GROUP * d, 1))
        arrs = new
    return arrs


def _s5_kernel(u_ref, mt_ref, vt_ref, w_ref, dec_ref, pw_ref, dsk_ref, y_ref, xg_ref, yg_ref, vv_ref, hin_ref,
               hs_ref, *, nsc):
    nj = S5_SUPER
    nch = u_ref.shape[0] // S5_CHUNK
    ng = S5_LANE_GROUPS
    lane = lax.broadcasted_iota(jnp.int32, (nch, LANES), 1)
    for h in range(2):
        zs = [u_ref[pl.ds(ng * h + t, nch, stride=S5_CHUNK), :] for t in range(ng)]
        for g, a in enumerate(_piece_transpose(zs, lane)):
            xg_ref[g, :, LANES * h:LANES * (h + 1)] = a
    hs_ref[...] = jnp.zeros_like(hs_ref)

    def group(g, carry):
        x = xg_ref[g]
        xb = x.astype(BF16)
        yg_ref[g] = jnp.dot(xb, mt_ref[g], preferred_element_type=F32) + x * dsk_ref[g]
        vv = jnp.dot(xb, vt_ref[g], preferred_element_type=F32)
        for n in range(4):
            vv_ref[n] = vv[:, LANES * n:LANES * (n + 1)]
        dec = dec_ref[g]
        pw = pw_ref[g]
        for d in range(2):
            ar16, ai16, ar256, ai256 = (dec[4 * d + n:4 * d + n + 1] for n in range(4))

            def v(j, d=d):
                return vv_ref[2 * d, pl.ds(j, nsc, stride=nj), :]

            def jv(j, d=d):
                return vv_ref[2 * d + 1, pl.ds(j, nsc, stride=nj), :]

            order = list(range(nj)) if d == 0 else list(range(nj - 1, -1, -1))
            loc = [None] * nj
            l = jnp.zeros((nsc, LANES), F32)
            gg = jnp.zeros((nsc, LANES), F32)
            for n, j in enumerate(order):
                if n > 0:
                    p = order[n - 1]
                    l, gg = ar16 * l + ai16 * gg + v(p), ar16 * gg - ai16 * l + jv(p)
                loc[j] = l
            p = order[-1]
            e, je = ar16 * l + ai16 * gg + v(p), ar16 * gg - ai16 * l + jv(p)
            seq = list(range(nsc)) if d == 0 else [0] + list(range(nsc - 1, 0, -1))
            cur = jnp.zeros((1, LANES), F32)
            jcur = jnp.zeros((1, LANES), F32)
            for m in seq:
                hs_ref[2 * d, m:m + 1, :] = cur
                hs_ref[2 * d + 1, m:m + 1, :] = jcur
                cur, jcur = (ar256 * cur + ai256 * jcur + e[m:m + 1], ar256 * jcur - ai256 * cur + je[m:m + 1])
            hs = hs_ref[2 * d, 0:nsc, :]
            jhs = hs_ref[2 * d + 1, 0:nsc, :]
            for n, j in enumerate(order):
                pr = pw[32 * d + n:32 * d + n + 1]
                pi = pw[32 * d + 16 + n:32 * d + 16 + n + 1]
                hin_ref[d, pl.ds(j, nsc, stride=nj), :] = loc[j] + pr * hs + pi * jhs
        hin = jnp.concatenate([hin_ref[0], hin_ref[1]], axis=1).astype(BF16)
        yg_ref[g] += jnp.dot(hin, w_ref[g], preferred_element_type=F32)
        return carry

    lax.fori_loop(0, ng, group, 0, unroll=2)
    for h in range(2):
        ys = [yg_ref[g, :, LANES * h:LANES * (h + 1)] for g in range(ng)]
        for t, a in enumerate(_piece_transpose(ys, lane)):
            y_ref[pl.ds(ng * h + t, nch, stride=S5_CHUNK), :] = a


def _s5_scan(px, ops, dsk, *, a_width):
    B, T, _ = px.shape
    mt, vt, w, dec, pw = ops
    ng = S5_LANE_GROUPS
    C = S5_CHUNK * S5_GROUP
    nch = T // S5_CHUNK
    nsc = nch // S5_SUPER
    nsp = (nsc + 7) // 8 * 8
    return pl.pallas_call(
        functools.partial(_s5_kernel, nsc=nsc),
        grid=(a_width // LANES, B),
        in_specs=[pl.BlockSpec((None, T, LANES), lambda gb, b: (b, 0, gb)),
                  pl.BlockSpec((ng, C, C), lambda gb, b: (gb, 0, 0)),
                  pl.BlockSpec((ng, C, 2 * C), lambda gb, b: (gb, 0, 0)),
                  pl.BlockSpec((ng, C, C), lambda gb, b: (gb, 0, 0)),
                  pl.BlockSpec((ng, 8, LANES), lambda gb, b: (gb, 0, 0)),
                  pl.BlockSpec((ng, 64, LANES), lambda gb, b: (gb, 0, 0)),
                  pl.BlockSpec((ng, 1, C), lambda gb, b: (gb, 0, 0))],
        out_specs=pl.BlockSpec((None, T, LANES), lambda gb, b: (b, 0, gb)),
        out_shape=jax.ShapeDtypeStruct((B, T, a_width), F32),
        scratch_shapes=[pltpu.VMEM((ng, nch, C), F32), pltpu.VMEM((ng, nch, C), F32),
                        pltpu.VMEM((4, nch, LANES), F32), pltpu.VMEM((2, nch, LANES), F32),
                        pltpu.VMEM((4, nsp, LANES), F32)],
        compiler_params=_cparams(("arbitrary", "arbitrary")),
        name="s5_scan",
    )(px, mt, vt, w, dec, pw, dsk)


def _s5_glu_kernel(y_ref, w_ref, o_ref):
    g = jax.nn.gelu(y_ref[...])
    z = jnp.dot(g.astype(BF16), w_ref[...], preferred_element_type=F32)
    o_ref[...] = (g * jax.nn.sigmoid(z)).astype(o_ref.dtype)


def _s5_glu(y, w, *, tm):
    B, T, A = y.shape
    return pl.pallas_call(
        _s5_glu_kernel,
        grid=(B, T // tm),
        in_specs=[pl.BlockSpec((None, tm, A), lambda b, i: (b, i, 0)),
                  pl.BlockSpec((A, A), lambda b, i: (0, 0))],
        out_specs=pl.BlockSpec((None, tm, A), lambda b, i: (b, i, 0)),
        out_shape=jax.ShapeDtypeStruct((B, T, A), BF16),
        compiler_params=_cparams(("arbitrary", "arbitrary")),
        name="s5_glu",
    )(y, w)


def _log_sigmoid(z):
    return jnp.minimum(z, 0.0) - jnp.log(1.0 + jnp.exp(-jnp.abs(z)))


def _gla_kernel(*refs, nblk, dk, dv, hp, n_cast):
    q_ref, k_ref, v_ref, r_ref, lr_ref, cos_ref, sin_ref, wg_ref, bg_ref, ng_ref = refs[:10]
    cast_in = refs[10:10 + n_cast]
    o_ref = refs[10 + n_cast]
    cast_out = refs[11 + n_cast:11 + 2 * n_cast]
    s_ref, of_ref = refs[11 + 2 * n_cast:]
    p = pl.program_id(2)
    s = pl.program_id(3)
    C = GLA_CHUNK
    nch = GLA_BLOCK // C
    blk = jnp.where(p == 0, s, jnp.where(s == 0, 0, nblk - s))

    @pl.when(s == 0)
    def _():
        s_ref[...] = jnp.zeros_like(s_ref)

    NB = GLA_BLOCK
    lane = lax.broadcasted_iota(jnp.int32, (NB, dk), 1)
    lower = (lane & 63) < 32
    rpos = lax.broadcasted_iota(jnp.int32, (NB, dk), 0) & (C - 1)
    row = lax.broadcasted_iota(jnp.int32, (NB, NB), 0)
    col = lax.broadcasted_iota(jnp.int32, (NB, NB), 1)
    same_chunk = (row // C) == (col // C)
    eye = (lax.broadcasted_iota(jnp.int32, (dk, dk), 0) == lax.broadcasted_iota(jnp.int32, (dk, dk), 1)).astype(BF16)

    def rope(z, cos, sin):
        return z * cos + jnp.where(lower, pltpu.roll(z, dk - 32, 1), pltpu.roll(z, 32, 1)) * sin

    def run(reverse):
        _side_cast(cast_in, cast_out)
        for hd in range(hp):
            head(reverse, hd)

    def head(reverse, hd):
        mask = same_chunk & ((row <= col) if reverse else (row >= col))
        cos = cos_ref[...]
        sin = sin_ref[...]
        kcols = slice(hd * dk, (hd + 1) * dk)
        vcols = slice(hd * dv, (hd + 1) * dv)
        q = rope(q_ref[:, kcols], cos, sin) * (dk ** -0.5)
        k = rope(k_ref[:, kcols], cos, sin)
        vb = v_ref[:, vcols].astype(BF16)
        lrb = lr_ref[...].astype(BF16)
        la = _log_sigmoid(jnp.dot(lrb, wg_ref[hd], preferred_element_type=F32) + bg_ref[hd]) * (1.0 / GLA_TAU)
        b = la
        sh = 1
        while sh < C:
            if reverse:
                b = b + jnp.where(rpos < C - sh, pltpu.roll(b, NB - sh, 0), 0.0)
            else:
                b = b + jnp.where(rpos >= sh, pltpu.roll(b, sh, 0), 0.0)
            sh *= 2
        last = C - 1 if not reverse else 0
        tots = [b[c * C + last:c * C + last + 1] for c in range(nch)]
        btot = jnp.concatenate([jnp.broadcast_to(t, (C, dk)) for t in tots], 0)
        dmat = jnp.exp(jnp.concatenate(tots + [jnp.zeros((8 - nch, dk), F32)], 0).T)
        qi = (q * jnp.exp(b)).astype(BF16)
        ki = (k * jnp.exp(-b)).astype(BF16)
        ko = (k * jnp.exp(btot - b)).astype(BF16)
        att = jnp.where(mask, _nt(qi, ki), 0.0).astype(BF16)
        o = jnp.dot(att, vb, preferred_element_type=F32)
        upd, dcol = [], []
        for c in range(nch):
            kot = _nt(eye, ko[c * C:(c + 1) * C]).astype(BF16)
            upd.append(jnp.dot(kot, vb[c * C:(c + 1) * C], preferred_element_type=F32))
            dcol.append(dmat[:, c:c + 1])
        st = s_ref[hd]
        o_state = [None] * nch
        for c in (range(nch - 1, -1, -1) if reverse else range(nch)):
            o_state[c] = jnp.dot(qi[c * C:(c + 1) * C], st.astype(BF16), preferred_element_type=F32)
            st = st * dcol[c] + upd[c]
        s_ref[hd] = st
        o = o + jnp.concatenate(o_state, axis=0)
        tok = pl.ds(pl.multiple_of(blk * NB, NB), NB)
        if not reverse:
            of_ref[tok, vcols] = o
        else:
            o = o + of_ref[tok, vcols]
            o = o * lax.rsqrt(jnp.mean(o * o, axis=-1, keepdims=True) + EPS) * ng_ref[hd]
            r = r_ref[:, vcols]
            o_ref[:, vcols] = (o * (r * jax.nn.sigmoid(r))).astype(o_ref.dtype)

    @pl.when(p == 0)
    def _():
        run(False)

    @pl.when(p == 1)
    def _():
        run(True)


def _rope_tables(n_ctx, seq, dk):
    nf = dk // 4
    inv = ROPE_THETA ** (-jnp.arange(nf, dtype=F32) / nf)
    pos = jnp.arange(seq)
    ang_r = (pos // GRID_W).astype(F32)[:, None] * inv[None, :]
    ang_c = (pos % GRID_W).astype(F32)[:, None] * inv[None, :]
    cos = jnp.concatenate([jnp.cos(ang_r)] * 2 + [jnp.cos(ang_c)] * 2, axis=-1)
    sin = jnp.concatenate([-jnp.sin(ang_r), jnp.sin(ang_r), -jnp.sin(ang_c), jnp.sin(ang_c)], axis=-1)
    cos = jnp.concatenate([jnp.ones((n_ctx, dk), F32), cos], axis=0)
    sin = jnp.concatenate([jnp.zeros((n_ctx, dk), F32), sin], axis=0)
    return cos, sin


def _gla(px, w_gate2, b_gate, norm_g, cast, *, n_ctx, a_width):
    B, T, _ = px.shape
    H = GLA_HEADS
    qk = w_gate2.shape[-1]
    dk = qk // H
    bw = norm_g.shape[-1]
    dv = bw // H
    nblk = T // GLA_BLOCK
    cos, sin = _rope_tables(n_ctx, T - n_ctx, dk)
    lr_col0 = a_width + 2 * qk + 2 * bw
    wg = jnp.zeros((2, H, LANES, dk), F32)
    for d in range(2):
        wg = wg.at[d, :, d * GLA_RANK:(d + 1) * GLA_RANK, :].set(
            jnp.transpose(w_gate2[d].reshape(GLA_RANK, H, dk), (1, 0, 2)))
    wg = wg.astype(BF16)
    bg = b_gate.astype(F32).reshape(2, H, 1, dk)
    ng = norm_g.astype(F32).reshape(H, 1, dv)
    hp = GLA_HEADS_PER_STEP

    def blk(p, s):
        return jnp.where(p == 0, s, jnp.where(s == 0, 0, nblk - s))

    kw, vw = hp * dk, hp * dv
    qb, kb = a_width // kw, (a_width + qk) // kw
    vbk, rbk = (a_width + 2 * qk) // vw, (a_width + 2 * qk + bw) // vw
    lb = lr_col0 // LANES
    ngrp = H // hp
    views, c_in, c_out, c_shapes = _side_cast_specs(
        cast, B * ngrp * 2 * nblk, lambda b, g, p, s: ((b * ngrp + g) * 2 + p) * nblk + s)
    outs = pl.pallas_call(
        functools.partial(_gla_kernel, nblk=nblk, dk=dk, dv=dv, hp=hp, n_cast=len(views)),
        grid=(B, ngrp, 2, nblk),
        in_specs=[pl.BlockSpec((None, GLA_BLOCK, kw), lambda b, g, p, s: (b, blk(p, s), qb + g)),
                  pl.BlockSpec((None, GLA_BLOCK, kw), lambda b, g, p, s: (b, blk(p, s), kb + g)),
                  pl.BlockSpec((None, GLA_BLOCK, vw), lambda b, g, p, s: (b, blk(p, s), vbk + g)),
                  pl.BlockSpec((None, GLA_BLOCK, vw), lambda b, g, p, s: (b, blk(p, s), rbk + g)),
                  pl.BlockSpec((None, GLA_BLOCK, LANES), lambda b, g, p, s: (b, blk(p, s), lb)),
                  pl.BlockSpec((GLA_BLOCK, dk), lambda b, g, p, s: (blk(p, s), 0)),
                  pl.BlockSpec((GLA_BLOCK, dk), lambda b, g, p, s: (blk(p, s), 0)),
                  pl.BlockSpec((None, hp, LANES, dk), lambda b, g, p, s: (p, g, 0, 0)),
                  pl.BlockSpec((None, hp, 1, dk), lambda b, g, p, s: (p, g, 0, 0)),
                  pl.BlockSpec((hp, 1, dv), lambda b, g, p, s: (g, 0, 0))] + c_in,
        out_specs=[pl.BlockSpec((None, GLA_BLOCK, vw),
                                lambda b, g, p, s: (b, jnp.where(p == 0, 0, blk(p, s)), g))] + c_out,
        out_shape=[jax.ShapeDtypeStruct((B, T, bw), BF16)] + c_shapes,
        scratch_shapes=[pltpu.VMEM((hp, dk, dv), F32), pltpu.VMEM((T, vw), F32)],
        compiler_params=_cparams(("arbitrary", "arbitrary", "arbitrary", "arbitrary")),
        name="gla",
    )(px, px, px, px, px, cos, sin, wg, bg, ng, *views)
    return outs[0], outs[1:]


def _na_kernel(*refs, n_ctx, rows, n_cast):
    q_ref, k_ref, v_ref, bias_ref, qn_ref, kn_ref = refs[:6]
    o_ref = refs[6 + n_cast]
    kb_ref, vb_ref = refs[7 + 2 * n_cast:]
    _side_cast(refs[6:6 + n_cast], refs[7 + n_cast:7 + 2 * n_cast])
    hd = NA_HEAD_DIM
    W = GRID_W
    T = k_ref.shape[0]
    step = n_ctx

    def prep(i, c):
        sl = pl.ds(pl.multiple_of(i * step, step), step)
        kk = k_ref[sl, :]
        kb_ref[sl, :] = (kk * lax.rsqrt(jnp.mean(kk * kk, axis=-1, keepdims=True) + EPS) * kn_ref[...]).astype(BF16)
        vb_ref[sl, :] = v_ref[sl, :].astype(BF16)
        return c

    lax.fori_loop(0, T // step, prep, 0)
    scale = hd ** -0.5
    ngrp = rows // NA_QROWS
    nq = NA_QROWS * W
    nk = NA_UROWS * W

    def group(g, c):
        r0 = g * NA_QROWS
        ustart = jnp.clip(r0 - NA_KH // 2, 0, rows - NA_UROWS)
        case = jnp.where(g == 0, 0, jnp.where(g == ngrp - 1, 2, 1))
        q = q_ref[pl.ds(pl.multiple_of(n_ctx + r0 * W, W), nq), :]
        q = (q * lax.rsqrt(jnp.mean(q * q, axis=-1, keepdims=True) + EPS) * qn_ref[...] * scale).astype(BF16)
        ws = pl.ds(pl.multiple_of(n_ctx + ustart * W, W), nk)
        s_lat = _nt(q, kb_ref[ws, :]) + bias_ref[case]
        s_ctx = _nt(q, kb_ref[0:n_ctx, :])
        m = jnp.maximum(jnp.max(s_lat, axis=-1, keepdims=True), jnp.max(s_ctx, axis=-1, keepdims=True))
        p_lat = jnp.exp(s_lat - m)
        p_ctx = jnp.exp(s_ctx - m)
        den = jnp.sum(p_lat, axis=-1, keepdims=True) + jnp.sum(p_ctx, axis=-1, keepdims=True)
        o = (jnp.dot(p_lat.astype(BF16), vb_ref[ws, :], preferred_element_type=F32)
             + jnp.dot(p_ctx.astype(BF16), vb_ref[0:n_ctx, :], preferred_element_type=F32))
        o_ref[pl.ds(pl.multiple_of(r0 * W, W), nq), :] = (o / den).astype(o_ref.dtype)
        return c

    lax.fori_loop(0, ngrp, group, 0, unroll=4)


def _na_bias(rpb, rows):
    H = rpb.shape[0]
    col = jnp.arange(GRID_W)
    cs = jnp.clip(col - NA_KW // 2, 0, GRID_W - NA_KW)
    col_ok = (col[None, :] >= cs[:, None]) & (col[None, :] < cs[:, None] + NA_KW)
    dc_idx = jnp.clip(col[None, :] - col[:, None] + NA_KW - 1, 0, 2 * NA_KW - 2)
    rpb_c = jnp.where(col_ok[None, None], rpb.astype(F32)[:, :, dc_idx], NEG_INF)
    ngrp = rows // NA_QROWS
    tables = []
    for g in (0, 1, ngrp - 1):
        r0 = g * NA_QROWS
        ustart = min(max(r0 - NA_KH // 2, 0), rows - NA_UROWS)
        qr = r0 + jnp.arange(NA_QROWS)
        start = jnp.clip(qr - NA_KH // 2, 0, rows - NA_KH)
        kr = ustart + jnp.arange(NA_UROWS)
        ok = (kr[None, :] >= start[:, None]) & (kr[None, :] < start[:, None] + NA_KH)
        idx = jnp.clip(kr[None, :] - qr[:, None] + NA_KH - 1, 0, 2 * NA_KH - 2)
        t = jnp.where(ok[None, :, :, None, None], rpb_c[:, idx], NEG_INF)
        tables.append(jnp.transpose(t, (0, 1, 3, 2, 4)).reshape(H, NA_QROWS * GRID_W, NA_UROWS * GRID_W))
    return jnp.stack(tables, axis=1)


def _na_attention(qkv, q_norm, k_norm, rpb, cast, *, n_ctx):
    B, T, D3 = qkv.shape
    D = D3 // 3
    H = D // NA_HEAD_DIM
    seq = T - n_ctx
    rows = seq // GRID_W
    assert rows % NA_QROWS == 0 and rows >= NA_UROWS + NA_QROWS
    bias = _na_bias(rpb, rows)
    hd = NA_HEAD_DIM
    views, c_in, c_out, c_shapes = _side_cast_specs(cast, B * H, lambda b, h: b * H + h)
    outs = pl.pallas_call(
        functools.partial(_na_kernel, n_ctx=n_ctx, rows=rows, n_cast=len(views)),
        grid=(B, H),
        in_specs=[pl.BlockSpec((None, T, hd), lambda b, h: (b, 0, h)),
                  pl.BlockSpec((None, T, hd), lambda b, h: (b, 0, H + h)),
                  pl.BlockSpec((None, T, hd), lambda b, h: (b, 0, 2 * H + h)),
                  pl.BlockSpec((None, 3, NA_QROWS * GRID_W, NA_UROWS * GRID_W), lambda b, h: (h, 0, 0, 0)),
                  pl.BlockSpec((1, hd), lambda b, h: (0, 0)),
                  pl.BlockSpec((1, hd), lambda b, h: (0, 0))] + c_in,
        out_specs=[pl.BlockSpec((None, seq, hd), lambda b, h: (b, 0, h))] + c_out,
        out_shape=[jax.ShapeDtypeStruct((B, seq, D), BF16)] + c_shapes,
        scratch_shapes=[pltpu.VMEM((T, hd), BF16), pltpu.VMEM((T, hd), BF16)],
        compiler_params=_cparams(("arbitrary", "arbitrary")),
        name="na_attention",
    )(qkv, qkv, qkv, bias, q_norm.astype(F32).reshape(1, hd), k_norm.astype(F32).reshape(1, hd), *views)
    return outs[0], outs[1:]


def _router_kernel(u_ref, g_ref, sh_ref, sc_ref, wr_ref, h_ref, mf_ref, mi_ref):
    h = _normmod(u_ref[...], g_ref[...], sh_ref[...], sc_ref[...])
    h_ref[...] = h
    h_hi = h.astype(BF16)
    h_lo = (h - h_hi.astype(F32)).astype(BF16)
    logits = (jnp.dot(h_hi, wr_ref[0], preferred_element_type=F32) + jnp.dot(h_hi, wr_ref[1], preferred_element_type=F32)
              + jnp.dot(h_lo, wr_ref[0], preferred_element_type=F32))
    lane = lax.broadcasted_iota(jnp.int32, logits.shape, 1).astype(F32)
    big = 1e9

    def first_max(vals):
        m = jnp.max(vals, axis=-1, keepdims=True)
        return m, jnp.min(jnp.where(vals == m, lane, big), axis=-1, keepdims=True)

    gl = jnp.where(lane < MOE_GROUPS, logits, -jnp.inf)
    gmax, gidx = first_max(gl)
    p_top = 1.0 / jnp.sum(jnp.exp(gl - gmax), axis=-1, keepdims=True)
    lo = MOE_GROUPS + MOE_PER_GROUP * gidx
    el = jnp.where((lane >= lo) & (lane < lo + MOE_PER_GROUP), logits, -jnp.inf)
    v1, i1 = first_max(el)
    v2, i2 = first_max(jnp.where(lane == i1, -jnp.inf, el))
    e2 = jnp.exp(v2 - v1)
    w1 = p_top / (1.0 + e2)
    w2 = p_top * e2 / (1.0 + e2)
    mf_ref[...] = jnp.where(lane == 0, w1, jnp.where(lane == 1, w2, 0.0))
    mi_ref[...] = jnp.where(lane == 0, i1 - MOE_GROUPS, jnp.where(lane == 1, i2 - MOE_GROUPS, 0.0)).astype(jnp.int32)


def _router(u, gamma, mods, k_shift, k_scale, wr, *, n_ctx, tm):
    B, T, D = u.shape
    sel = _mod_sel(n_ctx // tm, B)
    return pl.pallas_call(
        _router_kernel,
        grid=(B, T // tm),
        in_specs=[pl.BlockSpec((None, tm, D), lambda b, i: (b, i, 0)),
                  pl.BlockSpec((1, D), lambda b, i: (0, 0)),
                  pl.BlockSpec((None, None, 1, D), lambda b, i: (sel(b, i), k_shift, 0, 0)),
                  pl.BlockSpec((None, None, 1, D), lambda b, i: (sel(b, i), k_scale, 0, 0)),
                  pl.BlockSpec((2, D, LANES), lambda b, i: (0, 0, 0))],
        out_specs=[pl.BlockSpec((None, tm, D), lambda b, i: (b, i, 0)),
                   pl.BlockSpec((None, tm, LANES), lambda b, i: (b, i, 0)),
                   pl.BlockSpec((None, tm, LANES), lambda b, i: (b, i, 0))],
        out_shape=[jax.ShapeDtypeStruct((B, T, D), F32),
                   jax.ShapeDtypeStruct((B, T, LANES), F32),
                   jax.ShapeDtypeStruct((B, T, LANES), jnp.int32)],
        compiler_params=_cparams(("arbitrary", "arbitrary")),
        name="moe_router",
    )(u, gamma.reshape(1, D), mods, mods, wr)


def _moe_plan(eid2):
    N = eid2.shape[0]
    M = 2 * N
    E = MOE_EXPERTS
    eid = eid2.reshape(-1)
    onehot = (eid[:, None] == jnp.arange(E, dtype=jnp.int32)[None, :]).astype(jnp.int32)
    csum = jnp.cumsum(onehot, axis=0)
    rank = jnp.sum(onehot * csum, axis=1) - 1
    counts = csum[-1]
    padded = (counts + MOE_ROWS - 1) // MOE_ROWS * MOE_ROWS
    pend = jnp.cumsum(padded)
    pstart = pend - padded
    dest = (jnp.sum(onehot * pstart[None, :], axis=1) + rank).astype(jnp.int32)
    P = (M + E * (MOE_ROWS - 1) + MOE_ROWS - 1) // MOE_ROWS * MOE_ROWS
    nb = P // MOE_ROWS
    slot_tok = jnp.zeros((P,), jnp.int32).at[dest].set(jnp.arange(M, dtype=jnp.int32) // 2)
    pos = dest.reshape(N, 2)
    blk_start = jnp.arange(nb, dtype=jnp.int32) * MOE_ROWS
    blk_e = jnp.minimum(jnp.sum((pend[None, :] <= blk_start[:, None]).astype(jnp.int32), axis=1), E - 1)
    n_active = (pend[-1] // MOE_ROWS).astype(jnp.int32).reshape(1)
    return slot_tok.reshape(nb, MOE_ROWS), pos, blk_e.astype(jnp.int32), n_active


def _row_copy(src_hbm, idx_smem, dst_ref, sem, r):
    return pltpu.make_async_copy(src_hbm.at[pl.ds(idx_smem[r], 1), :], dst_ref.at[pl.ds(r, 1), :], sem)


def _expert_kernel(blk_e_ref, nact_ref, idx_hbm, h_hbm, wg_ref, wu_ref, wd_ref, o_ref, idx0, idx1, xbuf, sem_idx,
                   sem_rows):
    i = pl.program_id(0)
    nact = nact_ref[0]
    R = MOE_ROWS
    slot = lax.rem(i, 2)
    idx_bufs = (idx0, idx1)

    def idx_copy(blk, s):
        return pltpu.make_async_copy(idx_hbm.at[blk], idx_bufs[s], sem_idx.at[s])

    def issue_rows(s):
        def body(r, c):
            _row_copy(h_hbm, idx_bufs[s], xbuf.at[s], sem_rows.at[s], r).start(priority=1)
            return c

        lax.fori_loop(0, R, body, 0, unroll=8)

    @pl.when(i == 0)
    def _():
        first = idx_copy(0, 0)
        first.start()
        first.wait()
        issue_rows(0)

        @pl.when(nact > 1)
        def _():
            idx_copy(1, 1).start()

    def prefetch(s):
        @pl.when(i + 1 < nact)
        def _():
            idx_copy(i + 1, 1 - s).wait()
            issue_rows(1 - s)

        @pl.when(i + 2 < nact)
        def _():
            idx_copy(i + 2, s).start()

    @pl.when(slot == 0)
    def _():
        prefetch(0)

    @pl.when(slot == 1)
    def _():
        prefetch(1)

    @pl.when(i < nact)
    def _():
        pltpu.make_async_copy(h_hbm.at[pl.ds(0, R), :], xbuf.at[slot], sem_rows.at[slot]).wait()
        xb = xbuf[slot].astype(BF16)
        g = jnp.dot(xb, wg_ref[...], preferred_element_type=F32)
        u = jnp.dot(xb, wu_ref[...], preferred_element_type=F32)
        hmid = (g * jax.nn.sigmoid(g) * u).astype(BF16)
        o_ref[...] = jnp.dot(hmid, wd_ref[...], preferred_element_type=F32)

    @pl.when(i >= nact)
    def _():
        o_ref[...] = jnp.zeros_like(o_ref)


def _expert_mlp(h, slot_tok, blk_e, n_active, wg, wu, wd):
    nb, R = slot_tok.shape
    D = h.shape[1]
    F = wg.shape[2]
    grid_spec = pltpu.PrefetchScalarGridSpec(
        num_scalar_prefetch=2,
        grid=(nb,),
        in_specs=[pl.BlockSpec(memory_space=pl.ANY),
                  pl.BlockSpec(memory_space=pl.ANY),
                  pl.BlockSpec((None, D, F), lambda i, be, na: (be[i], 0, 0)),
                  pl.BlockSpec((None, D, F), lambda i, be, na: (be[i], 0, 0)),
                  pl.BlockSpec((None, F, D), lambda i, be, na: (be[i], 0, 0))],
        out_specs=pl.BlockSpec((R, D), lambda i, be, na: (i, 0)),
        scratch_shapes=[pltpu.SMEM((R,), jnp.int32), pltpu.SMEM((R,), jnp.int32), pltpu.VMEM((2, R, D), F32),
                        pltpu.SemaphoreType.DMA((2,)), pltpu.SemaphoreType.DMA((2,))],
    )
    return pl.pallas_call(
        _expert_kernel,
        grid_spec=grid_spec,
        out_shape=jax.ShapeDtypeStruct((nb * R, D), F32),
        compiler_params=_cparams(("arbitrary",)),
        name="moe_experts",
    )(blk_e, n_active, slot_tok, h, wg, wu, wd)


def _combine_kernel(*refs, n_cast):
    p1_hbm, p2_hbm, yb_hbm, mf_ref, r_ref, g_ref = refs[:6]
    o_ref = refs[6 + n_cast]
    i1a, i1b, i2a, i2b, g1_ref, g2_ref, sem_idx, sem_rows = refs[7 + 2 * n_cast:]
    i1_bufs, i2_bufs = (i1a, i1b), (i2a, i2b)
    t = pl.program_id(0) * pl.num_programs(1) + pl.program_id(1)
    n = pl.num_programs(0) * pl.num_programs(1)
    R = o_ref.shape[0]
    slot = lax.rem(t, 2)

    def idx_copies(tile, s):
        return (pltpu.make_async_copy(p1_hbm.at[tile], i1_bufs[s], sem_idx.at[s]),
                pltpu.make_async_copy(p2_hbm.at[tile], i2_bufs[s], sem_idx.at[s]))

    def issue_rows(s):
        def body(r, c):
            _row_copy(yb_hbm, i1_bufs[s], g1_ref.at[s], sem_rows.at[s], r).start(priority=0)
            _row_copy(yb_hbm, i2_bufs[s], g2_ref.at[s], sem_rows.at[s], r).start(priority=1)
            return c

        lax.fori_loop(0, R, body, 0, unroll=8)

    @pl.when(t == 0)
    def _():
        for c in idx_copies(0, 0):
            c.start()
        for c in idx_copies(0, 0):
            c.wait()
        issue_rows(0)

        @pl.when(n > 1)
        def _():
            for c in idx_copies(1, 1):
                c.start()

    def prefetch(s):
        @pl.when(t + 1 < n)
        def _():
            for c in idx_copies(t + 1, 1 - s):
                c.wait()
            issue_rows(1 - s)

        @pl.when(t + 2 < n)
        def _():
            for c in idx_copies(t + 2, s):
                c.start()

    @pl.when(slot == 0)
    def _():
        prefetch(0)

    @pl.when(slot == 1)
    def _():
        prefetch(1)

    _side_cast(refs[6:6 + n_cast], refs[7 + n_cast:7 + 2 * n_cast])
    pltpu.make_async_copy(yb_hbm.at[pl.ds(0, R), :], g1_ref.at[slot], sem_rows.at[slot]).wait()
    pltpu.make_async_copy(yb_hbm.at[pl.ds(0, R), :], g2_ref.at[slot], sem_rows.at[slot]).wait()
    mf = mf_ref[...]
    y = mf[:, 0:1] * g1_ref[slot] + mf[:, 1:2] * g2_ref[slot]
    o_ref[...] = r_ref[...] + g_ref[...] * y


def _moe_combine(yb, pos, mf, res, mods, k_gate, cast, *, n_ctx):
    B, T, _ = mf.shape
    D = yb.shape[1]
    R = MOE_ROWS
    nt = T // R
    sel = _mod_sel(n_ctx // R, B)
    p1 = pos[:, 0].reshape(B * nt, R)
    p2 = pos[:, 1].reshape(B * nt, R)
    views, c_in, c_out, c_shapes = _side_cast_specs(cast, B * nt, lambda b, i: b * nt + i)
    outs = pl.pallas_call(
        functools.partial(_combine_kernel, n_cast=len(views)),
        grid=(B, nt),
        in_specs=[pl.BlockSpec(memory_space=pl.ANY), pl.BlockSpec(memory_space=pl.ANY),
                  pl.BlockSpec(memory_space=pl.ANY),
                  pl.BlockSpec((None, R, LANES), lambda b, i: (b, i, 0)),
                  pl.BlockSpec((None, R, D), lambda b, i: (b, i, 0)),
                  pl.BlockSpec((None, None, 1, D), lambda b, i: (sel(b, i), k_gate, 0, 0))] + c_in,
        out_specs=[pl.BlockSpec((None, R, D), lambda b, i: (b, i, 0))] + c_out,
        out_shape=[jax.ShapeDtypeStruct((B, T, D), F32)] + c_shapes,
        scratch_shapes=[pltpu.SMEM((R,), jnp.int32)] * 4
        + [pltpu.VMEM((2, R, D), F32), pltpu.VMEM((2, R, D), F32),
           pltpu.SemaphoreType.DMA((2,)), pltpu.SemaphoreType.DMA((2,))],
        compiler_params=_cparams(("arbitrary", "arbitrary")),
        name="moe_combine",
    )(p1, p2, yb, mf, res, mods, *views)
    return outs[0], outs[1:]


def _hier_moe(u, gamma, mods, w_route_group, w_route_expert, wg, wu, wd, cast, *, n_ctx):
    B, T, D = u.shape
    E = MOE_EXPERTS
    wr = jnp.zeros((D, LANES), F32)
    wr = wr.at[:, :MOE_GROUPS].set(w_route_group.astype(F32))
    wr = wr.at[:, MOE_GROUPS:MOE_GROUPS + MOE_EXPERTS].set(w_route_expert.astype(F32))
    wr_hi = wr.astype(BF16)
    wr_split = jnp.stack([wr_hi, (wr - wr_hi.astype(F32)).astype(BF16)])
    h, mf, mi = _router(u, gamma, mods, 3, 4, wr_split, n_ctx=n_ctx, tm=MOE_ROWS)
    slot_tok, pos, blk_e, n_active = _moe_plan(mi[:, :, :2].reshape(B * T, 2))
    yb = _expert_mlp(h.reshape(B * T, D), slot_tok, blk_e, n_active, wg.reshape(E, D, -1), wu.reshape(E, D, -1),
                     wd.reshape(E, -1, D))
    return _moe_combine(yb, pos, mf, u, mods, 5, cast, n_ctx=n_ctx)


def _ab_layer(u, mods, gamma, w_in, w_out, s5p, s5_d, s5_w_glu, gla_w_gate2, gla_b_gate, gla_norm_g, cast, *, n_ctx):
    a_width = s5_w_glu.shape[0]
    n_in = w_in.shape[1]
    n_pad = (n_in + LANES - 1) // LANES * LANES
    w_in_b = jnp.pad(w_in, ((0, 0), (0, n_pad - n_in))).astype(BF16)
    px = _normmod_matmul(u, gamma, mods, 0, 1, w_in_b, n_ctx=n_ctx, tm=256, tn=n_pad // 3)
    G = a_width // S5_GROUP
    ops = _s5_operators(*s5p)
    dsk = jnp.tile(s5_d.astype(F32).reshape(G, 1, S5_GROUP), (1, S5_CHUNK, 1)).reshape(G, 1, S5_CHUNK * S5_GROUP)
    ya = _s5_glu(_s5_scan(px, ops, dsk, a_width=a_width), s5_w_glu.astype(BF16), tm=256)
    yb, casted = _gla(px, gla_w_gate2, gla_b_gate, gla_norm_g, cast, n_ctx=n_ctx, a_width=a_width)
    out = _matmul_residual([ya, yb], w_out.astype(BF16), u, mods, 2, n_ctx=n_ctx, row0=0, tm=256, tn=1024)
    return out, casted


def _na_layer(u, mods, gamma, w_qkv, w_out, q_norm, k_norm, rpb, cast, *, n_ctx):
    qkv = _normmod_matmul(u, gamma, mods, 0, 1, w_qkv.astype(BF16), n_ctx=n_ctx, tm=256, tn=2048)
    o, casted = _na_attention(qkv, q_norm, k_norm, rpb, cast, n_ctx=n_ctx)
    out = _matmul_residual([o], w_out.astype(BF16), u, mods, 2, n_ctx=0, row0=n_ctx, tm=256, tn=1024)
    return out, casted


def kernel(x, c, ctx, c_ctx, ada_w, ada_b, norm1_g, norm2_g, ab_w_in, ab_w_out, s5_lam_re, s5_lam_im, s5_log_dt,
           s5_b_re, s5_b_im, s5_c_re, s5_c_im, s5_d, s5_w_glu, gla_w_gate2, gla_b_gate, gla_norm_g, na_w_qkv,
           na_w_out, na_q_norm, na_k_norm, na_rpb, moe_w_route_group, moe_w_route_expert, moe_w_gate, moe_w_up,
           moe_w_down):
    B, seq, D = x.shape
    n_ctx = ctx.shape[1]
    depth = ada_w.shape[0]
    assert depth == 2 and n_ctx == S5_CHUNK * S5_SUPER and seq % (GRID_W * 4) == 0
    u = jnp.concatenate([ctx, x], axis=1)
    cvec = jnp.zeros((8, D), F32).at[:B].set(c).at[B].set(c_ctx)
    mods_all = _ada_mod(cvec, ada_w, ada_b)[:, :B + 1]
    mods = mods_all[0]
    s5p = (s5_lam_re[0], s5_lam_im[0], s5_log_dt[0], s5_b_re[0], s5_b_im[0], s5_c_re[0], s5_c_im[0])
    u, (wg0, wu0, wd0) = _ab_layer(u, mods, norm1_g[0], ab_w_in[0], ab_w_out[0], s5p, s5_d[0], s5_w_glu[0],
                                   gla_w_gate2[0], gla_b_gate[0], gla_norm_g[0],
                                   [(moe_w_gate, 0), (moe_w_up, 0), (moe_w_down, 0)], n_ctx=n_ctx)
    u, (wd1,) = _hier_moe(u, norm2_g[0], mods, moe_w_route_group[0], moe_w_route_expert[0], wg0, wu0, wd0,
                          [(moe_w_down, 1)], n_ctx=n_ctx)
    mods = mods_all[1]
    xl, (wg1, wu1) = _na_layer(u, mods, norm1_g[1], na_w_qkv[0], na_w_out[0], na_q_norm[0], na_k_norm[0], na_rpb[0],
                               [(moe_w_gate, 1), (moe_w_up, 1)], n_ctx=n_ctx)
    out, _ = _hier_moe(xl, norm2_g[1], mods, moe_w_route_group[1], moe_w_route_expert[1], wg1, wu1, wd1, [],
                       n_ctx=0)
    return out
```

```python
import functools

import jax
import jax.numpy as jnp
from jax import lax
from jax.experimental import pallas as pl
from jax.experimental.pallas import tpu as pltpu

F32 = jnp.float32
BF16 = jnp.bfloat16

V7X_VMEM_BYTES = 64 * 1024 * 1024
VMEM_LIMIT = V7X_VMEM_BYTES - 12 * 1024 * 1024
LANES = 128

EPS = 1e-6
NEG_INF = -1e30
GRID_W = 64
N_MOD = 6
S5_GROUP = 16
S5_STATE = 64
S5_CHUNK = 16
S5_SUPER = 16
GLA_HEADS = 4
GLA_RANK = 16
GLA_TAU = 16.0
GLA_CHUNK = 64
GLA_BLOCK = 256
GLA_HEADS_PER_STEP = 4
NA_HEAD_DIM = 128
NA_KH = 8
NA_KW = 16
NA_QROWS = 4
NA_UROWS = NA_KH + NA_QROWS - 1
ROPE_THETA = 10000.0
MOE_GROUPS = 4
MOE_PER_GROUP = 8
MOE_EXPERTS = MOE_GROUPS * MOE_PER_GROUP
MOE_ROWS = 256


def _cparams(sem):
    return pltpu.CompilerParams(dimension_semantics=sem, vmem_limit_bytes=VMEM_LIMIT)


def _nt(a, b):
    return lax.dot_general(a, b, (((1,), (1,)), ((), ())), preferred_element_type=F32)


def _side_cast_specs(tensors, n_steps, lin):
    nblocks = 1 << (n_steps.bit_length() - 1)

    def slab(*ids):
        return jnp.minimum(lin(*ids), nblocks - 1)

    views, in_specs, out_specs, out_shapes = [], [], [], []
    for t, layer in tensors:
        v = t.reshape(t.shape[0], -1, t.shape[-1])
        rows, cols = v.shape[1:]
        assert rows % (16 * nblocks) == 0
        views.append(v)
        in_specs.append(pl.BlockSpec((None, rows // nblocks, cols), functools.partial(
            lambda layer, *ids: (layer, slab(*ids), 0), layer)))
        out_specs.append(pl.BlockSpec((rows // nblocks, cols), lambda *ids: (slab(*ids), 0)))
        out_shapes.append(jax.ShapeDtypeStruct((rows, cols), BF16))
    return views, in_specs, out_specs, out_shapes


def _side_cast(in_refs, out_refs):
    for i_ref, o_ref in zip(in_refs, out_refs):
        o_ref[...] = i_ref[...].astype(BF16)


def _ada_kernel(c_ref, w_ref, b_ref, o_ref):
    c = c_ref[...]
    s = c * jax.nn.sigmoid(c)
    o_ref[...] = jnp.dot(s.astype(BF16), w_ref[...].astype(BF16), preferred_element_type=F32) + b_ref[...]


def _ada_mod(cvec, w, b):
    R, D = cvec.shape
    L, _, N = w.shape
    tn = 1024
    out = pl.pallas_call(
        _ada_kernel,
        grid=(L, N // tn),
        in_specs=[pl.BlockSpec((R, D), lambda l, j: (0, 0)),
                  pl.BlockSpec((None, D, tn), lambda l, j: (l, 0, j)),
                  pl.BlockSpec((None, 1, tn), lambda l, j: (l, 0, j))],
        out_specs=pl.BlockSpec((None, R, tn), lambda l, j: (l, 0, j)),
        out_shape=jax.ShapeDtypeStruct((L, R, N), F32),
        compiler_params=_cparams(("arbitrary", "arbitrary")),
        name="ada_mod",
    )(cvec, w, b.reshape(L, 1, N))
    return out.reshape(L, R, N_MOD, 1, D)


def _mod_sel(nct, nb):
    if nct == 0:
        return lambda b, i: b
    return lambda b, i: jnp.where(i < nct, nb, b)


def _normmod(x, g, sh, sc):
    y = x * lax.rsqrt(jnp.mean(x * x, axis=-1, keepdims=True) + EPS) * g
    return y * (1.0 + sc) + sh


def _normmod_mm_kernel(u_ref, g_ref, sh_ref, sc_ref, w_ref, o_ref, *, tn):
    xn = _normmod(u_ref[...], g_ref[...], sh_ref[...], sc_ref[...]).astype(BF16)
    for n0 in range(0, o_ref.shape[-1], tn):
        o_ref[:, n0:n0 + tn] = jnp.dot(xn, w_ref[:, n0:n0 + tn], preferred_element_type=F32).astype(o_ref.dtype)


def _normmod_matmul(u, gamma, mods, k_shift, k_scale, w, *, n_ctx, tm, tn, out_dtype=F32):
    B, T, D = u.shape
    N = w.shape[1]
    sel = _mod_sel(n_ctx // tm, B)
    return pl.pallas_call(
        functools.partial(_normmod_mm_kernel, tn=tn),
        grid=(B, T // tm),
        in_specs=[pl.BlockSpec((None, tm, D), lambda b, i: (b, i, 0)),
                  pl.BlockSpec((1, D), lambda b, i: (0, 0)),
                  pl.BlockSpec((None, None, 1, D), lambda b, i: (sel(b, i), k_shift, 0, 0)),
                  pl.BlockSpec((None, None, 1, D), lambda b, i: (sel(b, i), k_scale, 0, 0)),
                  pl.BlockSpec((D, N), lambda b, i: (0, 0), pipeline_mode=pl.Buffered(1))],
        out_specs=pl.BlockSpec((None, tm, N), lambda b, i: (b, i, 0)),
        out_shape=jax.ShapeDtypeStruct((B, T, N), out_dtype),
        compiler_params=_cparams(("arbitrary", "arbitrary")),
        name="normmod_matmul",
    )(u, gamma.reshape(1, D), mods, mods, w)


def _mm_res_kernel(*refs, n_lhs, tn):
    a_refs, w_ref, r_ref, g_ref, o_ref = refs[:n_lhs], refs[n_lhs], refs[n_lhs + 1], refs[n_lhs + 2], refs[n_lhs + 3]
    lhs = [a[...] for a in a_refs]
    for n0 in range(0, o_ref.shape[-1], tn):
        acc, k0 = None, 0
        for a in lhs:
            part = jnp.dot(a, w_ref[k0:k0 + a.shape[1], n0:n0 + tn], preferred_element_type=F32)
            acc = part if acc is None else acc + part
            k0 += a.shape[1]
        o_ref[:, n0:n0 + tn] = r_ref[:, n0:n0 + tn] + g_ref[:, n0:n0 + tn] * acc


def _matmul_residual(lhs, w, res, mods, k_gate, *, n_ctx, row0, tm, tn):
    B, Ta, _ = lhs[0].shape
    K, N = w.shape
    sel = _mod_sel(n_ctx // tm, B)
    r0 = row0 // tm
    return pl.pallas_call(
        functools.partial(_mm_res_kernel, n_lhs=len(lhs), tn=tn),
        grid=(B, Ta // tm),
        in_specs=[pl.BlockSpec((None, tm, a.shape[2]), lambda b, i: (b, i, 0)) for a in lhs]
        + [pl.BlockSpec((K, N), lambda b, i: (0, 0), pipeline_mode=pl.Buffered(1)),
           pl.BlockSpec((None, tm, N), lambda b, i: (b, i + r0, 0)),
           pl.BlockSpec((None, None, 1, N), lambda b, i: (sel(b, i), k_gate, 0, 0))],
        out_specs=pl.BlockSpec((None, tm, N), lambda b, i: (b, i, 0)),
        out_shape=jax.ShapeDtypeStruct((B, Ta, N), F32),
        compiler_params=_cparams(("arbitrary", "arbitrary")),
        name="matmul_residual",
    )(*lhs, w, res, mods)


def _s5_operators(lam_re, lam_im, log_dt, b_re, b_im, c_re, c_im):
    T = S5_CHUNK
    lr = jnp.minimum(lam_re.astype(F32), -1e-4)
    li = lam_im.astype(F32)
    dt = jnp.exp(log_dt.astype(F32))[..., None]
    mag = jnp.exp(lr * dt)
    a_re = mag * jnp.cos(li * dt)
    a_im = mag * jnp.sin(li * dt)
    num_re = a_re - 1.0
    den = lr * lr + li * li
    f_re = (num_re * lr + a_im * li) / den
    f_im = (a_im * lr - num_re * li) / den
    bb_re = f_re[..., None] * b_re.astype(F32) - f_im[..., None] * b_im.astype(F32)
    bb_im = f_re[..., None] * b_im.astype(F32) + f_im[..., None] * b_re.astype(F32)
    cr, ci = c_re.astype(F32), c_im.astype(F32)

    def powers(step, n):
        m = (step * jnp.arange(n + 1, dtype=F32))[:, None, None, None]
        mg = jnp.exp(m * (lr * dt))
        return mg * jnp.cos(m * (li * dt)), mg * jnp.sin(m * (li * dt))

    pr, pi = powers(1, T)
    pr16, pi16 = powers(T, S5_SUPER)
    ca_re = cr[None] * pr[:T, :, :, None, :] - ci[None] * pi[:T, :, :, None, :]
    ca_im = cr[None] * pi[:T, :, :, None, :] + ci[None] * pr[:T, :, :, None, :]
    kk = (jnp.einsum('kdgip,dgpj->dkgij', ca_re, bb_re) - jnp.einsum('kdgip,dgpj->dkgij', ca_im, bb_im))
    G = lr.shape[1]
    I = S5_GROUP
    s_idx = jnp.arange(T)[:, None]
    t_idx = jnp.arange(T)[None, :]

    def toeplitz(kd, lag):
        m = jnp.where((lag >= 0)[:, :, None, None, None], kd[jnp.clip(lag, 0, T - 1)], 0.0)
        return jnp.transpose(m, (2, 0, 4, 1, 3)).reshape(G, T * I, T * I)

    mt = toeplitz(kk[0], t_idx - s_idx) + toeplitz(kk[1], s_idx - t_idx)

    def v_op(d, qr, qi):
        vr = qr[:, :, :, None] * bb_re[d][None] - qi[:, :, :, None] * bb_im[d][None]
        vi = qr[:, :, :, None] * bb_im[d][None] + qi[:, :, :, None] * bb_re[d][None]
        vr = jnp.transpose(vr, (1, 0, 3, 2)).reshape(G, T * I, S5_STATE)
        vi = jnp.transpose(vi, (1, 0, 3, 2)).reshape(G, T * I, S5_STATE)
        return jnp.concatenate([vr, vi, -vi, vr], axis=-1)

    vt = jnp.concatenate([v_op(0, pr[:T, 0][::-1], pi[:T, 0][::-1]), v_op(1, pr[:T, 1], pi[:T, 1])], axis=-1)

    def w_op(d, qr, qi):
        wr = cr[d][None] * qr[:, :, None, :] - ci[d][None] * qi[:, :, None, :]
        wi = cr[d][None] * qi[:, :, None, :] + ci[d][None] * qr[:, :, None, :]
        wr = jnp.transpose(wr, (1, 3, 0, 2)).reshape(G, S5_STATE, T * I)
        wi = jnp.transpose(wi, (1, 3, 0, 2)).reshape(G, S5_STATE, T * I)
        return jnp.concatenate([wr, -wi], axis=1)

    w = jnp.concatenate([w_op(0, pr[1:T + 1, 0], pi[1:T + 1, 0]),
                         w_op(1, pr[1:T + 1, 1][::-1], pi[1:T + 1, 1][::-1])], axis=1)

    def dup(z):
        return jnp.concatenate([z, z], -1)

    rows = []
    for d in range(2):
        rows += [dup(pr16[1, d]), dup(pi16[1, d]), dup(pr16[S5_SUPER, d]), dup(pi16[S5_SUPER, d])]
    dec = jnp.stack(rows, axis=1)
    prs = []
    for d in range(2):
        prs += [dup(jnp.transpose(pr16[:S5_SUPER, d], (1, 0, 2))), dup(jnp.transpose(pi16[:S5_SUPER, d], (1, 0, 2)))]
    pwt = jnp.concatenate(prs, axis=1)
    return mt.astype(BF16), vt.astype(BF16), w.astype(BF16), dec, pwt


S5_LANE_GROUPS = LANES // S5_GROUP


def _piece_transpose(arrs, lane):
    for d in (4, 2, 1):
        bit = ((lane // S5_GROUP) & d) != 0
        new = list(arrs)
        for r in range(S5_LANE_GROUPS):
            if r & d == 0:
                a, b = arrs[r], arrs[r + d]
                new[r] = jnp.where(bit, pltpu.roll(b, S5_GROUP * d, 1), a)
                new[r + d] = jnp.where(bit, b, pltpu.roll(a, LANES - S5_GROUP * d, 1))
        arrs = new
    return arrs


def _s5_kernel(u_ref, mt_ref, vt_ref, w_ref, dec_ref, pw_ref, dsk_ref, y_ref, xg_ref, yg_ref, vv_ref, hin_ref,
               hs_ref, *, nsc):
    nj = S5_SUPER
    nch = u_ref.shape[0] // S5_CHUNK
    ng = S5_LANE_GROUPS
    lane = lax.broadcasted_iota(jnp.int32, (nch, LANES), 1)
    for h in range(2):
        zs = [u_ref[pl.ds(ng * h + t, nch, stride=S5_CHUNK), :] for t in range(ng)]
        for g, a in enumerate(_piece_transpose(zs, lane)):
            xg_ref[g, :, LANES * h:LANES * (h + 1)] = a
    hs_ref[...] = jnp.zeros_like(hs_ref)

    def group(g, carry):
        x = xg_ref[g]
        xb = x.astype(BF16)
        yg_ref[g] = jnp.dot(xb, mt_ref[g], preferred_element_type=F32) + x * dsk_ref[g]
        vv = jnp.dot(xb, vt_ref[g], preferred_element_type=F32)
        for n in range(4):
            vv_ref[n] = vv[:, LANES * n:LANES * (n + 1)]
        dec = dec_ref[g]
        pw = pw_ref[g]
        for d in range(2):
            ar16, ai16, ar256, ai256 = (dec[4 * d + n:4 * d + n + 1] for n in range(4))

            def v(j, d=d):
                return vv_ref[2 * d, pl.ds(j, nsc, stride=nj), :]

            def jv(j, d=d):
                return vv_ref[2 * d + 1, pl.ds(j, nsc, stride=nj), :]

            order = list(range(nj)) if d == 0 else list(range(nj - 1, -1, -1))
            loc = [None] * nj
            l = jnp.zeros((nsc, LANES), F32)
            gg = jnp.zeros((nsc, LANES), F32)
            for n, j in enumerate(order):
                if n > 0:
                    p = order[n - 1]
                    l, gg = ar16 * l + ai16 * gg + v(p), ar16 * gg - ai16 * l + jv(p)
                loc[j] = l
            p = order[-1]
            e, je = ar16 * l + ai16 * gg + v(p), ar16 * gg - ai16 * l + jv(p)
            seq = list(range(nsc)) if d == 0 else [0] + list(range(nsc - 1, 0, -1))
            cur = jnp.zeros((1, LANES), F32)
            jcur = jnp.zeros((1, LANES), F32)
            for m in seq:
                hs_ref[2 * d, m:m + 1, :] = cur
                hs_ref[2 * d + 1, m:m + 1, :] = jcur
                cur, jcur = (ar256 * cur + ai256 * jcur + e[m:m + 1], ar256 * jcur - ai256 * cur + je[m:m + 1])
            hs = hs_ref[2 * d, 0:nsc, :]
            jhs = hs_ref[2 * d + 1, 0:nsc, :]
            for n, j in enumerate(order):
                pr = pw[32 * d + n:32 * d + n + 1]
                pi = pw[32 * d + 16 + n:32 * d + 16 + n + 1]
                hin_ref[d, pl.ds(j, nsc, stride=nj), :] = loc[j] + pr * hs + pi * jhs
        hin = jnp.concatenate([hin_ref[0], hin_ref[1]], axis=1).astype(BF16)
        yg_ref[g] += jnp.dot(hin, w_ref[g], preferred_element_type=F32)
        return carry

    lax.fori_loop(0, ng, group, 0, unroll=2)
    for h in range(2):
        ys = [yg_ref[g, :, LANES * h:LANES * (h + 1)] for g in range(ng)]
        for t, a in enumerate(_piece_transpose(ys, lane)):
            y_ref[pl.ds(ng * h + t, nch, stride=S5_CHUNK), :] = a


def _s5_scan(px, ops, dsk, *, a_width):
    B, T, _ = px.shape
    mt, vt, w, dec, pw = ops
    ng = S5_LANE_GROUPS
    C = S5_CHUNK * S5_GROUP
    nch = T // S5_CHUNK
    nsc = nch // S5_SUPER
    nsp = (nsc + 7) // 8 * 8
    return pl.pallas_call(
        functools.partial(_s5_kernel, nsc=nsc),
        grid=(a_width // LANES, B),
        in_specs=[pl.BlockSpec((None, T, LANES), lambda gb, b: (b, 0, gb)),
                  pl.BlockSpec((ng, C, C), lambda gb, b: (gb, 0, 0)),
                  pl.BlockSpec((ng, C, 2 * C), lambda gb, b: (gb, 0, 0)),
                  pl.BlockSpec((ng, C, C), lambda gb, b: (gb, 0, 0)),
                  pl.BlockSpec((ng, 8, LANES), lambda gb, b: (gb, 0, 0)),
                  pl.BlockSpec((ng, 64, LANES), lambda gb, b: (gb, 0, 0)),
                  pl.BlockSpec((ng, 1, C), lambda gb, b: (gb, 0, 0))],
        out_specs=pl.BlockSpec((None, T, LANES), lambda gb, b: (b, 0, gb)),
        out_shape=jax.ShapeDtypeStruct((B, T, a_width), F32),
        scratch_shapes=[pltpu.VMEM((ng, nch, C), F32), pltpu.VMEM((ng, nch, C), F32),
                        pltpu.VMEM((4, nch, LANES), F32), pltpu.VMEM((2, nch, LANES), F32),
                        pltpu.VMEM((4, nsp, LANES), F32)],
        compiler_params=_cparams(("arbitrary", "arbitrary")),
        name="s5_scan",
    )(px, mt, vt, w, dec, pw, dsk)


def _s5_glu_kernel(y_ref, w_ref, o_ref):
    g = jax.nn.gelu(y_ref[...])
    z = jnp.dot(g.astype(BF16), w_ref[...], preferred_element_type=F32)
    o_ref[...] = (g * jax.nn.sigmoid(z)).astype(o_ref.dtype)


def _s5_glu(y, w, *, tm):
    B, T, A = y.shape
    return pl.pallas_call(
        _s5_glu_kernel,
        grid=(B, T // tm),
        in_specs=[pl.BlockSpec((None, tm, A), lambda b, i: (b, i, 0)),
                  pl.BlockSpec((A, A), lambda b, i: (0, 0))],
        out_specs=pl.BlockSpec((None, tm, A), lambda b, i: (b, i, 0)),
        out_shape=jax.ShapeDtypeStruct((B, T, A), BF16),
        compiler_params=_cparams(("arbitrary", "arbitrary")),
        name="s5_glu",
    )(y, w)


def _log_sigmoid(z):
    return jnp.minimum(z, 0.0) - jnp.log(1.0 + jnp.exp(-jnp.abs(z)))


def _gla_kernel(*refs, nblk, dk, dv, hp, n_cast):
    q_ref, k_ref, v_ref, r_ref, lr_ref, cos_ref, sin_ref, wg_ref, bg_ref, ng_ref = refs[:10]
    cast_in = refs[10:10 + n_cast]
    o_ref = refs[10 + n_cast]
    cast_out = refs[11 + n_cast:11 + 2 * n_cast]
    s_ref, of_ref = refs[11 + 2 * n_cast:]
    p = pl.program_id(2)
    s = pl.program_id(3)
    C = GLA_CHUNK
    nch = GLA_BLOCK // C
    blk = jnp.where(p == 0, s, jnp.where(s == 0, 0, nblk - s))

    @pl.when(s == 0)
    def _():
        s_ref[...] = jnp.zeros_like(s_ref)

    NB = GLA_BLOCK
    lane = lax.broadcasted_iota(jnp.int32, (NB, dk), 1)
    lower = (lane & 63) < 32
    rpos = lax.broadcasted_iota(jnp.int32, (NB, dk), 0) & (C - 1)
    row = lax.broadcasted_iota(jnp.int32, (NB, NB), 0)
    col = lax.broadcasted_iota(jnp.int32, (NB, NB), 1)
    same_chunk = (row // C) == (col // C)
    eye = (lax.broadcasted_iota(jnp.int32, (dk, dk), 0) == lax.broadcasted_iota(jnp.int32, (dk, dk), 1)).astype(BF16)

    def rope(z, cos, sin):
        return z * cos + jnp.where(lower, pltpu.roll(z, dk - 32, 1), pltpu.roll(z, 32, 1)) * sin

    def run(reverse):
        _side_cast(cast_in, cast_out)
        for hd in range(hp):
            head(reverse, hd)

    def head(reverse, hd):
        mask = same_chunk & ((row <= col) if reverse else (row >= col))
        cos = cos_ref[...]
        sin = sin_ref[...]
        kcols = slice(hd * dk, (hd + 1) * dk)
        vcols = slice(hd * dv, (hd + 1) * dv)
        q = rope(q_ref[:, kcols], cos, sin) * (dk ** -0.5)
        k = rope(k_ref[:, kcols], cos, sin)
        vb = v_ref[:, vcols].astype(BF16)
        lrb = lr_ref[...].astype(BF16)
        la = _log_sigmoid(jnp.dot(lrb, wg_ref[hd], preferred_element_type=F32) + bg_ref[hd]) * (1.0 / GLA_TAU)
        b = la
        sh = 1
        while sh < C:
            if reverse:
                b = b + jnp.where(rpos < C - sh, pltpu.roll(b, NB - sh, 0), 0.0)
            else:
                b = b + jnp.where(rpos >= sh, pltpu.roll(b, sh, 0), 0.0)
            sh *= 2
        last = C - 1 if not reverse else 0
        tots = [b[c * C + last:c * C + last + 1] for c in range(nch)]
        btot = jnp.concatenate([jnp.broadcast_to(t, (C, dk)) for t in tots], 0)
        dmat = jnp.exp(jnp.concatenate(tots + [jnp.zeros((8 - nch, dk), F32)], 0).T)
        qi = (q * jnp.exp(b)).astype(BF16)
        ki = (k * jnp.exp(-b)).astype(BF16)
        ko = (k * jnp.exp(btot - b)).astype(BF16)
        att = jnp.where(mask, _nt(qi, ki), 0.0).astype(BF16)
        o = jnp.dot(att, vb, preferred_element_type=F32)
        upd, dcol = [], []
        for c in range(nch):
            kot = _nt(eye, ko[c * C:(c + 1) * C]).astype(BF16)
            upd.append(jnp.dot(kot, vb[c * C:(c + 1) * C], preferred_element_type=F32))
            dcol.append(dmat[:, c:c + 1])
        st = s_ref[hd]
        o_state = [None] * nch
        for c in (range(nch - 1, -1, -1) if reverse else range(nch)):
            o_state[c] = jnp.dot(qi[c * C:(c + 1) * C], st.astype(BF16), preferred_element_type=F32)
            st = st * dcol[c] + upd[c]
        s_ref[hd] = st
        o = o + jnp.concatenate(o_state, axis=0)
        tok = pl.ds(pl.multiple_of(blk * NB, NB), NB)
        if not reverse:
            of_ref[tok, vcols] = o
        else:
            o = o + of_ref[tok, vcols]
            o = o * lax.rsqrt(jnp.mean(o * o, axis=-1, keepdims=True) + EPS) * ng_ref[hd]
            r = r_ref[:, vcols]
            o_ref[:, vcols] = (o * (r * jax.nn.sigmoid(r))).astype(o_ref.dtype)

    @pl.when(p == 0)
    def _():
        run(False)

    @pl.when(p == 1)
    def _():
        run(True)


def _rope_tables(n_ctx, seq, dk):
    nf = dk // 4
    inv = ROPE_THETA ** (-jnp.arange(nf, dtype=F32) / nf)
    pos = jnp.arange(seq)
    ang_r = (pos // GRID_W).astype(F32)[:, None] * inv[None, :]
    ang_c = (pos % GRID_W).astype(F32)[:, None] * inv[None, :]
    cos = jnp.concatenate([jnp.cos(ang_r)] * 2 + [jnp.cos(ang_c)] * 2, axis=-1)
    sin = jnp.concatenate([-jnp.sin(ang_r), jnp.sin(ang_r), -jnp.sin(ang_c), jnp.sin(ang_c)], axis=-1)
    cos = jnp.concatenate([jnp.ones((n_ctx, dk), F32), cos], axis=0)
    sin = jnp.concatenate([jnp.zeros((n_ctx, dk), F32), sin], axis=0)
    return cos, sin


def _gla(px, w_gate2, b_gate, norm_g, cast, *, n_ctx, a_width):
    B, T, _ = px.shape
    H = GLA_HEADS
    qk = w_gate2.shape[-1]
    dk = qk // H
    bw = norm_g.shape[-1]
    dv = bw // H
    nblk = T // GLA_BLOCK
    cos, sin = _rope_tables(n_ctx, T - n_ctx, dk)
    lr_col0 = a_width + 2 * qk + 2 * bw
    wg = jnp.zeros((2, H, LANES, dk), F32)
    for d in range(2):
        wg = wg.at[d, :, d * GLA_RANK:(d + 1) * GLA_RANK, :].set(
            jnp.transpose(w_gate2[d].reshape(GLA_RANK, H, dk), (1, 0, 2)))
    wg = wg.astype(BF16)
    bg = b_gate.astype(F32).reshape(2, H, 1, dk)
    ng = norm_g.astype(F32).reshape(H, 1, dv)
    hp = GLA_HEADS_PER_STEP

    def blk(p, s):
        return jnp.where(p == 0, s, jnp.where(s == 0, 0, nblk - s))

    kw, vw = hp * dk, hp * dv
    qb, kb = a_width // kw, (a_width + qk) // kw
    vbk, rbk = (a_width + 2 * qk) // vw, (a_width + 2 * qk + bw) // vw
    lb = lr_col0 // LANES
    ngrp = H // hp
    views, c_in, c_out, c_shapes = _side_cast_specs(
        cast, B * ngrp * 2 * nblk, lambda b, g, p, s: ((b * ngrp + g) * 2 + p) * nblk + s)
    outs = pl.pallas_call(
        functools.partial(_gla_kernel, nblk=nblk, dk=dk, dv=dv, hp=hp, n_cast=len(views)),
        grid=(B, ngrp, 2, nblk),
        in_specs=[pl.BlockSpec((None, GLA_BLOCK, kw), lambda b, g, p, s: (b, blk(p, s), qb + g)),
                  pl.BlockSpec((None, GLA_BLOCK, kw), lambda b, g, p, s: (b, blk(p, s), kb + g)),
                  pl.BlockSpec((None, GLA_BLOCK, vw), lambda b, g, p, s: (b, blk(p, s), vbk + g)),
                  pl.BlockSpec((None, GLA_BLOCK, vw), lambda b, g, p, s: (b, blk(p, s), rbk + g)),
                  pl.BlockSpec((None, GLA_BLOCK, LANES), lambda b, g, p, s: (b, blk(p, s), lb)),
                  pl.BlockSpec((GLA_BLOCK, dk), lambda b, g, p, s: (blk(p, s), 0)),
                  pl.BlockSpec((GLA_BLOCK, dk), lambda b, g, p, s: (blk(p, s), 0)),
                  pl.BlockSpec((None, hp, LANES, dk), lambda b, g, p, s: (p, g, 0, 0)),
                  pl.BlockSpec((None, hp, 1, dk), lambda b, g, p, s: (p, g, 0, 0)),
                  pl.BlockSpec((hp, 1, dv), lambda b, g, p, s: (g, 0, 0))] + c_in,
        out_specs=[pl.BlockSpec((None, GLA_BLOCK, vw),
                                lambda b, g, p, s: (b, jnp.where(p == 0, 0, blk(p, s)), g))] + c_out,
        out_shape=[jax.ShapeDtypeStruct((B, T, bw), BF16)] + c_shapes,
        scratch_shapes=[pltpu.VMEM((hp, dk, dv), F32), pltpu.VMEM((T, vw), F32)],
        compiler_params=_cparams(("arbitrary", "arbitrary", "arbitrary", "arbitrary")),
        name="gla",
    )(px, px, px, px, px, cos, sin, wg, bg, ng, *views)
    return outs[0], outs[1:]


def _na_kernel(*refs, n_ctx, rows, n_cast):
    q_ref, k_ref, v_ref, bias_ref, qn_ref, kn_ref = refs[:6]
    o_ref = refs[6 + n_cast]
    kb_ref, vb_ref = refs[7 + 2 * n_cast:]
    _side_cast(refs[6:6 + n_cast], refs[7 + n_cast:7 + 2 * n_cast])
    hd = NA_HEAD_DIM
    W = GRID_W
    T = k_ref.shape[0]
    step = n_ctx

    def prep(i, c):
        sl = pl.ds(pl.multiple_of(i * step, step), step)
        kk = k_ref[sl, :]
        kb_ref[sl, :] = (kk * lax.rsqrt(jnp.mean(kk * kk, axis=-1, keepdims=True) + EPS) * kn_ref[...]).astype(BF16)
        vb_ref[sl, :] = v_ref[sl, :].astype(BF16)
        return c

    lax.fori_loop(0, T // step, prep, 0)
    scale = hd ** -0.5
    ngrp = rows // NA_QROWS
    nq = NA_QROWS * W
    nk = NA_UROWS * W

    def group(g, c):
        r0 = g * NA_QROWS
        ustart = jnp.clip(r0 - NA_KH // 2, 0, rows - NA_UROWS)
        case = jnp.where(g == 0, 0, jnp.where(g == ngrp - 1, 2, 1))
        q = q_ref[pl.ds(pl.multiple_of(n_ctx + r0 * W, W), nq), :]
        q = (q * lax.rsqrt(jnp.mean(q * q, axis=-1, keepdims=True) + EPS) * qn_ref[...] * scale).astype(BF16)
        ws = pl.ds(pl.multiple_of(n_ctx + ustart * W, W), nk)
        s_lat = _nt(q, kb_ref[ws, :]) + bias_ref[case]
        s_ctx = _nt(q, kb_ref[0:n_ctx, :])
        m = jnp.maximum(jnp.max(s_lat, axis=-1, keepdims=True), jnp.max(s_ctx, axis=-1, keepdims=True))
        p_lat = jnp.exp(s_lat - m)
        p_ctx = jnp.exp(s_ctx - m)
        den = jnp.sum(p_lat, axis=-1, keepdims=True) + jnp.sum(p_ctx, axis=-1, keepdims=True)
        o = (jnp.dot(p_lat.astype(BF16), vb_ref[ws, :], preferred_element_type=F32)
             + jnp.dot(p_ctx.astype(BF16), vb_ref[0:n_ctx, :], preferred_element_type=F32))
        o_ref[pl.ds(pl.multiple_of(r0 * W, W), nq), :] = (o / den).astype(o_ref.dtype)
        return c

    lax.fori_loop(0, ngrp, group, 0, unroll=4)


def _na_bias(rpb, rows):
    H = rpb.shape[0]
    col = jnp.arange(GRID_W)
    cs = jnp.clip(col - NA_KW // 2, 0, GRID_W - NA_KW)
    col_ok = (col[None, :] >= cs[:, None]) & (col[None, :] < cs[:, None] + NA_KW)
    dc_idx = jnp.clip(col[None, :] - col[:, None] + NA_KW - 1, 0, 2 * NA_KW - 2)
    rpb_c = jnp.where(col_ok[None, None], rpb.astype(F32)[:, :, dc_idx], NEG_INF)
    ngrp = rows // NA_QROWS
    tables = []
    for g in (0, 1, ngrp - 1):
        r0 = g * NA_QROWS
        ustart = min(max(r0 - NA_KH // 2, 0), rows - NA_UROWS)
        qr = r0 + jnp.arange(NA_QROWS)
        start = jnp.clip(qr - NA_KH // 2, 0, rows - NA_KH)
        kr = ustart + jnp.arange(NA_UROWS)
        ok = (kr[None, :] >= start[:, None]) & (kr[None, :] < start[:, None] + NA_KH)
        idx = jnp.clip(kr[None, :] - qr[:, None] + NA_KH - 1, 0, 2 * NA_KH - 2)
        t = jnp.where(ok[None, :, :, None, None], rpb_c[:, idx], NEG_INF)
        tables.append(jnp.transpose(t, (0, 1, 3, 2, 4)).reshape(H, NA_QROWS * GRID_W, NA_UROWS * GRID_W))
    return jnp.stack(tables, axis=1)


def _na_attention(qkv, q_norm, k_norm, rpb, cast, *, n_ctx):
    B, T, D3 = qkv.shape
    D = D3 // 3
    H = D // NA_HEAD_DIM
    seq = T - n_ctx
    rows = seq // GRID_W
    assert rows % NA_QROWS == 0 and rows >= NA_UROWS + NA_QROWS
    bias = _na_bias(rpb, rows)
    hd = NA_HEAD_DIM
    views, c_in, c_out, c_shapes = _side_cast_specs(cast, B * H, lambda b, h: b * H + h)
    outs = pl.pallas_call(
        functools.partial(_na_kernel, n_ctx=n_ctx, rows=rows, n_cast=len(views)),
        grid=(B, H),
        in_specs=[pl.BlockSpec((None, T, hd), lambda b, h: (b, 0, h)),
                  pl.BlockSpec((None, T, hd), lambda b, h: (b, 0, H + h)),
                  pl.BlockSpec((None, T, hd), lambda b, h: (b, 0, 2 * H + h)),
                  pl.BlockSpec((None, 3, NA_QROWS * GRID_W, NA_UROWS * GRID_W), lambda b, h: (h, 0, 0, 0)),
                  pl.BlockSpec((1, hd), lambda b, h: (0, 0)),
                  pl.BlockSpec((1, hd), lambda b, h: (0, 0))] + c_in,
        out_specs=[pl.BlockSpec((None, seq, hd), lambda b, h: (b, 0, h))] + c_out,
        out_shape=[jax.ShapeDtypeStruct((B, seq, D), BF16)] + c_shapes,
        scratch_shapes=[pltpu.VMEM((T, hd), BF16), pltpu.VMEM((T, hd), BF16)],
        compiler_params=_cparams(("arbitrary", "arbitrary")),
        name="na_attention",
    )(qkv, qkv, qkv, bias, q_norm.astype(F32).reshape(1, hd), k_norm.astype(F32).reshape(1, hd), *views)
    return outs[0], outs[1:]


def _router_kernel(u_ref, g_ref, sh_ref, sc_ref, wr_ref, h_ref, mf_ref, mi_ref):
    h = _normmod(u_ref[...], g_ref[...], sh_ref[...], sc_ref[...])
    h_ref[...] = h
    h_hi = h.astype(BF16)
    h_lo = (h - h_hi.astype(F32)).astype(BF16)
    logits = (jnp.dot(h_hi, wr_ref[0], preferred_element_type=F32) + jnp.dot(h_hi, wr_ref[1], preferred_element_type=F32)
              + jnp.dot(h_lo, wr_ref[0], preferred_element_type=F32))
    lane = lax.broadcasted_iota(jnp.int32, logits.shape, 1).astype(F32)
    big = 1e9

    def first_max(vals):
        m = jnp.max(vals, axis=-1, keepdims=True)
        return m, jnp.min(jnp.where(vals == m, lane, big), axis=-1, keepdims=True)

    gl = jnp.where(lane < MOE_GROUPS, logits, -jnp.inf)
    gmax, gidx = first_max(gl)
    p_top = 1.0 / jnp.sum(jnp.exp(gl - gmax), axis=-1, keepdims=True)
    lo = MOE_GROUPS + MOE_PER_GROUP * gidx
    el = jnp.where((lane >= lo) & (lane < lo + MOE_PER_GROUP), logits, -jnp.inf)
    v1, i1 = first_max(el)
    v2, i2 = first_max(jnp.where(lane == i1, -jnp.inf, el))
    e2 = jnp.exp(v2 - v1)
    w1 = p_top / (1.0 + e2)
    w2 = p_top * e2 / (1.0 + e2)
    mf_ref[...] = jnp.where(lane == 0, w1, jnp.where(lane == 1, w2, 0.0))
    mi_ref[...] = jnp.where(lane == 0, i1 - MOE_GROUPS, jnp.where(lane == 1, i2 - MOE_GROUPS, 0.0)).astype(jnp.int32)


def _router(u, gamma, mods, k_shift, k_scale, wr, *, n_ctx, tm):
    B, T, D = u.shape
    sel = _mod_sel(n_ctx // tm, B)
    return pl.pallas_call(
        _router_kernel,
        grid=(B, T // tm),
        in_specs=[pl.BlockSpec((None, tm, D), lambda b, i: (b, i, 0)),
                  pl.BlockSpec((1, D), lambda b, i: (0, 0)),
                  pl.BlockSpec((None, None, 1, D), lambda b, i: (sel(b, i), k_shift, 0, 0)),
                  pl.BlockSpec((None, None, 1, D), lambda b, i: (sel(b, i), k_scale, 0, 0)),
                  pl.BlockSpec((2, D, LANES), lambda b, i: (0, 0, 0))],
        out_specs=[pl.BlockSpec((None, tm, D), lambda b, i: (b, i, 0)),
                   pl.BlockSpec((None, tm, LANES), lambda b, i: (b, i, 0)),
                   pl.BlockSpec((None, tm, LANES), lambda b, i: (b, i, 0))],
        out_shape=[jax.ShapeDtypeStruct((B, T, D), F32),
                   jax.ShapeDtypeStruct((B, T, LANES), F32),
                   jax.ShapeDtypeStruct((B, T, LANES), jnp.int32)],
        compiler_params=_cparams(("arbitrary", "arbitrary")),
        name="moe_router",
    )(u, gamma.reshape(1, D), mods, mods, wr)


def _moe_plan(eid2):
    N = eid2.shape[0]
    M = 2 * N
    E = MOE_EXPERTS
    eid = eid2.reshape(-1)
    onehot = (eid[:, None] == jnp.arange(E, dtype=jnp.int32)[None, :]).astype(jnp.int32)
    csum = jnp.cumsum(onehot, axis=0)
    rank = jnp.sum(onehot * csum, axis=1) - 1
    counts = csum[-1]
    padded = (counts + MOE_ROWS - 1) // MOE_ROWS * MOE_ROWS
    pend = jnp.cumsum(padded)
    pstart = pend - padded
    dest = (jnp.sum(onehot * pstart[None, :], axis=1) + rank).astype(jnp.int32)
    P = (M + E * (MOE_ROWS - 1) + MOE_ROWS - 1) // MOE_ROWS * MOE_ROWS
    nb = P // MOE_ROWS
    slot_tok = jnp.zeros((P,), jnp.int32).at[dest].set(jnp.arange(M, dtype=jnp.int32) // 2)
    pos = dest.reshape(N, 2)
    blk_start = jnp.arange(nb, dtype=jnp.int32) * MOE_ROWS
    blk_e = jnp.minimum(jnp.sum((pend[None, :] <= blk_start[:, None]).astype(jnp.int32), axis=1), E - 1)
    n_active = (pend[-1] // MOE_ROWS).astype(jnp.int32).reshape(1)
    return slot_tok.reshape(nb, MOE_ROWS), pos, blk_e.astype(jnp.int32), n_active


def _row_copy(src_hbm, idx_smem, dst_ref, sem, r):
    return pltpu.make_async_copy(src_hbm.at[pl.ds(idx_smem[r], 1), :], dst_ref.at[pl.ds(r, 1), :], sem)


def _expert_kernel(blk_e_ref, nact_ref, idx_hbm, h_hbm, wg_ref, wu_ref, wd_ref, o_ref, idx0, idx1, xbuf, xb_ref,
                   sem_idx, sem_rows):
    i = pl.program_id(0)
    nact = nact_ref[0]
    R = MOE_ROWS
    slot = lax.rem(i, 2)
    idx_bufs = (idx0, idx1)

    def idx_copy(blk, s):
        return pltpu.make_async_copy(idx_hbm.at[blk], idx_bufs[s], sem_idx.at[s])

    def issue_rows(s):
        def body(r, c):
            _row_copy(h_hbm, idx_bufs[s], xbuf, sem_rows, r).start(priority=1)
            return c

        lax.fori_loop(0, R, body, 0, unroll=8)

    @pl.when(i == 0)
    def _():
        first = idx_copy(0, 0)
        first.start()
        first.wait()
        issue_rows(0)

        @pl.when(nact > 1)
        def _():
            idx_copy(1, 1).start()

    @pl.when(i < nact)
    def _():
        pltpu.make_async_copy(h_hbm.at[pl.ds(0, R), :], xbuf, sem_rows).wait()
        xb_ref[...] = xbuf[...].astype(BF16)

    def prefetch(s):
        @pl.when(i + 1 < nact)
        def _():
            idx_copy(i + 1, 1 - s).wait()
            issue_rows(1 - s)

        @pl.when(i + 2 < nact)
        def _():
            idx_copy(i + 2, s).start()

    @pl.when(slot == 0)
    def _():
        prefetch(0)

    @pl.when(slot == 1)
    def _():
        prefetch(1)

    @pl.when(i < nact)
    def _():
        xb = xb_ref[...]
        g = jnp.dot(xb, wg_ref[...], preferred_element_type=F32)
        u = jnp.dot(xb, wu_ref[...], preferred_element_type=F32)
        hmid = (g * jax.nn.sigmoid(g) * u).astype(BF16)
        o_ref[...] = jnp.dot(hmid, wd_ref[...], preferred_element_type=F32)

    @pl.when(i >= nact)
    def _():
        o_ref[...] = jnp.zeros_like(o_ref)


def _expert_mlp(h, slot_tok, blk_e, n_active, wg, wu, wd):
    nb, R = slot_tok.shape
    D = h.shape[1]
    F = wg.shape[2]
    grid_spec = pltpu.PrefetchScalarGridSpec(
        num_scalar_prefetch=2,
        grid=(nb,),
        in_specs=[pl.BlockSpec(memory_space=pl.ANY),
                  pl.BlockSpec(memory_space=pl.ANY),
                  pl.BlockSpec((None, D, F), lambda i, be, na: (be[i], 0, 0)),
                  pl.BlockSpec((None, D, F), lambda i, be, na: (be[i], 0, 0)),
                  pl.BlockSpec((None, F, D), lambda i, be, na: (be[i], 0, 0))],
        out_specs=pl.BlockSpec((R, D), lambda i, be, na: (i, 0)),
        scratch_shapes=[pltpu.SMEM((R,), jnp.int32), pltpu.SMEM((R,), jnp.int32), pltpu.VMEM((R, D), F32),
                        pltpu.VMEM((R, D), BF16), pltpu.SemaphoreType.DMA((2,)), pltpu.SemaphoreType.DMA],
    )
    return pl.pallas_call(
        _expert_kernel,
        grid_spec=grid_spec,
        out_shape=jax.ShapeDtypeStruct((nb * R, D), F32),
        compiler_params=_cparams(("arbitrary",)),
        name="moe_experts",
    )(blk_e, n_active, slot_tok, h, wg, wu, wd)


def _combine_kernel(*refs, n_cast):
    p1_hbm, p2_hbm, yb_hbm, mf_ref, r_ref, g_ref = refs[:6]
    o_ref = refs[6 + n_cast]
    i1a, i1b, i2a, i2b, g1_ref, g2_ref, sem_idx, sem_rows = refs[7 + 2 * n_cast:]
    i1_bufs, i2_bufs = (i1a, i1b), (i2a, i2b)
    t = pl.program_id(0) * pl.num_programs(1) + pl.program_id(1)
    n = pl.num_programs(0) * pl.num_programs(1)
    R = o_ref.shape[0]
    slot = lax.rem(t, 2)

    def idx_copies(tile, s):
        return (pltpu.make_async_copy(p1_hbm.at[tile], i1_bufs[s], sem_idx.at[s]),
                pltpu.make_async_copy(p2_hbm.at[tile], i2_bufs[s], sem_idx.at[s]))

    def issue_rows(s):
        def body(r, c):
            _row_copy(yb_hbm, i1_bufs[s], g1_ref.at[s], sem_rows.at[s], r).start(priority=0)
            _row_copy(yb_hbm, i2_bufs[s], g2_ref.at[s], sem_rows.at[s], r).start(priority=1)
            return c

        lax.fori_loop(0, R, body, 0, unroll=8)

    @pl.when(t == 0)
    def _():
        for c in idx_copies(0, 0):
            c.start()
        for c in idx_copies(0, 0):
            c.wait()
        issue_rows(0)

        @pl.when(n > 1)
        def _():
            for c in idx_copies(1, 1):
                c.start()

    def prefetch(s):
        @pl.when(t + 1 < n)
        def _():
            for c in idx_copies(t + 1, 1 - s):
                c.wait()
            issue_rows(1 - s)

        @pl.when(t + 2 < n)
        def _():
            for c in idx_copies(t + 2, s):
                c.start()

    @pl.when(slot == 0)
    def _():
        prefetch(0)

    @pl.when(slot == 1)
    def _():
        prefetch(1)

    _side_cast(refs[6:6 + n_cast], refs[7 + n_cast:7 + 2 * n_cast])
    pltpu.make_async_copy(yb_hbm.at[pl.ds(0, R), :], g1_ref.at[slot], sem_rows.at[slot]).wait()
    pltpu.make_async_copy(yb_hbm.at[pl.ds(0, R), :], g2_ref.at[slot], sem_rows.at[slot]).wait()
    mf = mf_ref[...]
    y = mf[:, 0:1] * g1_ref[slot] + mf[:, 1:2] * g2_ref[slot]
    o_ref[...] = r_ref[...] + g_ref[...] * y


def _moe_combine(yb, pos, mf, res, mods, k_gate, cast, *, n_ctx):
    B, T, _ = mf.shape
    D = yb.shape[1]
    R = MOE_ROWS
    nt = T // R
    sel = _mod_sel(n_ctx // R, B)
    p1 = pos[:, 0].reshape(B * nt, R)
    p2 = pos[:, 1].reshape(B * nt, R)
    views, c_in, c_out, c_shapes = _side_cast_specs(cast, B * nt, lambda b, i: b * nt + i)
    outs = pl.pallas_call(
        functools.partial(_combine_kernel, n_cast=len(views)),
        grid=(B, nt),
        in_specs=[pl.BlockSpec(memory_space=pl.ANY), pl.BlockSpec(memory_space=pl.ANY),
                  pl.BlockSpec(memory_space=pl.ANY),
                  pl.BlockSpec((None, R, LANES), lambda b, i: (b, i, 0)),
                  pl.BlockSpec((None, R, D), lambda b, i: (b, i, 0)),
                  pl.BlockSpec((None, None, 1, D), lambda b, i: (sel(b, i), k_gate, 0, 0))] + c_in,
        out_specs=[pl.BlockSpec((None, R, D), lambda b, i: (b, i, 0))] + c_out,
        out_shape=[jax.ShapeDtypeStruct((B, T, D), F32)] + c_shapes,
        scratch_shapes=[pltpu.SMEM((R,), jnp.int32)] * 4
        + [pltpu.VMEM((2, R, D), F32), pltpu.VMEM((2, R, D), F32),
           pltpu.SemaphoreType.DMA((2,)), pltpu.SemaphoreType.DMA((2,))],
        compiler_params=_cparams(("arbitrary", "arbitrary")),
        name="moe_combine",
    )(p1, p2, yb, mf, res, mods, *views)
    return outs[0], outs[1:]


def _hier_moe(u, gamma, mods, w_route_group, w_route_expert, wg, wu, wd, cast, *, n_ctx):
    B, T, D = u.shape
    E = MOE_EXPERTS
    wr = jnp.zeros((D, LANES), F32)
    wr = wr.at[:, :MOE_GROUPS].set(w_route_group.astype(F32))
    wr = wr.at[:, MOE_GROUPS:MOE_GROUPS + MOE_EXPERTS].set(w_route_expert.astype(F32))
    wr_hi = wr.astype(BF16)
    wr_split = jnp.stack([wr_hi, (wr - wr_hi.astype(F32)).astype(BF16)])
    h, mf, mi = _router(u, gamma, mods, 3, 4, wr_split, n_ctx=n_ctx, tm=MOE_ROWS)
    slot_tok, pos, blk_e, n_active = _moe_plan(mi[:, :, :2].reshape(B * T, 2))
    yb = _expert_mlp(h.reshape(B * T, D), slot_tok, blk_e, n_active, wg.reshape(E, D, -1), wu.reshape(E, D, -1),
                     wd.reshape(E, -1, D))
    return _moe_combine(yb, pos, mf, u, mods, 5, cast, n_ctx=n_ctx)


def _ab_layer(u, mods, gamma, w_in, w_out, s5p, s5_d, s5_w_glu, gla_w_gate2, gla_b_gate, gla_norm_g, cast, *, n_ctx):
    a_width = s5_w_glu.shape[0]
    n_in = w_in.shape[1]
    n_pad = (n_in + LANES - 1) // LANES * LANES
    w_in_b = jnp.pad(w_in, ((0, 0), (0, n_pad - n_in))).astype(BF16)
    px = _normmod_matmul(u, gamma, mods, 0, 1, w_in_b, n_ctx=n_ctx, tm=256, tn=n_pad // 3)
    G = a_width // S5_GROUP
    ops = _s5_operators(*s5p)
    dsk = jnp.tile(s5_d.astype(F32).reshape(G, 1, S5_GROUP), (1, S5_CHUNK, 1)).reshape(G, 1, S5_CHUNK * S5_GROUP)
    ya = _s5_glu(_s5_scan(px, ops, dsk, a_width=a_width), s5_w_glu.astype(BF16), tm=256)
    yb, casted = _gla(px, gla_w_gate2, gla_b_gate, gla_norm_g, cast, n_ctx=n_ctx, a_width=a_width)
    out = _matmul_residual([ya, yb], w_out.astype(BF16), u, mods, 2, n_ctx=n_ctx, row0=0, tm=256, tn=1024)
    return out, casted


def _na_layer(u, mods, gamma, w_qkv, w_out, q_norm, k_norm, rpb, cast, *, n_ctx):
    qkv = _normmod_matmul(u, gamma, mods, 0, 1, w_qkv.astype(BF16), n_ctx=n_ctx, tm=256, tn=2048)
    o, casted = _na_attention(qkv, q_norm, k_norm, rpb, cast, n_ctx=n_ctx)
    out = _matmul_residual([o], w_out.astype(BF16), u, mods, 2, n_ctx=0, row0=n_ctx, tm=256, tn=1024)
    return out, casted


def kernel(x, c, ctx, c_ctx, ada_w, ada_b, norm1_g, norm2_g, ab_w_in, ab_w_out, s5_lam_re, s5_lam_im, s5_log_dt,
           s5_b_re, s5_b_im, s5_c_re, s5_c_im, s5_d, s5_w_glu, gla_w_gate2, gla_b_gate, gla_norm_g, na_w_qkv,
           na_w_out, na_q_norm, na_k_norm, na_rpb, moe_w_route_group, moe_w_route_expert, moe_w_gate, moe_w_up,
           moe_w_down):
    B, seq, D = x.shape
    n_ctx = ctx.shape[1]
    depth = ada_w.shape[0]
    assert depth == 2 and n_ctx == S5_CHUNK * S5_SUPER and seq % (GRID_W * 4) == 0
    u = jnp.concatenate([ctx, x], axis=1)
    cvec = jnp.zeros((8, D), F32).at[:B].set(c).at[B].set(c_ctx)
    mods_all = _ada_mod(cvec, ada_w, ada_b)[:, :B + 1]
    mods = mods_all[0]
    s5p = (s5_lam_re[0], s5_lam_im[0], s5_log_dt[0], s5_b_re[0], s5_b_im[0], s5_c_re[0], s5_c_im[0])
    u, (wg0, wu0, wd0) = _ab_layer(u, mods, norm1_g[0], ab_w_in[0], ab_w_out[0], s5p, s5_d[0], s5_w_glu[0],
                                   gla_w_gate2[0], gla_b_gate[0], gla_norm_g[0],
                                   [(moe_w_gate, 0), (moe_w_up, 0), (moe_w_down, 0)], n_ctx=n_ctx)
    u, (wd1,) = _hier_moe(u, norm2_g[0], mods, moe_w_route_group[0], moe_w_route_expert[0], wg0, wu0, wd0,
                          [(moe_w_down, 1)], n_ctx=n_ctx)
    mods = mods_all[1]
    xl, (wg1, wu1) = _na_layer(u, mods, norm1_g[1], na_w_qkv[0], na_w_out[0], na_q_norm[0], na_k_norm[0], na_rpb[0],
                               [(moe_w_gate, 1), (moe_w_up, 1)], n_ctx=n_ctx)
    out, _ = _hier_moe(xl, norm2_g[1], mods, moe_w_route_group[1], moe_w_route_expert[1], wg1, wu1, wd1, [],
                       n_ctx=0)
    return out
```

```python
import functools

import jax
import jax.numpy as jnp
from jax import lax
from jax.experimental import pallas as pl
from jax.experimental.pallas import tpu as pltpu

F32 = jnp.float32
BF16 = jnp.bfloat16
HIGHEST = lax.Precision.HIGHEST

V7X_VMEM_BYTES = 64 * 1024 * 1024
VMEM_LIMIT = V7X_VMEM_BYTES - 12 * 1024 * 1024
LANES = 128

EPS = 1e-6
NEG_INF = -1e30
GRID_W = 64
N_MOD = 6
S5_GROUP = 16
S5_STATE = 64
S5_CHUNK = 16
S5_SUPER = 16
GLA_HEADS = 4
GLA_RANK = 16
GLA_TAU = 16.0
GLA_CHUNK = 64
GLA_BLOCK = 256
GLA_HEADS_PER_STEP = 4
NA_HEAD_DIM = 128
NA_KH = 8
NA_KW = 16
NA_QROWS = 4
NA_UROWS = NA_KH + NA_QROWS - 1
ROPE_THETA = 10000.0
MOE_GROUPS = 4
MOE_PER_GROUP = 8
MOE_EXPERTS = MOE_GROUPS * MOE_PER_GROUP
MOE_ROWS = 256


def _cparams(sem):
    return pltpu.CompilerParams(dimension_semantics=sem, vmem_limit_bytes=VMEM_LIMIT)


def _nt(a, b):
    return lax.dot_general(a, b, (((1,), (1,)), ((), ())), preferred_element_type=F32)


def _side_cast_specs(tensors, n_steps, lin):
    nblocks = 1 << (n_steps.bit_length() - 1)

    def slab(*ids):
        return jnp.minimum(lin(*ids), nblocks - 1)

    views, in_specs, out_specs, out_shapes = [], [], [], []
    for t, layer in tensors:
        v = t.reshape(t.shape[0], -1, t.shape[-1])
        rows, cols = v.shape[1:]
        assert rows % (16 * nblocks) == 0
        views.append(v)
        in_specs.append(pl.BlockSpec((None, rows // nblocks, cols), functools.partial(
            lambda layer, *ids: (layer, slab(*ids), 0), layer)))
        out_specs.append(pl.BlockSpec((rows // nblocks, cols), lambda *ids: (slab(*ids), 0)))
        out_shapes.append(jax.ShapeDtypeStruct((rows, cols), BF16))
    return views, in_specs, out_specs, out_shapes


def _side_cast(in_refs, out_refs):
    for i_ref, o_ref in zip(in_refs, out_refs):
        o_ref[...] = i_ref[...].astype(BF16)


def _ada_kernel(c_ref, w_ref, b_ref, o_ref):
    c = c_ref[...]
    s = c * jax.nn.sigmoid(c)
    o_ref[...] = jnp.dot(s.astype(BF16), w_ref[...].astype(BF16), preferred_element_type=F32) + b_ref[...]


def _ada_mod(cvec, w, b):
    R, D = cvec.shape
    L, _, N = w.shape
    tn = 1024
    out = pl.pallas_call(
        _ada_kernel,
        grid=(L, N // tn),
        in_specs=[pl.BlockSpec((R, D), lambda l, j: (0, 0)),
                  pl.BlockSpec((None, D, tn), lambda l, j: (l, 0, j)),
                  pl.BlockSpec((None, 1, tn), lambda l, j: (l, 0, j))],
        out_specs=pl.BlockSpec((None, R, tn), lambda l, j: (l, 0, j)),
        out_shape=jax.ShapeDtypeStruct((L, R, N), F32),
        compiler_params=_cparams(("arbitrary", "arbitrary")),
        name="ada_mod",
    )(cvec, w, b.reshape(L, 1, N))
    return out.reshape(L, R, N_MOD, 1, D)


def _mod_sel(nct, nb):
    if nct == 0:
        return lambda b, i: b
    return lambda b, i: jnp.where(i < nct, nb, b)


def _normmod(x, g, sh, sc):
    y = x * lax.rsqrt(jnp.mean(x * x, axis=-1, keepdims=True) + EPS) * g
    return y * (1.0 + sc) + sh


def _normmod_mm_kernel(u_ref, g_ref, sh_ref, sc_ref, w_ref, o_ref, *, tn):
    xn = _normmod(u_ref[...], g_ref[...], sh_ref[...], sc_ref[...]).astype(BF16)
    for n0 in range(0, o_ref.shape[-1], tn):
        o_ref[:, n0:n0 + tn] = jnp.dot(xn, w_ref[:, n0:n0 + tn], preferred_element_type=F32).astype(o_ref.dtype)


def _normmod_matmul(u, gamma, mods, k_shift, k_scale, w, *, n_ctx, tm, tn, out_dtype=F32):
    B, T, D = u.shape
    N = w.shape[1]
    sel = _mod_sel(n_ctx // tm, B)
    return pl.pallas_call(
        functools.partial(_normmod_mm_kernel, tn=tn),
        grid=(B, T // tm),
        in_specs=[pl.BlockSpec((None, tm, D), lambda b, i: (b, i, 0)),
                  pl.BlockSpec((1, D), lambda b, i: (0, 0)),
                  pl.BlockSpec((None, None, 1, D), lambda b, i: (sel(b, i), k_shift, 0, 0)),
                  pl.BlockSpec((None, None, 1, D), lambda b, i: (sel(b, i), k_scale, 0, 0)),
                  pl.BlockSpec((D, N), lambda b, i: (0, 0), pipeline_mode=pl.Buffered(1))],
        out_specs=pl.BlockSpec((None, tm, N), lambda b, i: (b, i, 0)),
        out_shape=jax.ShapeDtypeStruct((B, T, N), out_dtype),
        compiler_params=_cparams(("arbitrary", "arbitrary")),
        name="normmod_matmul",
    )(u, gamma.reshape(1, D), mods, mods, w)


def _mm_res_kernel(*refs, n_lhs, tn):
    a_refs, w_ref, r_ref, g_ref, o_ref = refs[:n_lhs], refs[n_lhs], refs[n_lhs + 1], refs[n_lhs + 2], refs[n_lhs + 3]
    lhs = [a[...] for a in a_refs]
    for n0 in range(0, o_ref.shape[-1], tn):
        acc, k0 = None, 0
        for a in lhs:
            part = jnp.dot(a, w_ref[k0:k0 + a.shape[1], n0:n0 + tn], preferred_element_type=F32)
            acc = part if acc is None else acc + part
            k0 += a.shape[1]
        o_ref[:, n0:n0 + tn] = r_ref[:, n0:n0 + tn] + g_ref[:, n0:n0 + tn] * acc


def _matmul_residual(lhs, w, res, mods, k_gate, *, n_ctx, row0, tm, tn):
    B, Ta, _ = lhs[0].shape
    K, N = w.shape
    sel = _mod_sel(n_ctx // tm, B)
    r0 = row0 // tm
    return pl.pallas_call(
        functools.partial(_mm_res_kernel, n_lhs=len(lhs), tn=tn),
        grid=(B, Ta // tm),
        in_specs=[pl.BlockSpec((None, tm, a.shape[2]), lambda b, i: (b, i, 0)) for a in lhs]
        + [pl.BlockSpec((K, N), lambda b, i: (0, 0), pipeline_mode=pl.Buffered(1)),
           pl.BlockSpec((None, tm, N), lambda b, i: (b, i + r0, 0)),
           pl.BlockSpec((None, None, 1, N), lambda b, i: (sel(b, i), k_gate, 0, 0))],
        out_specs=pl.BlockSpec((None, tm, N), lambda b, i: (b, i, 0)),
        out_shape=jax.ShapeDtypeStruct((B, Ta, N), F32),
        compiler_params=_cparams(("arbitrary", "arbitrary")),
        name="matmul_residual",
    )(*lhs, w, res, mods)


def _s5_operators(lam_re, lam_im, log_dt, b_re, b_im, c_re, c_im):
    T = S5_CHUNK
    lr = jnp.minimum(lam_re.astype(F32), -1e-4)
    li = lam_im.astype(F32)
    dt = jnp.exp(log_dt.astype(F32))[..., None]
    mag = jnp.exp(lr * dt)
    a_re = mag * jnp.cos(li * dt)
    a_im = mag * jnp.sin(li * dt)
    num_re = a_re - 1.0
    den = lr * lr + li * li
    f_re = (num_re * lr + a_im * li) / den
    f_im = (a_im * lr - num_re * li) / den
    bb_re = f_re[..., None] * b_re.astype(F32) - f_im[..., None] * b_im.astype(F32)
    bb_im = f_re[..., None] * b_im.astype(F32) + f_im[..., None] * b_re.astype(F32)
    cr, ci = c_re.astype(F32), c_im.astype(F32)

    def powers(step, n):
        m = (step * jnp.arange(n + 1, dtype=F32))[:, None, None, None]
        mg = jnp.exp(m * (lr * dt))
        return mg * jnp.cos(m * (li * dt)), mg * jnp.sin(m * (li * dt))

    pr, pi = powers(1, T)
    pr16, pi16 = powers(T, S5_SUPER)
    ca_re = cr[None] * pr[:T, :, :, None, :] - ci[None] * pi[:T, :, :, None, :]
    ca_im = cr[None] * pi[:T, :, :, None, :] + ci[None] * pr[:T, :, :, None, :]
    kk = (jnp.einsum('kdgip,dgpj->dkgij', ca_re, bb_re) - jnp.einsum('kdgip,dgpj->dkgij', ca_im, bb_im))
    G = lr.shape[1]
    I = S5_GROUP
    s_idx = jnp.arange(T)[:, None]
    t_idx = jnp.arange(T)[None, :]

    def toeplitz(kd, lag):
        m = jnp.where((lag >= 0)[:, :, None, None, None], kd[jnp.clip(lag, 0, T - 1)], 0.0)
        return jnp.transpose(m, (2, 0, 4, 1, 3)).reshape(G, T * I, T * I)

    mt = toeplitz(kk[0], t_idx - s_idx) + toeplitz(kk[1], s_idx - t_idx)

    def v_op(d, qr, qi):
        vr = qr[:, :, :, None] * bb_re[d][None] - qi[:, :, :, None] * bb_im[d][None]
        vi = qr[:, :, :, None] * bb_im[d][None] + qi[:, :, :, None] * bb_re[d][None]
        vr = jnp.transpose(vr, (1, 0, 3, 2)).reshape(G, T * I, S5_STATE)
        vi = jnp.transpose(vi, (1, 0, 3, 2)).reshape(G, T * I, S5_STATE)
        return jnp.concatenate([vr, vi, -vi, vr], axis=-1)

    vt = jnp.concatenate([v_op(0, pr[:T, 0][::-1], pi[:T, 0][::-1]), v_op(1, pr[:T, 1], pi[:T, 1])], axis=-1)

    def w_op(d, qr, qi):
        wr = cr[d][None] * qr[:, :, None, :] - ci[d][None] * qi[:, :, None, :]
        wi = cr[d][None] * qi[:, :, None, :] + ci[d][None] * qr[:, :, None, :]
        wr = jnp.transpose(wr, (1, 3, 0, 2)).reshape(G, S5_STATE, T * I)
        wi = jnp.transpose(wi, (1, 3, 0, 2)).reshape(G, S5_STATE, T * I)
        return jnp.concatenate([wr, -wi], axis=1)

    w = jnp.concatenate([w_op(0, pr[1:T + 1, 0], pi[1:T + 1, 0]),
                         w_op(1, pr[1:T + 1, 1][::-1], pi[1:T + 1, 1][::-1])], axis=1)

    def dup(z):
        return jnp.concatenate([z, z], -1)

    rows = []
    for d in range(2):
        rows += [dup(pr16[1, d]), dup(pi16[1, d]), dup(pr16[S5_SUPER, d]), dup(pi16[S5_SUPER, d])]
    dec = jnp.stack(rows, axis=1)
    prs = []
    for d in range(2):
        prs += [dup(jnp.transpose(pr16[:S5_SUPER, d], (1, 0, 2))), dup(jnp.transpose(pi16[:S5_SUPER, d], (1, 0, 2)))]
    pwt = jnp.concatenate(prs, axis=1)
    return mt.astype(BF16), vt.astype(BF16), w.astype(BF16), dec, pwt


S5_LANE_GROUPS = LANES // S5_GROUP


def _piece_transpose(arrs, lane):
    for d in (4, 2, 1):
        bit = ((lane // S5_GROUP) & d) != 0
        new = list(arrs)
        for r in range(S5_LANE_GROUPS):
            if r & d == 0:
                a, b = arrs[r], arrs[r + d]
                new[r] = jnp.where(bit, pltpu.roll(b, S5_GROUP * d, 1), a)
                new[r + d] = jnp.where(bit, b, pltpu.roll(a, LANES - S5_GROUP * d, 1))
        arrs = new
    return arrs


def _s5_kernel(u_ref, mt_ref, vt_ref, w_ref, dec_ref, pw_ref, dsk_ref, y_ref, xg_ref, yg_ref, vv_ref, hin_ref,
               hs_ref, *, nsc):
    nj = S5_SUPER
    nch = u_ref.shape[0] // S5_CHUNK
    ng = S5_LANE_GROUPS
    lane = lax.broadcasted_iota(jnp.int32, (nch, LANES), 1)
    for h in range(2):
        zs = [u_ref[pl.ds(ng * h + t, nch, stride=S5_CHUNK), :] for t in range(ng)]
        for g, a in enumerate(_piece_transpose(zs, lane)):
            xg_ref[g, :, LANES * h:LANES * (h + 1)] = a
    hs_ref[...] = jnp.zeros_like(hs_ref)

    def group(g, carry):
        x = xg_ref[g]
        xb = x.astype(BF16)
        yg_ref[g] = jnp.dot(xb, mt_ref[g], preferred_element_type=F32) + x * dsk_ref[g]
        vv = jnp.dot(xb, vt_ref[g], preferred_element_type=F32)
        for n in range(4):
            vv_ref[n] = vv[:, LANES * n:LANES * (n + 1)]
        dec = dec_ref[g]
        pw = pw_ref[g]
        for d in range(2):
            ar16, ai16, ar256, ai256 = (dec[4 * d + n:4 * d + n + 1] for n in range(4))

            def v(j, d=d):
                return vv_ref[2 * d, pl.ds(j, nsc, stride=nj), :]

            def jv(j, d=d):
                return vv_ref[2 * d + 1, pl.ds(j, nsc, stride=nj), :]

            order = list(range(nj)) if d == 0 else list(range(nj - 1, -1, -1))
            loc = [None] * nj
            l = jnp.zeros((nsc, LANES), F32)
            gg = jnp.zeros((nsc, LANES), F32)
            for n, j in enumerate(order):
                if n > 0:
                    p = order[n - 1]
                    l, gg = ar16 * l + ai16 * gg + v(p), ar16 * gg - ai16 * l + jv(p)
                loc[j] = l
            p = order[-1]
            e, je = ar16 * l + ai16 * gg + v(p), ar16 * gg - ai16 * l + jv(p)
            seq = list(range(nsc)) if d == 0 else [0] + list(range(nsc - 1, 0, -1))
            cur = jnp.zeros((1, LANES), F32)
            jcur = jnp.zeros((1, LANES), F32)
            for m in seq:
                hs_ref[2 * d, m:m + 1, :] = cur
                hs_ref[2 * d + 1, m:m + 1, :] = jcur
                cur, jcur = (ar256 * cur + ai256 * jcur + e[m:m + 1], ar256 * jcur - ai256 * cur + je[m:m + 1])
            hs = hs_ref[2 * d, 0:nsc, :]
            jhs = hs_ref[2 * d + 1, 0:nsc, :]
            for n, j in enumerate(order):
                pr = pw[32 * d + n:32 * d + n + 1]
                pi = pw[32 * d + 16 + n:32 * d + 16 + n + 1]
                hin_ref[d, pl.ds(j, nsc, stride=nj), :] = loc[j] + pr * hs + pi * jhs
        hin = jnp.concatenate([hin_ref[0], hin_ref[1]], axis=1).astype(BF16)
        yg_ref[g] += jnp.dot(hin, w_ref[g], preferred_element_type=F32)
        return carry

    lax.fori_loop(0, ng, group, 0, unroll=2)
    for h in range(2):
        ys = [yg_ref[g, :, LANES * h:LANES * (h + 1)] for g in range(ng)]
        for t, a in enumerate(_piece_transpose(ys, lane)):
            y_ref[pl.ds(ng * h + t, nch, stride=S5_CHUNK), :] = a


def _s5_scan(px, ops, dsk, *, a_width):
    B, T, _ = px.shape
    mt, vt, w, dec, pw = ops
    ng = S5_LANE_GROUPS
    C = S5_CHUNK * S5_GROUP
    nch = T // S5_CHUNK
    nsc = nch // S5_SUPER
    nsp = (nsc + 7) // 8 * 8
    return pl.pallas_call(
        functools.partial(_s5_kernel, nsc=nsc),
        grid=(a_width // LANES, B),
        in_specs=[pl.BlockSpec((None, T, LANES), lambda gb, b: (b, 0, gb)),
                  pl.BlockSpec((ng, C, C), lambda gb, b: (gb, 0, 0)),
                  pl.BlockSpec((ng, C, 2 * C), lambda gb, b: (gb, 0, 0)),
                  pl.BlockSpec((ng, C, C), lambda gb, b: (gb, 0, 0)),
                  pl.BlockSpec((ng, 8, LANES), lambda gb, b: (gb, 0, 0)),
                  pl.BlockSpec((ng, 64, LANES), lambda gb, b: (gb, 0, 0)),
                  pl.BlockSpec((ng, 1, C), lambda gb, b: (gb, 0, 0))],
        out_specs=pl.BlockSpec((None, T, LANES), lambda gb, b: (b, 0, gb)),
        out_shape=jax.ShapeDtypeStruct((B, T, a_width), F32),
        scratch_shapes=[pltpu.VMEM((ng, nch, C), F32), pltpu.VMEM((ng, nch, C), F32),
                        pltpu.VMEM((4, nch, LANES), F32), pltpu.VMEM((2, nch, LANES), F32),
                        pltpu.VMEM((4, nsp, LANES), F32)],
        compiler_params=_cparams(("arbitrary", "arbitrary")),
        name="s5_scan",
    )(px, mt, vt, w, dec, pw, dsk)


def _s5_glu_kernel(y_ref, w_ref, o_ref):
    g = jax.nn.gelu(y_ref[...])
    z = jnp.dot(g.astype(BF16), w_ref[...], preferred_element_type=F32)
    o_ref[...] = (g * jax.nn.sigmoid(z)).astype(o_ref.dtype)


def _s5_glu(y, w, *, tm):
    B, T, A = y.shape
    return pl.pallas_call(
        _s5_glu_kernel,
        grid=(B, T // tm),
        in_specs=[pl.BlockSpec((None, tm, A), lambda b, i: (b, i, 0)),
                  pl.BlockSpec((A, A), lambda b, i: (0, 0))],
        out_specs=pl.BlockSpec((None, tm, A), lambda b, i: (b, i, 0)),
        out_shape=jax.ShapeDtypeStruct((B, T, A), BF16),
        compiler_params=_cparams(("arbitrary", "arbitrary")),
        name="s5_glu",
    )(y, w)


def _log_sigmoid(z):
    return jnp.minimum(z, 0.0) - jnp.log(1.0 + jnp.exp(-jnp.abs(z)))


def _gla_kernel(*refs, nblk, dk, dv, hp, n_cast):
    q_ref, k_ref, v_ref, r_ref, lr_ref, cos_ref, sin_ref, wg_ref, bg_ref, ng_ref = refs[:10]
    cast_in = refs[10:10 + n_cast]
    o_ref = refs[10 + n_cast]
    cast_out = refs[11 + n_cast:11 + 2 * n_cast]
    s_ref, of_ref = refs[11 + 2 * n_cast:]
    p = pl.program_id(2)
    s = pl.program_id(3)
    C = GLA_CHUNK
    nch = GLA_BLOCK // C
    blk = jnp.where(p == 0, s, jnp.where(s == 0, 0, nblk - s))

    @pl.when(s == 0)
    def _():
        s_ref[...] = jnp.zeros_like(s_ref)

    NB = GLA_BLOCK
    lane = lax.broadcasted_iota(jnp.int32, (NB, dk), 1)
    lower = (lane & 63) < 32
    rpos = lax.broadcasted_iota(jnp.int32, (NB, dk), 0) & (C - 1)
    row = lax.broadcasted_iota(jnp.int32, (NB, NB), 0)
    col = lax.broadcasted_iota(jnp.int32, (NB, NB), 1)
    same_chunk = (row // C) == (col // C)
    eye = (lax.broadcasted_iota(jnp.int32, (dk, dk), 0) == lax.broadcasted_iota(jnp.int32, (dk, dk), 1)).astype(BF16)

    def rope(z, cos, sin):
        return z * cos + jnp.where(lower, pltpu.roll(z, dk - 32, 1), pltpu.roll(z, 32, 1)) * sin

    def run(reverse):
        _side_cast(cast_in, cast_out)
        for hd in range(hp):
            head(reverse, hd)

    def head(reverse, hd):
        mask = same_chunk & ((row <= col) if reverse else (row >= col))
        cos = cos_ref[...]
        sin = sin_ref[...]
        kcols = slice(hd * dk, (hd + 1) * dk)
        vcols = slice(hd * dv, (hd + 1) * dv)
        q = rope(q_ref[:, kcols], cos, sin) * (dk ** -0.5)
        k = rope(k_ref[:, kcols], cos, sin)
        vb = v_ref[:, vcols].astype(BF16)
        lrb = lr_ref[...].astype(BF16)
        la = _log_sigmoid(jnp.dot(lrb, wg_ref[hd], preferred_element_type=F32) + bg_ref[hd]) * (1.0 / GLA_TAU)
        b = la
        sh = 1
        while sh < C:
            if reverse:
                b = b + jnp.where(rpos < C - sh, pltpu.roll(b, NB - sh, 0), 0.0)
            else:
                b = b + jnp.where(rpos >= sh, pltpu.roll(b, sh, 0), 0.0)
            sh *= 2
        last = C - 1 if not reverse else 0
        tots = [b[c * C + last:c * C + last + 1] for c in range(nch)]
        btot = jnp.concatenate([jnp.broadcast_to(t, (C, dk)) for t in tots], 0)
        dmat = jnp.exp(jnp.concatenate(tots + [jnp.zeros((8 - nch, dk), F32)], 0).T)
        qi = (q * jnp.exp(b)).astype(BF16)
        ki = (k * jnp.exp(-b)).astype(BF16)
        ko = (k * jnp.exp(btot - b)).astype(BF16)
        att = jnp.where(mask, _nt(qi, ki), 0.0).astype(BF16)
        o = jnp.dot(att, vb, preferred_element_type=F32)
        upd, dcol = [], []
        for c in range(nch):
            kot = _nt(eye, ko[c * C:(c + 1) * C]).astype(BF16)
            upd.append(jnp.dot(kot, vb[c * C:(c + 1) * C], preferred_element_type=F32))
            dcol.append(dmat[:, c:c + 1])
        st = s_ref[hd]
        o_state = [None] * nch
        for c in (range(nch - 1, -1, -1) if reverse else range(nch)):
            o_state[c] = jnp.dot(qi[c * C:(c + 1) * C], st.astype(BF16), preferred_element_type=F32)
            st = st * dcol[c] + upd[c]
        s_ref[hd] = st
        o = o + jnp.concatenate(o_state, axis=0)
        tok = pl.ds(pl.multiple_of(blk * NB, NB), NB)
        if not reverse:
            of_ref[tok, vcols] = o
        else:
            o = o + of_ref[tok, vcols]
            o = o * lax.rsqrt(jnp.mean(o * o, axis=-1, keepdims=True) + EPS) * ng_ref[hd]
            r = r_ref[:, vcols]
            o_ref[:, vcols] = (o * (r * jax.nn.sigmoid(r))).astype(o_ref.dtype)

    @pl.when(p == 0)
    def _():
        run(False)

    @pl.when(p == 1)
    def _():
        run(True)


def _rope_tables(n_ctx, seq, dk):
    nf = dk // 4
    inv = ROPE_THETA ** (-jnp.arange(nf, dtype=F32) / nf)
    pos = jnp.arange(seq)
    ang_r = (pos // GRID_W).astype(F32)[:, None] * inv[None, :]
    ang_c = (pos % GRID_W).astype(F32)[:, None] * inv[None, :]
    cos = jnp.concatenate([jnp.cos(ang_r)] * 2 + [jnp.cos(ang_c)] * 2, axis=-1)
    sin = jnp.concatenate([-jnp.sin(ang_r), jnp.sin(ang_r), -jnp.sin(ang_c), jnp.sin(ang_c)], axis=-1)
    cos = jnp.concatenate([jnp.ones((n_ctx, dk), F32), cos], axis=0)
    sin = jnp.concatenate([jnp.zeros((n_ctx, dk), F32), sin], axis=0)
    return cos, sin


def _gla(px, w_gate2, b_gate, norm_g, cast, *, n_ctx, a_width):
    B, T, _ = px.shape
    H = GLA_HEADS
    qk = w_gate2.shape[-1]
    dk = qk // H
    bw = norm_g.shape[-1]
    dv = bw // H
    nblk = T // GLA_BLOCK
    cos, sin = _rope_tables(n_ctx, T - n_ctx, dk)
    lr_col0 = a_width + 2 * qk + 2 * bw
    wg = jnp.zeros((2, H, LANES, dk), F32)
    for d in range(2):
        wg = wg.at[d, :, d * GLA_RANK:(d + 1) * GLA_RANK, :].set(
            jnp.transpose(w_gate2[d].reshape(GLA_RANK, H, dk), (1, 0, 2)))
    wg = wg.astype(BF16)
    bg = b_gate.astype(F32).reshape(2, H, 1, dk)
    ng = norm_g.astype(F32).reshape(H, 1, dv)
    hp = GLA_HEADS_PER_STEP

    def blk(p, s):
        return jnp.where(p == 0, s, jnp.where(s == 0, 0, nblk - s))

    kw, vw = hp * dk, hp * dv
    qb, kb = a_width // kw, (a_width + qk) // kw
    vbk, rbk = (a_width + 2 * qk) // vw, (a_width + 2 * qk + bw) // vw
    lb = lr_col0 // LANES
    ngrp = H // hp
    views, c_in, c_out, c_shapes = _side_cast_specs(
        cast, B * ngrp * 2 * nblk, lambda b, g, p, s: ((b * ngrp + g) * 2 + p) * nblk + s)
    outs = pl.pallas_call(
        functools.partial(_gla_kernel, nblk=nblk, dk=dk, dv=dv, hp=hp, n_cast=len(views)),
        grid=(B, ngrp, 2, nblk),
        in_specs=[pl.BlockSpec((None, GLA_BLOCK, kw), lambda b, g, p, s: (b, blk(p, s), qb + g)),
                  pl.BlockSpec((None, GLA_BLOCK, kw), lambda b, g, p, s: (b, blk(p, s), kb + g)),
                  pl.BlockSpec((None, GLA_BLOCK, vw), lambda b, g, p, s: (b, blk(p, s), vbk + g)),
                  pl.BlockSpec((None, GLA_BLOCK, vw), lambda b, g, p, s: (b, blk(p, s), rbk + g)),
                  pl.BlockSpec((None, GLA_BLOCK, LANES), lambda b, g, p, s: (b, blk(p, s), lb)),
                  pl.BlockSpec((GLA_BLOCK, dk), lambda b, g, p, s: (blk(p, s), 0)),
                  pl.BlockSpec((GLA_BLOCK, dk), lambda b, g, p, s: (blk(p, s), 0)),
                  pl.BlockSpec((None, hp, LANES, dk), lambda b, g, p, s: (p, g, 0, 0)),
                  pl.BlockSpec((None, hp, 1, dk), lambda b, g, p, s: (p, g, 0, 0)),
                  pl.BlockSpec((hp, 1, dv), lambda b, g, p, s: (g, 0, 0))] + c_in,
        out_specs=[pl.BlockSpec((None, GLA_BLOCK, vw),
                                lambda b, g, p, s: (b, jnp.where(p == 0, 0, blk(p, s)), g))] + c_out,
        out_shape=[jax.ShapeDtypeStruct((B, T, bw), BF16)] + c_shapes,
        scratch_shapes=[pltpu.VMEM((hp, dk, dv), F32), pltpu.VMEM((T, vw), F32)],
        compiler_params=_cparams(("arbitrary", "arbitrary", "arbitrary", "arbitrary")),
        name="gla",
    )(px, px, px, px, px, cos, sin, wg, bg, ng, *views)
    return outs[0], outs[1:]


def _na_kernel(*refs, n_ctx, rows, n_cast):
    q_ref, k_ref, v_ref, bias_ref, qn_ref, kn_ref = refs[:6]
    o_ref = refs[6 + n_cast]
    kb_ref, vb_ref = refs[7 + 2 * n_cast:]
    _side_cast(refs[6:6 + n_cast], refs[7 + n_cast:7 + 2 * n_cast])
    hd = NA_HEAD_DIM
    W = GRID_W
    T = k_ref.shape[0]
    step = n_ctx

    def prep(i, c):
        sl = pl.ds(pl.multiple_of(i * step, step), step)
        kk = k_ref[sl, :]
        kb_ref[sl, :] = (kk * lax.rsqrt(jnp.mean(kk * kk, axis=-1, keepdims=True) + EPS) * kn_ref[...]).astype(BF16)
        vb_ref[sl, :] = v_ref[sl, :].astype(BF16)
        return c

    lax.fori_loop(0, T // step, prep, 0)
    scale = hd ** -0.5
    ngrp = rows // NA_QROWS
    nq = NA_QROWS * W
    nk = NA_UROWS * W

    def group(g, c):
        r0 = g * NA_QROWS
        ustart = jnp.clip(r0 - NA_KH // 2, 0, rows - NA_UROWS)
        case = jnp.where(g == 0, 0, jnp.where(g == ngrp - 1, 2, 1))
        q = q_ref[pl.ds(pl.multiple_of(n_ctx + r0 * W, W), nq), :]
        q = (q * lax.rsqrt(jnp.mean(q * q, axis=-1, keepdims=True) + EPS) * qn_ref[...] * scale).astype(BF16)
        ws = pl.ds(pl.multiple_of(n_ctx + ustart * W, W), nk)
        s_lat = _nt(q, kb_ref[ws, :]) + bias_ref[case]
        s_ctx = _nt(q, kb_ref[0:n_ctx, :])
        m = jnp.maximum(jnp.max(s_lat, axis=-1, keepdims=True), jnp.max(s_ctx, axis=-1, keepdims=True))
        p_lat = jnp.exp(s_lat - m)
        p_ctx = jnp.exp(s_ctx - m)
        den = jnp.sum(p_lat, axis=-1, keepdims=True) + jnp.sum(p_ctx, axis=-1, keepdims=True)
        o = (jnp.dot(p_lat.astype(BF16), vb_ref[ws, :], preferred_element_type=F32)
             + jnp.dot(p_ctx.astype(BF16), vb_ref[0:n_ctx, :], preferred_element_type=F32))
        o_ref[pl.ds(pl.multiple_of(r0 * W, W), nq), :] = (o / den).astype(o_ref.dtype)
        return c

    lax.fori_loop(0, ngrp, group, 0, unroll=4)


def _na_bias(rpb, rows):
    H = rpb.shape[0]
    col = jnp.arange(GRID_W)
    cs = jnp.clip(col - NA_KW // 2, 0, GRID_W - NA_KW)
    col_ok = (col[None, :] >= cs[:, None]) & (col[None, :] < cs[:, None] + NA_KW)
    dc_idx = jnp.clip(col[None, :] - col[:, None] + NA_KW - 1, 0, 2 * NA_KW - 2)
    rpb_c = jnp.where(col_ok[None, None], rpb.astype(F32)[:, :, dc_idx], NEG_INF)
    ngrp = rows // NA_QROWS
    tables = []
    for g in (0, 1, ngrp - 1):
        r0 = g * NA_QROWS
        ustart = min(max(r0 - NA_KH // 2, 0), rows - NA_UROWS)
        qr = r0 + jnp.arange(NA_QROWS)
        start = jnp.clip(qr - NA_KH // 2, 0, rows - NA_KH)
        kr = ustart + jnp.arange(NA_UROWS)
        ok = (kr[None, :] >= start[:, None]) & (kr[None, :] < start[:, None] + NA_KH)
        idx = jnp.clip(kr[None, :] - qr[:, None] + NA_KH - 1, 0, 2 * NA_KH - 2)
        t = jnp.where(ok[None, :, :, None, None], rpb_c[:, idx], NEG_INF)
        tables.append(jnp.transpose(t, (0, 1, 3, 2, 4)).reshape(H, NA_QROWS * GRID_W, NA_UROWS * GRID_W))
    return jnp.stack(tables, axis=1)


def _na_attention(qkv, q_norm, k_norm, rpb, cast, *, n_ctx):
    B, T, D3 = qkv.shape
    D = D3 // 3
    H = D // NA_HEAD_DIM
    seq = T - n_ctx
    rows = seq // GRID_W
    assert rows % NA_QROWS == 0 and rows >= NA_UROWS + NA_QROWS
    bias = _na_bias(rpb, rows)
    hd = NA_HEAD_DIM
    views, c_in, c_out, c_shapes = _side_cast_specs(cast, B * H, lambda b, h: b * H + h)
    outs = pl.pallas_call(
        functools.partial(_na_kernel, n_ctx=n_ctx, rows=rows, n_cast=len(views)),
        grid=(B, H),
        in_specs=[pl.BlockSpec((None, T, hd), lambda b, h: (b, 0, h)),
                  pl.BlockSpec((None, T, hd), lambda b, h: (b, 0, H + h)),
                  pl.BlockSpec((None, T, hd), lambda b, h: (b, 0, 2 * H + h)),
                  pl.BlockSpec((None, 3, NA_QROWS * GRID_W, NA_UROWS * GRID_W), lambda b, h: (h, 0, 0, 0)),
                  pl.BlockSpec((1, hd), lambda b, h: (0, 0)),
                  pl.BlockSpec((1, hd), lambda b, h: (0, 0))] + c_in,
        out_specs=[pl.BlockSpec((None, seq, hd), lambda b, h: (b, 0, h))] + c_out,
        out_shape=[jax.ShapeDtypeStruct((B, seq, D), BF16)] + c_shapes,
        scratch_shapes=[pltpu.VMEM((T, hd), BF16), pltpu.VMEM((T, hd), BF16)],
        compiler_params=_cparams(("arbitrary", "arbitrary")),
        name="na_attention",
    )(qkv, qkv, qkv, bias, q_norm.astype(F32).reshape(1, hd), k_norm.astype(F32).reshape(1, hd), *views)
    return outs[0], outs[1:]


def _router_kernel(u_ref, g_ref, sh_ref, sc_ref, wr_ref, h_ref, mf_ref, mi_ref):
    h = _normmod(u_ref[...], g_ref[...], sh_ref[...], sc_ref[...])
    h_ref[...] = h
    logits = jnp.dot(h, wr_ref[...], precision=HIGHEST, preferred_element_type=F32)
    lane = lax.broadcasted_iota(jnp.int32, logits.shape, 1).astype(F32)
    big = 1e9

    def first_max(vals):
        m = jnp.max(vals, axis=-1, keepdims=True)
        return m, jnp.min(jnp.where(vals == m, lane, big), axis=-1, keepdims=True)

    gl = jnp.where(lane < MOE_GROUPS, logits, -jnp.inf)
    gmax, gidx = first_max(gl)
    p_top = 1.0 / jnp.sum(jnp.exp(gl - gmax), axis=-1, keepdims=True)
    lo = MOE_GROUPS + MOE_PER_GROUP * gidx
    el = jnp.where((lane >= lo) & (lane < lo + MOE_PER_GROUP), logits, -jnp.inf)
    v1, i1 = first_max(el)
    v2, i2 = first_max(jnp.where(lane == i1, -jnp.inf, el))
    e2 = jnp.exp(v2 - v1)
    w1 = p_top / (1.0 + e2)
    w2 = p_top * e2 / (1.0 + e2)
    mf_ref[...] = jnp.where(lane == 0, w1, jnp.where(lane == 1, w2, 0.0))
    mi_ref[...] = jnp.where(lane == 0, i1 - MOE_GROUPS, jnp.where(lane == 1, i2 - MOE_GROUPS, 0.0)).astype(jnp.int32)


def _router(u, gamma, mods, k_shift, k_scale, wr, *, n_ctx, tm):
    B, T, D = u.shape
    sel = _mod_sel(n_ctx // tm, B)
    return pl.pallas_call(
        _router_kernel,
        grid=(B, T // tm),
        in_specs=[pl.BlockSpec((None, tm, D), lambda b, i: (b, i, 0)),
                  pl.BlockSpec((1, D), lambda b, i: (0, 0)),
                  pl.BlockSpec((None, None, 1, D), lambda b, i: (sel(b, i), k_shift, 0, 0)),
                  pl.BlockSpec((None, None, 1, D), lambda b, i: (sel(b, i), k_scale, 0, 0)),
                  pl.BlockSpec((D, LANES), lambda b, i: (0, 0))],
        out_specs=[pl.BlockSpec((None, tm, D), lambda b, i: (b, i, 0)),
                   pl.BlockSpec((None, tm, LANES), lambda b, i: (b, i, 0)),
                   pl.BlockSpec((None, tm, LANES), lambda b, i: (b, i, 0))],
        out_shape=[jax.ShapeDtypeStruct((B, T, D), F32),
                   jax.ShapeDtypeStruct((B, T, LANES), F32),
                   jax.ShapeDtypeStruct((B, T, LANES), jnp.int32)],
        compiler_params=_cparams(("arbitrary", "arbitrary")),
        name="moe_router",
    )(u, gamma.reshape(1, D), mods, mods, wr)


def _moe_plan(eid2):
    N = eid2.shape[0]
    M = 2 * N
    E = MOE_EXPERTS
    eid = eid2.reshape(-1)
    onehot = (eid[:, None] == jnp.arange(E, dtype=jnp.int32)[None, :]).astype(jnp.int32)
    csum = jnp.cumsum(onehot, axis=0)
    rank = jnp.sum(onehot * csum, axis=1) - 1
    counts = csum[-1]
    padded = (counts + MOE_ROWS - 1) // MOE_ROWS * MOE_ROWS
    pend = jnp.cumsum(padded)
    pstart = pend - padded
    dest = (jnp.sum(onehot * pstart[None, :], axis=1) + rank).astype(jnp.int32)
    P = (M + E * (MOE_ROWS - 1) + MOE_ROWS - 1) // MOE_ROWS * MOE_ROWS
    nb = P // MOE_ROWS
    slot_tok = jnp.zeros((P,), jnp.int32).at[dest].set(jnp.arange(M, dtype=jnp.int32) // 2)
    pos = dest.reshape(N, 2)
    blk_start = jnp.arange(nb, dtype=jnp.int32) * MOE_ROWS
    blk_e = jnp.minimum(jnp.sum((pend[None, :] <= blk_start[:, None]).astype(jnp.int32), axis=1), E - 1)
    n_active = (pend[-1] // MOE_ROWS).astype(jnp.int32).reshape(1)
    return slot_tok.reshape(nb, MOE_ROWS), pos, blk_e.astype(jnp.int32), n_active


def _row_copy(src_hbm, idx_smem, dst_ref, sem, r):
    return pltpu.make_async_copy(src_hbm.at[pl.ds(idx_smem[r], 1), :], dst_ref.at[pl.ds(r, 1), :], sem)


def _expert_kernel(blk_e_ref, nact_ref, idx_hbm, h_hbm, wg_ref, wu_ref, wd_ref, o_ref, idx0, idx1, xbuf, sem_idx,
                   sem_rows):
    i = pl.program_id(0)
    nact = nact_ref[0]
    R = MOE_ROWS
    slot = lax.rem(i, 2)
    idx_bufs = (idx0, idx1)

    def idx_copy(blk, s):
        return pltpu.make_async_copy(idx_hbm.at[blk], idx_bufs[s], sem_idx.at[s])

    def issue_rows(s):
        def body(r, c):
            _row_copy(h_hbm, idx_bufs[s], xbuf.at[s], sem_rows.at[s], r).start(priority=1)
            return c

        lax.fori_loop(0, R, body, 0, unroll=8)

    @pl.when(i == 0)
    def _():
        first = idx_copy(0, 0)
        first.start()
        first.wait()
        issue_rows(0)

        @pl.when(nact > 1)
        def _():
            idx_copy(1, 1).start()

    def prefetch(s):
        @pl.when(i + 1 < nact)
        def _():
            idx_copy(i + 1, 1 - s).wait()
            issue_rows(1 - s)

        @pl.when(i + 2 < nact)
        def _():
            idx_copy(i + 2, s).start()

    @pl.when(slot == 0)
    def _():
        prefetch(0)

    @pl.when(slot == 1)
    def _():
        prefetch(1)

    @pl.when(i < nact)
    def _():
        pltpu.make_async_copy(h_hbm.at[pl.ds(0, R), :], xbuf.at[slot], sem_rows.at[slot]).wait()
        xb = xbuf[slot].astype(BF16)
        g = jnp.dot(xb, wg_ref[...], preferred_element_type=F32)
        u = jnp.dot(xb, wu_ref[...], preferred_element_type=F32)
        hmid = (g * jax.nn.sigmoid(g) * u).astype(BF16)
        o_ref[...] = jnp.dot(hmid, wd_ref[...], preferred_element_type=F32)

    @pl.when(i >= nact)
    def _():
        o_ref[...] = jnp.zeros_like(o_ref)


def _expert_mlp(h, slot_tok, blk_e, n_active, wg, wu, wd):
    nb, R = slot_tok.shape
    D = h.shape[1]
    F = wg.shape[2]
    grid_spec = pltpu.PrefetchScalarGridSpec(
        num_scalar_prefetch=2,
        grid=(nb,),
        in_specs=[pl.BlockSpec(memory_space=pl.ANY),
                  pl.BlockSpec(memory_space=pl.ANY),
                  pl.BlockSpec((None, D, F), lambda i, be, na: (be[i], 0, 0)),
                  pl.BlockSpec((None, D, F), lambda i, be, na: (be[i], 0, 0)),
                  pl.BlockSpec((None, F, D), lambda i, be, na: (be[i], 0, 0))],
        out_specs=pl.BlockSpec((R, D), lambda i, be, na: (i, 0)),
        scratch_shapes=[pltpu.SMEM((R,), jnp.int32), pltpu.SMEM((R,), jnp.int32), pltpu.VMEM((2, R, D), F32),
                        pltpu.SemaphoreType.DMA((2,)), pltpu.SemaphoreType.DMA((2,))],
    )
    return pl.pallas_call(
        _expert_kernel,
        grid_spec=grid_spec,
        out_shape=jax.ShapeDtypeStruct((nb * R, D), F32),
        compiler_params=_cparams(("arbitrary",)),
        name="moe_experts",
    )(blk_e, n_active, slot_tok, h, wg, wu, wd)


def _combine_kernel(*refs, n_cast):
    p1_hbm, p2_hbm, yb_hbm, mf_ref, r_ref, g_ref = refs[:6]
    o_ref = refs[6 + n_cast]
    i1a, i1b, i2a, i2b, g1_ref, g2_ref, sem_idx, sem_rows = refs[7 + 2 * n_cast:]
    i1_bufs, i2_bufs = (i1a, i1b), (i2a, i2b)
    t = pl.program_id(0) * pl.num_programs(1) + pl.program_id(1)
    n = pl.num_programs(0) * pl.num_programs(1)
    R = o_ref.shape[0]
    slot = lax.rem(t, 2)

    def idx_copies(tile, s):
        return (pltpu.make_async_copy(p1_hbm.at[tile], i1_bufs[s], sem_idx.at[s]),
                pltpu.make_async_copy(p2_hbm.at[tile], i2_bufs[s], sem_idx.at[s]))

    def issue_rows(s):
        def body(r, c):
            _row_copy(yb_hbm, i1_bufs[s], g1_ref.at[s], sem_rows.at[s], r).start(priority=0)
            _row_copy(yb_hbm, i2_bufs[s], g2_ref.at[s], sem_rows.at[s], r).start(priority=1)
            return c

        lax.fori_loop(0, R, body, 0, unroll=8)

    @pl.when(t == 0)
    def _():
        for c in idx_copies(0, 0):
            c.start()
        for c in idx_copies(0, 0):
            c.wait()
        issue_rows(0)

        @pl.when(n > 1)
        def _():
            for c in idx_copies(1, 1):
                c.start()

    def prefetch(s):
        @pl.when(t + 1 < n)
        def _():
            for c in idx_copies(t + 1, 1 - s):
                c.wait()
            issue_rows(1 - s)

        @pl.when(t + 2 < n)
        def _():
            for c in idx_copies(t + 2, s):
                c.start()

    @pl.when(slot == 0)
    def _():
        prefetch(0)

    @pl.when(slot == 1)
    def _():
        prefetch(1)

    _side_cast(refs[6:6 + n_cast], refs[7 + n_cast:7 + 2 * n_cast])
    pltpu.make_async_copy(yb_hbm.at[pl.ds(0, R), :], g1_ref.at[slot], sem_rows.at[slot]).wait()
    pltpu.make_async_copy(yb_hbm.at[pl.ds(0, R), :], g2_ref.at[slot], sem_rows.at[slot]).wait()
    mf = mf_ref[...]
    y = mf[:, 0:1] * g1_ref[slot] + mf[:, 1:2] * g2_ref[slot]
    o_ref[...] = r_ref[...] + g_ref[...] * y


def _moe_combine(yb, pos, mf, res, mods, k_gate, cast, *, n_ctx):
    B, T, _ = mf.shape
    D = yb.shape[1]
    R = MOE_ROWS
    nt = T // R
    sel = _mod_sel(n_ctx // R, B)
    p1 = pos[:, 0].reshape(B * nt, R)
    p2 = pos[:, 1].reshape(B * nt, R)
    views, c_in, c_out, c_shapes = _side_cast_specs(cast, B * nt, lambda b, i: b * nt + i)
    outs = pl.pallas_call(
        functools.partial(_combine_kernel, n_cast=len(views)),
        grid=(B, nt),
        in_specs=[pl.BlockSpec(memory_space=pl.ANY), pl.BlockSpec(memory_space=pl.ANY),
                  pl.BlockSpec(memory_space=pl.ANY),
                  pl.BlockSpec((None, R, LANES), lambda b, i: (b, i, 0)),
                  pl.BlockSpec((None, R, D), lambda b, i: (b, i, 0)),
                  pl.BlockSpec((None, None, 1, D), lambda b, i: (sel(b, i), k_gate, 0, 0))] + c_in,
        out_specs=[pl.BlockSpec((None, R, D), lambda b, i: (b, i, 0))] + c_out,
        out_shape=[jax.ShapeDtypeStruct((B, T, D), F32)] + c_shapes,
        scratch_shapes=[pltpu.SMEM((R,), jnp.int32)] * 4
        + [pltpu.VMEM((2, R, D), F32), pltpu.VMEM((2, R, D), F32),
           pltpu.SemaphoreType.DMA((2,)), pltpu.SemaphoreType.DMA((2,))],
        compiler_params=_cparams(("arbitrary", "arbitrary")),
        name="moe_combine",
    )(p1, p2, yb, mf, res, mods, *views)
    return outs[0], outs[1:]


def _hier_moe(u, gamma, mods, w_route_group, w_route_expert, wg, wu, wd, cast, *, n_ctx):
    B, T, D = u.shape
    E = MOE_EXPERTS
    wr = jnp.zeros((D, LANES), F32)
    wr = wr.at[:, :MOE_GROUPS].set(w_route_group.astype(F32))
    wr = wr.at[:, MOE_GROUPS:MOE_GROUPS + MOE_EXPERTS].set(w_route_expert.astype(F32))
    h, mf, mi = _router(u, gamma, mods, 3, 4, wr, n_ctx=n_ctx, tm=MOE_ROWS)
    slot_tok, pos, blk_e, n_active = _moe_plan(mi[:, :, :2].reshape(B * T, 2))
    yb = _expert_mlp(h.reshape(B * T, D), slot_tok, blk_e, n_active, wg.reshape(E, D, -1), wu.reshape(E, D, -1),
                     wd.reshape(E, -1, D))
    return _moe_combine(yb, pos, mf, u, mods, 5, cast, n_ctx=n_ctx)


def _ab_layer(u, mods, gamma, w_in, w_out, s5p, s5_d, s5_w_glu, gla_w_gate2, gla_b_gate, gla_norm_g, cast, *, n_ctx):
    a_width = s5_w_glu.shape[0]
    n_in = w_in.shape[1]
    n_pad = (n_in + LANES - 1) // LANES * LANES
    w_in_b = jnp.pad(w_in, ((0, 0), (0, n_pad - n_in))).astype(BF16)
    px = _normmod_matmul(u, gamma, mods, 0, 1, w_in_b, n_ctx=n_ctx, tm=256, tn=n_pad // 3)
    G = a_width // S5_GROUP
    ops = _s5_operators(*s5p)
    dsk = jnp.tile(s5_d.astype(F32).reshape(G, 1, S5_GROUP), (1, S5_CHUNK, 1)).reshape(G, 1, S5_CHUNK * S5_GROUP)
    ya = _s5_glu(_s5_scan(px, ops, dsk, a_width=a_width), s5_w_glu.astype(BF16), tm=256)
    yb, casted = _gla(px, gla_w_gate2, gla_b_gate, gla_norm_g, cast, n_ctx=n_ctx, a_width=a_width)
    out = _matmul_residual([ya, yb], w_out.astype(BF16), u, mods, 2, n_ctx=n_ctx, row0=0, tm=256, tn=1024)
    return out, casted


def _na_layer(u, mods, gamma, w_qkv, w_out, q_norm, k_norm, rpb, cast, *, n_ctx):
    qkv = _normmod_matmul(u, gamma, mods, 0, 1, w_qkv.astype(BF16), n_ctx=n_ctx, tm=256, tn=2048)
    o, casted = _na_attention(qkv, q_norm, k_norm, rpb, cast, n_ctx=n_ctx)
    out = _matmul_residual([o], w_out.astype(BF16), u, mods, 2, n_ctx=0, row0=n_ctx, tm=256, tn=1024)
    return out, casted


def kernel(x, c, ctx, c_ctx, ada_w, ada_b, norm1_g, norm2_g, ab_w_in, ab_w_out, s5_lam_re, s5_lam_im, s5_log_dt,
           s5_b_re, s5_b_im, s5_c_re, s5_c_im, s5_d, s5_w_glu, gla_w_gate2, gla_b_gate, gla_norm_g, na_w_qkv,
           na_w_out, na_q_norm, na_k_norm, na_rpb, moe_w_route_group, moe_w_route_expert, moe_w_gate, moe_w_up,
           moe_w_down):
    B, seq, D = x.shape
    n_ctx = ctx.shape[1]
    depth = ada_w.shape[0]
    assert depth == 2 and n_ctx == S5_CHUNK * S5_SUPER and seq % (GRID_W * 4) == 0
    u = jnp.concatenate([ctx, x], axis=1)
    cvec = jnp.zeros((8, D), F32).at[:B].set(c).at[B].set(c_ctx)
    mods_all = _ada_mod(cvec, ada_w, ada_b)[:, :B + 1]
    mods = mods_all[0]
    s5p = (s5_lam_re[0], s5_lam_im[0], s5_log_dt[0], s5_b_re[0], s5_b_im[0], s5_c_re[0], s5_c_im[0])
    u, (wg0, wu0, wd0) = _ab_layer(u, mods, norm1_g[0], ab_w_in[0], ab_w_out[0], s5p, s5_d[0], s5_w_glu[0],
                                   gla_w_gate2[0], gla_b_gate[0], gla_norm_g[0],
                                   [(moe_w_gate, 0), (moe_w_up, 0), (moe_w_down, 0)], n_ctx=n_ctx)
    u, (wd1,) = _hier_moe(u, norm2_g[0], mods, moe_w_route_group[0], moe_w_route_expert[0], wg0, wu0, wd0,
                          [(moe_w_down, 1)], n_ctx=n_ctx)
    mods = mods_all[1]
    xl, (wg1, wu1) = _na_layer(u, mods, norm1_g[1], na_w_qkv[0], na_w_out[0], na_q_norm[0], na_k_norm[0], na_rpb[0],
                               [(moe_w_gate, 1), (moe_w_up, 1)], n_ctx=n_ctx)
    out, _ = _hier_moe(xl, norm2_g[1], mods, moe_w_route_group[1], moe_w_route_expert[1], wg1, wu1, wd1, [],
                       n_ctx=0)
    return out
```

```python
import functools

import jax
import jax.numpy as jnp
from jax import lax
from jax.experimental import pallas as pl
from jax.experimental.pallas import tpu as pltpu

F32 = jnp.float32
BF16 = jnp.bfloat16

V7X_VMEM_BYTES = 64 * 1024 * 1024
VMEM_LIMIT = V7X_VMEM_BYTES - 12 * 1024 * 1024
LANES = 128

EPS = 1e-6
NEG_INF = -1e30
GRID_W = 64
N_MOD = 6
S5_GROUP = 16
S5_STATE = 64
S5_CHUNK = 16
S5_SUPER = 16
GLA_HEADS = 4
GLA_RANK = 16
GLA_TAU = 16.0
GLA_CHUNK = 64
GLA_BLOCK = 256
GLA_HEADS_PER_STEP = 4
NA_HEAD_DIM = 128
NA_KH = 8
NA_KW = 16
NA_QROWS = 4
NA_UROWS = NA_KH + NA_QROWS - 1
ROPE_THETA = 10000.0
MOE_GROUPS = 4
MOE_PER_GROUP = 8
MOE_EXPERTS = MOE_GROUPS * MOE_PER_GROUP
MOE_ROWS = 256


def _cparams(sem):
    return pltpu.CompilerParams(dimension_semantics=sem, vmem_limit_bytes=VMEM_LIMIT)


def _nt(a, b):
    return lax.dot_general(a, b, (((1,), (1,)), ((), ())), preferred_element_type=F32)


def _side_cast_specs(tensors, n_steps, lin):
    nblocks = 1 << (n_steps.bit_length() - 1)

    def slab(*ids):
        return jnp.minimum(lin(*ids), nblocks - 1)

    views, in_specs, out_specs, out_shapes = [], [], [], []
    for t, layer in tensors:
        v = t.reshape(t.shape[0], -1, t.shape[-1])
        rows, cols = v.shape[1:]
        assert rows % (16 * nblocks) == 0
        views.append(v)
        in_specs.append(pl.BlockSpec((None, rows // nblocks, cols), functools.partial(
            lambda layer, *ids: (layer, slab(*ids), 0), layer)))
        out_specs.append(pl.BlockSpec((rows // nblocks, cols), lambda *ids: (slab(*ids), 0)))
        out_shapes.append(jax.ShapeDtypeStruct((rows, cols), BF16))
    return views, in_specs, out_specs, out_shapes


def _side_cast(in_refs, out_refs):
    for i_ref, o_ref in zip(in_refs, out_refs):
        o_ref[...] = i_ref[...].astype(BF16)


def _ada_kernel(c_ref, w_ref, b_ref, o_ref):
    c = c_ref[...]
    s = c * jax.nn.sigmoid(c)
    o_ref[...] = jnp.dot(s.astype(BF16), w_ref[...].astype(BF16), preferred_element_type=F32) + b_ref[...]


def _ada_mod(cvec, w, b):
    R, D = cvec.shape
    L, _, N = w.shape
    tn = 1024
    out = pl.pallas_call(
        _ada_kernel,
        grid=(L, N // tn),
        in_specs=[pl.BlockSpec((R, D), lambda l, j: (0, 0)),
                  pl.BlockSpec((None, D, tn), lambda l, j: (l, 0, j)),
                  pl.BlockSpec((None, 1, tn), lambda l, j: (l, 0, j))],
        out_specs=pl.BlockSpec((None, R, tn), lambda l, j: (l, 0, j)),
        out_shape=jax.ShapeDtypeStruct((L, R, N), F32),
        compiler_params=_cparams(("arbitrary", "arbitrary")),
        name="ada_mod",
    )(cvec, w, b.reshape(L, 1, N))
    return out.reshape(L, R, N_MOD, 1, D)


def _mod_sel(nct, nb):
    if nct == 0:
        return lambda b, i: b
    return lambda b, i: jnp.where(i < nct, nb, b)


def _normmod(x, g, sh, sc):
    y = x * lax.rsqrt(jnp.mean(x * x, axis=-1, keepdims=True) + EPS) * g
    return y * (1.0 + sc) + sh


def _normmod_mm_kernel(u_ref, g_ref, sh_ref, sc_ref, w_ref, o_ref, *, tn):
    xn = _normmod(u_ref[...], g_ref[...], sh_ref[...], sc_ref[...]).astype(BF16)
    for n0 in range(0, o_ref.shape[-1], tn):
        o_ref[:, n0:n0 + tn] = jnp.dot(xn, w_ref[:, n0:n0 + tn], preferred_element_type=F32).astype(o_ref.dtype)


def _normmod_matmul(u, gamma, mods, k_shift, k_scale, w, *, n_ctx, tm, tn, out_dtype=F32):
    B, T, D = u.shape
    N = w.shape[1]
    sel = _mod_sel(n_ctx // tm, B)
    return pl.pallas_call(
        functools.partial(_normmod_mm_kernel, tn=tn),
        grid=(B, T // tm),
        in_specs=[pl.BlockSpec((None, tm, D), lambda b, i: (b, i, 0)),
                  pl.BlockSpec((1, D), lambda b, i: (0, 0)),
                  pl.BlockSpec((None, None, 1, D), lambda b, i: (sel(b, i), k_shift, 0, 0)),
                  pl.BlockSpec((None, None, 1, D), lambda b, i: (sel(b, i), k_scale, 0, 0)),
                  pl.BlockSpec((D, N), lambda b, i: (0, 0), pipeline_mode=pl.Buffered(1))],
        out_specs=pl.BlockSpec((None, tm, N), lambda b, i: (b, i, 0)),
        out_shape=jax.ShapeDtypeStruct((B, T, N), out_dtype),
        compiler_params=_cparams(("arbitrary", "arbitrary")),
        name="normmod_matmul",
    )(u, gamma.reshape(1, D), mods, mods, w)


def _mm_res_kernel(*refs, n_lhs, tn):
    a_refs, w_ref, r_ref, g_ref, o_ref = refs[:n_lhs], refs[n_lhs], refs[n_lhs + 1], refs[n_lhs + 2], refs[n_lhs + 3]
    lhs = [a[...] for a in a_refs]
    for n0 in range(0, o_ref.shape[-1], tn):
        acc, k0 = None, 0
        for a in lhs:
            part = jnp.dot(a, w_ref[k0:k0 + a.shape[1], n0:n0 + tn], preferred_element_type=F32)
            acc = part if acc is None else acc + part
            k0 += a.shape[1]
        o_ref[:, n0:n0 + tn] = r_ref[:, n0:n0 + tn] + g_ref[:, n0:n0 + tn] * acc


def _matmul_residual(lhs, w, res, mods, k_gate, *, n_ctx, row0, tm, tn):
    B, Ta, _ = lhs[0].shape
    K, N = w.shape
    sel = _mod_sel(n_ctx // tm, B)
    r0 = row0 // tm
    return pl.pallas_call(
        functools.partial(_mm_res_kernel, n_lhs=len(lhs), tn=tn),
        grid=(B, Ta // tm),
        in_specs=[pl.BlockSpec((None, tm, a.shape[2]), lambda b, i: (b, i, 0)) for a in lhs]
        + [pl.BlockSpec((K, N), lambda b, i: (0, 0), pipeline_mode=pl.Buffered(1)),
           pl.BlockSpec((None, tm, N), lambda b, i: (b, i + r0, 0)),
           pl.BlockSpec((None, None, 1, N), lambda b, i: (sel(b, i), k_gate, 0, 0))],
        out_specs=pl.BlockSpec((None, tm, N), lambda b, i: (b, i, 0)),
        out_shape=jax.ShapeDtypeStruct((B, Ta, N), F32),
        compiler_params=_cparams(("arbitrary", "arbitrary")),
        name="matmul_residual",
    )(*lhs, w, res, mods)


def _s5_operators(lam_re, lam_im, log_dt, b_re, b_im, c_re, c_im):
    T = S5_CHUNK
    lr = jnp.minimum(lam_re.astype(F32), -1e-4)
    li = lam_im.astype(F32)
    dt = jnp.exp(log_dt.astype(F32))[..., None]
    mag = jnp.exp(lr * dt)
    a_re = mag * jnp.cos(li * dt)
    a_im = mag * jnp.sin(li * dt)
    num_re = a_re - 1.0
    den = lr * lr + li * li
    f_re = (num_re * lr + a_im * li) / den
    f_im = (a_im * lr - num_re * li) / den
    bb_re = f_re[..., None] * b_re.astype(F32) - f_im[..., None] * b_im.astype(F32)
    bb_im = f_re[..., None] * b_im.astype(F32) + f_im[..., None] * b_re.astype(F32)
    cr, ci = c_re.astype(F32), c_im.astype(F32)

    def powers(step, n):
        m = (step * jnp.arange(n + 1, dtype=F32))[:, None, None, None]
        mg = jnp.exp(m * (lr * dt))
        return mg * jnp.cos(m * (li * dt)), mg * jnp.sin(m * (li * dt))

    pr, pi = powers(1, T)
    pr16, pi16 = powers(T, S5_SUPER)
    ca_re = cr[None] * pr[:T, :, :, None, :] - ci[None] * pi[:T, :, :, None, :]
    ca_im = cr[None] * pi[:T, :, :, None, :] + ci[None] * pr[:T, :, :, None, :]
    kk = (jnp.einsum('kdgip,dgpj->dkgij', ca_re, bb_re) - jnp.einsum('kdgip,dgpj->dkgij', ca_im, bb_im))
    G = lr.shape[1]
    I = S5_GROUP
    s_idx = jnp.arange(T)[:, None]
    t_idx = jnp.arange(T)[None, :]

    def toeplitz(kd, lag):
        m = jnp.where((lag >= 0)[:, :, None, None, None], kd[jnp.clip(lag, 0, T - 1)], 0.0)
        return jnp.transpose(m, (2, 0, 4, 1, 3)).reshape(G, T * I, T * I)

    mt = toeplitz(kk[0], t_idx - s_idx) + toeplitz(kk[1], s_idx - t_idx)

    def v_op(d, qr, qi):
        vr = qr[:, :, :, None] * bb_re[d][None] - qi[:, :, :, None] * bb_im[d][None]
        vi = qr[:, :, :, None] * bb_im[d][None] + qi[:, :, :, None] * bb_re[d][None]
        vr = jnp.transpose(vr, (1, 0, 3, 2)).reshape(G, T * I, S5_STATE)
        vi = jnp.transpose(vi, (1, 0, 3, 2)).reshape(G, T * I, S5_STATE)
        return jnp.concatenate([vr, vi, -vi, vr], axis=-1)

    vt = jnp.concatenate([v_op(0, pr[:T, 0][::-1], pi[:T, 0][::-1]), v_op(1, pr[:T, 1], pi[:T, 1])], axis=-1)

    def w_op(d, qr, qi):
        wr = cr[d][None] * qr[:, :, None, :] - ci[d][None] * qi[:, :, None, :]
        wi = cr[d][None] * qi[:, :, None, :] + ci[d][None] * qr[:, :, None, :]
        wr = jnp.transpose(wr, (1, 3, 0, 2)).reshape(G, S5_STATE, T * I)
        wi = jnp.transpose(wi, (1, 3, 0, 2)).reshape(G, S5_STATE, T * I)
        return jnp.concatenate([wr, -wi], axis=1)

    w = jnp.concatenate([w_op(0, pr[1:T + 1, 0], pi[1:T + 1, 0]),
                         w_op(1, pr[1:T + 1, 1][::-1], pi[1:T + 1, 1][::-1])], axis=1)

    def dup(z):
        return jnp.concatenate([z, z], -1)

    rows = []
    for d in range(2):
        rows += [dup(pr16[1, d]), dup(pi16[1, d]), dup(pr16[S5_SUPER, d]), dup(pi16[S5_SUPER, d])]
    dec = jnp.stack(rows, axis=1)
    prs = []
    for d in range(2):
        prs += [dup(jnp.transpose(pr16[:S5_SUPER, d], (1, 0, 2))), dup(jnp.transpose(pi16[:S5_SUPER, d], (1, 0, 2)))]
    pwt = jnp.concatenate(prs, axis=1)
    return mt.astype(BF16), vt.astype(BF16), w.astype(BF16), dec, pwt


S5_LANE_GROUPS = LANES // S5_GROUP


def _piece_transpose(arrs, lane):
    for d in (4, 2, 1):
        bit = ((lane // S5_GROUP) & d) != 0
        new = list(arrs)
        for r in range(S5_LANE_GROUPS):
            if r & d == 0:
                a, b = arrs[r], arrs[r + d]
                new[r] = jnp.where(bit, pltpu.roll(b, S5_GROUP * d, 1), a)
                new[r + d] = jnp.where(bit, b, pltpu.roll(a, LANES - S5_GROUP * d, 1))
        arrs = new
    return arrs


def _s5_kernel(u_ref, mt_ref, vt_ref, w_ref, dec_ref, pw_ref, dsk_ref, y_ref, xg_ref, yg_ref, vv_ref, hin_ref,
               hs_ref, *, nsc):
    nj = S5_SUPER
    nch = u_ref.shape[0] // S5_CHUNK
    ng = S5_LANE_GROUPS
    lane = lax.broadcasted_iota(jnp.int32, (nch, LANES), 1)
    for h in range(2):
        zs = [u_ref[pl.ds(ng * h + t, nch, stride=S5_CHUNK), :] for t in range(ng)]
        for g, a in enumerate(_piece_transpose(zs, lane)):
            xg_ref[g, :, LANES * h:LANES * (h + 1)] = a
    hs_ref[...] = jnp.zeros_like(hs_ref)

    def group(g, carry):
        x = xg_ref[g]
        xb = x.astype(BF16)
        yg_ref[g] = jnp.dot(xb, mt_ref[g], preferred_element_type=F32) + x * dsk_ref[g]
        vv = jnp.dot(xb, vt_ref[g], preferred_element_type=F32)
        for n in range(4):
            vv_ref[n] = vv[:, LANES * n:LANES * (n + 1)]
        dec = dec_ref[g]
        pw = pw_ref[g]
        for d in range(2):
            ar16, ai16, ar256, ai256 = (dec[4 * d + n:4 * d + n + 1] for n in range(4))

            def v(j, d=d):
                return vv_ref[2 * d, pl.ds(j, nsc, stride=nj), :]

            def jv(j, d=d):
                return vv_ref[2 * d + 1, pl.ds(j, nsc, stride=nj), :]

            order = list(range(nj)) if d == 0 else list(range(nj - 1, -1, -1))
            loc = [None] * nj
            l = jnp.zeros((nsc, LANES), F32)
            gg = jnp.zeros((nsc, LANES), F32)
            for n, j in enumerate(order):
                if n > 0:
                    p = order[n - 1]
                    l, gg = ar16 * l + ai16 * gg + v(p), ar16 * gg - ai16 * l + jv(p)
                loc[j] = l
            p = order[-1]
            e, je = ar16 * l + ai16 * gg + v(p), ar16 * gg - ai16 * l + jv(p)
            seq = list(range(nsc)) if d == 0 else [0] + list(range(nsc - 1, 0, -1))
            cur = jnp.zeros((1, LANES), F32)
            jcur = jnp.zeros((1, LANES), F32)
            for m in seq:
                hs_ref[2 * d, m:m + 1, :] = cur
                hs_ref[2 * d + 1, m:m + 1, :] = jcur
                cur, jcur = (ar256 * cur + ai256 * jcur + e[m:m + 1], ar256 * jcur - ai256 * cur + je[m:m + 1])
            hs = hs_ref[2 * d, 0:nsc, :]
            jhs = hs_ref[2 * d + 1, 0:nsc, :]
            for n, j in enumerate(order):
                pr = pw[32 * d + n:32 * d + n + 1]
                pi = pw[32 * d + 16 + n:32 * d + 16 + n + 1]
                hin_ref[d, pl.ds(j, nsc, stride=nj), :] = loc[j] + pr * hs + pi * jhs
        hin = jnp.concatenate([hin_ref[0], hin_ref[1]], axis=1).astype(BF16)
        yg_ref[g] += jnp.dot(hin, w_ref[g], preferred_element_type=F32)
        return carry

    lax.fori_loop(0, ng, group, 0, unroll=2)
    for h in range(2):
        ys = [yg_ref[g, :, LANES * h:LANES * (h + 1)] for g in range(ng)]
        for t, a in enumerate(_piece_transpose(ys, lane)):
            y_ref[pl.ds(ng * h + t, nch, stride=S5_CHUNK), :] = a


def _s5_scan(px, ops, dsk, *, a_width):
    B, T, _ = px.shape
    mt, vt, w, dec, pw = ops
    ng = S5_LANE_GROUPS
    C = S5_CHUNK * S5_GROUP
    nch = T // S5_CHUNK
    nsc = nch // S5_SUPER
    nsp = (nsc + 7) // 8 * 8
    return pl.pallas_call(
        functools.partial(_s5_kernel, nsc=nsc),
        grid=(a_width // LANES, B),
        in_specs=[pl.BlockSpec((None, T, LANES), lambda gb, b: (b, 0, gb)),
                  pl.BlockSpec((ng, C, C), lambda gb, b: (gb, 0, 0)),
                  pl.BlockSpec((ng, C, 2 * C), lambda gb, b: (gb, 0, 0)),
                  pl.BlockSpec((ng, C, C), lambda gb, b: (gb, 0, 0)),
                  pl.BlockSpec((ng, 8, LANES), lambda gb, b: (gb, 0, 0)),
                  pl.BlockSpec((ng, 64, LANES), lambda gb, b: (gb, 0, 0)),
                  pl.BlockSpec((ng, 1, C), lambda gb, b: (gb, 0, 0))],
        out_specs=pl.BlockSpec((None, T, LANES), lambda gb, b: (b, 0, gb)),
        out_shape=jax.ShapeDtypeStruct((B, T, a_width), F32),
        scratch_shapes=[pltpu.VMEM((ng, nch, C), F32), pltpu.VMEM((ng, nch, C), F32),
                        pltpu.VMEM((4, nch, LANES), F32), pltpu.VMEM((2, nch, LANES), F32),
                        pltpu.VMEM((4, nsp, LANES), F32)],
        compiler_params=_cparams(("arbitrary", "arbitrary")),
        name="s5_scan",
    )(px, mt, vt, w, dec, pw, dsk)


def _s5_glu_kernel(y_ref, w_ref, o_ref):
    g = jax.nn.gelu(y_ref[...])
    z = jnp.dot(g.astype(BF16), w_ref[...], preferred_element_type=F32)
    o_ref[...] = (g * jax.nn.sigmoid(z)).astype(o_ref.dtype)


def _s5_glu(y, w, *, tm):
    B, T, A = y.shape
    return pl.pallas_call(
        _s5_glu_kernel,
        grid=(B, T // tm),
        in_specs=[pl.BlockSpec((None, tm, A), lambda b, i: (b, i, 0)),
                  pl.BlockSpec((A, A), lambda b, i: (0, 0))],
        out_specs=pl.BlockSpec((None, tm, A), lambda b, i: (b, i, 0)),
        out_shape=jax.ShapeDtypeStruct((B, T, A), BF16),
        compiler_params=_cparams(("arbitrary", "arbitrary")),
        name="s5_glu",
    )(y, w)


def _log_sigmoid(z):
    return jnp.minimum(z, 0.0) - jnp.log(1.0 + jnp.exp(-jnp.abs(z)))


def _gla_kernel(*refs, nblk, dk, dv, hp, n_cast):
    q_ref, k_ref, v_ref, r_ref, lr_ref, cos_ref, sin_ref, wg_ref, bg_ref, ng_ref = refs[:10]
    cast_in = refs[10:10 + n_cast]
    o_ref = refs[10 + n_cast]
    cast_out = refs[11 + n_cast:11 + 2 * n_cast]
    s_ref, of_ref = refs[11 + 2 * n_cast:]
    p = pl.program_id(2)
    s = pl.program_id(3)
    C = GLA_CHUNK
    nch = GLA_BLOCK // C
    blk = jnp.where(p == 0, s, jnp.where(s == 0, 0, nblk - s))

    @pl.when(s == 0)
    def _():
        s_ref[...] = jnp.zeros_like(s_ref)

    NB = GLA_BLOCK
    lane = lax.broadcasted_iota(jnp.int32, (NB, dk), 1)
    lower = (lane & 63) < 32
    rpos = lax.broadcasted_iota(jnp.int32, (NB, dk), 0) & (C - 1)
    row = lax.broadcasted_iota(jnp.int32, (NB, NB), 0)
    col = lax.broadcasted_iota(jnp.int32, (NB, NB), 1)
    same_chunk = (row // C) == (col // C)
    eye = (lax.broadcasted_iota(jnp.int32, (dk, dk), 0) == lax.broadcasted_iota(jnp.int32, (dk, dk), 1)).astype(BF16)

    def rope(z, cos, sin):
        return z * cos + jnp.where(lower, pltpu.roll(z, dk - 32, 1), pltpu.roll(z, 32, 1)) * sin

    def run(reverse):
        _side_cast(cast_in, cast_out)
        for hd in range(hp):
            head(reverse, hd)

    def head(reverse, hd):
        mask = same_chunk & ((row <= col) if reverse else (row >= col))
        cos = cos_ref[...]
        sin = sin_ref[...]
        kcols = slice(hd * dk, (hd + 1) * dk)
        vcols = slice(hd * dv, (hd + 1) * dv)
        q = rope(q_ref[:, kcols], cos, sin) * (dk ** -0.5)
        k = rope(k_ref[:, kcols], cos, sin)
        vb = v_ref[:, vcols].astype(BF16)
        lrb = lr_ref[...].astype(BF16)
        la = _log_sigmoid(jnp.dot(lrb, wg_ref[hd], preferred_element_type=F32) + bg_ref[hd]) * (1.0 / GLA_TAU)
        b = la
        sh = 1
        while sh < C:
            if reverse:
                b = b + jnp.where(rpos < C - sh, pltpu.roll(b, NB - sh, 0), 0.0)
            else:
                b = b + jnp.where(rpos >= sh, pltpu.roll(b, sh, 0), 0.0)
            sh *= 2
        last = C - 1 if not reverse else 0
        tots = [b[c * C + last:c * C + last + 1] for c in range(nch)]
        btot = jnp.concatenate([jnp.broadcast_to(t, (C, dk)) for t in tots], 0)
        dmat = jnp.exp(jnp.concatenate(tots + [jnp.zeros((8 - nch, dk), F32)], 0).T)
        qi = (q * jnp.exp(b)).astype(BF16)
        ki = (k * jnp.exp(-b)).astype(BF16)
        ko = (k * jnp.exp(btot - b)).astype(BF16)
        att = jnp.where(mask, _nt(qi, ki), 0.0).astype(BF16)
        o = jnp.dot(att, vb, preferred_element_type=F32)
        upd, dcol = [], []
        for c in range(nch):
            kot = _nt(eye, ko[c * C:(c + 1) * C]).astype(BF16)
            upd.append(jnp.dot(kot, vb[c * C:(c + 1) * C], preferred_element_type=F32))
            dcol.append(dmat[:, c:c + 1])
        st = s_ref[hd]
        o_state = [None] * nch
        for c in (range(nch - 1, -1, -1) if reverse else range(nch)):
            o_state[c] = jnp.dot(qi[c * C:(c + 1) * C], st.astype(BF16), preferred_element_type=F32)
            st = st * dcol[c] + upd[c]
        s_ref[hd] = st
        o = o + jnp.concatenate(o_state, axis=0)
        tok = pl.ds(pl.multiple_of(blk * NB, NB), NB)
        if not reverse:
            of_ref[tok, vcols] = o
        else:
            o = o + of_ref[tok, vcols]
            o = o * lax.rsqrt(jnp.mean(o * o, axis=-1, keepdims=True) + EPS) * ng_ref[hd]
            r = r_ref[:, vcols]
            o_ref[:, vcols] = (o * (r * jax.nn.sigmoid(r))).astype(o_ref.dtype)

    @pl.when(p == 0)
    def _():
        run(False)

    @pl.when(p == 1)
    def _():
        run(True)


def _rope_tables(n_ctx, seq, dk):
    nf = dk // 4
    inv = ROPE_THETA ** (-jnp.arange(nf, dtype=F32) / nf)
    pos = jnp.arange(seq)
    ang_r = (pos // GRID_W).astype(F32)[:, None] * inv[None, :]
    ang_c = (pos % GRID_W).astype(F32)[:, None] * inv[None, :]
    cos = jnp.concatenate([jnp.cos(ang_r)] * 2 + [jnp.cos(ang_c)] * 2, axis=-1)
    sin = jnp.concatenate([-jnp.sin(ang_r), jnp.sin(ang_r), -jnp.sin(ang_c), jnp.sin(ang_c)], axis=-1)
    cos = jnp.concatenate([jnp.ones((n_ctx, dk), F32), cos], axis=0)
    sin = jnp.concatenate([jnp.zeros((n_ctx, dk), F32), sin], axis=0)
    return cos, sin


def _gla(px, w_gate2, b_gate, norm_g, cast, *, n_ctx, a_width):
    B, T, _ = px.shape
    H = GLA_HEADS
    qk = w_gate2.shape[-1]
    dk = qk // H
    bw = norm_g.shape[-1]
    dv = bw // H
    nblk = T // GLA_BLOCK
    cos, sin = _rope_tables(n_ctx, T - n_ctx, dk)
    lr_col0 = a_width + 2 * qk + 2 * bw
    wg = jnp.zeros((2, H, LANES, dk), F32)
    for d in range(2):
        wg = wg.at[d, :, d * GLA_RANK:(d + 1) * GLA_RANK, :].set(
            jnp.transpose(w_gate2[d].reshape(GLA_RANK, H, dk), (1, 0, 2)))
    wg = wg.astype(BF16)
    bg = b_gate.astype(F32).reshape(2, H, 1, dk)
    ng = norm_g.astype(F32).reshape(H, 1, dv)
    hp = GLA_HEADS_PER_STEP

    def blk(p, s):
        return jnp.where(p == 0, s, jnp.where(s == 0, 0, nblk - s))

    kw, vw = hp * dk, hp * dv
    qb, kb = a_width // kw, (a_width + qk) // kw
    vbk, rbk = (a_width + 2 * qk) // vw, (a_width + 2 * qk + bw) // vw
    lb = lr_col0 // LANES
    ngrp = H // hp
    views, c_in, c_out, c_shapes = _side_cast_specs(
        cast, B * ngrp * 2 * nblk, lambda b, g, p, s: ((b * ngrp + g) * 2 + p) * nblk + s)
    outs = pl.pallas_call(
        functools.partial(_gla_kernel, nblk=nblk, dk=dk, dv=dv, hp=hp, n_cast=len(views)),
        grid=(B, ngrp, 2, nblk),
        in_specs=[pl.BlockSpec((None, GLA_BLOCK, kw), lambda b, g, p, s: (b, blk(p, s), qb + g)),
                  pl.BlockSpec((None, GLA_BLOCK, kw), lambda b, g, p, s: (b, blk(p, s), kb + g)),
                  pl.BlockSpec((None, GLA_BLOCK, vw), lambda b, g, p, s: (b, blk(p, s), vbk + g)),
                  pl.BlockSpec((None, GLA_BLOCK, vw), lambda b, g, p, s: (b, blk(p, s), rbk + g)),
                  pl.BlockSpec((None, GLA_BLOCK, LANES), lambda b, g, p, s: (b, blk(p, s), lb)),
                  pl.BlockSpec((GLA_BLOCK, dk), lambda b, g, p, s: (blk(p, s), 0)),
                  pl.BlockSpec((GLA_BLOCK, dk), lambda b, g, p, s: (blk(p, s), 0)),
                  pl.BlockSpec((None, hp, LANES, dk), lambda b, g, p, s: (p, g, 0, 0)),
                  pl.BlockSpec((None, hp, 1, dk), lambda b, g, p, s: (p, g, 0, 0)),
                  pl.BlockSpec((hp, 1, dv), lambda b, g, p, s: (g, 0, 0))] + c_in,
        out_specs=[pl.BlockSpec((None, GLA_BLOCK, vw),
                                lambda b, g, p, s: (b, jnp.where(p == 0, 0, blk(p, s)), g))] + c_out,
        out_shape=[jax.ShapeDtypeStruct((B, T, bw), BF16)] + c_shapes,
        scratch_shapes=[pltpu.VMEM((hp, dk, dv), F32), pltpu.VMEM((T, vw), F32)],
        compiler_params=_cparams(("arbitrary", "arbitrary", "arbitrary", "arbitrary")),
        name="gla",
    )(px, px, px, px, px, cos, sin, wg, bg, ng, *views)
    return outs[0], outs[1:]


def _na_kernel(*refs, n_ctx, rows, n_cast):
    q_ref, k_ref, v_ref, bias_ref, qn_ref, kn_ref = refs[:6]
    o_ref = refs[6 + n_cast]
    kb_ref, vb_ref = refs[7 + 2 * n_cast:]
    _side_cast(refs[6:6 + n_cast], refs[7 + n_cast:7 + 2 * n_cast])
    hd = NA_HEAD_DIM
    W = GRID_W
    T = k_ref.shape[0]
    step = n_ctx

    def prep(i, c):
        sl = pl.ds(pl.multiple_of(i * step, step), step)
        kk = k_ref[sl, :]
        kb_ref[sl, :] = (kk * lax.rsqrt(jnp.mean(kk * kk, axis=-1, keepdims=True) + EPS) * kn_ref[...]).astype(BF16)
        vb_ref[sl, :] = v_ref[sl, :].astype(BF16)
        return c

    lax.fori_loop(0, T // step, prep, 0)
    scale = hd ** -0.5
    ngrp = rows // NA_QROWS
    nq = NA_QROWS * W
    nk = NA_UROWS * W

    def group(g, c):
        r0 = g * NA_QROWS
        ustart = jnp.clip(r0 - NA_KH // 2, 0, rows - NA_UROWS)
        case = jnp.where(g == 0, 0, jnp.where(g == ngrp - 1, 2, 1))
        q = q_ref[pl.ds(pl.multiple_of(n_ctx + r0 * W, W), nq), :]
        q = (q * lax.rsqrt(jnp.mean(q * q, axis=-1, keepdims=True) + EPS) * qn_ref[...] * scale).astype(BF16)
        ws = pl.ds(pl.multiple_of(n_ctx + ustart * W, W), nk)
        s_lat = _nt(q, kb_ref[ws, :]) + bias_ref[case]
        s_ctx = _nt(q, kb_ref[0:n_ctx, :])
        m = jnp.maximum(jnp.max(s_lat, axis=-1, keepdims=True), jnp.max(s_ctx, axis=-1, keepdims=True))
        p_lat = jnp.exp(s_lat - m)
        p_ctx = jnp.exp(s_ctx - m)
        den = jnp.sum(p_lat, axis=-1, keepdims=True) + jnp.sum(p_ctx, axis=-1, keepdims=True)
        o = (jnp.dot(p_lat.astype(BF16), vb_ref[ws, :], preferred_element_type=F32)
             + jnp.dot(p_ctx.astype(BF16), vb_ref[0:n_ctx, :], preferred_element_type=F32))
        o_ref[pl.ds(pl.multiple_of(r0 * W, W), nq), :] = (o / den).astype(o_ref.dtype)
        return c

    lax.fori_loop(0, ngrp, group, 0, unroll=4)


def _na_bias(rpb, rows):
    H = rpb.shape[0]
    col = jnp.arange(GRID_W)
    cs = jnp.clip(col - NA_KW // 2, 0, GRID_W - NA_KW)
    col_ok = (col[None, :] >= cs[:, None]) & (col[None, :] < cs[:, None] + NA_KW)
    dc_idx = jnp.clip(col[None, :] - col[:, None] + NA_KW - 1, 0, 2 * NA_KW - 2)
    rpb_c = jnp.where(col_ok[None, None], rpb.astype(F32)[:, :, dc_idx], NEG_INF)
    ngrp = rows // NA_QROWS
    tables = []
    for g in (0, 1, ngrp - 1):
        r0 = g * NA_QROWS
        ustart = min(max(r0 - NA_KH // 2, 0), rows - NA_UROWS)
        qr = r0 + jnp.arange(NA_QROWS)
        start = jnp.clip(qr - NA_KH // 2, 0, rows - NA_KH)
        kr = ustart + jnp.arange(NA_UROWS)
        ok = (kr[None, :] >= start[:, None]) & (kr[None, :] < start[:, None] + NA_KH)
        idx = jnp.clip(kr[None, :] - qr[:, None] + NA_KH - 1, 0, 2 * NA_KH - 2)
        t = jnp.where(ok[None, :, :, None, None], rpb_c[:, idx], NEG_INF)
        tables.append(jnp.transpose(t, (0, 1, 3, 2, 4)).reshape(H, NA_QROWS * GRID_W, NA_UROWS * GRID_W))
    return jnp.stack(tables, axis=1)


def _na_attention(qkv, q_norm, k_norm, rpb, cast, *, n_ctx):
    B, T, D3 = qkv.shape
    D = D3 // 3
    H = D // NA_HEAD_DIM
    seq = T - n_ctx
    rows = seq // GRID_W
    assert rows % NA_QROWS == 0 and rows >= NA_UROWS + NA_QROWS
    bias = _na_bias(rpb, rows)
    hd = NA_HEAD_DIM
    views, c_in, c_out, c_shapes = _side_cast_specs(cast, B * H, lambda b, h: b * H + h)
    outs = pl.pallas_call(
        functools.partial(_na_kernel, n_ctx=n_ctx, rows=rows, n_cast=len(views)),
        grid=(B, H),
        in_specs=[pl.BlockSpec((None, T, hd), lambda b, h: (b, 0, h)),
                  pl.BlockSpec((None, T, hd), lambda b, h: (b, 0, H + h)),
                  pl.BlockSpec((None, T, hd), lambda b, h: (b, 0, 2 * H + h)),
                  pl.BlockSpec((None, 3, NA_QROWS * GRID_W, NA_UROWS * GRID_W), lambda b, h: (h, 0, 0, 0)),
                  pl.BlockSpec((1, hd), lambda b, h: (0, 0)),
                  pl.BlockSpec((1, hd), lambda b, h: (0, 0))] + c_in,
        out_specs=[pl.BlockSpec((None, seq, hd), lambda b, h: (b, 0, h))] + c_out,
        out_shape=[jax.ShapeDtypeStruct((B, seq, D), BF16)] + c_shapes,
        scratch_shapes=[pltpu.VMEM((T, hd), BF16), pltpu.VMEM((T, hd), BF16)],
        compiler_params=_cparams(("arbitrary", "arbitrary")),
        name="na_attention",
    )(qkv, qkv, qkv, bias, q_norm.astype(F32).reshape(1, hd), k_norm.astype(F32).reshape(1, hd), *views)
    return outs[0], outs[1:]


def _router_kernel(u_ref, g_ref, sh_ref, sc_ref, wr_ref, h_ref, mf_ref, mi_ref):
    h = _normmod(u_ref[...], g_ref[...], sh_ref[...], sc_ref[...])
    h_ref[...] = h
    w = wr_ref[...]
    w_hi = w.astype(BF16)
    w_lo = (w - w_hi.astype(F32)).astype(BF16)
    h_hi = h.astype(BF16)
    h_lo = (h - h_hi.astype(F32)).astype(BF16)
    logits = (jnp.dot(h_hi, w_hi, preferred_element_type=F32) + jnp.dot(h_hi, w_lo, preferred_element_type=F32)
              + jnp.dot(h_lo, w_hi, preferred_element_type=F32))
    lane = lax.broadcasted_iota(jnp.int32, logits.shape, 1).astype(F32)
    big = 1e9

    def first_max(vals):
        m = jnp.max(vals, axis=-1, keepdims=True)
        return m, jnp.min(jnp.where(vals == m, lane, big), axis=-1, keepdims=True)

    gl = jnp.where(lane < MOE_GROUPS, logits, -jnp.inf)
    gmax, gidx = first_max(gl)
    p_top = 1.0 / jnp.sum(jnp.exp(gl - gmax), axis=-1, keepdims=True)
    lo = MOE_GROUPS + MOE_PER_GROUP * gidx
    el = jnp.where((lane >= lo) & (lane < lo + MOE_PER_GROUP), logits, -jnp.inf)
    v1, i1 = first_max(el)
    v2, i2 = first_max(jnp.where(lane == i1, -jnp.inf, el))
    e2 = jnp.exp(v2 - v1)
    w1 = p_top / (1.0 + e2)
    w2 = p_top * e2 / (1.0 + e2)
    mf_ref[...] = jnp.where(lane == 0, w1, jnp.where(lane == 1, w2, 0.0))
    mi_ref[...] = jnp.where(lane == 0, i1 - MOE_GROUPS, jnp.where(lane == 1, i2 - MOE_GROUPS, 0.0)).astype(jnp.int32)


def _router(u, gamma, mods, k_shift, k_scale, wr, *, n_ctx, tm):
    B, T, D = u.shape
    sel = _mod_sel(n_ctx // tm, B)
    return pl.pallas_call(
        _router_kernel,
        grid=(B, T // tm),
        in_specs=[pl.BlockSpec((None, tm, D), lambda b, i: (b, i, 0)),
                  pl.BlockSpec((1, D), lambda b, i: (0, 0)),
                  pl.BlockSpec((None, None, 1, D), lambda b, i: (sel(b, i), k_shift, 0, 0)),
                  pl.BlockSpec((None, None, 1, D), lambda b, i: (sel(b, i), k_scale, 0, 0)),
                  pl.BlockSpec((D, LANES), lambda b, i: (0, 0))],
        out_specs=[pl.BlockSpec((None, tm, D), lambda b, i: (b, i, 0)),
                   pl.BlockSpec((None, tm, LANES), lambda b, i: (b, i, 0)),
                   pl.BlockSpec((None, tm, LANES), lambda b, i: (b, i, 0))],
        out_shape=[jax.ShapeDtypeStruct((B, T, D), F32),
                   jax.ShapeDtypeStruct((B, T, LANES), F32),
                   jax.ShapeDtypeStruct((B, T, LANES), jnp.int32)],
        compiler_params=_cparams(("arbitrary", "arbitrary")),
        name="moe_router",
    )(u, gamma.reshape(1, D), mods, mods, wr)


def _moe_plan(eid2):
    N = eid2.shape[0]
    M = 2 * N
    E = MOE_EXPERTS
    eid = eid2.reshape(-1)
    onehot = (eid[:, None] == jnp.arange(E, dtype=jnp.int32)[None, :]).astype(jnp.int32)
    csum = jnp.cumsum(onehot, axis=0)
    rank = jnp.sum(onehot * csum, axis=1) - 1
    counts = csum[-1]
    padded = (counts + MOE_ROWS - 1) // MOE_ROWS * MOE_ROWS
    pend = jnp.cumsum(padded)
    pstart = pend - padded
    dest = (jnp.sum(onehot * pstart[None, :], axis=1) + rank).astype(jnp.int32)
    P = (M + E * (MOE_ROWS - 1) + MOE_ROWS - 1) // MOE_ROWS * MOE_ROWS
    nb = P // MOE_ROWS
    slot_tok = jnp.zeros((P,), jnp.int32).at[dest].set(jnp.arange(M, dtype=jnp.int32) // 2)
    pos = dest.reshape(N, 2)
    blk_start = jnp.arange(nb, dtype=jnp.int32) * MOE_ROWS
    blk_e = jnp.minimum(jnp.sum((pend[None, :] <= blk_start[:, None]).astype(jnp.int32), axis=1), E - 1)
    n_active = (pend[-1] // MOE_ROWS).astype(jnp.int32).reshape(1)
    return slot_tok.reshape(nb, MOE_ROWS), pos, blk_e.astype(jnp.int32), n_active


def _row_copy(src_hbm, idx_smem, dst_ref, sem, r):
    return pltpu.make_async_copy(src_hbm.at[pl.ds(idx_smem[r], 1), :], dst_ref.at[pl.ds(r, 1), :], sem)


def _expert_kernel(blk_e_ref, nact_ref, idx_hbm, h_hbm, wg_ref, wu_ref, wd_ref, o_ref, idx0, idx1, xbuf, sem_idx,
                   sem_rows):
    i = pl.program_id(0)
    nact = nact_ref[0]
    R = MOE_ROWS
    slot = lax.rem(i, 2)
    idx_bufs = (idx0, idx1)

    def idx_copy(blk, s):
        return pltpu.make_async_copy(idx_hbm.at[blk], idx_bufs[s], sem_idx.at[s])

    def issue_rows(s):
        def body(r, c):
            _row_copy(h_hbm, idx_bufs[s], xbuf.at[s], sem_rows.at[s], r).start(priority=1)
            return c

        lax.fori_loop(0, R, body, 0, unroll=8)

    @pl.when(i == 0)
    def _():
        first = idx_copy(0, 0)
        first.start()
        first.wait()
        issue_rows(0)

        @pl.when(nact > 1)
        def _():
            idx_copy(1, 1).start()

    def prefetch(s):
        @pl.when(i + 1 < nact)
        def _():
            idx_copy(i + 1, 1 - s).wait()
            issue_rows(1 - s)

        @pl.when(i + 2 < nact)
        def _():
            idx_copy(i + 2, s).start()

    @pl.when(slot == 0)
    def _():
        prefetch(0)

    @pl.when(slot == 1)
    def _():
        prefetch(1)

    @pl.when(i < nact)
    def _():
        pltpu.make_async_copy(h_hbm.at[pl.ds(0, R), :], xbuf.at[slot], sem_rows.at[slot]).wait()
        xb = xbuf[slot].astype(BF16)
        g = jnp.dot(xb, wg_ref[...], preferred_element_type=F32)
        u = jnp.dot(xb, wu_ref[...], preferred_element_type=F32)
        hmid = (g * jax.nn.sigmoid(g) * u).astype(BF16)
        o_ref[...] = jnp.dot(hmid, wd_ref[...], preferred_element_type=F32)

    @pl.when(i >= nact)
    def _():
        o_ref[...] = jnp.zeros_like(o_ref)


def _expert_mlp(h, slot_tok, blk_e, n_active, wg, wu, wd):
    nb, R = slot_tok.shape
    D = h.shape[1]
    F = wg.shape[2]
    grid_spec = pltpu.PrefetchScalarGridSpec(
        num_scalar_prefetch=2,
        grid=(nb,),
        in_specs=[pl.BlockSpec(memory_space=pl.ANY),
                  pl.BlockSpec(memory_space=pl.ANY),
                  pl.BlockSpec((None, D, F), lambda i, be, na: (be[i], 0, 0)),
                  pl.BlockSpec((None, D, F), lambda i, be, na: (be[i], 0, 0)),
                  pl.BlockSpec((None, F, D), lambda i, be, na: (be[i], 0, 0))],
        out_specs=pl.BlockSpec((R, D), lambda i, be, na: (i, 0)),
        scratch_shapes=[pltpu.SMEM((R,), jnp.int32), pltpu.SMEM((R,), jnp.int32), pltpu.VMEM((2, R, D), F32),
                        pltpu.SemaphoreType.DMA((2,)), pltpu.SemaphoreType.DMA((2,))],
    )
    return pl.pallas_call(
        _expert_kernel,
        grid_spec=grid_spec,
        out_shape=jax.ShapeDtypeStruct((nb * R, D), F32),
        compiler_params=_cparams(("arbitrary",)),
        name="moe_experts",
    )(blk_e, n_active, slot_tok, h, wg, wu, wd)


def _combine_kernel(*refs, n_cast):
    p1_hbm, p2_hbm, yb_hbm, mf_ref, r_ref, g_ref = refs[:6]
    o_ref = refs[6 + n_cast]
    i1a, i1b, i2a, i2b, g1_ref, g2_ref, sem_idx, sem_rows = refs[7 + 2 * n_cast:]
    i1_bufs, i2_bufs = (i1a, i1b), (i2a, i2b)
    t = pl.program_id(0) * pl.num_programs(1) + pl.program_id(1)
    n = pl.num_programs(0) * pl.num_programs(1)
    R = o_ref.shape[0]
    slot = lax.rem(t, 2)

    def idx_copies(tile, s):
        return (pltpu.make_async_copy(p1_hbm.at[tile], i1_bufs[s], sem_idx.at[s]),
                pltpu.make_async_copy(p2_hbm.at[tile], i2_bufs[s], sem_idx.at[s]))

    def issue_rows(s):
        def body(r, c):
            _row_copy(yb_hbm, i1_bufs[s], g1_ref.at[s], sem_rows.at[s], r).start(priority=0)
            _row_copy(yb_hbm, i2_bufs[s], g2_ref.at[s], sem_rows.at[s], r).start(priority=1)
            return c

        lax.fori_loop(0, R, body, 0, unroll=8)

    @pl.when(t == 0)
    def _():
        for c in idx_copies(0, 0):
            c.start()
        for c in idx_copies(0, 0):
            c.wait()
        issue_rows(0)

        @pl.when(n > 1)
        def _():
            for c in idx_copies(1, 1):
                c.start()

    def prefetch(s):
        @pl.when(t + 1 < n)
        def _():
            for c in idx_copies(t + 1, 1 - s):
                c.wait()
            issue_rows(1 - s)

        @pl.when(t + 2 < n)
        def _():
            for c in idx_copies(t + 2, s):
                c.start()

    @pl.when(slot == 0)
    def _():
        prefetch(0)

    @pl.when(slot == 1)
    def _():
        prefetch(1)

    _side_cast(refs[6:6 + n_cast], refs[7 + n_cast:7 + 2 * n_cast])
    pltpu.make_async_copy(yb_hbm.at[pl.ds(0, R), :], g1_ref.at[slot], sem_rows.at[slot]).wait()
    pltpu.make_async_copy(yb_hbm.at[pl.ds(0, R), :], g2_ref.at[slot], sem_rows.at[slot]).wait()
    mf = mf_ref[...]
    y = mf[:, 0:1] * g1_ref[slot] + mf[:, 1:2] * g2_ref[slot]
    o_ref[...] = r_ref[...] + g_ref[...] * y


def _moe_combine(yb, pos, mf, res, mods, k_gate, cast, *, n_ctx):
    B, T, _ = mf.shape
    D = yb.shape[1]
    R = MOE_ROWS
    nt = T // R
    sel = _mod_sel(n_ctx // R, B)
    p1 = pos[:, 0].reshape(B * nt, R)
    p2 = pos[:, 1].reshape(B * nt, R)
    views, c_in, c_out, c_shapes = _side_cast_specs(cast, B * nt, lambda b, i: b * nt + i)
    outs = pl.pallas_call(
        functools.partial(_combine_kernel, n_cast=len(views)),
        grid=(B, nt),
        in_specs=[pl.BlockSpec(memory_space=pl.ANY), pl.BlockSpec(memory_space=pl.ANY),
                  pl.BlockSpec(memory_space=pl.ANY),
                  pl.BlockSpec((None, R, LANES), lambda b, i: (b, i, 0)),
                  pl.BlockSpec((None, R, D), lambda b, i: (b, i, 0)),
                  pl.BlockSpec((None, None, 1, D), lambda b, i: (sel(b, i), k_gate, 0, 0))] + c_in,
        out_specs=[pl.BlockSpec((None, R, D), lambda b, i: (b, i, 0))] + c_out,
        out_shape=[jax.ShapeDtypeStruct((B, T, D), F32)] + c_shapes,
        scratch_shapes=[pltpu.SMEM((R,), jnp.int32)] * 4
        + [pltpu.VMEM((2, R, D), F32), pltpu.VMEM((2, R, D), F32),
           pltpu.SemaphoreType.DMA((2,)), pltpu.SemaphoreType.DMA((2,))],
        compiler_params=_cparams(("arbitrary", "arbitrary")),
        name="moe_combine",
    )(p1, p2, yb, mf, res, mods, *views)
    return outs[0], outs[1:]


def _hier_moe(u, gamma, mods, w_route_group, w_route_expert, wg, wu, wd, cast, *, n_ctx):
    B, T, D = u.shape
    E = MOE_EXPERTS
    wr = jnp.zeros((D, LANES), F32)
    wr = wr.at[:, :MOE_GROUPS].set(w_route_group.astype(F32))
    wr = wr.at[:, MOE_GROUPS:MOE_GROUPS + MOE_EXPERTS].set(w_route_expert.astype(F32))
    h, mf, mi = _router(u, gamma, mods, 3, 4, wr, n_ctx=n_ctx, tm=MOE_ROWS)
    slot_tok, pos, blk_e, n_active = _moe_plan(mi[:, :, :2].reshape(B * T, 2))
    yb = _expert_mlp(h.reshape(B * T, D), slot_tok, blk_e, n_active, wg.reshape(E, D, -1), wu.reshape(E, D, -1),
                     wd.reshape(E, -1, D))
    return _moe_combine(yb, pos, mf, u, mods, 5, cast, n_ctx=n_ctx)


def _ab_layer(u, mods, gamma, w_in, w_out, s5p, s5_d, s5_w_glu, gla_w_gate2, gla_b_gate, gla_norm_g, cast, *, n_ctx):
    a_width = s5_w_glu.shape[0]
    n_in = w_in.shape[1]
    n_pad = (n_in + LANES - 1) // LANES * LANES
    w_in_b = jnp.pad(w_in, ((0, 0), (0, n_pad - n_in))).astype(BF16)
    px = _normmod_matmul(u, gamma, mods, 0, 1, w_in_b, n_ctx=n_ctx, tm=256, tn=n_pad // 3)
    G = a_width // S5_GROUP
    ops = _s5_operators(*s5p)
    dsk = jnp.tile(s5_d.astype(F32).reshape(G, 1, S5_GROUP), (1, S5_CHUNK, 1)).reshape(G, 1, S5_CHUNK * S5_GROUP)
    ya = _s5_glu(_s5_scan(px, ops, dsk, a_width=a_width), s5_w_glu.astype(BF16), tm=256)
    yb, casted = _gla(px, gla_w_gate2, gla_b_gate, gla_norm_g, cast, n_ctx=n_ctx, a_width=a_width)
    out = _matmul_residual([ya, yb], w_out.astype(BF16), u, mods, 2, n_ctx=n_ctx, row0=0, tm=256, tn=1024)
    return out, casted


def _na_layer(u, mods, gamma, w_qkv, w_out, q_norm, k_norm, rpb, cast, *, n_ctx):
    qkv = _normmod_matmul(u, gamma, mods, 0, 1, w_qkv.astype(BF16), n_ctx=n_ctx, tm=256, tn=2048)
    o, casted = _na_attention(qkv, q_norm, k_norm, rpb, cast, n_ctx=n_ctx)
    out = _matmul_residual([o], w_out.astype(BF16), u, mods, 2, n_ctx=0, row0=n_ctx, tm=256, tn=1024)
    return out, casted


def kernel(x, c, ctx, c_ctx, ada_w, ada_b, norm1_g, norm2_g, ab_w_in, ab_w_out, s5_lam_re, s5_lam_im, s5_log_dt,
           s5_b_re, s5_b_im, s5_c_re, s5_c_im, s5_d, s5_w_glu, gla_w_gate2, gla_b_gate, gla_norm_g, na_w_qkv,
           na_w_out, na_q_norm, na_k_norm, na_rpb, moe_w_route_group, moe_w_route_expert, moe_w_gate, moe_w_up,
           moe_w_down):
    B, seq, D = x.shape
    n_ctx = ctx.shape[1]
    depth = ada_w.shape[0]
    assert depth == 2 and n_ctx == S5_CHUNK * S5_SUPER and seq % (GRID_W * 4) == 0
    u = jnp.concatenate([ctx, x], axis=1)
    cvec = jnp.zeros((8, D), F32).at[:B].set(c).at[B].set(c_ctx)
    mods_all = _ada_mod(cvec, ada_w, ada_b)[:, :B + 1]
    mods = mods_all[0]
    s5p = (s5_lam_re[0], s5_lam_im[0], s5_log_dt[0], s5_b_re[0], s5_b_im[0], s5_c_re[0], s5_c_im[0])
    u, (wg0, wu0, wd0) = _ab_layer(u, mods, norm1_g[0], ab_w_in[0], ab_w_out[0], s5p, s5_d[0], s5_w_glu[0],
                                   gla_w_gate2[0], gla_b_gate[0], gla_norm_g[0],
                                   [(moe_w_gate, 0), (moe_w_up, 0), (moe_w_down, 0)], n_ctx=n_ctx)
    u, (wd1,) = _hier_moe(u, norm2_g[0], mods, moe_w_route_group[0], moe_w_route_expert[0], wg0, wu0, wd0,
                          [(moe_w_down, 1)], n_ctx=n_ctx)
    mods = mods_all[1]
    xl, (wg1, wu1) = _na_layer(u, mods, norm1_g[1], na_w_qkv[0], na_w_out[0], na_q_norm[0], na_k_norm[0], na_rpb[0],
                               [(moe_w_gate, 1), (moe_w_up, 1)], n_ctx=n_ctx)
    out, _ = _hier_moe(xl, norm2_g[1], mods, moe_w_route_group[1], moe_w_route_expert[1], wg1, wu1, wd1, [],
                       n_ctx=0)
    return out
```

```python
import functools

import jax
import jax.numpy as jnp
from jax import lax
from jax.experimental import pallas as pl
from jax.experimental.pallas import tpu as pltpu

F32 = jnp.float32
BF16 = jnp.bfloat16

V7X_VMEM_BYTES = 64 * 1024 * 1024
VMEM_LIMIT = V7X_VMEM_BYTES - 12 * 1024 * 1024
LANES = 128

EPS = 1e-6
NEG_INF = -1e30
GRID_W = 64
N_MOD = 6
S5_GROUP = 16
S5_STATE = 64
S5_CHUNK = 16
S5_SUPER = 16
GLA_HEADS = 4
GLA_RANK = 16
GLA_TAU = 16.0
GLA_CHUNK = 64
GLA_BLOCK = 256
GLA_HEADS_PER_STEP = 4
NA_HEAD_DIM = 128
NA_KH = 8
NA_KW = 16
NA_QROWS = 4
NA_UROWS = NA_KH + NA_QROWS - 1
ROPE_THETA = 10000.0
MOE_GROUPS = 4
MOE_PER_GROUP = 8
MOE_EXPERTS = MOE_GROUPS * MOE_PER_GROUP
MOE_ROWS = 256


def _cparams(sem):
    return pltpu.CompilerParams(dimension_semantics=sem, vmem_limit_bytes=VMEM_LIMIT)


def _nt(a, b):
    return lax.dot_general(a, b, (((1,), (1,)), ((), ())), preferred_element_type=F32)


def _side_cast_specs(tensors, n_steps, lin):
    nblocks = 1 << (n_steps.bit_length() - 1)

    def slab(*ids):
        return jnp.minimum(lin(*ids), nblocks - 1)

    views, in_specs, out_specs, out_shapes = [], [], [], []
    for t, layer in tensors:
        v = t.reshape(t.shape[0], -1, t.shape[-1])
        rows, cols = v.shape[1:]
        assert rows % (16 * nblocks) == 0
        views.append(v)
        in_specs.append(pl.BlockSpec((None, rows // nblocks, cols), functools.partial(
            lambda layer, *ids: (layer, slab(*ids), 0), layer)))
        out_specs.append(pl.BlockSpec((rows // nblocks, cols), lambda *ids: (slab(*ids), 0)))
        out_shapes.append(jax.ShapeDtypeStruct((rows, cols), BF16))
    return views, in_specs, out_specs, out_shapes


def _side_cast(in_refs, out_refs):
    for i_ref, o_ref in zip(in_refs, out_refs):
        o_ref[...] = i_ref[...].astype(BF16)


def _ada_kernel(c_ref, w_ref, b_ref, o_ref):
    c = c_ref[...]
    s = c * jax.nn.sigmoid(c)
    o_ref[...] = jnp.dot(s.astype(BF16), w_ref[...].astype(BF16), preferred_element_type=F32) + b_ref[...]


def _ada_mod(cvec, w, b):
    R, D = cvec.shape
    L, _, N = w.shape
    tn = 1024
    out = pl.pallas_call(
        _ada_kernel,
        grid=(L, N // tn),
        in_specs=[pl.BlockSpec((R, D), lambda l, j: (0, 0)),
                  pl.BlockSpec((None, D, tn), lambda l, j: (l, 0, j)),
                  pl.BlockSpec((None, 1, tn), lambda l, j: (l, 0, j))],
        out_specs=pl.BlockSpec((None, R, tn), lambda l, j: (l, 0, j)),
        out_shape=jax.ShapeDtypeStruct((L, R, N), F32),
        compiler_params=_cparams(("arbitrary", "arbitrary")),
        name="ada_mod",
    )(cvec, w, b.reshape(L, 1, N))
    return out.reshape(L, R, N_MOD, 1, D)


def _mod_sel(nct, nb):
    if nct == 0:
        return lambda b, i: b
    return lambda b, i: jnp.where(i < nct, nb, b)


def _normmod(x, g, sh, sc):
    y = x * lax.rsqrt(jnp.mean(x * x, axis=-1, keepdims=True) + EPS) * g
    return y * (1.0 + sc) + sh


def _normmod_mm_kernel(u_ref, g_ref, sh_ref, sc_ref, w_ref, o_ref, *, tn):
    xn = _normmod(u_ref[...], g_ref[...], sh_ref[...], sc_ref[...]).astype(BF16)
    for n0 in range(0, o_ref.shape[-1], tn):
        o_ref[:, n0:n0 + tn] = jnp.dot(xn, w_ref[:, n0:n0 + tn], preferred_element_type=F32).astype(o_ref.dtype)


def _normmod_matmul(u, gamma, mods, k_shift, k_scale, w, *, n_ctx, tm, tn, out_dtype=F32):
    B, T, D = u.shape
    N = w.shape[1]
    sel = _mod_sel(n_ctx // tm, B)
    return pl.pallas_call(
        functools.partial(_normmod_mm_kernel, tn=tn),
        grid=(B, T // tm),
        in_specs=[pl.BlockSpec((None, tm, D), lambda b, i: (b, i, 0)),
                  pl.BlockSpec((1, D), lambda b, i: (0, 0)),
                  pl.BlockSpec((None, None, 1, D), lambda b, i: (sel(b, i), k_shift, 0, 0)),
                  pl.BlockSpec((None, None, 1, D), lambda b, i: (sel(b, i), k_scale, 0, 0)),
                  pl.BlockSpec((D, N), lambda b, i: (0, 0), pipeline_mode=pl.Buffered(1))],
        out_specs=pl.BlockSpec((None, tm, N), lambda b, i: (b, i, 0)),
        out_shape=jax.ShapeDtypeStruct((B, T, N), out_dtype),
        compiler_params=_cparams(("arbitrary", "arbitrary")),
        name="normmod_matmul",
    )(u, gamma.reshape(1, D), mods, mods, w)


def _mm_res_kernel(*refs, n_lhs, tn):
    a_refs, w_ref, r_ref, g_ref, o_ref = refs[:n_lhs], refs[n_lhs], refs[n_lhs + 1], refs[n_lhs + 2], refs[n_lhs + 3]
    lhs = [a[...] for a in a_refs]
    for n0 in range(0, o_ref.shape[-1], tn):
        acc, k0 = None, 0
        for a in lhs:
            part = jnp.dot(a, w_ref[k0:k0 + a.shape[1], n0:n0 + tn], preferred_element_type=F32)
            acc = part if acc is None else acc + part
            k0 += a.shape[1]
        o_ref[:, n0:n0 + tn] = r_ref[:, n0:n0 + tn] + g_ref[:, n0:n0 + tn] * acc


def _matmul_residual(lhs, w, res, mods, k_gate, *, n_ctx, row0, tm, tn):
    B, Ta, _ = lhs[0].shape
    K, N = w.shape
    sel = _mod_sel(n_ctx // tm, B)
    r0 = row0 // tm
    return pl.pallas_call(
        functools.partial(_mm_res_kernel, n_lhs=len(lhs), tn=tn),
        grid=(B, Ta // tm),
        in_specs=[pl.BlockSpec((None, tm, a.shape[2]), lambda b, i: (b, i, 0)) for a in lhs]
        + [pl.BlockSpec((K, N), lambda b, i: (0, 0), pipeline_mode=pl.Buffered(1)),
           pl.BlockSpec((None, tm, N), lambda b, i: (b, i + r0, 0)),
           pl.BlockSpec((None, None, 1, N), lambda b, i: (sel(b, i), k_gate, 0, 0))],
        out_specs=pl.BlockSpec((None, tm, N), lambda b, i: (b, i, 0)),
        out_shape=jax.ShapeDtypeStruct((B, Ta, N), F32),
        compiler_params=_cparams(("arbitrary", "arbitrary")),
        name="matmul_residual",
    )(*lhs, w, res, mods)


def _s5_operators(lam_re, lam_im, log_dt, b_re, b_im, c_re, c_im):
    T = S5_CHUNK
    lr = jnp.minimum(lam_re.astype(F32), -1e-4)
    li = lam_im.astype(F32)
    dt = jnp.exp(log_dt.astype(F32))[..., None]
    mag = jnp.exp(lr * dt)
    a_re = mag * jnp.cos(li * dt)
    a_im = mag * jnp.sin(li * dt)
    num_re = a_re - 1.0
    den = lr * lr + li * li
    f_re = (num_re * lr + a_im * li) / den
    f_im = (a_im * lr - num_re * li) / den
    bb_re = f_re[..., None] * b_re.astype(F32) - f_im[..., None] * b_im.astype(F32)
    bb_im = f_re[..., None] * b_im.astype(F32) + f_im[..., None] * b_re.astype(F32)
    cr, ci = c_re.astype(F32), c_im.astype(F32)

    def powers(step, n):
        m = (step * jnp.arange(n + 1, dtype=F32))[:, None, None, None]
        mg = jnp.exp(m * (lr * dt))
        return mg * jnp.cos(m * (li * dt)), mg * jnp.sin(m * (li * dt))

    pr, pi = powers(1, T)
    pr16, pi16 = powers(T, S5_SUPER)
    ca_re = cr[None] * pr[:T, :, :, None, :] - ci[None] * pi[:T, :, :, None, :]
    ca_im = cr[None] * pi[:T, :, :, None, :] + ci[None] * pr[:T, :, :, None, :]
    kk = (jnp.einsum('kdgip,dgpj->dkgij', ca_re, bb_re) - jnp.einsum('kdgip,dgpj->dkgij', ca_im, bb_im))
    G = lr.shape[1]
    I = S5_GROUP
    s_idx = jnp.arange(T)[:, None]
    t_idx = jnp.arange(T)[None, :]

    def toeplitz(kd, lag):
        m = jnp.where((lag >= 0)[:, :, None, None, None], kd[jnp.clip(lag, 0, T - 1)], 0.0)
        return jnp.transpose(m, (2, 0, 4, 1, 3)).reshape(G, T * I, T * I)

    mt = toeplitz(kk[0], t_idx - s_idx) + toeplitz(kk[1], s_idx - t_idx)

    def v_op(d, qr, qi):
        vr = qr[:, :, :, None] * bb_re[d][None] - qi[:, :, :, None] * bb_im[d][None]
        vi = qr[:, :, :, None] * bb_im[d][None] + qi[:, :, :, None] * bb_re[d][None]
        vr = jnp.transpose(vr, (1, 0, 3, 2)).reshape(G, T * I, S5_STATE)
        vi = jnp.transpose(vi, (1, 0, 3, 2)).reshape(G, T * I, S5_STATE)
        return jnp.concatenate([vr, vi, -vi, vr], axis=-1)

    vt = jnp.concatenate([v_op(0, pr[:T, 0][::-1], pi[:T, 0][::-1]), v_op(1, pr[:T, 1], pi[:T, 1])], axis=-1)

    def w_op(d, qr, qi):
        wr = cr[d][None] * qr[:, :, None, :] - ci[d][None] * qi[:, :, None, :]
        wi = cr[d][None] * qi[:, :, None, :] + ci[d][None] * qr[:, :, None, :]
        wr = jnp.transpose(wr, (1, 3, 0, 2)).reshape(G, S5_STATE, T * I)
        wi = jnp.transpose(wi, (1, 3, 0, 2)).reshape(G, S5_STATE, T * I)
        return jnp.concatenate([wr, -wi], axis=1)

    w = jnp.concatenate([w_op(0, pr[1:T + 1, 0], pi[1:T + 1, 0]),
                         w_op(1, pr[1:T + 1, 1][::-1], pi[1:T + 1, 1][::-1])], axis=1)

    def dup(z):
        return jnp.concatenate([z, z], -1)

    rows = []
    for d in range(2):
        rows += [dup(pr16[1, d]), dup(pi16[1, d]), dup(pr16[S5_SUPER, d]), dup(pi16[S5_SUPER, d])]
    dec = jnp.stack(rows, axis=1)
    prs = []
    for d in range(2):
        prs += [dup(jnp.transpose(pr16[:S5_SUPER, d], (1, 0, 2))), dup(jnp.transpose(pi16[:S5_SUPER, d], (1, 0, 2)))]
    pwt = jnp.concatenate(prs, axis=1)
    return mt.astype(BF16), vt.astype(BF16), w.astype(BF16), dec, pwt


S5_LANE_GROUPS = LANES // S5_GROUP


def _piece_transpose(arrs, lane):
    for d in (4, 2, 1):
        bit = ((lane // S5_GROUP) & d) != 0
        new = list(arrs)
        for r in range(S5_LANE_GROUPS):
            if r & d == 0:
                a, b = arrs[r], arrs[r + d]
                if 2 * S5_GROUP * d == LANES:
                    c = pltpu.roll(jnp.where(bit, a, b), S5_GROUP * d, 1)
                    new[r] = jnp.where(bit, c, a)
                    new[r + d] = jnp.where(bit, b, c)
                    continue
                new[r] = jnp.where(bit, pltpu.roll(b, S5_GROUP * d, 1), a)
                new[r + d] = jnp.where(bit, b, pltpu.roll(a, LANES - S5_GROUP * d, 1))
        arrs = new
    return arrs


def _s5_kernel(u_ref, mt_ref, vt_ref, w_ref, dec_ref, pw_ref, dsk_ref, y_ref, xg_ref, yg_ref, vv_ref, hin_ref,
               hs_ref, *, nsc):
    nj = S5_SUPER
    nch = u_ref.shape[0] // S5_CHUNK
    ng = S5_LANE_GROUPS
    lane = lax.broadcasted_iota(jnp.int32, (nch, LANES), 1)
    for h in range(2):
        zs = [u_ref[pl.ds(ng * h + t, nch, stride=S5_CHUNK), :] for t in range(ng)]
        for g, a in enumerate(_piece_transpose(zs, lane)):
            xg_ref[g, :, LANES * h:LANES * (h + 1)] = a
    hs_ref[...] = jnp.zeros_like(hs_ref)

    def group(g, carry):
        x = xg_ref[g]
        xb = x.astype(BF16)
        yg_ref[g] = jnp.dot(xb, mt_ref[g], preferred_element_type=F32) + x * dsk_ref[g]
        vv = jnp.dot(xb, vt_ref[g], preferred_element_type=F32)
        for n in range(4):
            vv_ref[n] = vv[:, LANES * n:LANES * (n + 1)]
        dec = dec_ref[g]
        pw = pw_ref[g]
        for d in range(2):
            ar16, ai16, ar256, ai256 = (dec[4 * d + n:4 * d + n + 1] for n in range(4))

            def v(j, d=d):
                return vv_ref[2 * d, pl.ds(j, nsc, stride=nj), :]

            def jv(j, d=d):
                return vv_ref[2 * d + 1, pl.ds(j, nsc, stride=nj), :]

            order = list(range(nj)) if d == 0 else list(range(nj - 1, -1, -1))
            loc = [None] * nj
            l = jnp.zeros((nsc, LANES), F32)
            gg = jnp.zeros((nsc, LANES), F32)
            for n, j in enumerate(order):
                if n > 0:
                    p = order[n - 1]
                    l, gg = ar16 * l + ai16 * gg + v(p), ar16 * gg - ai16 * l + jv(p)
                loc[j] = l
            p = order[-1]
            e, je = ar16 * l + ai16 * gg + v(p), ar16 * gg - ai16 * l + jv(p)
            seq = list(range(nsc)) if d == 0 else [0] + list(range(nsc - 1, 0, -1))
            cur = jnp.zeros((1, LANES), F32)
            jcur = jnp.zeros((1, LANES), F32)
            for m in seq:
                hs_ref[2 * d, m:m + 1, :] = cur
                hs_ref[2 * d + 1, m:m + 1, :] = jcur
                cur, jcur = (ar256 * cur + ai256 * jcur + e[m:m + 1], ar256 * jcur - ai256 * cur + je[m:m + 1])
            hs = hs_ref[2 * d, 0:nsc, :]
            jhs = hs_ref[2 * d + 1, 0:nsc, :]
            for n, j in enumerate(order):
                pr = pw[32 * d + n:32 * d + n + 1]
                pi = pw[32 * d + 16 + n:32 * d + 16 + n + 1]
                hin_ref[d, pl.ds(j, nsc, stride=nj), :] = loc[j] + pr * hs + pi * jhs
        hin = jnp.concatenate([hin_ref[0], hin_ref[1]], axis=1).astype(BF16)
        yg_ref[g] += jnp.dot(hin, w_ref[g], preferred_element_type=F32)
        return carry

    lax.fori_loop(0, ng, group, 0, unroll=2)
    for h in range(2):
        ys = [yg_ref[g, :, LANES * h:LANES * (h + 1)] for g in range(ng)]
        for t, a in enumerate(_piece_transpose(ys, lane)):
            y_ref[pl.ds(ng * h + t, nch, stride=S5_CHUNK), :] = a


def _s5_scan(px, ops, dsk, *, a_width):
    B, T, _ = px.shape
    mt, vt, w, dec, pw = ops
    ng = S5_LANE_GROUPS
    C = S5_CHUNK * S5_GROUP
    nch = T // S5_CHUNK
    nsc = nch // S5_SUPER
    nsp = (nsc + 7) // 8 * 8
    return pl.pallas_call(
        functools.partial(_s5_kernel, nsc=nsc),
        grid=(a_width // LANES, B),
        in_specs=[pl.BlockSpec((None, T, LANES), lambda gb, b: (b, 0, gb)),
                  pl.BlockSpec((ng, C, C), lambda gb, b: (gb, 0, 0)),
                  pl.BlockSpec((ng, C, 2 * C), lambda gb, b: (gb, 0, 0)),
                  pl.BlockSpec((ng, C, C), lambda gb, b: (gb, 0, 0)),
                  pl.BlockSpec((ng, 8, LANES), lambda gb, b: (gb, 0, 0)),
                  pl.BlockSpec((ng, 64, LANES), lambda gb, b: (gb, 0, 0)),
                  pl.BlockSpec((ng, 1, C), lambda gb, b: (gb, 0, 0))],
        out_specs=pl.BlockSpec((None, T, LANES), lambda gb, b: (b, 0, gb)),
        out_shape=jax.ShapeDtypeStruct((B, T, a_width), F32),
        scratch_shapes=[pltpu.VMEM((ng, nch, C), F32), pltpu.VMEM((ng, nch, C), F32),
                        pltpu.VMEM((4, nch, LANES), F32), pltpu.VMEM((2, nch, LANES), F32),
                        pltpu.VMEM((4, nsp, LANES), F32)],
        compiler_params=_cparams(("arbitrary", "arbitrary")),
        name="s5_scan",
    )(px, mt, vt, w, dec, pw, dsk)


def _s5_glu_kernel(y_ref, w_ref, o_ref):
    g = jax.nn.gelu(y_ref[...])
    z = jnp.dot(g.astype(BF16), w_ref[...], preferred_element_type=F32)
    o_ref[...] = (g * jax.nn.sigmoid(z)).astype(o_ref.dtype)


def _s5_glu(y, w, *, tm):
    B, T, A = y.shape
    return pl.pallas_call(
        _s5_glu_kernel,
        grid=(B, T // tm),
        in_specs=[pl.BlockSpec((None, tm, A), lambda b, i: (b, i, 0)),
                  pl.BlockSpec((A, A), lambda b, i: (0, 0))],
        out_specs=pl.BlockSpec((None, tm, A), lambda b, i: (b, i, 0)),
        out_shape=jax.ShapeDtypeStruct((B, T, A), BF16),
        compiler_params=_cparams(("arbitrary", "arbitrary")),
        name="s5_glu",
    )(y, w)


def _log_sigmoid(z):
    return jnp.minimum(z, 0.0) - jnp.log(1.0 + jnp.exp(-jnp.abs(z)))


def _gla_kernel(*refs, nblk, dk, dv, hp, n_cast):
    q_ref, k_ref, v_ref, r_ref, lr_ref, cos_ref, sin_ref, wg_ref, bg_ref, ng_ref = refs[:10]
    cast_in = refs[10:10 + n_cast]
    o_ref = refs[10 + n_cast]
    cast_out = refs[11 + n_cast:11 + 2 * n_cast]
    s_ref, of_ref = refs[11 + 2 * n_cast:]
    p = pl.program_id(2)
    s = pl.program_id(3)
    C = GLA_CHUNK
    nch = GLA_BLOCK // C
    blk = jnp.where(p == 0, s, jnp.where(s == 0, 0, nblk - s))

    @pl.when(s == 0)
    def _():
        s_ref[...] = jnp.zeros_like(s_ref)

    NB = GLA_BLOCK
    lane = lax.broadcasted_iota(jnp.int32, (NB, dk), 1)
    lower = (lane & 63) < 32
    rpos = lax.broadcasted_iota(jnp.int32, (NB, dk), 0) & (C - 1)
    row = lax.broadcasted_iota(jnp.int32, (NB, NB), 0)
    col = lax.broadcasted_iota(jnp.int32, (NB, NB), 1)
    same_chunk = (row // C) == (col // C)
    eye = (lax.broadcasted_iota(jnp.int32, (dk, dk), 0) == lax.broadcasted_iota(jnp.int32, (dk, dk), 1)).astype(BF16)

    def rope(z, cos, sin):
        return z * cos + jnp.where(lower, pltpu.roll(z, dk - 32, 1), pltpu.roll(z, 32, 1)) * sin

    def run(reverse):
        _side_cast(cast_in, cast_out)
        for hd in range(hp):
            head(reverse, hd)

    def head(reverse, hd):
        mask = same_chunk & ((row <= col) if reverse else (row >= col))
        cos = cos_ref[...]
        sin = sin_ref[...]
        kcols = slice(hd * dk, (hd + 1) * dk)
        vcols = slice(hd * dv, (hd + 1) * dv)
        q = rope(q_ref[:, kcols], cos, sin) * (dk ** -0.5)
        k = rope(k_ref[:, kcols], cos, sin)
        vb = v_ref[:, vcols].astype(BF16)
        lrb = lr_ref[...].astype(BF16)
        la = _log_sigmoid(jnp.dot(lrb, wg_ref[hd], preferred_element_type=F32) + bg_ref[hd]) * (1.0 / GLA_TAU)
        b = la
        sh = 1
        while sh < C:
            if reverse:
                b = b + jnp.where(rpos < C - sh, pltpu.roll(b, NB - sh, 0), 0.0)
            else:
                b = b + jnp.where(rpos >= sh, pltpu.roll(b, sh, 0), 0.0)
            sh *= 2
        last = C - 1 if not reverse else 0
        tots = [b[c * C + last:c * C + last + 1] for c in range(nch)]
        btot = jnp.concatenate([jnp.broadcast_to(t, (C, dk)) for t in tots], 0)
        dmat = jnp.exp(jnp.concatenate(tots + [jnp.zeros((8 - nch, dk), F32)], 0).T)
        qi = (q * jnp.exp(b)).astype(BF16)
        ki = (k * jnp.exp(-b)).astype(BF16)
        ko = (k * jnp.exp(btot - b)).astype(BF16)
        att = jnp.where(mask, _nt(qi, ki), 0.0).astype(BF16)
        o = jnp.dot(att, vb, preferred_element_type=F32)
        upd, dcol = [], []
        for c in range(nch):
            kot = _nt(eye, ko[c * C:(c + 1) * C]).astype(BF16)
            upd.append(jnp.dot(kot, vb[c * C:(c + 1) * C], preferred_element_type=F32))
            dcol.append(dmat[:, c:c + 1])
        st = s_ref[hd]
        o_state = [None] * nch
        for c in (range(nch - 1, -1, -1) if reverse else range(nch)):
            o_state[c] = jnp.dot(qi[c * C:(c + 1) * C], st.astype(BF16), preferred_element_type=F32)
            st = st * dcol[c] + upd[c]
        s_ref[hd] = st
        o = o + jnp.concatenate(o_state, axis=0)
        tok = pl.ds(pl.multiple_of(blk * NB, NB), NB)
        if not reverse:
            of_ref[tok, vcols] = o
        else:
            o = o + of_ref[tok, vcols]
            o = o * lax.rsqrt(jnp.mean(o * o, axis=-1, keepdims=True) + EPS) * ng_ref[hd]
            r = r_ref[:, vcols]
            o_ref[:, vcols] = (o * (r * jax.nn.sigmoid(r))).astype(o_ref.dtype)

    @pl.when(p == 0)
    def _():
        run(False)

    @pl.when(p == 1)
    def _():
        run(True)


def _rope_tables(n_ctx, seq, dk):
    nf = dk // 4
    inv = ROPE_THETA ** (-jnp.arange(nf, dtype=F32) / nf)
    pos = jnp.arange(seq)
    ang_r = (pos // GRID_W).astype(F32)[:, None] * inv[None, :]
    ang_c = (pos % GRID_W).astype(F32)[:, None] * inv[None, :]
    cos = jnp.concatenate([jnp.cos(ang_r)] * 2 + [jnp.cos(ang_c)] * 2, axis=-1)
    sin = jnp.concatenate([-jnp.sin(ang_r), jnp.sin(ang_r), -jnp.sin(ang_c), jnp.sin(ang_c)], axis=-1)
    cos = jnp.concatenate([jnp.ones((n_ctx, dk), F32), cos], axis=0)
    sin = jnp.concatenate([jnp.zeros((n_ctx, dk), F32), sin], axis=0)
    return cos, sin


def _gla(px, w_gate2, b_gate, norm_g, cast, *, n_ctx, a_width):
    B, T, _ = px.shape
    H = GLA_HEADS
    qk = w_gate2.shape[-1]
    dk = qk // H
    bw = norm_g.shape[-1]
    dv = bw // H
    nblk = T // GLA_BLOCK
    cos, sin = _rope_tables(n_ctx, T - n_ctx, dk)
    lr_col0 = a_width + 2 * qk + 2 * bw
    wg = jnp.zeros((2, H, LANES, dk), F32)
    for d in range(2):
        wg = wg.at[d, :, d * GLA_RANK:(d + 1) * GLA_RANK, :].set(
            jnp.transpose(w_gate2[d].reshape(GLA_RANK, H, dk), (1, 0, 2)))
    wg = wg.astype(BF16)
    bg = b_gate.astype(F32).reshape(2, H, 1, dk)
    ng = norm_g.astype(F32).reshape(H, 1, dv)
    hp = GLA_HEADS_PER_STEP

    def blk(p, s):
        return jnp.where(p == 0, s, jnp.where(s == 0, 0, nblk - s))

    kw, vw = hp * dk, hp * dv
    qb, kb = a_width // kw, (a_width + qk) // kw
    vbk, rbk = (a_width + 2 * qk) // vw, (a_width + 2 * qk + bw) // vw
    lb = lr_col0 // LANES
    ngrp = H // hp
    views, c_in, c_out, c_shapes = _side_cast_specs(
        cast, B * ngrp * 2 * nblk, lambda b, g, p, s: ((b * ngrp + g) * 2 + p) * nblk + s)
    outs = pl.pallas_call(
        functools.partial(_gla_kernel, nblk=nblk, dk=dk, dv=dv, hp=hp, n_cast=len(views)),
        grid=(B, ngrp, 2, nblk),
        in_specs=[pl.BlockSpec((None, GLA_BLOCK, kw), lambda b, g, p, s: (b, blk(p, s), qb + g)),
                  pl.BlockSpec((None, GLA_BLOCK, kw), lambda b, g, p, s: (b, blk(p, s), kb + g)),
                  pl.BlockSpec((None, GLA_BLOCK, vw), lambda b, g, p, s: (b, blk(p, s), vbk + g)),
                  pl.BlockSpec((None, GLA_BLOCK, vw), lambda b, g, p, s: (b, blk(p, s), rbk + g)),
                  pl.BlockSpec((None, GLA_BLOCK, LANES), lambda b, g, p, s: (b, blk(p, s), lb)),
                  pl.BlockSpec((GLA_BLOCK, dk), lambda b, g, p, s: (blk(p, s), 0)),
                  pl.BlockSpec((GLA_BLOCK, dk), lambda b, g, p, s: (blk(p, s), 0)),
                  pl.BlockSpec((None, hp, LANES, dk), lambda b, g, p, s: (p, g, 0, 0)),
                  pl.BlockSpec((None, hp, 1, dk), lambda b, g, p, s: (p, g, 0, 0)),
                  pl.BlockSpec((hp, 1, dv), lambda b, g, p, s: (g, 0, 0))] + c_in,
        out_specs=[pl.BlockSpec((None, GLA_BLOCK, vw),
                                lambda b, g, p, s: (b, jnp.where(p == 0, 0, blk(p, s)), g))] + c_out,
        out_shape=[jax.ShapeDtypeStruct((B, T, bw), BF16)] + c_shapes,
        scratch_shapes=[pltpu.VMEM((hp, dk, dv), F32), pltpu.VMEM((T, vw), F32)],
        compiler_params=_cparams(("arbitrary", "arbitrary", "arbitrary", "arbitrary")),
        name="gla",
    )(px, px, px, px, px, cos, sin, wg, bg, ng, *views)
    return outs[0], outs[1:]


def _na_kernel(*refs, n_ctx, rows, n_cast):
    q_ref, k_ref, v_ref, bias_ref, qn_ref, kn_ref = refs[:6]
    o_ref = refs[6 + n_cast]
    kb_ref, vb_ref = refs[7 + 2 * n_cast:]
    _side_cast(refs[6:6 + n_cast], refs[7 + n_cast:7 + 2 * n_cast])
    hd = NA_HEAD_DIM
    W = GRID_W
    T = k_ref.shape[0]
    step = n_ctx

    def prep(i, c):
        sl = pl.ds(pl.multiple_of(i * step, step), step)
        kk = k_ref[sl, :]
        kb_ref[sl, :] = (kk * lax.rsqrt(jnp.mean(kk * kk, axis=-1, keepdims=True) + EPS) * kn_ref[...]).astype(BF16)
        vb_ref[sl, :] = v_ref[sl, :].astype(BF16)
        return c

    lax.fori_loop(0, T // step, prep, 0)
    scale = hd ** -0.5
    ngrp = rows // NA_QROWS
    nq = NA_QROWS * W
    nk = NA_UROWS * W

    def group(g, c):
        r0 = g * NA_QROWS
        ustart = jnp.clip(r0 - NA_KH // 2, 0, rows - NA_UROWS)
        case = jnp.where(g == 0, 0, jnp.where(g == ngrp - 1, 2, 1))
        q = q_ref[pl.ds(pl.multiple_of(n_ctx + r0 * W, W), nq), :]
        q = (q * lax.rsqrt(jnp.mean(q * q, axis=-1, keepdims=True) + EPS) * qn_ref[...] * scale).astype(BF16)
        ws = pl.ds(pl.multiple_of(n_ctx + ustart * W, W), nk)
        s_lat = _nt(q, kb_ref[ws, :]) + bias_ref[case]
        s_ctx = _nt(q, kb_ref[0:n_ctx, :])
        m = jnp.maximum(jnp.max(s_lat, axis=-1, keepdims=True), jnp.max(s_ctx, axis=-1, keepdims=True))
        p_lat = jnp.exp(s_lat - m)
        p_ctx = jnp.exp(s_ctx - m)
        den = jnp.sum(p_lat, axis=-1, keepdims=True) + jnp.sum(p_ctx, axis=-1, keepdims=True)
        o = (jnp.dot(p_lat.astype(BF16), vb_ref[ws, :], preferred_element_type=F32)
             + jnp.dot(p_ctx.astype(BF16), vb_ref[0:n_ctx, :], preferred_element_type=F32))
        o_ref[pl.ds(pl.multiple_of(r0 * W, W), nq), :] = (o / den).astype(o_ref.dtype)
        return c

    lax.fori_loop(0, ngrp, group, 0, unroll=4)


def _na_bias(rpb, rows):
    H = rpb.shape[0]
    col = jnp.arange(GRID_W)
    cs = jnp.clip(col - NA_KW // 2, 0, GRID_W - NA_KW)
    col_ok = (col[None, :] >= cs[:, None]) & (col[None, :] < cs[:, None] + NA_KW)
    dc_idx = jnp.clip(col[None, :] - col[:, None] + NA_KW - 1, 0, 2 * NA_KW - 2)
    rpb_c = jnp.where(col_ok[None, None], rpb.astype(F32)[:, :, dc_idx], NEG_INF)
    ngrp = rows // NA_QROWS
    tables = []
    for g in (0, 1, ngrp - 1):
        r0 = g * NA_QROWS
        ustart = min(max(r0 - NA_KH // 2, 0), rows - NA_UROWS)
        qr = r0 + jnp.arange(NA_QROWS)
        start = jnp.clip(qr - NA_KH // 2, 0, rows - NA_KH)
        kr = ustart + jnp.arange(NA_UROWS)
        ok = (kr[None, :] >= start[:, None]) & (kr[None, :] < start[:, None] + NA_KH)
        idx = jnp.clip(kr[None, :] - qr[:, None] + NA_KH - 1, 0, 2 * NA_KH - 2)
        t = jnp.where(ok[None, :, :, None, None], rpb_c[:, idx], NEG_INF)
        tables.append(jnp.transpose(t, (0, 1, 3, 2, 4)).reshape(H, NA_QROWS * GRID_W, NA_UROWS * GRID_W))
    return jnp.stack(tables, axis=1)


def _na_attention(qkv, q_norm, k_norm, rpb, cast, *, n_ctx):
    B, T, D3 = qkv.shape
    D = D3 // 3
    H = D // NA_HEAD_DIM
    seq = T - n_ctx
    rows = seq // GRID_W
    assert rows % NA_QROWS == 0 and rows >= NA_UROWS + NA_QROWS
    bias = _na_bias(rpb, rows)
    hd = NA_HEAD_DIM
    views, c_in, c_out, c_shapes = _side_cast_specs(cast, B * H, lambda b, h: b * H + h)
    outs = pl.pallas_call(
        functools.partial(_na_kernel, n_ctx=n_ctx, rows=rows, n_cast=len(views)),
        grid=(B, H),
        in_specs=[pl.BlockSpec((None, T, hd), lambda b, h: (b, 0, h)),
                  pl.BlockSpec((None, T, hd), lambda b, h: (b, 0, H + h)),
                  pl.BlockSpec((None, T, hd), lambda b, h: (b, 0, 2 * H + h)),
                  pl.BlockSpec((None, 3, NA_QROWS * GRID_W, NA_UROWS * GRID_W), lambda b, h: (h, 0, 0, 0)),
                  pl.BlockSpec((1, hd), lambda b, h: (0, 0)),
                  pl.BlockSpec((1, hd), lambda b, h: (0, 0))] + c_in,
        out_specs=[pl.BlockSpec((None, seq, hd), lambda b, h: (b, 0, h))] + c_out,
        out_shape=[jax.ShapeDtypeStruct((B, seq, D), BF16)] + c_shapes,
        scratch_shapes=[pltpu.VMEM((T, hd), BF16), pltpu.VMEM((T, hd), BF16)],
        compiler_params=_cparams(("arbitrary", "arbitrary")),
        name="na_attention",
    )(qkv, qkv, qkv, bias, q_norm.astype(F32).reshape(1, hd), k_norm.astype(F32).reshape(1, hd), *views)
    return outs[0], outs[1:]


def _router_kernel(u_ref, g_ref, sh_ref, sc_ref, wr_ref, h_ref, mf_ref, mi_ref):
    h = _normmod(u_ref[...], g_ref[...], sh_ref[...], sc_ref[...])
    h_ref[...] = h
    w = wr_ref[...]
    w_hi = w.astype(BF16)
    w_lo = (w - w_hi.astype(F32)).astype(BF16)
    h_hi = h.astype(BF16)
    h_lo = (h - h_hi.astype(F32)).astype(BF16)
    logits = (jnp.dot(h_hi, w_hi, preferred_element_type=F32) + jnp.dot(h_hi, w_lo, preferred_element_type=F32)
              + jnp.dot(h_lo, w_hi, preferred_element_type=F32))
    lane = lax.broadcasted_iota(jnp.int32, logits.shape, 1).astype(F32)
    big = 1e9

    def first_max(vals):
        m = jnp.max(vals, axis=-1, keepdims=True)
        return m, jnp.min(jnp.where(vals == m, lane, big), axis=-1, keepdims=True)

    gl = jnp.where(lane < MOE_GROUPS, logits, -jnp.inf)
    gmax, gidx = first_max(gl)
    p_top = 1.0 / jnp.sum(jnp.exp(gl - gmax), axis=-1, keepdims=True)
    lo = MOE_GROUPS + MOE_PER_GROUP * gidx
    el = jnp.where((lane >= lo) & (lane < lo + MOE_PER_GROUP), logits, -jnp.inf)
    v1, i1 = first_max(el)
    v2, i2 = first_max(jnp.where(lane == i1, -jnp.inf, el))
    e2 = jnp.exp(v2 - v1)
    w1 = p_top / (1.0 + e2)
    w2 = p_top * e2 / (1.0 + e2)
    mf_ref[...] = jnp.where(lane == 0, w1, jnp.where(lane == 1, w2, 0.0))
    mi_ref[...] = jnp.where(lane == 0, i1 - MOE_GROUPS, jnp.where(lane == 1, i2 - MOE_GROUPS, 0.0)).astype(jnp.int32)


def _router(u, gamma, mods, k_shift, k_scale, wr, *, n_ctx, tm):
    B, T, D = u.shape
    sel = _mod_sel(n_ctx // tm, B)
    return pl.pallas_call(
        _router_kernel,
        grid=(B, T // tm),
        in_specs=[pl.BlockSpec((None, tm, D), lambda b, i: (b, i, 0)),
                  pl.BlockSpec((1, D), lambda b, i: (0, 0)),
                  pl.BlockSpec((None, None, 1, D), lambda b, i: (sel(b, i), k_shift, 0, 0)),
                  pl.BlockSpec((None, None, 1, D), lambda b, i: (sel(b, i), k_scale, 0, 0)),
                  pl.BlockSpec((D, LANES), lambda b, i: (0, 0))],
        out_specs=[pl.BlockSpec((None, tm, D), lambda b, i: (b, i, 0)),
                   pl.BlockSpec((None, tm, LANES), lambda b, i: (b, i, 0)),
                   pl.BlockSpec((None, tm, LANES), lambda b, i: (b, i, 0))],
        out_shape=[jax.ShapeDtypeStruct((B, T, D), F32),
                   jax.ShapeDtypeStruct((B, T, LANES), F32),
                   jax.ShapeDtypeStruct((B, T, LANES), jnp.int32)],
        compiler_params=_cparams(("arbitrary", "arbitrary")),
        name="moe_router",
    )(u, gamma.reshape(1, D), mods, mods, wr)


def _moe_plan(eid2):
    N = eid2.shape[0]
    M = 2 * N
    E = MOE_EXPERTS
    eid = eid2.reshape(-1)
    onehot = (eid[:, None] == jnp.arange(E, dtype=jnp.int32)[None, :]).astype(jnp.int32)
    csum = jnp.cumsum(onehot, axis=0)
    rank = jnp.sum(onehot * csum, axis=1) - 1
    counts = csum[-1]
    padded = (counts + MOE_ROWS - 1) // MOE_ROWS * MOE_ROWS
    pend = jnp.cumsum(padded)
    pstart = pend - padded
    dest = (jnp.sum(onehot * pstart[None, :], axis=1) + rank).astype(jnp.int32)
    P = (M + E * (MOE_ROWS - 1) + MOE_ROWS - 1) // MOE_ROWS * MOE_ROWS
    nb = P // MOE_ROWS
    slot_tok = jnp.zeros((P,), jnp.int32).at[dest].set(jnp.arange(M, dtype=jnp.int32) // 2)
    pos = dest.reshape(N, 2)
    blk_start = jnp.arange(nb, dtype=jnp.int32) * MOE_ROWS
    blk_e = jnp.minimum(jnp.sum((pend[None, :] <= blk_start[:, None]).astype(jnp.int32), axis=1), E - 1)
    n_active = (pend[-1] // MOE_ROWS).astype(jnp.int32).reshape(1)
    return slot_tok.reshape(nb, MOE_ROWS), pos, blk_e.astype(jnp.int32), n_active


def _row_copy(src_hbm, idx_smem, dst_ref, sem, r):
    return pltpu.make_async_copy(src_hbm.at[pl.ds(idx_smem[r], 1), :], dst_ref.at[pl.ds(r, 1), :], sem)


def _expert_kernel(blk_e_ref, nact_ref, idx_hbm, h_hbm, wg_ref, wu_ref, wd_ref, o_ref, idx0, idx1, xbuf, sem_idx,
                   sem_rows):
    i = pl.program_id(0)
    nact = nact_ref[0]
    R = MOE_ROWS
    slot = lax.rem(i, 2)
    idx_bufs = (idx0, idx1)

    def idx_copy(blk, s):
        return pltpu.make_async_copy(idx_hbm.at[blk], idx_bufs[s], sem_idx.at[s])

    def issue_rows(s):
        def body(r, c):
            _row_copy(h_hbm, idx_bufs[s], xbuf.at[s], sem_rows.at[s], r).start(priority=1)
            return c

        lax.fori_loop(0, R, body, 0, unroll=8)

    @pl.when(i == 0)
    def _():
        first = idx_copy(0, 0)
        first.start()
        first.wait()
        issue_rows(0)

        @pl.when(nact > 1)
        def _():
            idx_copy(1, 1).start()

    def prefetch(s):
        @pl.when(i + 1 < nact)
        def _():
            idx_copy(i + 1, 1 - s).wait()
            issue_rows(1 - s)

        @pl.when(i + 2 < nact)
        def _():
            idx_copy(i + 2, s).start()

    @pl.when(slot == 0)
    def _():
        prefetch(0)

    @pl.when(slot == 1)
    def _():
        prefetch(1)

    @pl.when(i < nact)
    def _():
        pltpu.make_async_copy(h_hbm.at[pl.ds(0, R), :], xbuf.at[slot], sem_rows.at[slot]).wait()
        xb = xbuf[slot].astype(BF16)
        g = jnp.dot(xb, wg_ref[...], preferred_element_type=F32)
        u = jnp.dot(xb, wu_ref[...], preferred_element_type=F32)
        hmid = (g * jax.nn.sigmoid(g) * u).astype(BF16)
        o_ref[...] = jnp.dot(hmid, wd_ref[...], preferred_element_type=F32)

    @pl.when(i >= nact)
    def _():
        o_ref[...] = jnp.zeros_like(o_ref)


def _expert_mlp(h, slot_tok, blk_e, n_active, wg, wu, wd):
    nb, R = slot_tok.shape
    D = h.shape[1]
    F = wg.shape[2]
    grid_spec = pltpu.PrefetchScalarGridSpec(
        num_scalar_prefetch=2,
        grid=(nb,),
        in_specs=[pl.BlockSpec(memory_space=pl.ANY),
                  pl.BlockSpec(memory_space=pl.ANY),
                  pl.BlockSpec((None, D, F), lambda i, be, na: (be[i], 0, 0)),
                  pl.BlockSpec((None, D, F), lambda i, be, na: (be[i], 0, 0)),
                  pl.BlockSpec((None, F, D), lambda i, be, na: (be[i], 0, 0))],
        out_specs=pl.BlockSpec((R, D), lambda i, be, na: (i, 0)),
        scratch_shapes=[pltpu.SMEM((R,), jnp.int32), pltpu.SMEM((R,), jnp.int32), pltpu.VMEM((2, R, D), F32),
                        pltpu.SemaphoreType.DMA((2,)), pltpu.SemaphoreType.DMA((2,))],
    )
    return pl.pallas_call(
        _expert_kernel,
        grid_spec=grid_spec,
        out_shape=jax.ShapeDtypeStruct((nb * R, D), F32),
        compiler_params=_cparams(("arbitrary",)),
        name="moe_experts",
    )(blk_e, n_active, slot_tok, h, wg, wu, wd)


def _combine_kernel(*refs, n_cast):
    p1_hbm, p2_hbm, yb_hbm, mf_ref, r_ref, g_ref = refs[:6]
    o_ref = refs[6 + n_cast]
    i1a, i1b, i2a, i2b, g1_ref, g2_ref, sem_idx, sem_rows = refs[7 + 2 * n_cast:]
    i1_bufs, i2_bufs = (i1a, i1b), (i2a, i2b)
    t = pl.program_id(0) * pl.num_programs(1) + pl.program_id(1)
    n = pl.num_programs(0) * pl.num_programs(1)
    R = o_ref.shape[0]
    slot = lax.rem(t, 2)

    def idx_copies(tile, s):
        return (pltpu.make_async_copy(p1_hbm.at[tile], i1_bufs[s], sem_idx.at[s]),
                pltpu.make_async_copy(p2_hbm.at[tile], i2_bufs[s], sem_idx.at[s]))

    def issue_rows(s):
        def body(r, c):
            _row_copy(yb_hbm, i1_bufs[s], g1_ref.at[s], sem_rows.at[s], r).start(priority=0)
            _row_copy(yb_hbm, i2_bufs[s], g2_ref.at[s], sem_rows.at[s], r).start(priority=1)
            return c

        lax.fori_loop(0, R, body, 0, unroll=8)

    @pl.when(t == 0)
    def _():
        for c in idx_copies(0, 0):
            c.start()
        for c in idx_copies(0, 0):
            c.wait()
        issue_rows(0)

        @pl.when(n > 1)
        def _():
            for c in idx_copies(1, 1):
                c.start()

    def prefetch(s):
        @pl.when(t + 1 < n)
        def _():
            for c in idx_copies(t + 1, 1 - s):
                c.wait()
            issue_rows(1 - s)

        @pl.when(t + 2 < n)
        def _():
            for c in idx_copies(t + 2, s):
                c.start()

    @pl.when(slot == 0)
    def _():
        prefetch(0)

    @pl.when(slot == 1)
    def _():
        prefetch(1)

    _side_cast(refs[6:6 + n_cast], refs[7 + n_cast:7 + 2 * n_cast])
    pltpu.make_async_copy(yb_hbm.at[pl.ds(0, R), :], g1_ref.at[slot], sem_rows.at[slot]).wait()
    pltpu.make_async_copy(yb_hbm.at[pl.ds(0, R), :], g2_ref.at[slot], sem_rows.at[slot]).wait()
    mf = mf_ref[...]
    y = mf[:, 0:1] * g1_ref[slot] + mf[:, 1:2] * g2_ref[slot]
    o_ref[...] = r_ref[...] + g_ref[...] * y


def _moe_combine(yb, pos, mf, res, mods, k_gate, cast, *, n_ctx):
    B, T, _ = mf.shape
    D = yb.shape[1]
    R = MOE_ROWS
    nt = T // R
    sel = _mod_sel(n_ctx // R, B)
    p1 = pos[:, 0].reshape(B * nt, R)
    p2 = pos[:, 1].reshape(B * nt, R)
    views, c_in, c_out, c_shapes = _side_cast_specs(cast, B * nt, lambda b, i: b * nt + i)
    outs = pl.pallas_call(
        functools.partial(_combine_kernel, n_cast=len(views)),
        grid=(B, nt),
        in_specs=[pl.BlockSpec(memory_space=pl.ANY), pl.BlockSpec(memory_space=pl.ANY),
                  pl.BlockSpec(memory_space=pl.ANY),
                  pl.BlockSpec((None, R, LANES), lambda b, i: (b, i, 0)),
                  pl.BlockSpec((None, R, D), lambda b, i: (b, i, 0)),
                  pl.BlockSpec((None, None, 1, D), lambda b, i: (sel(b, i), k_gate, 0, 0))] + c_in,
        out_specs=[pl.BlockSpec((None, R, D), lambda b, i: (b, i, 0))] + c_out,
        out_shape=[jax.ShapeDtypeStruct((B, T, D), F32)] + c_shapes,
        scratch_shapes=[pltpu.SMEM((R,), jnp.int32)] * 4
        + [pltpu.VMEM((2, R, D), F32), pltpu.VMEM((2, R, D), F32),
           pltpu.SemaphoreType.DMA((2,)), pltpu.SemaphoreType.DMA((2,))],
        compiler_params=_cparams(("arbitrary", "arbitrary")),
        name="moe_combine",
    )(p1, p2, yb, mf, res, mods, *views)
    return outs[0], outs[1:]


def _hier_moe(u, gamma, mods, w_route_group, w_route_expert, wg, wu, wd, cast, *, n_ctx):
    B, T, D = u.shape
    E = MOE_EXPERTS
    wr = jnp.zeros((D, LANES), F32)
    wr = wr.at[:, :MOE_GROUPS].set(w_route_group.astype(F32))
    wr = wr.at[:, MOE_GROUPS:MOE_GROUPS + MOE_EXPERTS].set(w_route_expert.astype(F32))
    h, mf, mi = _router(u, gamma, mods, 3, 4, wr, n_ctx=n_ctx, tm=MOE_ROWS)
    slot_tok, pos, blk_e, n_active = _moe_plan(mi[:, :, :2].reshape(B * T, 2))
    yb = _expert_mlp(h.reshape(B * T, D), slot_tok, blk_e, n_active, wg.reshape(E, D, -1), wu.reshape(E, D, -1),
                     wd.reshape(E, -1, D))
    return _moe_combine(yb, pos, mf, u, mods, 5, cast, n_ctx=n_ctx)


def _ab_layer(u, mods, gamma, w_in, w_out, s5p, s5_d, s5_w_glu, gla_w_gate2, gla_b_gate, gla_norm_g, cast, *, n_ctx):
    a_width = s5_w_glu.shape[0]
    n_in = w_in.shape[1]
    n_pad = (n_in + LANES - 1) // LANES * LANES
    w_in_b = jnp.pad(w_in, ((0, 0), (0, n_pad - n_in))).astype(BF16)
    px = _normmod_matmul(u, gamma, mods, 0, 1, w_in_b, n_ctx=n_ctx, tm=256, tn=n_pad // 3)
    G = a_width // S5_GROUP
    ops = _s5_operators(*s5p)
    dsk = jnp.tile(s5_d.astype(F32).reshape(G, 1, S5_GROUP), (1, S5_CHUNK, 1)).reshape(G, 1, S5_CHUNK * S5_GROUP)
    ya = _s5_glu(_s5_scan(px, ops, dsk, a_width=a_width), s5_w_glu.astype(BF16), tm=256)
    yb, casted = _gla(px, gla_w_gate2, gla_b_gate, gla_norm_g, cast, n_ctx=n_ctx, a_width=a_width)
    out = _matmul_residual([ya, yb], w_out.astype(BF16), u, mods, 2, n_ctx=n_ctx, row0=0, tm=256, tn=1024)
    return out, casted


def _na_layer(u, mods, gamma, w_qkv, w_out, q_norm, k_norm, rpb, cast, *, n_ctx):
    qkv = _normmod_matmul(u, gamma, mods, 0, 1, w_qkv.astype(BF16), n_ctx=n_ctx, tm=256, tn=2048)
    o, casted = _na_attention(qkv, q_norm, k_norm, rpb, cast, n_ctx=n_ctx)
    out = _matmul_residual([o], w_out.astype(BF16), u, mods, 2, n_ctx=0, row0=n_ctx, tm=256, tn=1024)
    return out, casted


def kernel(x, c, ctx, c_ctx, ada_w, ada_b, norm1_g, norm2_g, ab_w_in, ab_w_out, s5_lam_re, s5_lam_im, s5_log_dt,
           s5_b_re, s5_b_im, s5_c_re, s5_c_im, s5_d, s5_w_glu, gla_w_gate2, gla_b_gate, gla_norm_g, na_w_qkv,
           na_w_out, na_q_norm, na_k_norm, na_rpb, moe_w_route_group, moe_w_route_expert, moe_w_gate, moe_w_up,
           moe_w_down):
    B, seq, D = x.shape
    n_ctx = ctx.shape[1]
    depth = ada_w.shape[0]
    assert depth == 2 and n_ctx == S5_CHUNK * S5_SUPER and seq % (GRID_W * 4) == 0
    u = jnp.concatenate([ctx, x], axis=1)
    cvec = jnp.zeros((8, D), F32).at[:B].set(c).at[B].set(c_ctx)
    mods_all = _ada_mod(cvec, ada_w, ada_b)[:, :B + 1]
    mods = mods_all[0]
    s5p = (s5_lam_re[0], s5_lam_im[0], s5_log_dt[0], s5_b_re[0], s5_b_im[0], s5_c_re[0], s5_c_im[0])
    u, (wg0, wu0, wd0) = _ab_layer(u, mods, norm1_g[0], ab_w_in[0], ab_w_out[0], s5p, s5_d[0], s5_w_glu[0],
                                   gla_w_gate2[0], gla_b_gate[0], gla_norm_g[0],
                                   [(moe_w_gate, 0), (moe_w_up, 0), (moe_w_down, 0)], n_ctx=n_ctx)
    u, (wd1,) = _hier_moe(u, norm2_g[0], mods, moe_w_route_group[0], moe_w_route_expert[0], wg0, wu0, wd0,
                          [(moe_w_down, 1)], n_ctx=n_ctx)
    mods = mods_all[1]
    xl, (wg1, wu1) = _na_layer(u, mods, norm1_g[1], na_w_qkv[0], na_w_out[0], na_q_norm[0], na_k_norm[0], na_rpb[0],
                               [(moe_w_gate, 1), (moe_w_up, 1)], n_ctx=n_ctx)
    out, _ = _hier_moe(xl, norm2_g[1], mods, moe_w_route_group[1], moe_w_route_expert[1], wg1, wu1, wd1, [],
                       n_ctx=0)
    return out
```
